```python
import math
import jax
import jax.numpy as jnp
from jax import lax
import numpy as np

D_MODEL = 2048
BATCH = 16
SEQ = 256
DEPTH = 1
DEC_BATCH = 2
DEC_SEQ = 1024
PAST_LEN = 512

GRID_W = 64
H_A = 8
DK_A = 128
DV_A = 128
CONV_K = 3
CHUNK = 64
H_B = 8
DQK_B = 64
DV_B = 2 * DQK_B
Q_BLOCK = 128
ROPE_BASE = 10000.0
N_GROUPS = 4
E_PER_GROUP = 8
N_EXPERTS = N_GROUPS * E_PER_GROUP
TOP_K = 2
D_FF_E = D_MODEL // 4
EPS = 1e-6
QKV_A = 2 * H_A * DK_A + H_A * DV_A
IN_SIZES = (QKV_A, H_A * DV_A, 2 * H_A, 2 * H_A, H_B * 2 * DQK_B, H_B * 2 * DQK_B, H_B * DV_B, 2 * D_MODEL)
N_IN = QKV_A + H_A * DV_A + 4 * H_A + 2 * H_B * 2 * DQK_B + H_B * DV_B + 2 * D_MODEL
F32 = jnp.float32

kernel_name = 'hybrid_deltanet_diffattn_hmoe_dit_step'


def _rmsnorm(x, g):
    xf = x.astype(F32)
    y = xf * lax.rsqrt(jnp.mean(xf * xf, axis=-1, keepdims=True) + EPS)
    return (y * g.astype(F32)).astype(x.dtype)


def _l2norm(x):
    xf = x.astype(F32)
    return (xf * lax.rsqrt(jnp.sum(xf * xf, axis=-1, keepdims=True) + EPS)).astype(x.dtype)


def _adaln(cvec, w_ada, b_ada):
    m = jax.nn.silu(cvec) @ w_ada + b_ada
    return jnp.split(m[:, None, :], 6, axis=-1)


def _modulate(h, shift, scale):
    return h * (1 + scale) + shift


def _split_cols(z):
    idx, acc = [], 0
    for s in IN_SIZES[:-1]:
        acc += s
        idx.append(acc)
    return jnp.split(z, idx, axis=-1)


def _dwconv(x, w):
    pad = CONV_K // 2
    return lax.conv_general_dilated(x, w[:, None, :], window_strides=(1,), padding=[(pad, pad)],
                                    dimension_numbers=('NWC', 'WIO', 'NWC'), feature_group_count=x.shape[-1])


def _axial_rope(x):
    t = x.shape[1]
    rows = t // GRID_W
    row = jnp.broadcast_to(jnp.arange(rows)[:, None], (rows, GRID_W)).reshape(t)
    col = jnp.broadcast_to(jnp.arange(GRID_W)[None, :], (rows, GRID_W)).reshape(t)
    half = DQK_B // 2
    nf = half // 2
    inv_freq = ROPE_BASE ** (-jnp.arange(nf, dtype=F32) / nf)

    def rot(xh, pos):
        ang = pos.astype(F32)[:, None] * inv_freq
        cos = jnp.cos(ang)[None, :, None, None, :]
        sin = jnp.sin(ang)[None, :, None, None, :]
        x1, x2 = xh[..., :nf], xh[..., nf:]
        return jnp.concatenate([x1 * cos - x2 * sin, x2 * cos + x1 * sin], axis=-1)

    xf = x.astype(F32)
    return jnp.concatenate([rot(xf[..., :half], row), rot(xf[..., half:], col)], axis=-1).astype(x.dtype)


def _diff_attention(q, k, v, lam):
    b, tq, h, _, d = q.shape
    kf, vf = k.astype(F32), v.astype(F32)
    qb = jnp.moveaxis(q.reshape(b, tq // Q_BLOCK, Q_BLOCK, h, 2, d), 1, 0)

    def block(qblk):
        s = jnp.einsum('bqhmd,bkhmd->bhmqk', qblk.astype(F32), kf) * (d ** -0.5)
        p = jax.nn.softmax(s, axis=-1)
        a = p[:, :, 0] - lam * p[:, :, 1]
        return jnp.einsum('bhqk,bkhe->bqhe', a, vf)

    o = lax.map(block, qb)
    return jnp.moveaxis(o, 0, 1).reshape(b, tq, h, v.shape[-1]).astype(v.dtype)


def _gated_delta_chunked(q, k, v, g, beta, s0):
    out_dtype = v.dtype
    q, k, v = q.astype(F32), k.astype(F32), v.astype(F32)
    lead = q.shape[:-2]
    t = q.shape[-2]
    n = t // CHUNK
    dv = v.shape[-1]
    ch = lambda a: a.reshape(lead + (n, CHUNK) + a.shape[len(lead) + 1:])
    qc = ch(q) * (DK_A ** -0.5)
    kc, vc = ch(k), ch(v)
    gc = jnp.cumsum(ch(g.astype(F32)), axis=-1)
    bc = ch(beta.astype(F32))
    causal = jnp.tril(jnp.ones((CHUNK, CHUNK), dtype=bool))
    decay = jnp.where(causal, jnp.exp(jnp.where(causal, gc[..., :, None] - gc[..., None, :], 0.0)), 0.0)
    kb = kc * bc[..., None]
    a_low = jnp.tril(jnp.einsum('...id,...jd->...ij', kb, kc) * decay, -1)
    eye = jnp.eye(CHUNK, dtype=F32)
    rhs = jnp.concatenate([vc * bc[..., None], kb * jnp.exp(gc)[..., None]], axis=-1)
    sol = lax.linalg.triangular_solve(a_low + eye, rhs, left_side=True, lower=True, unit_diagonal=True)
    u, w = sol[..., :dv], sol[..., dv:]
    a_qk = jnp.einsum('...id,...jd->...ij', qc, kc) * decay
    mv = lambda a, ax: jnp.moveaxis(a, ax, 0)
    xs = (mv(qc * jnp.exp(gc)[..., None], -3), mv(kc * jnp.exp(gc[..., -1:] - gc)[..., None], -3),
          mv(u, -3), mv(w, -3), mv(gc[..., -1], -1), mv(a_qk, -3))

    def step(s, inp):
        q_i, kd_i, u_i, w_i, glast_i, aqk_i = inp
        v_new = u_i - jnp.einsum('...cd,...de->...ce', w_i, s)
        o_i = jnp.einsum('...cd,...de->...ce', q_i, s) + jnp.einsum('...ij,...je->...ie', aqk_i, v_new)
        s = s * jnp.exp(glast_i)[..., None, None] + jnp.einsum('...cd,...ce->...de', kd_i, v_new)
        return s, o_i

    s_fin, o = lax.scan(step, s0.astype(F32), xs)
    o = jnp.moveaxis(o, 0, -3).reshape(lead + (t, dv))
    return o.astype(out_dtype), s_fin


def _mixer(h, lp, lam_init, ctx):
    b, t, _ = h.shape
    qkv_a, gate_a, beta_l, alpha_l, q_b, k_b, v_b, g_merge = _split_cols(h @ lp['w_in'])
    qkv_a = jax.nn.silu(_dwconv(qkv_a, lp['conv_a']))
    q_a, k_a, v_a = jnp.split(qkv_a, [H_A * DK_A, 2 * H_A * DK_A], axis=-1)
    q_a = _l2norm(q_a.reshape(b, t, H_A, DK_A)).transpose(0, 2, 1, 3)
    k_a = _l2norm(k_a.reshape(b, t, H_A, DK_A)).transpose(0, 2, 1, 3)
    v_a = v_a.reshape(b, t, H_A, DV_A).transpose(0, 2, 1, 3)
    per_dir = lambda a: a.astype(F32).reshape(b, t, 2, H_A).transpose(2, 0, 3, 1)
    beta = jax.nn.sigmoid(per_dir(beta_l))
    g = -jnp.exp(lp['a_log'].astype(F32))[:, None, :, None] * jax.nn.softplus(
        per_dir(alpha_l) + lp['dt_bias'].astype(F32)[:, None, :, None])
    both = lambda a: jnp.stack([a, jnp.flip(a, axis=2)])
    rev = lambda a: jnp.stack([a[0], jnp.flip(a[1], axis=-1)])
    if ctx is None:
        s0 = jnp.zeros((2, b, H_A, DK_A, DV_A), F32)
    else:
        s0 = jnp.swapaxes(ctx[0], 0, 1)
    o2, s_fin = _gated_delta_chunked(both(q_a), both(k_a), both(v_a), rev(g), rev(beta), s0)
    o_a = (o2[0] + jnp.flip(o2[1], axis=2)).transpose(0, 2, 1, 3)
    o_a = _rmsnorm(o_a, lp['onorm_a']) * jax.nn.silu(gate_a.reshape(b, t, H_A, DV_A))
    q_b = q_b.reshape(b, t, H_B, 2, DQK_B)
    k_b = k_b.reshape(b, t, H_B, 2, DQK_B)
    v_b = v_b.reshape(b, t, H_B, DV_B)
    if ctx is None:
        keys, vals = k_b, v_b
    else:
        q_b = _axial_rope(q_b)
        n_ctx = ctx[1].shape[2]
        k_ctx = ctx[1].reshape(b, H_B, n_ctx, 2, DQK_B).transpose(0, 2, 1, 3, 4).astype(k_b.dtype)
        keys = jnp.concatenate([k_ctx, _axial_rope(k_b)], axis=1)
        vals = jnp.concatenate([ctx[2].transpose(0, 2, 1, 3).astype(v_b.dtype), v_b], axis=1)
    lv = lp['lam'].astype(F32)
    lam = jnp.exp(jnp.sum(lv[0] * lv[1])) - jnp.exp(jnp.sum(lv[2] * lv[3])) + lam_init
    o_b = _rmsnorm(_diff_attention(q_b, keys, vals, lam), lp['subln_b']) * (1.0 - lam_init)
    g_a, g_b = jnp.split(jax.nn.sigmoid(g_merge), 2, axis=-1)
    mixed = g_a * (o_a.reshape(b, t, -1) @ lp['w_up_a']) + g_b * (o_b.reshape(b, t, -1) @ lp['w_up_b'])
    out = mixed @ lp['w_o']
    if ctx is None:
        return out, (jnp.swapaxes(s_fin, 0, 1),
                     k_b.reshape(b, t, H_B, 2 * DQK_B).transpose(0, 2, 1, 3),
                     v_b.transpose(0, 2, 1, 3))
    return out, None


def _hier_moe(x, w_rg, b_rg, w_re, b_re, w_g, w_u, w_d):
    b, t, d = x.shape
    h = x.reshape(b * t, d)
    p_grp = jax.nn.softmax((h @ w_rg).astype(F32) + b_rg.astype(F32), axis=-1)
    pg_top, g_idx = lax.top_k(p_grp, 1)
    le = ((h @ w_re).astype(F32) + b_re.astype(F32)).reshape(-1, N_GROUPS, E_PER_GROUP)
    le_sel = jnp.einsum('nge,ng->ne', le, jax.nn.one_hot(g_idx[:, 0], N_GROUPS, dtype=F32))
    p_exp = jax.nn.softmax(le_sel, axis=-1)
    pe_top, e_idx = lax.top_k(p_exp, TOP_K)
    w_tok = pg_top * pe_top / jnp.sum(pe_top, axis=-1, keepdims=True)
    combine = jnp.einsum('nk,nke->ne', w_tok,
                         jax.nn.one_hot(g_idx * E_PER_GROUP + e_idx, N_EXPERTS, dtype=F32)).astype(h.dtype)
    y = jnp.zeros_like(h)
    for gi in range(N_GROUPS):
        e0, e1 = gi * E_PER_GROUP, (gi + 1) * E_PER_GROUP
        act = jax.nn.silu(jnp.einsum('nd,edf->enf', h, w_g[e0:e1])) * jnp.einsum('nd,edf->enf', h, w_u[e0:e1])
        y = y + jnp.einsum('enf,efd->nd', act * combine[:, e0:e1].T[:, :, None], w_d[e0:e1])
    return y.reshape(b, t, d)


def setup_inputs(seed: int = 0) -> dict:
    key = jax.random.key(seed)
    ks = jax.random.split(key, 32)
    nrm = lambda k, shape, s: jax.random.normal(k, shape, F32) * s
    gain = lambda k, shape: 1.0 + 0.02 * jax.random.normal(k, shape, F32)
    dt = jnp.exp(jax.random.uniform(ks[14], (DEPTH, 2, H_A), F32, math.log(1e-3), math.log(1e-1)))
    return {
        'x_prompt': nrm(ks[0], (BATCH, SEQ, D_MODEL), 1.0),
        'x_sample': nrm(ks[1], (DEC_BATCH, DEC_SEQ, D_MODEL), 1.0),
        'c': nrm(ks[2], (DEC_BATCH, D_MODEL), 1.0),
        'state_delta': nrm(ks[3], (DEC_BATCH, DEPTH, 2, H_A, DK_A, DV_A), 0.1),
        'cache_k': nrm(ks[4], (DEC_BATCH, DEPTH, H_B, PAST_LEN, 2 * DQK_B), 1.0),
        'cache_v': nrm(ks[5], (DEC_BATCH, DEPTH, H_B, PAST_LEN, DV_B), 1.0),
        'c_ctx': nrm(ks[6], (D_MODEL,), 1.0),
        'w_ada': nrm(ks[7], (DEPTH, D_MODEL, 6 * D_MODEL), 0.5 * D_MODEL ** -0.5),
        'b_ada': nrm(ks[8], (DEPTH, 6 * D_MODEL), 0.02),
        'norm_mix': gain(ks[9], (DEPTH, D_MODEL)),
        'norm_ffn': gain(ks[10], (DEPTH, D_MODEL)),
        'w_in': nrm(ks[11], (DEPTH, D_MODEL, N_IN), D_MODEL ** -0.5),
        'conv_a': nrm(ks[12], (DEPTH, CONV_K, QKV_A), CONV_K ** -0.5),
        'a_log': jnp.log(jax.random.uniform(ks[13], (DEPTH, 2, H_A), F32, 1.0, 16.0)),
        'dt_bias': dt + jnp.log(-jnp.expm1(-dt)),
        'onorm_a': gain(ks[15], (DEPTH, DV_A)),
        'lam': nrm(ks[16], (DEPTH, 4, DQK_B), 0.1),
        'subln_b': gain(ks[17], (DEPTH, DV_B)),
        'w_up_a': nrm(ks[18], (DEPTH, H_A * DV_A, D_MODEL), (H_A * DV_A) ** -0.5),
        'w_up_b': nrm(ks[19], (DEPTH, H_B * DV_B, D_MODEL), (H_B * DV_B) ** -0.5),
        'w_o': nrm(ks[20], (DEPTH, D_MODEL, D_MODEL), D_MODEL ** -0.5),
        'w_rg': nrm(ks[21], (DEPTH, D_MODEL, N_GROUPS), D_MODEL ** -0.5),
        'b_rg': nrm(ks[22], (DEPTH, N_GROUPS), 0.01),
        'w_re': nrm(ks[23], (DEPTH, D_MODEL, N_EXPERTS), D_MODEL ** -0.5),
        'b_re': nrm(ks[24], (DEPTH, N_EXPERTS), 0.01),
        'w_e_gate': nrm(ks[25], (DEPTH, N_EXPERTS, D_MODEL, D_FF_E), D_MODEL ** -0.5),
        'w_e_up': nrm(ks[26], (DEPTH, N_EXPERTS, D_MODEL, D_FF_E), D_MODEL ** -0.5),
        'w_e_down': nrm(ks[27], (DEPTH, N_EXPERTS, D_FF_E, D_MODEL), D_FF_E ** -0.5),
        'norm_final': gain(ks[28], (D_MODEL,)),
    }


def reference(x_prompt, x_sample, c, state_delta, cache_k, cache_v, c_ctx,
              w_ada, b_ada, norm_mix, norm_ffn, w_in, conv_a, a_log, dt_bias, onorm_a,
              lam, subln_b, w_up_a, w_up_b, w_o, w_rg, b_rg, w_re, b_re,
              w_e_gate, w_e_up, w_e_down, norm_final):
    xp, xs = x_prompt, x_sample
    st_new, k_new, v_new = [], [], []
    for l in range(DEPTH):
        lam_init = 0.8 - 0.6 * math.exp(-0.3 * l)
        lp = {'w_in': w_in[l], 'conv_a': conv_a[l], 'a_log': a_log[l], 'dt_bias': dt_bias[l],
              'onorm_a': onorm_a[l], 'lam': lam[l], 'subln_b': subln_b[l],
              'w_up_a': w_up_a[l], 'w_up_b': w_up_b[l], 'w_o': w_o[l]}
        mp = (w_rg[l], b_rg[l], w_re[l], b_re[l], w_e_gate[l], w_e_up[l], w_e_down[l])
        sh1, sc1, gt1, sh2, sc2, gt2 = _adaln(c_ctx[None, :], w_ada[l], b_ada[l])
        mix, (s_ctx, k_ctx, v_ctx) = _mixer(_modulate(_rmsnorm(xp, norm_mix[l]), sh1, sc1), lp, lam_init, None)
        xp = xp + gt1 * mix
        xp = xp + gt2 * _hier_moe(_modulate(_rmsnorm(xp, norm_ffn[l]), sh2, sc2), *mp)
        st_new.append(s_ctx)
        k_new.append(k_ctx)
        v_new.append(v_ctx)
        sh1, sc1, gt1, sh2, sc2, gt2 = _adaln(c, w_ada[l], b_ada[l])
        mix, _ = _mixer(_modulate(_rmsnorm(xs, norm_mix[l]), sh1, sc1), lp, lam_init,
                        (state_delta[:, l], cache_k[:, l], cache_v[:, l]))
        xs = xs + gt1 * mix
        xs = xs + gt2 * _hier_moe(_modulate(_rmsnorm(xs, norm_ffn[l]), sh2, sc2), *mp)
    y_prompt = _rmsnorm(xp, norm_final)
    y_sample = _rmsnorm(xs, norm_final)
    new_state_delta = jnp.stack(st_new, axis=1)
    new_cache_k = jnp.stack(k_new, axis=1)
    new_cache_v = jnp.stack(v_new, axis=1)
    return (y_prompt, y_sample, new_state_delta, new_cache_k, new_cache_v)
```

```python
import functools
import math

import jax
import jax.numpy as jnp
from jax import lax
from jax.experimental import pallas as pl
from jax.experimental.pallas import tpu as pltpu

F32 = jnp.float32
BF16 = jnp.bfloat16

D_MODEL = 2048
GRID_W = 64
H_A = 8
DK_A = 128
DV_A = 128
CONV_K = 3
CHUNK = 64
H_B = 8
DQK_B = 64
DV_B = 2 * DQK_B
ROPE_BASE = 10000.0
N_GROUPS = 4
E_PER_GROUP = 8
N_EXPERTS = N_GROUPS * E_PER_GROUP
TOP_K = 2
D_FF_E = D_MODEL // 4
EPS = 1e-6

LANES = 128
QKV_A = 2 * H_A * DK_A + H_A * DV_A
N_SMALL = 4 * H_A
Z_GATE_A = QKV_A
Z_QB = Z_GATE_A + H_A * DV_A
Z_KB = Z_QB + H_B * 2 * DQK_B
Z_VB = Z_KB + H_B * 2 * DQK_B
Z_GM = Z_VB + H_B * DV_B
N_MAIN = Z_GM + 2 * D_MODEL

N_MOD_ROWS = 8
MOE_TILE = 256
VMEM_LIMIT = 56 * 1024 * 1024

NT_DIMS = (((1,), (1,)), ((), ()))


def _params(*sem):
    return pltpu.CompilerParams(dimension_semantics=sem, vmem_limit_bytes=VMEM_LIMIT)


def _mm(a, b):
    return jnp.dot(a.astype(BF16), b.astype(BF16), preferred_element_type=F32)


def _mm_nt(a, b):
    return lax.dot_general(a.astype(BF16), b.astype(BF16), NT_DIMS, preferred_element_type=F32)


def _silu(x):
    return x * jax.nn.sigmoid(x)


def _rms_scale(x):
    return x * lax.rsqrt(jnp.mean(x * x, axis=-1, keepdims=True) + EPS)


def _adaln_body(c_ref, w_ref, b_ref, o_ref):
    s = _silu(c_ref[...])
    o_ref[...] = _mm(s, w_ref[...]) + b_ref[...]


def _adaln(cvec, w, b):
    d, n = w.shape
    tn = 1024
    return pl.pallas_call(
        _adaln_body,
        grid=(n // tn,),
        in_specs=[pl.BlockSpec((N_MOD_ROWS, d), lambda j: (0, 0)),
                  pl.BlockSpec((d, tn), lambda j: (0, j)),
                  pl.BlockSpec((1, tn), lambda j: (0, j))],
        out_specs=pl.BlockSpec((N_MOD_ROWS, tn), lambda j: (0, j)),
        out_shape=jax.ShapeDtypeStruct((N_MOD_ROWS, n), F32),
        compiler_params=_params("arbitrary"),
        name="adaln",
    )(cvec, w, b)


class _Rows:
    def __init__(self, n_ctx, n_lat, t_lat, tm):
        assert n_ctx % tm == 0 and n_lat % tm == 0 and t_lat % tm == 0
        self.tm = tm
        self.nct = n_ctx // tm
        self.nlt = n_lat // tm
        self.per_seq = t_lat // tm
        self.n = self.nct + self.nlt

    def ctx_idx(self, i):
        return jnp.minimum(i, self.nct - 1)

    def lat_idx(self, i):
        return jnp.maximum(i - self.nct, 0)

    def mod_idx(self, i):
        return jnp.where(i < self.nct, 0, 1 + (i - self.nct) // self.per_seq)


def _inproj_body(nct, xp_ref, xs_ref, g_ref, sh_ref, sc_ref, w_ref, ws_ref, z_ref, zs_ref, h_scr):
    i = pl.program_id(0)
    j = pl.program_id(1)

    def prologue(x_ref):
        h = _rms_scale(x_ref[...]) * g_ref[...]
        h = (h * (1.0 + sc_ref[0]) + sh_ref[0]).astype(BF16)
        h_scr[...] = h
        zs_ref[...] = jnp.dot(h, ws_ref[...], preferred_element_type=F32)

    @pl.when(jnp.logical_and(j == 0, i < nct))
    def _():
        prologue(xp_ref)

    @pl.when(jnp.logical_and(j == 0, i >= nct))
    def _():
        prologue(xs_ref)

    z_ref[...] = jnp.dot(h_scr[...], w_ref[...], preferred_element_type=F32)


def _inproj(rows, xp, xs, gain, sh, sc, w_main, w_small):
    d = xp.shape[1]
    tm, tn = rows.tm, 1024
    n_tok = rows.n * tm
    return pl.pallas_call(
        functools.partial(_inproj_body, rows.nct),
        grid=(rows.n, N_MAIN // tn),
        in_specs=[pl.BlockSpec((tm, d), lambda i, j: (rows.ctx_idx(i), 0)),
                  pl.BlockSpec((tm, d), lambda i, j: (rows.lat_idx(i), 0)),
                  pl.BlockSpec((1, d), lambda i, j: (0, 0)),
                  pl.BlockSpec((1, 1, d), lambda i, j: (rows.mod_idx(i), 0, 0)),
                  pl.BlockSpec((1, 1, d), lambda i, j: (rows.mod_idx(i), 0, 0)),
                  pl.BlockSpec((d, tn), lambda i, j: (0, j)),
                  pl.BlockSpec((d, LANES), lambda i, j: (0, 0))],
        out_specs=[pl.BlockSpec((tm, tn), lambda i, j: (i, j)),
                   pl.BlockSpec((tm, LANES), lambda i, j: (i, 0))],
        out_shape=[jax.ShapeDtypeStruct((n_tok, N_MAIN), F32),
                   jax.ShapeDtypeStruct((n_tok, LANES), F32)],
        scratch_shapes=[pltpu.VMEM((tm, d), BF16)],
        compiler_params=_params("arbitrary", "arbitrary"),
        name="inproj",
    )(xp, xs, gain, sh, sc, w_main, w_small)


GATE_KINDS = 4
INV_BLOCK = 16


def _delta_chunk(q, k, v, gates, s, reverse, ii, jj):
    base = GATE_KINDS if reverse else 0
    gc = gates[:, base + 0:base + 1]
    beta = gates[:, base + 1:base + 2]
    egc = gates[:, base + 2:base + 3]
    ekd = gates[:, base + 3:base + 4]
    last = 0 if reverse else CHUNK - 1
    eg_tot = egc[last:last + 1, :]
    incl = (jj >= ii) if reverse else (jj <= ii)
    strict = (jj > ii) if reverse else (jj < ii)
    eye = ii == jj
    gc_row = jnp.sum(jnp.where(eye, gc, 0.0), axis=0, keepdims=True)
    dec = jnp.where(incl, jnp.exp(jnp.where(incl, gc - gc_row, 0.0)), 0.0)
    kb = k * beta
    a = jnp.where(strict, _mm_nt(kb, k) * dec, 0.0)
    a_qk = _mm_nt(q, k) * dec
    blk = (ii // INV_BLOCK) == (jj // INV_BLOCK)
    blk2 = (ii // (2 * INV_BLOCK)) == (jj // (2 * INV_BLOCK))
    a0 = jnp.where(blk, a, 0.0)
    off = jnp.where(blk, 0.0, a)
    d0 = jnp.where(eye, 1.0, 0.0) - a0
    p = _mm(a0, a0)
    for _ in range(int(math.log2(INV_BLOCK)) - 2):
        r = _mm(jnp.concatenate([p, d0], axis=0), p)
        p = r[:CHUNK]
        d0 = d0 + r[CHUNK:]
    d0 = d0 + _mm(d0, p)
    rhs = jnp.concatenate([v * beta, kb * egc], axis=1)
    wl = _mm(d0, off)
    x1 = jnp.where(blk2, wl, 0.0)
    yl = jnp.where(blk2, 0.0, wl)
    yr = _mm(d0, rhs)
    zl = yl - _mm(x1, yl)
    zr = yr - _mm(x1, yr)
    sol = zr - _mm(zl, zr)
    u = sol[:, :DV_A]
    w = sol[:, DV_A:]
    v_new = u - _mm(w, s)
    o = _mm(q * egc, s) + _mm(a_qk, v_new)
    s_new = s * eg_tot + _mm((k * ekd).T, v_new)
    return o, s_new


def _delta_body(t_len, has_s0, q_ref, k_ref, v_ref, cq_ref, ck_ref, cv_ref, g_ref, *rest):
    if has_s0:
        s0_ref, o_ref, qs, ks, vs, s_scr = rest
    else:
        o_ref, sfin_ref, qs, ks, vs, s_scr = rest
    tpos = lax.broadcasted_iota(jnp.int32, (t_len, 1), 0)

    def conv_act(x_ref, w_ref):
        x = x_ref[...]
        w = w_ref[...]
        x_prev = jnp.where(tpos == 0, 0.0, pltpu.roll(x, 1, 0))
        x_next = jnp.where(tpos == t_len - 1, 0.0, pltpu.roll(x, t_len - 1, 0))
        return _silu(x_prev * w[0:1] + x * w[1:2] + x_next * w[2:3])

    def l2n(x):
        return x * lax.rsqrt(jnp.sum(x * x, axis=-1, keepdims=True) + EPS)

    qs[...] = l2n(conv_act(q_ref, cq_ref)) * (DK_A ** -0.5)
    ks[...] = l2n(conv_act(k_ref, ck_ref))
    vs[...] = conv_act(v_ref, cv_ref)
    if has_s0:
        s_scr[...] = s0_ref[...]
    else:
        s_scr[...] = jnp.zeros_like(s_scr)
    o_ref[...] = jnp.zeros_like(o_ref)
    n = t_len // CHUNK
    ii = lax.broadcasted_iota(jnp.int32, (CHUNK, CHUNK), 0)
    jj = lax.broadcasted_iota(jnp.int32, (CHUNK, CHUNK), 1)

    def step(c, carry):
        for d in (0, 1):
            cc = c if d == 0 else n - 1 - c
            r = pl.ds(pl.multiple_of(cc * CHUNK, CHUNK), CHUNK)
            o, s_new = _delta_chunk(qs[r, :], ks[r, :], vs[r, :], g_ref[r, :], s_scr[d], d == 1, ii, jj)
            s_scr[d] = s_new
            o_ref[r, :] += o
        return carry

    lax.fori_loop(0, n, step, 0)
    if not has_s0:
        sfin_ref[...] = s_scr[...]


def _deltanet(z, conv_w, gates, s0, n_seq, t_len, row0):
    rb = row0 // t_len
    seq_blk = lambda col0: pl.BlockSpec((t_len, LANES), lambda b, h: (rb + b, col0 + h))
    cw_blk = lambda col0: pl.BlockSpec((CONV_K, LANES), lambda b, h: (0, col0 + h))
    state_blk = pl.BlockSpec((None, None, 2, None, DK_A, DV_A), lambda b, h: (b, 0, 0, h, 0, 0))
    in_specs = [seq_blk(0), seq_blk(H_A), seq_blk(2 * H_A), cw_blk(0), cw_blk(H_A), cw_blk(2 * H_A),
                pl.BlockSpec((t_len, LANES), lambda b, h: (rb + b, h))]
    args = [z, z, z, conv_w, conv_w, conv_w, gates]
    o_spec = pl.BlockSpec((t_len, LANES), lambda b, h: (b, h))
    o_shape = jax.ShapeDtypeStruct((n_seq * t_len, H_A * DV_A), F32)
    has_s0 = s0 is not None
    if has_s0:
        in_specs += [state_blk]
        args += [s0]
        out_specs, out_shape = o_spec, o_shape
    else:
        out_specs = [o_spec, state_blk]
        out_shape = [o_shape, jax.ShapeDtypeStruct((n_seq, 1, 2, H_A, DK_A, DV_A), F32)]

    return pl.pallas_call(
        functools.partial(_delta_body, t_len, has_s0),
        grid=(n_seq, H_A),
        in_specs=in_specs,
        out_specs=out_specs,
        out_shape=out_shape,
        scratch_shapes=[pltpu.VMEM((t_len, LANES), F32)] * 3 + [pltpu.VMEM((2, DK_A, DV_A), F32)],
        compiler_params=_params("arbitrary", "arbitrary"),
        name="deltanet_lat" if has_s0 else "deltanet_ctx",
    )(*args)


def _diff_attn(q, keys, vals, lam):
    qb = (q * (DQK_B ** -0.5)).astype(BF16)

    def probs(lo):
        s = lax.dot_general(qb[:, lo:lo + DQK_B], keys[:, lo:lo + DQK_B], NT_DIMS, preferred_element_type=F32)
        e = jnp.exp(s - jnp.max(s, axis=-1, keepdims=True))
        return e / jnp.sum(e, axis=-1, keepdims=True)

    a = probs(0) - lam * probs(DQK_B)
    return jnp.dot(a.astype(BF16), vals, preferred_element_type=F32)


def _subln(o, sub_ref, lam_init):
    return _rms_scale(o) * sub_ref[...] * (1.0 - lam_init)


def _attn_ctx_body(lam_init, q_ref, k_ref, v_ref, lam_ref, sub_ref, o_ref, ck_ref, cv_ref):
    k = k_ref[...]
    v = v_ref[...]
    ck_ref[...] = k
    cv_ref[...] = v
    o = _diff_attn(q_ref[...], k.astype(BF16), v.astype(BF16), lam_ref[0:1, 0:1])
    o_ref[...] = _subln(o, sub_ref, lam_init)


def _attn_ctx(z, lam, subln, n_seq, t_len, lam_init):
    n_tok = n_seq * t_len
    blk = lambda col0: pl.BlockSpec((t_len, LANES), lambda b, h: (b, col0 // LANES + h))
    cache_blk = pl.BlockSpec((None, None, None, t_len, LANES), lambda b, h: (b, 0, h, 0, 0))
    cache_shape = jax.ShapeDtypeStruct((n_seq, 1, H_B, t_len, LANES), F32)
    return pl.pallas_call(
        functools.partial(_attn_ctx_body, lam_init),
        grid=(n_seq, H_B),
        in_specs=[blk(Z_QB), blk(Z_KB), blk(Z_VB),
                  pl.BlockSpec((1, LANES), lambda b, h: (0, 0)),
                  pl.BlockSpec((1, LANES), lambda b, h: (0, 0))],
        out_specs=[pl.BlockSpec((t_len, LANES), lambda b, h: (b, h)), cache_blk, cache_blk],
        out_shape=[jax.ShapeDtypeStruct((n_tok, H_B * DV_B), F32), cache_shape, cache_shape],
        compiler_params=_params("arbitrary", "arbitrary"),
        name="attn_ctx",
    )(z, z, z, lam, subln)


def _rope(x, cos, sin_signed):
    lane = lax.broadcasted_iota(jnp.int32, (1, LANES), 1)
    first = (lane % 32) < 16
    partner = jnp.where(first, pltpu.roll(x, LANES - 16, 1), pltpu.roll(x, 16, 1))
    return x * cos + partner * sin_signed


def _attn_lat_body(lam_init, n_past, q_ref, k_ref, v_ref, pk_ref, pv_ref, cosq_ref, sinq_ref, cos_ref, sin_ref,
                   lam_ref, sub_ref, o_ref, keys, vals):
    @pl.when(pl.program_id(2) == 0)
    def _():
        keys[0:n_past, :] = pk_ref[...].astype(BF16)
        vals[0:n_past, :] = pv_ref[...].astype(BF16)
        keys[n_past:, :] = _rope(k_ref[...], cos_ref[...], sin_ref[...]).astype(BF16)
        vals[n_past:, :] = v_ref[...].astype(BF16)

    q = _rope(q_ref[...], cosq_ref[...], sinq_ref[...])
    o = _diff_attn(q, keys[...], vals[...], lam_ref[0:1, 0:1])
    o_ref[...] = _subln(o, sub_ref, lam_init)


def _attn_lat(z, cache_k, cache_v, cos, sin, lam, subln, n_seq, t_len, row0, lam_init):
    tq = 256
    nq = t_len // tq
    n_past = cache_k.shape[3]
    rbq = row0 // tq
    rbs = row0 // t_len
    seq_blk = lambda col0: pl.BlockSpec((t_len, LANES), lambda b, h, qi: (rbs + b, col0 // LANES + h))
    past_blk = pl.BlockSpec((None, None, None, n_past, LANES), lambda b, h, qi: (b, 0, h, 0, 0))
    row_vec = pl.BlockSpec((1, LANES), lambda b, h, qi: (0, 0))
    return pl.pallas_call(
        functools.partial(_attn_lat_body, lam_init, n_past),
        grid=(n_seq, H_B, nq),
        in_specs=[pl.BlockSpec((tq, LANES), lambda b, h, qi: (rbq + b * nq + qi, Z_QB // LANES + h)),
                  seq_blk(Z_KB), seq_blk(Z_VB), past_blk, past_blk,
                  pl.BlockSpec((tq, LANES), lambda b, h, qi: (qi, 0)),
                  pl.BlockSpec((tq, LANES), lambda b, h, qi: (qi, 0)),
                  pl.BlockSpec((t_len, LANES), lambda b, h, qi: (0, 0)),
                  pl.BlockSpec((t_len, LANES), lambda b, h, qi: (0, 0)),
                  row_vec, row_vec],
        out_specs=pl.BlockSpec((tq, LANES), lambda b, h, qi: (b * nq + qi, h)),
        out_shape=jax.ShapeDtypeStruct((n_seq * t_len, H_B * DV_B), F32),
        scratch_shapes=[pltpu.VMEM((n_past + t_len, LANES), BF16)] * 2,
        compiler_params=_params("arbitrary", "arbitrary", "arbitrary"),
        name="attn_lat",
    )(z, z, z, cache_k, cache_v, cos, sin, cos, sin, lam, subln)


def _rope_tables(t_len):
    t = jnp.arange(t_len)
    pos = jnp.stack([t // GRID_W, t % GRID_W], axis=1).astype(F32)
    nf = DQK_B // 4
    inv_freq = ROPE_BASE ** (-jnp.arange(nf, dtype=F32) / nf)
    lane = jnp.arange(LANES)
    half = (lane % DQK_B) // (DQK_B // 2)
    ang = pos[:, half] * inv_freq[lane % nf][None, :]
    sign = jnp.where((lane % (DQK_B // 2)) < nf, -1.0, 1.0).astype(F32)
    return jnp.cos(ang), jnp.sin(ang) * sign[None, :]


def _merge_body(nct, oac_ref, oal_ref, ga_ref, on_ref, obc_ref, obl_ref, wa_ref, wb_ref, gma_ref, gmb_ref, m_ref,
                a_scr, b_scr):
    i = pl.program_id(0)
    first = pl.program_id(1) == 0

    def prologue(oa_ref, ob_ref):
        for h in range(H_A):
            c = slice(h * DV_A, (h + 1) * DV_A)
            a_scr[:, c] = (_rms_scale(oa_ref[:, c]) * on_ref[...] * _silu(ga_ref[:, c])).astype(BF16)
        b_scr[...] = ob_ref[...].astype(BF16)

    @pl.when(jnp.logical_and(first, i < nct))
    def _():
        prologue(oac_ref, obc_ref)

    @pl.when(jnp.logical_and(first, i >= nct))
    def _():
        prologue(oal_ref, obl_ref)

    ya = jnp.dot(a_scr[...], wa_ref[...], preferred_element_type=F32)
    yb = jnp.dot(b_scr[...], wb_ref[...], preferred_element_type=F32)
    m_ref[...] = (jax.nn.sigmoid(gma_ref[...]) * ya + jax.nn.sigmoid(gmb_ref[...]) * yb).astype(BF16)


def _merge(rows, z, oa_c, oa_l, ob_c, ob_l, onorm, w_up_a, w_up_b):
    n_tok = z.shape[0]
    tm = rows.tm
    d = w_up_a.shape[1]
    ka = w_up_a.shape[0]
    tn = 1024
    nj = d // tn
    ctx_blk = pl.BlockSpec((tm, ka), lambda i, j: (rows.ctx_idx(i), 0))
    lat_blk = pl.BlockSpec((tm, ka), lambda i, j: (rows.lat_idx(i), 0))
    return pl.pallas_call(
        functools.partial(_merge_body, rows.nct),
        grid=(n_tok // tm, nj),
        in_specs=[ctx_blk, lat_blk,
                  pl.BlockSpec((tm, ka), lambda i, j: (i, Z_GATE_A // ka)),
                  pl.BlockSpec((1, DV_A), lambda i, j: (0, 0)),
                  ctx_blk, lat_blk,
                  pl.BlockSpec((ka, tn), lambda i, j: (0, j)),
                  pl.BlockSpec((ka, tn), lambda i, j: (0, j)),
                  pl.BlockSpec((tm, tn), lambda i, j: (i, Z_GM // tn + j)),
                  pl.BlockSpec((tm, tn), lambda i, j: (i, Z_GM // tn + nj + j))],
        out_specs=pl.BlockSpec((tm, tn), lambda i, j: (i, j)),
        out_shape=jax.ShapeDtypeStruct((n_tok, d), BF16),
        scratch_shapes=[pltpu.VMEM((tm, ka), BF16)] * 2,
        compiler_params=_params("arbitrary", "arbitrary"),
        name="merge",
    )(oa_c, oa_l, z, onorm, ob_c, ob_l, w_up_a, w_up_b, z, z)


def _outproj_body(nct, m_ref, wo_ref, xp_ref, xs_ref, gt_ref, g2_ref, sh_ref, sc_ref, wr_ref, br_ref,
                  x1_ref, h2_ref, lg_ref):
    i = pl.program_id(0)
    y = jnp.dot(m_ref[...], wo_ref[...], preferred_element_type=F32)

    def finish(x_ref):
        x1 = x_ref[...] + gt_ref[0] * y
        x1_ref[...] = x1
        h2 = _rms_scale(x1) * g2_ref[...]
        h2 = h2 * (1.0 + sc_ref[0]) + sh_ref[0]
        h2_ref[...] = h2
        lg_ref[...] = jnp.dot(h2, wr_ref[...], preferred_element_type=F32,
                              precision=lax.Precision.HIGHEST) + br_ref[...]

    @pl.when(i < nct)
    def _():
        finish(xp_ref)

    @pl.when(i >= nct)
    def _():
        finish(xs_ref)


def _outproj(rows, mixed, w_o, xp, xs, gt1, gain2, sh2, sc2, w_r, b_r):
    d = xp.shape[1]
    tm = rows.tm
    n_tok = rows.n * tm
    mod = lambda: pl.BlockSpec((1, 1, d), lambda i: (rows.mod_idx(i), 0, 0))
    tok = pl.BlockSpec((tm, d), lambda i: (i, 0))
    return pl.pallas_call(
        functools.partial(_outproj_body, rows.nct),
        grid=(rows.n,),
        in_specs=[tok,
                  pl.BlockSpec((d, d), lambda i: (0, 0)),
                  pl.BlockSpec((tm, d), lambda i: (rows.ctx_idx(i), 0)),
                  pl.BlockSpec((tm, d), lambda i: (rows.lat_idx(i), 0)),
                  mod(),
                  pl.BlockSpec((1, d), lambda i: (0, 0)),
                  mod(), mod(),
                  pl.BlockSpec((d, LANES), lambda i: (0, 0)),
                  pl.BlockSpec((1, LANES), lambda i: (0, 0))],
        out_specs=[tok, tok, pl.BlockSpec((tm, LANES), lambda i: (i, 0))],
        out_shape=[jax.ShapeDtypeStruct((n_tok, d), F32), jax.ShapeDtypeStruct((n_tok, d), F32),
                   jax.ShapeDtypeStruct((n_tok, LANES), F32)],
        compiler_params=_params("arbitrary"),
        name="outproj",
    )(mixed, w_o, xp, xs, gt1, gain2, sh2, sc2, w_r, b_r)


def _moe_body(n_tiles, te_ref, tv_ref, tok_ref, h_hbm, cw_ref, wg_ref, wu_ref, wd_ref, y_ref,
              xbuf, wg_b, wu_b, wd_b, sem):
    i = pl.program_id(0)
    slot = i % 2

    def gather(tile, dst_slot, start):
        def row(r, carry):
            t = tok_ref[tile * MOE_TILE + r]
            cp = pltpu.make_async_copy(h_hbm.at[pl.ds(t, 1), :], xbuf.at[dst_slot, pl.ds(r, 1), :], sem.at[dst_slot])
            if start:
                cp.start()
            else:
                cp.wait()
            return carry

        lax.fori_loop(0, MOE_TILE, row, 0)

    @pl.when(jnp.logical_and(i == 0, tv_ref[0] == 1))
    def _():
        gather(0, 0, True)

    nxt = jnp.minimum(i + 1, n_tiles - 1)

    @pl.when(jnp.logical_and(i + 1 < n_tiles, tv_ref[nxt] == 1))
    def _():
        gather(nxt, 1 - slot, True)

    prev = jnp.maximum(i - 1, 0)

    @pl.when(jnp.logical_or(i == 0, te_ref[i] != te_ref[prev]))
    def _():
        wg_b[...] = wg_ref[...].astype(BF16)
        wu_b[...] = wu_ref[...].astype(BF16)
        wd_b[...] = wd_ref[...].astype(BF16)

    @pl.when(tv_ref[i] == 1)
    def _():
        gather(i, slot, False)
        x = xbuf[slot].astype(BF16)
        g = jnp.dot(x, wg_b[...], preferred_element_type=F32)
        u = jnp.dot(x, wu_b[...], preferred_element_type=F32)
        cw = cw_ref[...]
        act = _silu(g) * u * jnp.concatenate([cw] * (D_FF_E // LANES), axis=1)
        y_ref[...] = jnp.dot(act.astype(BF16), wd_b[...], preferred_element_type=F32)

    @pl.when(tv_ref[i] == 0)
    def _():
        y_ref[...] = jnp.zeros_like(y_ref)


def _moe(h2, tile_expert, tile_valid, tok_sorted, cw_sorted, w_g, w_u, w_d):
    n_tiles = tile_expert.shape[0]
    d = h2.shape[1]
    f = w_g.shape[-1]
    grid_spec = pltpu.PrefetchScalarGridSpec(
        num_scalar_prefetch=3,
        grid=(n_tiles,),
        in_specs=[pl.BlockSpec(memory_space=pl.ANY),
                  pl.BlockSpec((MOE_TILE, LANES), lambda i, te, tv, tk: (i, 0)),
                  pl.BlockSpec((None, d, f), lambda i, te, tv, tk: (te[i], 0, 0)),
                  pl.BlockSpec((None, d, f), lambda i, te, tv, tk: (te[i], 0, 0)),
                  pl.BlockSpec((None, f, d), lambda i, te, tv, tk: (te[i], 0, 0))],
        out_specs=pl.BlockSpec((MOE_TILE, d), lambda i, te, tv, tk: (i, 0)),
        scratch_shapes=[pltpu.VMEM((2, MOE_TILE, d), F32),
                        pltpu.VMEM((d, f), BF16), pltpu.VMEM((d, f), BF16), pltpu.VMEM((f, d), BF16),
                        pltpu.SemaphoreType.DMA((2,))],
    )
    return pl.pallas_call(
        functools.partial(_moe_body, n_tiles),
        grid_spec=grid_spec,
        out_shape=jax.ShapeDtypeStruct((n_tiles * MOE_TILE, d), F32),
        compiler_params=_params("arbitrary"),
        name="moe",
    )(tile_expert, tile_valid, tok_sorted, h2, cw_sorted, w_g, w_u, w_d)


def _route(logits, n_tiles):
    n_tok = logits.shape[0]
    p_grp = jax.nn.softmax(logits[:, :N_GROUPS], axis=-1)
    pg_top, g_idx = lax.top_k(p_grp, 1)
    le = logits[:, N_GROUPS:N_GROUPS + N_EXPERTS].reshape(n_tok, N_GROUPS, E_PER_GROUP)
    le_sel = jnp.take_along_axis(le, g_idx[:, :, None], axis=1)[:, 0, :]
    p_exp = jax.nn.softmax(le_sel, axis=-1)
    pe_top, e_idx = lax.top_k(p_exp, TOP_K)
    w_tok = pg_top * pe_top / jnp.sum(pe_top, axis=-1, keepdims=True)
    flat_e = (g_idx * E_PER_GROUP + e_idx).reshape(-1).astype(jnp.int32)
    flat_w = w_tok.reshape(-1)
    n_flat = flat_e.shape[0]
    flat_t = (jnp.arange(n_flat, dtype=jnp.int32) // TOP_K)
    order = jnp.argsort(flat_e, stable=True).astype(jnp.int32)
    counts = jnp.zeros((N_EXPERTS,), jnp.int32).at[flat_e].add(1)
    padded = ((counts + MOE_TILE - 1) // MOE_TILE) * MOE_TILE
    pad_end = jnp.cumsum(padded)
    pad_off = pad_end - padded
    off = jnp.cumsum(counts) - counts
    sorted_e = flat_e[order]
    pos_sorted = pad_off[sorted_e] + jnp.arange(n_flat, dtype=jnp.int32) - off[sorted_e]
    n_rows = n_tiles * MOE_TILE
    tok_sorted = jnp.zeros((n_rows,), jnp.int32).at[pos_sorted].set(flat_t[order])
    cw_sorted = jnp.zeros((n_rows,), F32).at[pos_sorted].set(flat_w[order])
    pos_of = jnp.zeros((n_flat,), jnp.int32).at[order].set(pos_sorted)
    tile_start = jnp.arange(n_tiles, dtype=jnp.int32) * MOE_TILE
    tile_expert = jnp.minimum(jnp.searchsorted(pad_end, tile_start, side="right"), N_EXPERTS - 1).astype(jnp.int32)
    tile_valid = (tile_start < pad_end[-1]).astype(jnp.int32)
    cw_rows = jnp.broadcast_to(cw_sorted[:, None], (n_rows, LANES))
    return tile_expert, tile_valid, tok_sorted, cw_rows, pos_of


def _combine_body(tm, row0, pos_ref, y_hbm, x1_ref, gt_ref, gf_ref, o_ref, ybuf, sem):
    i = pl.program_id(0)

    def copies(start):
        def row(r, carry):
            for k in range(TOP_K):
                p = pos_ref[(row0 + i * tm + r) * TOP_K + k]
                cp = pltpu.make_async_copy(y_hbm.at[pl.ds(p, 1), :], ybuf.at[k, pl.ds(r, 1), :], sem.at[0])
                if start:
                    cp.start()
                else:
                    cp.wait()
            return carry

        lax.fori_loop(0, tm, row, 0)

    copies(True)
    copies(False)
    y = ybuf[0] + ybuf[1]
    x2 = x1_ref[...] + gt_ref[0] * y
    o_ref[...] = _rms_scale(x2) * gf_ref[...]


def _combine(y_sorted, pos_of, x1, gt2, gain_f, row0, n_rows, mod_of_tile, tm):
    d = x1.shape[1]
    rb = row0 // tm
    grid_spec = pltpu.PrefetchScalarGridSpec(
        num_scalar_prefetch=1,
        grid=(n_rows // tm,),
        in_specs=[pl.BlockSpec(memory_space=pl.ANY),
                  pl.BlockSpec((tm, d), lambda i, pos: (rb + i, 0)),
                  pl.BlockSpec((1, 1, d), lambda i, pos: (mod_of_tile(i), 0, 0)),
                  pl.BlockSpec((1, d), lambda i, pos: (0, 0))],
        out_specs=pl.BlockSpec((tm, d), lambda i, pos: (i, 0)),
        scratch_shapes=[pltpu.VMEM((TOP_K, tm, d), F32), pltpu.SemaphoreType.DMA((1,))],
    )
    return pl.pallas_call(
        functools.partial(_combine_body, tm, row0),
        grid_spec=grid_spec,
        out_shape=jax.ShapeDtypeStruct((n_rows, d), F32),
        compiler_params=_params("arbitrary"),
        name="combine",
    )(pos_of, y_sorted, x1, gt2, gain_f)


def _gate_table(zs, a_log, dt_bias):
    n_tok = zs.shape[0]
    beta = jax.nn.sigmoid(zs[:, :2 * H_A])
    g = -jnp.exp(a_log.reshape(1, 2 * H_A)) * jax.nn.softplus(zs[:, 2 * H_A:4 * H_A] + dt_bias.reshape(1, 2 * H_A))
    gch = g.reshape(n_tok // CHUNK, CHUNK, 2, H_A)
    fwd = jnp.cumsum(gch[:, :, 0], axis=1)
    bwd = jnp.flip(jnp.cumsum(jnp.flip(gch[:, :, 1], axis=1), axis=1), axis=1)
    gc = jnp.stack([fwd, bwd], axis=2)
    tot = jnp.sum(gch, axis=1, keepdims=True)
    kinds = jnp.stack([gc, beta.reshape(gc.shape), jnp.exp(gc), jnp.exp(tot - gc)], axis=-1)
    tbl = kinds.transpose(0, 1, 3, 2, 4).reshape(n_tok, H_A, 2 * GATE_KINDS)
    tbl = jnp.pad(tbl, ((0, 0), (0, 0), (0, LANES - 2 * GATE_KINDS)))
    return tbl.reshape(n_tok, H_A * LANES)


def kernel(x_prompt, x_sample, c, state_delta, cache_k, cache_v, c_ctx, w_ada, b_ada, norm_mix, norm_ffn, w_in,
           conv_a, a_log, dt_bias, onorm_a, lam, subln_b, w_up_a, w_up_b, w_o, w_rg, b_rg, w_re, b_re,
           w_e_gate, w_e_up, w_e_down, norm_final):
    assert w_in.shape[0] == 1, "single trunk layer"
    l = 0
    lam_init = 0.8 - 0.6 * math.exp(-0.3 * l)
    bc, tc, d = x_prompt.shape
    bl, tl, _ = x_sample.shape
    n_ctx, n_lat = bc * tc, bl * tl
    xp = x_prompt.reshape(n_ctx, d)
    xs = x_sample.reshape(n_lat, d)

    cvec = jnp.zeros((N_MOD_ROWS, d), F32).at[0].set(c_ctx).at[1:1 + bl].set(c)
    mod = _adaln(cvec, w_ada[l], b_ada[l][None, :])
    sh1, sc1, gt1, sh2, sc2, gt2 = [m.reshape(N_MOD_ROWS, 1, d) for m in jnp.split(mod, 6, axis=1)]

    w_l = w_in[l]
    small0 = QKV_A + H_A * DV_A
    w_main = jnp.concatenate([w_l[:, :small0], w_l[:, small0 + N_SMALL:]], axis=1).astype(BF16)
    w_small = jnp.pad(w_l[:, small0:small0 + N_SMALL], ((0, 0), (0, LANES - N_SMALL))).astype(BF16)
    rows = _Rows(n_ctx, n_lat, tl, 512)
    z, zs = _inproj(rows, xp, xs, norm_mix[l][None, :], sh1, sc1, w_main, w_small)

    gates = _gate_table(zs, a_log[l], dt_bias[l])
    oa_c, new_state = _deltanet(z, conv_a[l], gates, None, bc, tc, 0)
    oa_l = _deltanet(z, conv_a[l], gates, state_delta, bl, tl, n_ctx)

    lv = lam[l]
    lam_val = jnp.exp(jnp.sum(lv[0] * lv[1])) - jnp.exp(jnp.sum(lv[2] * lv[3])) + lam_init
    lam_row = jnp.full((1, LANES), lam_val, F32)
    sub_row = subln_b[l][None, :]
    ob_c, new_k, new_v = _attn_ctx(z, lam_row, sub_row, bc, tc, lam_init)
    cos, sin = _rope_tables(tl)
    ob_l = _attn_lat(z, cache_k, cache_v, cos, sin, lam_row, sub_row, bl, tl, n_ctx, lam_init)

    mixed = _merge(rows, z, oa_c, oa_l, ob_c, ob_l, onorm_a[l][None, :],
                   w_up_a[l].astype(BF16), w_up_b[l].astype(BF16))
    w_r = jnp.pad(jnp.concatenate([w_rg[l], w_re[l]], axis=1), ((0, 0), (0, LANES - N_GROUPS - N_EXPERTS)))
    b_r = jnp.pad(jnp.concatenate([b_rg[l], b_re[l]]), (0, LANES - N_GROUPS - N_EXPERTS))[None, :]
    x1, h2, logits = _outproj(rows, mixed, w_o[l].astype(BF16), xp, xs, gt1, norm_ffn[l][None, :], sh2, sc2, w_r, b_r)

    n_tok = n_ctx + n_lat
    n_tiles = (n_tok * TOP_K) // MOE_TILE + N_EXPERTS
    tile_expert, tile_valid, tok_sorted, cw_rows, pos_of = _route(logits, n_tiles)
    y_sorted = _moe(h2, tile_expert, tile_valid, tok_sorted, cw_rows, w_e_gate[l], w_e_up[l], w_e_down[l])
    gf = norm_final[None, :]
    tm_c = 256
    y_prompt = _combine(y_sorted, pos_of, x1, gt2, gf, 0, n_ctx, lambda i: 0 * i, tm_c)
    y_sample = _combine(y_sorted, pos_of, x1, gt2, gf, n_ctx, n_lat, lambda i: 1 + i // (tl // tm_c), tm_c)
    return (y_prompt.reshape(bc, tc, d), y_sample.reshape(bl, tl, d), new_state, new_k, new_v)
```

```python
import functools
import math

import jax
import jax.numpy as jnp
from jax import lax
from jax.experimental import pallas as pl
from jax.experimental.pallas import tpu as pltpu

F32 = jnp.float32
BF16 = jnp.bfloat16

D_MODEL = 2048
GRID_W = 64
H_A = 8
DK_A = 128
DV_A = 128
CONV_K = 3
CHUNK = 64
H_B = 8
DQK_B = 64
DV_B = 2 * DQK_B
ROPE_BASE = 10000.0
N_GROUPS = 4
E_PER_GROUP = 8
N_EXPERTS = N_GROUPS * E_PER_GROUP
TOP_K = 2
D_FF_E = D_MODEL // 4
EPS = 1e-6

LANES = 128
QKV_A = 2 * H_A * DK_A + H_A * DV_A
N_SMALL = 4 * H_A
Z_GATE_A = QKV_A
Z_QB = Z_GATE_A + H_A * DV_A
Z_KB = Z_QB + H_B * 2 * DQK_B
Z_VB = Z_KB + H_B * 2 * DQK_B
Z_GM = Z_VB + H_B * DV_B
N_MAIN = Z_GM + 2 * D_MODEL

N_MOD_ROWS = 8
MOE_TILE = 256
VMEM_LIMIT = 56 * 1024 * 1024

NT_DIMS = (((1,), (1,)), ((), ()))


def _params(*sem):
    return pltpu.CompilerParams(dimension_semantics=sem, vmem_limit_bytes=VMEM_LIMIT)


def _mm(a, b):
    return jnp.dot(a.astype(BF16), b.astype(BF16), preferred_element_type=F32)


def _mm_nt(a, b):
    return lax.dot_general(a.astype(BF16), b.astype(BF16), NT_DIMS, preferred_element_type=F32)


def _silu(x):
    return x * jax.nn.sigmoid(x)


def _rms_scale(x):
    return x * lax.rsqrt(jnp.mean(x * x, axis=-1, keepdims=True) + EPS)


def _adaln_body(c_ref, w_ref, b_ref, o_ref):
    s = _silu(c_ref[...])
    o_ref[...] = _mm(s, w_ref[...]) + b_ref[...]


def _adaln(cvec, w, b):
    d, n = w.shape
    tn = 1024
    return pl.pallas_call(
        _adaln_body,
        grid=(n // tn,),
        in_specs=[pl.BlockSpec((N_MOD_ROWS, d), lambda j: (0, 0)),
                  pl.BlockSpec((d, tn), lambda j: (0, j)),
                  pl.BlockSpec((1, tn), lambda j: (0, j))],
        out_specs=pl.BlockSpec((N_MOD_ROWS, tn), lambda j: (0, j)),
        out_shape=jax.ShapeDtypeStruct((N_MOD_ROWS, n), F32),
        compiler_params=_params("arbitrary"),
        name="adaln",
    )(cvec, w, b)


class _Rows:
    def __init__(self, n_ctx, n_lat, t_lat, tm):
        assert n_ctx % tm == 0 and n_lat % tm == 0 and t_lat % tm == 0
        self.tm = tm
        self.nct = n_ctx // tm
        self.nlt = n_lat // tm
        self.per_seq = t_lat // tm
        self.n = self.nct + self.nlt

    def ctx_idx(self, i):
        return jnp.minimum(i, self.nct - 1)

    def lat_idx(self, i):
        return jnp.maximum(i - self.nct, 0)

    def mod_idx(self, i):
        return jnp.where(i < self.nct, 0, 1 + (i - self.nct) // self.per_seq)


def _inproj_body(nct, xp_ref, xs_ref, g_ref, sh_ref, sc_ref, w_ref, ws_ref, z_ref, zs_ref, h_scr):
    i = pl.program_id(0)
    j = pl.program_id(1)

    def prologue(x_ref):
        h = _rms_scale(x_ref[...]) * g_ref[...]
        h = (h * (1.0 + sc_ref[0]) + sh_ref[0]).astype(BF16)
        h_scr[...] = h
        zs_ref[...] = jnp.dot(h, ws_ref[...], preferred_element_type=F32)

    @pl.when(jnp.logical_and(j == 0, i < nct))
    def _():
        prologue(xp_ref)

    @pl.when(jnp.logical_and(j == 0, i >= nct))
    def _():
        prologue(xs_ref)

    z_ref[...] = jnp.dot(h_scr[...], w_ref[...], preferred_element_type=F32)


def _inproj(rows, xp, xs, gain, sh, sc, w_main, w_small):
    d = xp.shape[1]
    tm, tn = rows.tm, 1024
    n_tok = rows.n * tm
    return pl.pallas_call(
        functools.partial(_inproj_body, rows.nct),
        grid=(rows.n, N_MAIN // tn),
        in_specs=[pl.BlockSpec((tm, d), lambda i, j: (rows.ctx_idx(i), 0)),
                  pl.BlockSpec((tm, d), lambda i, j: (rows.lat_idx(i), 0)),
                  pl.BlockSpec((1, d), lambda i, j: (0, 0)),
                  pl.BlockSpec((1, 1, d), lambda i, j: (rows.mod_idx(i), 0, 0)),
                  pl.BlockSpec((1, 1, d), lambda i, j: (rows.mod_idx(i), 0, 0)),
                  pl.BlockSpec((d, tn), lambda i, j: (0, j)),
                  pl.BlockSpec((d, LANES), lambda i, j: (0, 0))],
        out_specs=[pl.BlockSpec((tm, tn), lambda i, j: (i, j)),
                   pl.BlockSpec((tm, LANES), lambda i, j: (i, 0))],
        out_shape=[jax.ShapeDtypeStruct((n_tok, N_MAIN), F32),
                   jax.ShapeDtypeStruct((n_tok, LANES), F32)],
        scratch_shapes=[pltpu.VMEM((tm, d), BF16)],
        compiler_params=_params("arbitrary", "arbitrary"),
        name="inproj",
    )(xp, xs, gain, sh, sc, w_main, w_small)


GATE_KINDS = 4
INV_BLOCK = 16


def _chunk_solve(chains, ii, jj):
    eye = ii == jj
    blk = (ii // INV_BLOCK) == (jj // INV_BLOCK)
    blk2 = (ii // (2 * INV_BLOCK)) == (jj // (2 * INV_BLOCK))
    cols, a_qk, a0, off, d0, rhs = [], [], [], [], [], []
    for kk, qk, v, k, gates, reverse in chains:
        base = GATE_KINDS if reverse else 0
        gc, beta, egc, ekd = (gates[:, base + i:base + i + 1] for i in range(GATE_KINDS))
        incl = (jj >= ii) if reverse else (jj <= ii)
        strict = (jj > ii) if reverse else (jj < ii)
        gc_row = jnp.sum(jnp.where(eye, gc, 0.0), axis=0, keepdims=True)
        dec = jnp.where(incl, jnp.exp(jnp.where(incl, gc - gc_row, 0.0)), 0.0)
        a = jnp.where(strict, kk * dec, 0.0)
        cols.append((egc, ekd))
        a_qk.append(qk * dec)
        a0.append(jnp.where(blk, a, 0.0))
        off.append(jnp.where(blk, 0.0, a))
        d0.append(jnp.where(eye, 1.0, 0.0) - a0[-1])
        rhs.append(jnp.concatenate([v * beta, k * (beta * egc)], axis=1))
    p = [_mm(x, x) for x in a0]
    for _ in range(int(math.log2(INV_BLOCK)) - 2):
        r = [_mm(jnp.concatenate([pi, di], axis=0), pi) for pi, di in zip(p, d0)]
        p = [ri[:CHUNK] for ri in r]
        d0 = [di + ri[CHUNK:] for di, ri in zip(d0, r)]
    d0 = [di + _mm(di, pi) for di, pi in zip(d0, p)]
    wl = [_mm(di, oi) for di, oi in zip(d0, off)]
    yr = [_mm(di, ri) for di, ri in zip(d0, rhs)]
    x1 = [jnp.where(blk2, x, 0.0) for x in wl]
    yl = [jnp.where(blk2, 0.0, x) for x in wl]
    zl = [y - _mm(x, y) for x, y in zip(x1, yl)]
    zr = [y - _mm(x, y) for x, y in zip(x1, yr)]
    sol = [y - _mm(x, y) for x, y in zip(zl, zr)]
    return [(so[:, :DV_A], so[:, DV_A:], aq, egc, ekd) for so, aq, (egc, ekd) in zip(sol, a_qk, cols)]


def _delta_body(t_len, hp, has_s0, q_ref, k_ref, v_ref, cq_ref, ck_ref, cv_ref, g_ref, *rest):
    if has_s0:
        s0_ref, o_ref, qs, ks, vs, u_scr, wq_scr, ak_scr, s_scr = rest
    else:
        o_ref, sfin_ref, qs, ks, vs, u_scr, wq_scr, ak_scr, s_scr = rest
    n = t_len // CHUNK
    tpos = lax.broadcasted_iota(jnp.int32, (t_len, 1), 0)

    def conv_act(x, w):
        x_prev = jnp.where(tpos == 0, 0.0, pltpu.roll(x, 1, 0))
        x_next = jnp.where(tpos == t_len - 1, 0.0, pltpu.roll(x, t_len - 1, 0))
        return _silu(x_prev * w[0:1] + x * w[1:2] + x_next * w[2:3])

    def l2n(x):
        return x * lax.rsqrt(jnp.sum(x * x, axis=-1, keepdims=True) + EPS)

    for hh in range(hp):
        cols = slice(hh * LANES, (hh + 1) * LANES)
        qs[:, cols] = l2n(conv_act(q_ref[:, cols], cq_ref[:, cols])) * (DK_A ** -0.5)
        ks[:, cols] = l2n(conv_act(k_ref[:, cols], ck_ref[:, cols]))
        vs[:, cols] = conv_act(v_ref[:, cols], cv_ref[:, cols])
    if has_s0:
        s_scr[...] = s0_ref[...]
    else:
        s_scr[...] = jnp.zeros_like(s_scr)
    o_ref[...] = jnp.zeros_like(o_ref)
    ii = lax.broadcasted_iota(jnp.int32, (CHUNK, CHUNK), 0)
    jj = lax.broadcasted_iota(jnp.int32, (CHUNK, CHUNK), 1)

    def prep(c, carry):
        r = pl.ds(pl.multiple_of(c * CHUNK, CHUNK), CHUNK)
        heads = []
        for hh in range(hp):
            cols = slice(hh * LANES, (hh + 1) * LANES)
            heads.append((qs[r, cols], ks[r, cols], vs[r, cols], g_ref[r, cols]))
        rr = [_mm_nt(jnp.concatenate([k * g[:, 1:2], k * g[:, GATE_KINDS + 1:GATE_KINDS + 2], q], axis=0), k)
              for q, k, v, g in heads]
        chains = [(rr[hh][d * CHUNK:(d + 1) * CHUNK], rr[hh][2 * CHUNK:], heads[hh][2], heads[hh][1], heads[hh][3],
                   d == 1) for hh in range(hp) for d in (0, 1)]
        solved = _chunk_solve(chains, ii, jj)
        for hh in range(hp):
            cols = slice(hh * LANES, (hh + 1) * LANES)
            q, k = heads[hh][0], heads[hh][1]
            for d in (0, 1):
                u, w, a_qk, egc, ekd = solved[hh * 2 + d]
                slot = (d * hp + hh) * n + c
                u_scr[d, r, cols] = u
                wq_scr[slot] = jnp.concatenate([w, q * egc], axis=0).astype(BF16)
                ak_scr[slot] = jnp.concatenate([a_qk, (k * ekd).T], axis=0).astype(BF16)
        return carry

    lax.fori_loop(0, n, prep, 0)

    def scan(c, carry):
        chains = []
        for hh in range(hp):
            for d in (0, 1):
                cc = c if d == 0 else n - 1 - c
                chains.append((hh, d, pl.multiple_of(cc * CHUNK, CHUNK), (d * hp + hh) * n + cc))
        s = [s_scr[d, hh] for hh, d, r0, slot in chains]
        r1 = [jnp.dot(wq_scr[slot], si.astype(BF16), preferred_element_type=F32)
              for si, (hh, d, r0, slot) in zip(s, chains)]
        v_new = [u_scr[d, pl.ds(r0, CHUNK), hh * LANES:(hh + 1) * LANES] - ri[:CHUNK]
                 for ri, (hh, d, r0, slot) in zip(r1, chains)]
        r2 = [jnp.dot(ak_scr[slot], vi.astype(BF16), preferred_element_type=F32)
              for vi, (hh, d, r0, slot) in zip(v_new, chains)]
        for si, r1i, r2i, (hh, d, r0, slot) in zip(s, r1, r2, chains):
            o_ref[pl.ds(r0, CHUNK), hh * LANES:(hh + 1) * LANES] += r1i[CHUNK:] + r2i[:CHUNK]
            lane = hh * LANES + d * GATE_KINDS + 2
            edge = g_ref[pl.ds(pl.multiple_of(r0 + (0 if d else CHUNK - 8), 8), 8), lane:lane + 1]
            s_scr[d, hh] = si * (edge[0:1] if d else edge[7:8]) + r2i[CHUNK:]
        return carry

    lax.fori_loop(0, n, scan, 0)
    if not has_s0:
        sfin_ref[...] = s_scr[...]


def _deltanet(z, conv_w, gates, s0, n_seq, t_len, row0, hp):
    rb = row0 // t_len
    w = hp * LANES
    n = t_len // CHUNK
    seq_blk = lambda col0: pl.BlockSpec((t_len, w), lambda b, h: (rb + b, col0 // hp + h))
    cw_blk = lambda col0: pl.BlockSpec((CONV_K, w), lambda b, h: (0, col0 // hp + h))
    state_blk = pl.BlockSpec((None, None, 2, hp, DK_A, DV_A), lambda b, h: (b, 0, 0, h, 0, 0))
    in_specs = [seq_blk(0), seq_blk(H_A), seq_blk(2 * H_A), cw_blk(0), cw_blk(H_A), cw_blk(2 * H_A),
                pl.BlockSpec((t_len, w), lambda b, h: (rb + b, h))]
    args = [z, z, z, conv_w, conv_w, conv_w, gates]
    o_spec = pl.BlockSpec((t_len, w), lambda b, h: (b, h))
    o_shape = jax.ShapeDtypeStruct((n_seq * t_len, H_A * DV_A), F32)
    has_s0 = s0 is not None
    if has_s0:
        in_specs += [state_blk]
        args += [s0]
        out_specs, out_shape = o_spec, o_shape
    else:
        out_specs = [o_spec, state_blk]
        out_shape = [o_shape, jax.ShapeDtypeStruct((n_seq, 1, 2, H_A, DK_A, DV_A), F32)]

    return pl.pallas_call(
        functools.partial(_delta_body, t_len, hp, has_s0),
        grid=(n_seq, H_A // hp),
        in_specs=in_specs,
        out_specs=out_specs,
        out_shape=out_shape,
        scratch_shapes=[pltpu.VMEM((t_len, w), F32)] * 3
        + [pltpu.VMEM((2, t_len, w), F32),
           pltpu.VMEM((2 * hp * n, 2 * CHUNK, DV_A), BF16),
           pltpu.VMEM((2 * hp * n, CHUNK + DK_A, CHUNK), BF16),
           pltpu.VMEM((2, hp, DK_A, DV_A), F32)],
        compiler_params=_params("arbitrary", "arbitrary"),
        name="deltanet_lat" if has_s0 else "deltanet_ctx",
    )(*args)


def _diff_attn(q, keys, vals, lam):
    qb = (q * (DQK_B ** -0.5)).astype(BF16)

    def probs(lo):
        s = lax.dot_general(qb[:, lo:lo + DQK_B], keys[:, lo:lo + DQK_B], NT_DIMS, preferred_element_type=F32)
        e = jnp.exp(s - jnp.max(s, axis=-1, keepdims=True))
        return e / jnp.sum(e, axis=-1, keepdims=True)

    a = probs(0) - lam * probs(DQK_B)
    return jnp.dot(a.astype(BF16), vals, preferred_element_type=F32)


def _subln(o, sub_ref, lam_init):
    return _rms_scale(o) * sub_ref[...] * (1.0 - lam_init)


def _attn_ctx_body(lam_init, q_ref, k_ref, v_ref, lam_ref, sub_ref, o_ref, ck_ref, cv_ref):
    k = k_ref[...]
    v = v_ref[...]
    ck_ref[...] = k
    cv_ref[...] = v
    o = _diff_attn(q_ref[...], k.astype(BF16), v.astype(BF16), lam_ref[0:1, 0:1])
    o_ref[...] = _subln(o, sub_ref, lam_init)


def _attn_ctx(z, lam, subln, n_seq, t_len, lam_init):
    n_tok = n_seq * t_len
    blk = lambda col0: pl.BlockSpec((t_len, LANES), lambda b, h: (b, col0 // LANES + h))
    cache_blk = pl.BlockSpec((None, None, None, t_len, LANES), lambda b, h: (b, 0, h, 0, 0))
    cache_shape = jax.ShapeDtypeStruct((n_seq, 1, H_B, t_len, LANES), F32)
    return pl.pallas_call(
        functools.partial(_attn_ctx_body, lam_init),
        grid=(n_seq, H_B),
        in_specs=[blk(Z_QB), blk(Z_KB), blk(Z_VB),
                  pl.BlockSpec((1, LANES), lambda b, h: (0, 0)),
                  pl.BlockSpec((1, LANES), lambda b, h: (0, 0))],
        out_specs=[pl.BlockSpec((t_len, LANES), lambda b, h: (b, h)), cache_blk, cache_blk],
        out_shape=[jax.ShapeDtypeStruct((n_tok, H_B * DV_B), F32), cache_shape, cache_shape],
        compiler_params=_params("arbitrary", "arbitrary"),
        name="attn_ctx",
    )(z, z, z, lam, subln)


def _rope(x, cos, sin_signed):
    lane = lax.broadcasted_iota(jnp.int32, (1, LANES), 1)
    first = (lane % 32) < 16
    partner = jnp.where(first, pltpu.roll(x, LANES - 16, 1), pltpu.roll(x, 16, 1))
    return x * cos + partner * sin_signed


def _attn_lat_body(lam_init, n_past, q_ref, k_ref, v_ref, pk_ref, pv_ref, cosq_ref, sinq_ref, cos_ref, sin_ref,
                   lam_ref, sub_ref, o_ref, keys, vals):
    @pl.when(pl.program_id(2) == 0)
    def _():
        keys[0:n_past, :] = pk_ref[...].astype(BF16)
        vals[0:n_past, :] = pv_ref[...].astype(BF16)
        keys[n_past:, :] = _rope(k_ref[...], cos_ref[...], sin_ref[...]).astype(BF16)
        vals[n_past:, :] = v_ref[...].astype(BF16)

    q = _rope(q_ref[...], cosq_ref[...], sinq_ref[...])
    o = _diff_attn(q, keys[...], vals[...], lam_ref[0:1, 0:1])
    o_ref[...] = _subln(o, sub_ref, lam_init)


def _attn_lat(z, cache_k, cache_v, cos, sin, lam, subln, n_seq, t_len, row0, lam_init):
    tq = 256
    nq = t_len // tq
    n_past = cache_k.shape[3]
    rbq = row0 // tq
    rbs = row0 // t_len
    seq_blk = lambda col0: pl.BlockSpec((t_len, LANES), lambda b, h, qi: (rbs + b, col0 // LANES + h))
    past_blk = pl.BlockSpec((None, None, None, n_past, LANES), lambda b, h, qi: (b, 0, h, 0, 0))
    row_vec = pl.BlockSpec((1, LANES), lambda b, h, qi: (0, 0))
    return pl.pallas_call(
        functools.partial(_attn_lat_body, lam_init, n_past),
        grid=(n_seq, H_B, nq),
        in_specs=[pl.BlockSpec((tq, LANES), lambda b, h, qi: (rbq + b * nq + qi, Z_QB // LANES + h)),
                  seq_blk(Z_KB), seq_blk(Z_VB), past_blk, past_blk,
                  pl.BlockSpec((tq, LANES), lambda b, h, qi: (qi, 0)),
                  pl.BlockSpec((tq, LANES), lambda b, h, qi: (qi, 0)),
                  pl.BlockSpec((t_len, LANES), lambda b, h, qi: (0, 0)),
                  pl.BlockSpec((t_len, LANES), lambda b, h, qi: (0, 0)),
                  row_vec, row_vec],
        out_specs=pl.BlockSpec((tq, LANES), lambda b, h, qi: (b * nq + qi, h)),
        out_shape=jax.ShapeDtypeStruct((n_seq * t_len, H_B * DV_B), F32),
        scratch_shapes=[pltpu.VMEM((n_past + t_len, LANES), BF16)] * 2,
        compiler_params=_params("arbitrary", "arbitrary", "arbitrary"),
        name="attn_lat",
    )(z, z, z, cache_k, cache_v, cos, sin, cos, sin, lam, subln)


def _rope_tables(t_len):
    t = jnp.arange(t_len)
    pos = jnp.stack([t // GRID_W, t % GRID_W], axis=1).astype(F32)
    nf = DQK_B // 4
    inv_freq = ROPE_BASE ** (-jnp.arange(nf, dtype=F32) / nf)
    lane = jnp.arange(LANES)
    half = (lane % DQK_B) // (DQK_B // 2)
    ang = pos[:, half] * inv_freq[lane % nf][None, :]
    sign = jnp.where((lane % (DQK_B // 2)) < nf, -1.0, 1.0).astype(F32)
    return jnp.cos(ang), jnp.sin(ang) * sign[None, :]


def _merge_body(nct, oac_ref, oal_ref, ga_ref, on_ref, obc_ref, obl_ref, wa_ref, wb_ref, gma_ref, gmb_ref, m_ref,
                a_scr, b_scr):
    i = pl.program_id(0)
    first = pl.program_id(1) == 0

    def prologue(oa_ref, ob_ref):
        for h in range(H_A):
            c = slice(h * DV_A, (h + 1) * DV_A)
            a_scr[:, c] = (_rms_scale(oa_ref[:, c]) * on_ref[...] * _silu(ga_ref[:, c])).astype(BF16)
        b_scr[...] = ob_ref[...].astype(BF16)

    @pl.when(jnp.logical_and(first, i < nct))
    def _():
        prologue(oac_ref, obc_ref)

    @pl.when(jnp.logical_and(first, i >= nct))
    def _():
        prologue(oal_ref, obl_ref)

    ya = jnp.dot(a_scr[...], wa_ref[...], preferred_element_type=F32)
    yb = jnp.dot(b_scr[...], wb_ref[...], preferred_element_type=F32)
    m_ref[...] = (jax.nn.sigmoid(gma_ref[...]) * ya + jax.nn.sigmoid(gmb_ref[...]) * yb).astype(BF16)


def _merge(rows, z, oa_c, oa_l, ob_c, ob_l, onorm, w_up_a, w_up_b):
    n_tok = z.shape[0]
    tm = rows.tm
    d = w_up_a.shape[1]
    ka = w_up_a.shape[0]
    tn = 1024
    nj = d // tn
    ctx_blk = pl.BlockSpec((tm, ka), lambda i, j: (rows.ctx_idx(i), 0))
    lat_blk = pl.BlockSpec((tm, ka), lambda i, j: (rows.lat_idx(i), 0))
    return pl.pallas_call(
        functools.partial(_merge_body, rows.nct),
        grid=(n_tok // tm, nj),
        in_specs=[ctx_blk, lat_blk,
                  pl.BlockSpec((tm, ka), lambda i, j: (i, Z_GATE_A // ka)),
                  pl.BlockSpec((1, DV_A), lambda i, j: (0, 0)),
                  ctx_blk, lat_blk,
                  pl.BlockSpec((ka, tn), lambda i, j: (0, j)),
                  pl.BlockSpec((ka, tn), lambda i, j: (0, j)),
                  pl.BlockSpec((tm, tn), lambda i, j: (i, Z_GM // tn + j)),
                  pl.BlockSpec((tm, tn), lambda i, j: (i, Z_GM // tn + nj + j))],
        out_specs=pl.BlockSpec((tm, tn), lambda i, j: (i, j)),
        out_shape=jax.ShapeDtypeStruct((n_tok, d), BF16),
        scratch_shapes=[pltpu.VMEM((tm, ka), BF16)] * 2,
        compiler_params=_params("arbitrary", "arbitrary"),
        name="merge",
    )(oa_c, oa_l, z, onorm, ob_c, ob_l, w_up_a, w_up_b, z, z)


ROUTE_E = 0
ROUTE_W = TOP_K


def _route_rows(lg):
    lane = lax.broadcasted_iota(jnp.int32, lg.shape, 1)
    neg = -jnp.inf

    def first_max(x):
        m = jnp.max(x, axis=1, keepdims=True)
        return m, jnp.min(jnp.where(x == m, lane, LANES), axis=1, keepdims=True)

    gl = jnp.where(lane < N_GROUPS, lg, neg)
    gmax, g_idx = first_max(gl)
    pg_top = 1.0 / jnp.sum(jnp.exp(gl - gmax), axis=1, keepdims=True)
    lo = N_GROUPS + E_PER_GROUP * g_idx
    el = jnp.where(jnp.logical_and(lane >= lo, lane < lo + E_PER_GROUP), lg, neg)
    emax, i1 = first_max(el)
    esum = jnp.sum(jnp.exp(el - emax), axis=1, keepdims=True)
    e2max, i2 = first_max(jnp.where(lane == i1, neg, el))
    p1 = 1.0 / esum
    p2 = jnp.exp(e2max - emax) / esum
    den = p1 + p2
    vals = [(i1 - N_GROUPS).astype(F32), (i2 - N_GROUPS).astype(F32), pg_top * p1 / den, pg_top * p2 / den]
    out = jnp.zeros(lg.shape, F32)
    for pos, val in enumerate(vals):
        out = jnp.where(lane == pos, val, out)
    return out


def _outproj_body(nct, m_ref, wo_ref, xp_ref, xs_ref, gt_ref, g2_ref, sh_ref, sc_ref, wr_ref, br_ref,
                  x1_ref, h2_ref, rt_ref):
    i = pl.program_id(0)
    y = jnp.dot(m_ref[...], wo_ref[...], preferred_element_type=F32)

    def finish(x_ref):
        x1 = x_ref[...] + gt_ref[0] * y
        x1_ref[...] = x1
        h2 = _rms_scale(x1) * g2_ref[...]
        h2 = h2 * (1.0 + sc_ref[0]) + sh_ref[0]
        h2_ref[...] = h2
        lg = jnp.dot(h2, wr_ref[...], preferred_element_type=F32, precision=lax.Precision.HIGHEST) + br_ref[...]
        rt_ref[...] = _route_rows(lg)

    @pl.when(i < nct)
    def _():
        finish(xp_ref)

    @pl.when(i >= nct)
    def _():
        finish(xs_ref)


def _outproj(rows, mixed, w_o, xp, xs, gt1, gain2, sh2, sc2, w_r, b_r):
    d = xp.shape[1]
    tm = rows.tm
    n_tok = rows.n * tm
    mod = lambda: pl.BlockSpec((1, 1, d), lambda i: (rows.mod_idx(i), 0, 0))
    tok = pl.BlockSpec((tm, d), lambda i: (i, 0))
    return pl.pallas_call(
        functools.partial(_outproj_body, rows.nct),
        grid=(rows.n,),
        in_specs=[tok,
                  pl.BlockSpec((d, d), lambda i: (0, 0)),
                  pl.BlockSpec((tm, d), lambda i: (rows.ctx_idx(i), 0)),
                  pl.BlockSpec((tm, d), lambda i: (rows.lat_idx(i), 0)),
                  mod(),
                  pl.BlockSpec((1, d), lambda i: (0, 0)),
                  mod(), mod(),
                  pl.BlockSpec((d, LANES), lambda i: (0, 0)),
                  pl.BlockSpec((1, LANES), lambda i: (0, 0))],
        out_specs=[tok, tok, pl.BlockSpec((tm, LANES), lambda i: (i, 0))],
        out_shape=[jax.ShapeDtypeStruct((n_tok, d), F32), jax.ShapeDtypeStruct((n_tok, d), F32),
                   jax.ShapeDtypeStruct((n_tok, LANES), F32)],
        compiler_params=_params("arbitrary"),
        name="outproj",
    )(mixed, w_o, xp, xs, gt1, gain2, sh2, sc2, w_r, b_r)


def _moe_body(n_tiles, te_ref, tv_ref, tok_ref, h_hbm, cw_ref, wg_ref, wu_ref, wd_ref, y_ref,
              xbuf, wg_b, wu_b, wd_b, sem):
    i = pl.program_id(0)
    slot = i % 2

    def gather(tile, dst_slot, start):
        def row(r, carry):
            t = tok_ref[tile * MOE_TILE + r]
            cp = pltpu.make_async_copy(h_hbm.at[pl.ds(t, 1), :], xbuf.at[dst_slot, pl.ds(r, 1), :], sem.at[dst_slot])
            if start:
                cp.start()
            else:
                cp.wait()
            return carry

        lax.fori_loop(0, MOE_TILE, row, 0)

    @pl.when(jnp.logical_and(i == 0, tv_ref[0] == 1))
    def _():
        gather(0, 0, True)

    nxt = jnp.minimum(i + 1, n_tiles - 1)

    @pl.when(jnp.logical_and(i + 1 < n_tiles, tv_ref[nxt] == 1))
    def _():
        gather(nxt, 1 - slot, True)

    prev = jnp.maximum(i - 1, 0)

    @pl.when(jnp.logical_or(i == 0, te_ref[i] != te_ref[prev]))
    def _():
        wg_b[...] = wg_ref[...].astype(BF16)
        wu_b[...] = wu_ref[...].astype(BF16)
        wd_b[...] = wd_ref[...].astype(BF16)

    @pl.when(tv_ref[i] == 1)
    def _():
        gather(i, slot, False)
        x = xbuf[slot].astype(BF16)
        g = jnp.dot(x, wg_b[...], preferred_element_type=F32)
        u = jnp.dot(x, wu_b[...], preferred_element_type=F32)
        cw = cw_ref[...]
        act = _silu(g) * u * jnp.concatenate([cw] * (D_FF_E // LANES), axis=1)
        y_ref[...] = jnp.dot(act.astype(BF16), wd_b[...], preferred_element_type=F32)

    @pl.when(tv_ref[i] == 0)
    def _():
        y_ref[...] = jnp.zeros_like(y_ref)


def _moe(h2, tile_expert, tile_valid, tok_sorted, cw_sorted, w_g, w_u, w_d):
    n_tiles = tile_expert.shape[0]
    d = h2.shape[1]
    f = w_g.shape[-1]
    grid_spec = pltpu.PrefetchScalarGridSpec(
        num_scalar_prefetch=3,
        grid=(n_tiles,),
        in_specs=[pl.BlockSpec(memory_space=pl.ANY),
                  pl.BlockSpec((MOE_TILE, LANES), lambda i, te, tv, tk: (i, 0)),
                  pl.BlockSpec((None, d, f), lambda i, te, tv, tk: (te[i], 0, 0)),
                  pl.BlockSpec((None, d, f), lambda i, te, tv, tk: (te[i], 0, 0)),
                  pl.BlockSpec((None, f, d), lambda i, te, tv, tk: (te[i], 0, 0))],
        out_specs=pl.BlockSpec((MOE_TILE, d), lambda i, te, tv, tk: (i, 0)),
        scratch_shapes=[pltpu.VMEM((2, MOE_TILE, d), F32),
                        pltpu.VMEM((d, f), BF16), pltpu.VMEM((d, f), BF16), pltpu.VMEM((f, d), BF16),
                        pltpu.SemaphoreType.DMA((2,))],
    )
    return pl.pallas_call(
        functools.partial(_moe_body, n_tiles),
        grid_spec=grid_spec,
        out_shape=jax.ShapeDtypeStruct((n_tiles * MOE_TILE, d), F32),
        compiler_params=_params("arbitrary"),
        name="moe",
    )(tile_expert, tile_valid, tok_sorted, h2, cw_sorted, w_g, w_u, w_d)


def _dispatch(route, n_tiles):
    flat_e = route[:, ROUTE_E:ROUTE_E + TOP_K].reshape(-1).astype(jnp.int32)
    flat_w = route[:, ROUTE_W:ROUTE_W + TOP_K].reshape(-1)
    n_flat = flat_e.shape[0]
    flat_t = (jnp.arange(n_flat, dtype=jnp.int32) // TOP_K)
    order = jnp.argsort(flat_e, stable=True).astype(jnp.int32)
    counts = jnp.zeros((N_EXPERTS,), jnp.int32).at[flat_e].add(1)
    padded = ((counts + MOE_TILE - 1) // MOE_TILE) * MOE_TILE
    pad_end = jnp.cumsum(padded)
    pad_off = pad_end - padded
    off = jnp.cumsum(counts) - counts
    sorted_e = flat_e[order]
    pos_sorted = pad_off[sorted_e] + jnp.arange(n_flat, dtype=jnp.int32) - off[sorted_e]
    n_rows = n_tiles * MOE_TILE
    tok_sorted = jnp.zeros((n_rows,), jnp.int32).at[pos_sorted].set(flat_t[order])
    cw_sorted = jnp.zeros((n_rows,), F32).at[pos_sorted].set(flat_w[order])
    pos_of = jnp.zeros((n_flat,), jnp.int32).at[order].set(pos_sorted)
    tile_start = jnp.arange(n_tiles, dtype=jnp.int32) * MOE_TILE
    tile_expert = jnp.minimum(jnp.searchsorted(pad_end, tile_start, side="right"), N_EXPERTS - 1).astype(jnp.int32)
    tile_valid = (tile_start < pad_end[-1]).astype(jnp.int32)
    cw_rows = jnp.broadcast_to(cw_sorted[:, None], (n_rows, LANES))
    return tile_expert, tile_valid, tok_sorted, cw_rows, pos_of


def _combine_body(tm, row0, pos_ref, y_hbm, x1_ref, gt_ref, gf_ref, o_ref, ybuf, sem):
    i = pl.program_id(0)

    def copies(start):
        def row(r, carry):
            for k in range(TOP_K):
                p = pos_ref[(row0 + i * tm + r) * TOP_K + k]
                cp = pltpu.make_async_copy(y_hbm.at[pl.ds(p, 1), :], ybuf.at[k, pl.ds(r, 1), :], sem.at[0])
                if start:
                    cp.start()
                else:
                    cp.wait()
            return carry

        lax.fori_loop(0, tm, row, 0)

    copies(True)
    copies(False)
    y = ybuf[0] + ybuf[1]
    x2 = x1_ref[...] + gt_ref[0] * y
    o_ref[...] = _rms_scale(x2) * gf_ref[...]


def _combine(y_sorted, pos_of, x1, gt2, gain_f, row0, n_rows, mod_of_tile, tm):
    d = x1.shape[1]
    rb = row0 // tm
    grid_spec = pltpu.PrefetchScalarGridSpec(
        num_scalar_prefetch=1,
        grid=(n_rows // tm,),
        in_specs=[pl.BlockSpec(memory_space=pl.ANY),
                  pl.BlockSpec((tm, d), lambda i, pos: (rb + i, 0)),
                  pl.BlockSpec((1, 1, d), lambda i, pos: (mod_of_tile(i), 0, 0)),
                  pl.BlockSpec((1, d), lambda i, pos: (0, 0))],
        out_specs=pl.BlockSpec((tm, d), lambda i, pos: (i, 0)),
        scratch_shapes=[pltpu.VMEM((TOP_K, tm, d), F32), pltpu.SemaphoreType.DMA((1,))],
    )
    return pl.pallas_call(
        functools.partial(_combine_body, tm, row0),
        grid_spec=grid_spec,
        out_shape=jax.ShapeDtypeStruct((n_rows, d), F32),
        compiler_params=_params("arbitrary"),
        name="combine",
    )(pos_of, y_sorted, x1, gt2, gain_f)


def _gate_table(zs, a_log, dt_bias):
    n_tok = zs.shape[0]
    beta = jax.nn.sigmoid(zs[:, :2 * H_A])
    g = -jnp.exp(a_log.reshape(1, 2 * H_A)) * jax.nn.softplus(zs[:, 2 * H_A:4 * H_A] + dt_bias.reshape(1, 2 * H_A))
    gch = g.reshape(n_tok // CHUNK, CHUNK, 2, H_A)
    fwd = jnp.cumsum(gch[:, :, 0], axis=1)
    bwd = jnp.flip(jnp.cumsum(jnp.flip(gch[:, :, 1], axis=1), axis=1), axis=1)
    gc = jnp.stack([fwd, bwd], axis=2)
    tot = jnp.sum(gch, axis=1, keepdims=True)
    kinds = jnp.stack([gc, beta.reshape(gc.shape), jnp.exp(gc), jnp.exp(tot - gc)], axis=-1)
    tbl = kinds.transpose(0, 1, 3, 2, 4).reshape(n_tok, H_A, 2 * GATE_KINDS)
    tbl = jnp.pad(tbl, ((0, 0), (0, 0), (0, LANES - 2 * GATE_KINDS)))
    return tbl.reshape(n_tok, H_A * LANES)


def kernel(x_prompt, x_sample, c, state_delta, cache_k, cache_v, c_ctx, w_ada, b_ada, norm_mix, norm_ffn, w_in,
           conv_a, a_log, dt_bias, onorm_a, lam, subln_b, w_up_a, w_up_b, w_o, w_rg, b_rg, w_re, b_re,
           w_e_gate, w_e_up, w_e_down, norm_final):
    assert w_in.shape[0] == 1, "single trunk layer"
    l = 0
    lam_init = 0.8 - 0.6 * math.exp(-0.3 * l)
    bc, tc, d = x_prompt.shape
    bl, tl, _ = x_sample.shape
    n_ctx, n_lat = bc * tc, bl * tl
    xp = x_prompt.reshape(n_ctx, d)
    xs = x_sample.reshape(n_lat, d)

    cvec = jnp.zeros((N_MOD_ROWS, d), F32).at[0].set(c_ctx).at[1:1 + bl].set(c)
    mod = _adaln(cvec, w_ada[l], b_ada[l][None, :])
    sh1, sc1, gt1, sh2, sc2, gt2 = [m.reshape(N_MOD_ROWS, 1, d) for m in jnp.split(mod, 6, axis=1)]

    w_l = w_in[l]
    small0 = QKV_A + H_A * DV_A
    w_main = jnp.concatenate([w_l[:, :small0], w_l[:, small0 + N_SMALL:]], axis=1).astype(BF16)
    w_small = jnp.pad(w_l[:, small0:small0 + N_SMALL], ((0, 0), (0, LANES - N_SMALL))).astype(BF16)
    rows = _Rows(n_ctx, n_lat, tl, 512)
    z, zs = _inproj(rows, xp, xs, norm_mix[l][None, :], sh1, sc1, w_main, w_small)

    gates = _gate_table(zs, a_log[l], dt_bias[l])
    oa_c, new_state = _deltanet(z, conv_a[l], gates, None, bc, tc, 0, 8)
    oa_l = _deltanet(z, conv_a[l], gates, state_delta, bl, tl, n_ctx, 4)

    lv = lam[l]
    lam_val = jnp.exp(jnp.sum(lv[0] * lv[1])) - jnp.exp(jnp.sum(lv[2] * lv[3])) + lam_init
    lam_row = jnp.full((1, LANES), lam_val, F32)
    sub_row = subln_b[l][None, :]
    ob_c, new_k, new_v = _attn_ctx(z, lam_row, sub_row, bc, tc, lam_init)
    cos, sin = _rope_tables(tl)
    ob_l = _attn_lat(z, cache_k, cache_v, cos, sin, lam_row, sub_row, bl, tl, n_ctx, lam_init)

    mixed = _merge(rows, z, oa_c, oa_l, ob_c, ob_l, onorm_a[l][None, :],
                   w_up_a[l].astype(BF16), w_up_b[l].astype(BF16))
    w_r = jnp.pad(jnp.concatenate([w_rg[l], w_re[l]], axis=1), ((0, 0), (0, LANES - N_GROUPS - N_EXPERTS)))
    b_r = jnp.pad(jnp.concatenate([b_rg[l], b_re[l]]), (0, LANES - N_GROUPS - N_EXPERTS))[None, :]
    x1, h2, route = _outproj(rows, mixed, w_o[l].astype(BF16), xp, xs, gt1, norm_ffn[l][None, :], sh2, sc2, w_r, b_r)

    n_tok = n_ctx + n_lat
    n_tiles = (n_tok * TOP_K) // MOE_TILE + N_EXPERTS
    tile_expert, tile_valid, tok_sorted, cw_rows, pos_of = _dispatch(route, n_tiles)
    y_sorted = _moe(h2, tile_expert, tile_valid, tok_sorted, cw_rows, w_e_gate[l], w_e_up[l], w_e_down[l])
    gf = norm_final[None, :]
    tm_c = 256
    y_prompt = _combine(y_sorted, pos_of, x1, gt2, gf, 0, n_ctx, lambda i: 0 * i, tm_c)
    y_sample = _combine(y_sorted, pos_of, x1, gt2, gf, n_ctx, n_lat, lambda i: 1 + i // (tl // tm_c), tm_c)
    return (y_prompt.reshape(bc, tc, d), y_sample.reshape(bl, tl, d), new_state, new_k, new_v)
```

```python
import functools
import math

import jax
import jax.numpy as jnp
from jax import lax
from jax.experimental import pallas as pl
from jax.experimental.pallas import tpu as pltpu

F32 = jnp.float32
BF16 = jnp.bfloat16

D_MODEL = 2048
GRID_W = 64
H_A = 8
DK_A = 128
DV_A = 128
CONV_K = 3
CHUNK = 64
H_B = 8
DQK_B = 64
DV_B = 2 * DQK_B
ROPE_BASE = 10000.0
N_GROUPS = 4
E_PER_GROUP = 8
N_EXPERTS = N_GROUPS * E_PER_GROUP
TOP_K = 2
D_FF_E = D_MODEL // 4
EPS = 1e-6

LANES = 128
QKV_A = 2 * H_A * DK_A + H_A * DV_A
N_SMALL = 4 * H_A
Z_GATE_A = QKV_A
Z_QB = Z_GATE_A + H_A * DV_A
Z_KB = Z_QB + H_B * 2 * DQK_B
Z_VB = Z_KB + H_B * 2 * DQK_B
Z_GM = Z_VB + H_B * DV_B
N_MAIN = Z_GM + 2 * D_MODEL

N_MOD_ROWS = 8
MOE_TILE = 256
VMEM_LIMIT = 56 * 1024 * 1024

NT_DIMS = (((1,), (1,)), ((), ()))


def _params(*sem):
    return pltpu.CompilerParams(dimension_semantics=sem, vmem_limit_bytes=VMEM_LIMIT)


def _mm(a, b):
    return jnp.dot(a.astype(BF16), b.astype(BF16), preferred_element_type=F32)


def _mm_nt(a, b):
    return lax.dot_general(a.astype(BF16), b.astype(BF16), NT_DIMS, preferred_element_type=F32)


def _silu(x):
    return x * jax.nn.sigmoid(x)


def _rms_scale(x):
    return x * lax.rsqrt(jnp.mean(x * x, axis=-1, keepdims=True) + EPS)


def _adaln_body(c_ref, w_ref, b_ref, o_ref):
    s = _silu(c_ref[...])
    o_ref[...] = _mm(s, w_ref[...]) + b_ref[...]


def _adaln(cvec, w, b):
    d, n = w.shape
    tn = 1024
    return pl.pallas_call(
        _adaln_body,
        grid=(n // tn,),
        in_specs=[pl.BlockSpec((N_MOD_ROWS, d), lambda j: (0, 0)),
                  pl.BlockSpec((d, tn), lambda j: (0, j)),
                  pl.BlockSpec((1, tn), lambda j: (0, j))],
        out_specs=pl.BlockSpec((N_MOD_ROWS, tn), lambda j: (0, j)),
        out_shape=jax.ShapeDtypeStruct((N_MOD_ROWS, n), F32),
        compiler_params=_params("arbitrary"),
        name="adaln",
    )(cvec, w, b)


class _Rows:
    def __init__(self, n_ctx, n_lat, t_lat, tm):
        assert n_ctx % tm == 0 and n_lat % tm == 0 and t_lat % tm == 0
        self.tm = tm
        self.nct = n_ctx // tm
        self.nlt = n_lat // tm
        self.per_seq = t_lat // tm
        self.n = self.nct + self.nlt

    def ctx_idx(self, i):
        return jnp.minimum(i, self.nct - 1)

    def lat_idx(self, i):
        return jnp.maximum(i - self.nct, 0)

    def mod_idx(self, i):
        return jnp.where(i < self.nct, 0, 1 + (i - self.nct) // self.per_seq)


def _inproj_body(nct, xp_ref, xs_ref, g_ref, sh_ref, sc_ref, w_ref, ws_ref, z_ref, zs_ref, h_scr):
    i = pl.program_id(0)
    j = pl.program_id(1)

    def prologue(x_ref):
        h = _rms_scale(x_ref[...]) * g_ref[...]
        h = (h * (1.0 + sc_ref[0]) + sh_ref[0]).astype(BF16)
        h_scr[...] = h
        zs_ref[...] = jnp.dot(h, ws_ref[...], preferred_element_type=F32)

    @pl.when(jnp.logical_and(j == 0, i < nct))
    def _():
        prologue(xp_ref)

    @pl.when(jnp.logical_and(j == 0, i >= nct))
    def _():
        prologue(xs_ref)

    z_ref[...] = jnp.dot(h_scr[...], w_ref[...], preferred_element_type=F32)


def _inproj(rows, xp, xs, gain, sh, sc, w_main, w_small):
    d = xp.shape[1]
    tm, tn = rows.tm, 1024
    n_tok = rows.n * tm
    return pl.pallas_call(
        functools.partial(_inproj_body, rows.nct),
        grid=(rows.n, N_MAIN // tn),
        in_specs=[pl.BlockSpec((tm, d), lambda i, j: (rows.ctx_idx(i), 0)),
                  pl.BlockSpec((tm, d), lambda i, j: (rows.lat_idx(i), 0)),
                  pl.BlockSpec((1, d), lambda i, j: (0, 0)),
                  pl.BlockSpec((1, 1, d), lambda i, j: (rows.mod_idx(i), 0, 0)),
                  pl.BlockSpec((1, 1, d), lambda i, j: (rows.mod_idx(i), 0, 0)),
                  pl.BlockSpec((d, tn), lambda i, j: (0, j)),
                  pl.BlockSpec((d, LANES), lambda i, j: (0, 0))],
        out_specs=[pl.BlockSpec((tm, tn), lambda i, j: (i, j)),
                   pl.BlockSpec((tm, LANES), lambda i, j: (i, 0))],
        out_shape=[jax.ShapeDtypeStruct((n_tok, N_MAIN), F32),
                   jax.ShapeDtypeStruct((n_tok, LANES), F32)],
        scratch_shapes=[pltpu.VMEM((tm, d), BF16)],
        compiler_params=_params("arbitrary", "arbitrary"),
        name="inproj",
    )(xp, xs, gain, sh, sc, w_main, w_small)


GATE_KINDS = 4
INV_BLOCK = 16


def _chunk_solve(chains, ii, jj):
    eye = ii == jj
    blk = (ii // INV_BLOCK) == (jj // INV_BLOCK)
    blk2 = (ii // (2 * INV_BLOCK)) == (jj // (2 * INV_BLOCK))
    cols, a_qk, a0, off, d0, rhs = [], [], [], [], [], []
    for kk, qk, v, k, gates, reverse in chains:
        base = GATE_KINDS if reverse else 0
        gc, beta, egc, ekd = (gates[:, base + i:base + i + 1] for i in range(GATE_KINDS))
        incl = (jj >= ii) if reverse else (jj <= ii)
        strict = (jj > ii) if reverse else (jj < ii)
        gc_row = jnp.sum(jnp.where(eye, gc, 0.0), axis=0, keepdims=True)
        dec = jnp.where(incl, jnp.exp(jnp.where(incl, gc - gc_row, 0.0)), 0.0)
        a = jnp.where(strict, kk * dec, 0.0)
        cols.append((egc, ekd))
        a_qk.append(qk * dec)
        a0.append(jnp.where(blk, a, 0.0))
        off.append(jnp.where(blk, 0.0, a))
        d0.append(jnp.where(eye, 1.0, 0.0) - a0[-1])
        rhs.append(jnp.concatenate([v * beta, k * (beta * egc)], axis=1))
    p = [_mm(x, x) for x in a0]
    for _ in range(int(math.log2(INV_BLOCK)) - 2):
        r = [_mm(jnp.concatenate([pi, di], axis=0), pi) for pi, di in zip(p, d0)]
        p = [ri[:CHUNK] for ri in r]
        d0 = [di + ri[CHUNK:] for di, ri in zip(d0, r)]
    d0 = [di + _mm(di, pi) for di, pi in zip(d0, p)]
    wl = [_mm(di, oi) for di, oi in zip(d0, off)]
    yr = [_mm(di, ri) for di, ri in zip(d0, rhs)]
    x1 = [jnp.where(blk2, x, 0.0) for x in wl]
    yl = [jnp.where(blk2, 0.0, x) for x in wl]
    zl = [y - _mm(x, y) for x, y in zip(x1, yl)]
    zr = [y - _mm(x, y) for x, y in zip(x1, yr)]
    sol = [y - _mm(x, y) for x, y in zip(zl, zr)]
    return [(so[:, :DV_A], so[:, DV_A:], aq, egc, ekd) for so, aq, (egc, ekd) in zip(sol, a_qk, cols)]


def _delta_body(t_len, hp, has_s0, q_ref, k_ref, v_ref, cq_ref, ck_ref, cv_ref, g_ref, *rest):
    if has_s0:
        s0_ref, o_ref, qs, ks, vs, u_scr, wq_scr, ak_scr, s_scr = rest
    else:
        o_ref, sfin_ref, qs, ks, vs, u_scr, wq_scr, ak_scr, s_scr = rest
    n = t_len // CHUNK
    tpos = lax.broadcasted_iota(jnp.int32, (t_len, 1), 0)

    def conv_act(x, w):
        x_prev = jnp.where(tpos == 0, 0.0, pltpu.roll(x, 1, 0))
        x_next = jnp.where(tpos == t_len - 1, 0.0, pltpu.roll(x, t_len - 1, 0))
        return _silu(x_prev * w[0:1] + x * w[1:2] + x_next * w[2:3])

    def l2n(x):
        return x * lax.rsqrt(jnp.sum(x * x, axis=-1, keepdims=True) + EPS)

    for hh in range(hp):
        cols = slice(hh * LANES, (hh + 1) * LANES)
        qs[:, cols] = l2n(conv_act(q_ref[:, cols], cq_ref[:, cols])) * (DK_A ** -0.5)
        ks[:, cols] = l2n(conv_act(k_ref[:, cols], ck_ref[:, cols]))
        vs[:, cols] = conv_act(v_ref[:, cols], cv_ref[:, cols])
    if has_s0:
        s_scr[...] = s0_ref[...]
    else:
        s_scr[...] = jnp.zeros_like(s_scr)
    o_ref[...] = jnp.zeros_like(o_ref)
    ii = lax.broadcasted_iota(jnp.int32, (CHUNK, CHUNK), 0)
    jj = lax.broadcasted_iota(jnp.int32, (CHUNK, CHUNK), 1)

    def prep(c, carry):
        r = pl.ds(pl.multiple_of(c * CHUNK, CHUNK), CHUNK)
        heads = []
        for hh in range(hp):
            cols = slice(hh * LANES, (hh + 1) * LANES)
            heads.append((qs[r, cols], ks[r, cols], vs[r, cols], g_ref[r, cols]))
        rr = [_mm_nt(jnp.concatenate([k * g[:, 1:2], k * g[:, GATE_KINDS + 1:GATE_KINDS + 2], q], axis=0), k)
              for q, k, v, g in heads]
        chains = [(rr[hh][d * CHUNK:(d + 1) * CHUNK], rr[hh][2 * CHUNK:], heads[hh][2], heads[hh][1], heads[hh][3],
                   d == 1) for hh in range(hp) for d in (0, 1)]
        solved = _chunk_solve(chains, ii, jj)
        for hh in range(hp):
            cols = slice(hh * LANES, (hh + 1) * LANES)
            q, k = heads[hh][0], heads[hh][1]
            for d in (0, 1):
                u, w, a_qk, egc, ekd = solved[hh * 2 + d]
                slot = (d * hp + hh) * n + c
                u_scr[d, r, cols] = u
                wq_scr[slot] = jnp.concatenate([w, q * egc], axis=0).astype(BF16)
                ak_scr[slot] = jnp.concatenate([a_qk, (k * ekd).T], axis=0).astype(BF16)
        return carry

    lax.fori_loop(0, n, prep, 0)

    def scan(c, carry):
        chains = []
        for hh in range(hp):
            for d in (0, 1):
                cc = c if d == 0 else n - 1 - c
                chains.append((hh, d, pl.multiple_of(cc * CHUNK, CHUNK), (d * hp + hh) * n + cc))
        s = [s_scr[d, hh] for hh, d, r0, slot in chains]
        r1 = [jnp.dot(wq_scr[slot], si.astype(BF16), preferred_element_type=F32)
              for si, (hh, d, r0, slot) in zip(s, chains)]
        v_new = [u_scr[d, pl.ds(r0, CHUNK), hh * LANES:(hh + 1) * LANES] - ri[:CHUNK]
                 for ri, (hh, d, r0, slot) in zip(r1, chains)]
        r2 = [jnp.dot(ak_scr[slot], vi.astype(BF16), preferred_element_type=F32)
              for vi, (hh, d, r0, slot) in zip(v_new, chains)]
        for si, r1i, r2i, (hh, d, r0, slot) in zip(s, r1, r2, chains):
            o_ref[pl.ds(r0, CHUNK), hh * LANES:(hh + 1) * LANES] += r1i[CHUNK:] + r2i[:CHUNK]
            lane = hh * LANES + d * GATE_KINDS + 2
            edge = g_ref[pl.ds(pl.multiple_of(r0 + (0 if d else CHUNK - 8), 8), 8), lane:lane + 1]
            s_scr[d, hh] = si * (edge[0:1] if d else edge[7:8]) + r2i[CHUNK:]
        return carry

    lax.fori_loop(0, n, scan, 0)
    if not has_s0:
        sfin_ref[...] = s_scr[...]


def _deltanet(z, conv_w, gates, s0, n_seq, t_len, row0, hp):
    rb = row0 // t_len
    w = hp * LANES
    n = t_len // CHUNK
    seq_blk = lambda col0: pl.BlockSpec((t_len, w), lambda b, h: (rb + b, col0 // hp + h))
    cw_blk = lambda col0: pl.BlockSpec((CONV_K, w), lambda b, h: (0, col0 // hp + h))
    state_blk = pl.BlockSpec((None, None, 2, hp, DK_A, DV_A), lambda b, h: (b, 0, 0, h, 0, 0))
    in_specs = [seq_blk(0), seq_blk(H_A), seq_blk(2 * H_A), cw_blk(0), cw_blk(H_A), cw_blk(2 * H_A),
                pl.BlockSpec((t_len, w), lambda b, h: (rb + b, h))]
    args = [z, z, z, conv_w, conv_w, conv_w, gates]
    o_spec = pl.BlockSpec((t_len, w), lambda b, h: (b, h))
    o_shape = jax.ShapeDtypeStruct((n_seq * t_len, H_A * DV_A), F32)
    has_s0 = s0 is not None
    if has_s0:
        in_specs += [state_blk]
        args += [s0]
        out_specs, out_shape = o_spec, o_shape
    else:
        out_specs = [o_spec, state_blk]
        out_shape = [o_shape, jax.ShapeDtypeStruct((n_seq, 1, 2, H_A, DK_A, DV_A), F32)]

    return pl.pallas_call(
        functools.partial(_delta_body, t_len, hp, has_s0),
        grid=(n_seq, H_A // hp),
        in_specs=in_specs,
        out_specs=out_specs,
        out_shape=out_shape,
        scratch_shapes=[pltpu.VMEM((t_len, w), F32)] * 3
        + [pltpu.VMEM((2, t_len, w), F32),
           pltpu.VMEM((2 * hp * n, 2 * CHUNK, DV_A), BF16),
           pltpu.VMEM((2 * hp * n, CHUNK + DK_A, CHUNK), BF16),
           pltpu.VMEM((2, hp, DK_A, DV_A), F32)],
        compiler_params=_params("arbitrary", "arbitrary"),
        name="deltanet_lat" if has_s0 else "deltanet_ctx",
    )(*args)


def _diff_attn(q, keys, vals, lam):
    qb = (q * (DQK_B ** -0.5)).astype(BF16)

    def probs(lo):
        s = lax.dot_general(qb[:, lo:lo + DQK_B], keys[:, lo:lo + DQK_B], NT_DIMS, preferred_element_type=F32)
        e = jnp.exp(s - jnp.max(s, axis=-1, keepdims=True))
        return e / jnp.sum(e, axis=-1, keepdims=True)

    a = probs(0) - lam * probs(DQK_B)
    return jnp.dot(a.astype(BF16), vals, preferred_element_type=F32)


def _subln(o, sub_ref, lam_init):
    return _rms_scale(o) * sub_ref[...] * (1.0 - lam_init)


def _attn_ctx_body(lam_init, q_ref, k_ref, v_ref, lam_ref, sub_ref, o_ref, ck_ref, cv_ref):
    k = k_ref[...]
    v = v_ref[...]
    ck_ref[...] = k
    cv_ref[...] = v
    o = _diff_attn(q_ref[...], k.astype(BF16), v.astype(BF16), lam_ref[0:1, 0:1])
    o_ref[...] = _subln(o, sub_ref, lam_init)


def _attn_ctx(z, lam, subln, n_seq, t_len, lam_init):
    n_tok = n_seq * t_len
    blk = lambda col0: pl.BlockSpec((t_len, LANES), lambda b, h: (b, col0 // LANES + h))
    cache_blk = pl.BlockSpec((None, None, None, t_len, LANES), lambda b, h: (b, 0, h, 0, 0))
    cache_shape = jax.ShapeDtypeStruct((n_seq, 1, H_B, t_len, LANES), F32)
    return pl.pallas_call(
        functools.partial(_attn_ctx_body, lam_init),
        grid=(n_seq, H_B),
        in_specs=[blk(Z_QB), blk(Z_KB), blk(Z_VB),
                  pl.BlockSpec((1, LANES), lambda b, h: (0, 0)),
                  pl.BlockSpec((1, LANES), lambda b, h: (0, 0))],
        out_specs=[pl.BlockSpec((t_len, LANES), lambda b, h: (b, h)), cache_blk, cache_blk],
        out_shape=[jax.ShapeDtypeStruct((n_tok, H_B * DV_B), F32), cache_shape, cache_shape],
        compiler_params=_params("arbitrary", "arbitrary"),
        name="attn_ctx",
    )(z, z, z, lam, subln)


def _rope(x, cos, sin_signed):
    lane = lax.broadcasted_iota(jnp.int32, (1, LANES), 1)
    first = (lane % 32) < 16
    partner = jnp.where(first, pltpu.roll(x, LANES - 16, 1), pltpu.roll(x, 16, 1))
    return x * cos + partner * sin_signed


def _attn_lat_body(lam_init, n_past, q_ref, k_ref, v_ref, pk_ref, pv_ref, cosq_ref, sinq_ref, cos_ref, sin_ref,
                   lam_ref, sub_ref, o_ref, keys, vals):
    @pl.when(pl.program_id(2) == 0)
    def _():
        keys[0:n_past, :] = pk_ref[...].astype(BF16)
        vals[0:n_past, :] = pv_ref[...].astype(BF16)
        keys[n_past:, :] = _rope(k_ref[...], cos_ref[...], sin_ref[...]).astype(BF16)
        vals[n_past:, :] = v_ref[...].astype(BF16)

    q = _rope(q_ref[...], cosq_ref[...], sinq_ref[...])
    o = _diff_attn(q, keys[...], vals[...], lam_ref[0:1, 0:1])
    o_ref[...] = _subln(o, sub_ref, lam_init)


def _attn_lat(z, cache_k, cache_v, cos, sin, lam, subln, n_seq, t_len, row0, lam_init):
    tq = 256
    nq = t_len // tq
    n_past = cache_k.shape[3]
    rbq = row0 // tq
    rbs = row0 // t_len
    seq_blk = lambda col0: pl.BlockSpec((t_len, LANES), lambda b, h, qi: (rbs + b, col0 // LANES + h))
    past_blk = pl.BlockSpec((None, None, None, n_past, LANES), lambda b, h, qi: (b, 0, h, 0, 0))
    row_vec = pl.BlockSpec((1, LANES), lambda b, h, qi: (0, 0))
    return pl.pallas_call(
        functools.partial(_attn_lat_body, lam_init, n_past),
        grid=(n_seq, H_B, nq),
        in_specs=[pl.BlockSpec((tq, LANES), lambda b, h, qi: (rbq + b * nq + qi, Z_QB // LANES + h)),
                  seq_blk(Z_KB), seq_blk(Z_VB), past_blk, past_blk,
                  pl.BlockSpec((tq, LANES), lambda b, h, qi: (qi, 0)),
                  pl.BlockSpec((tq, LANES), lambda b, h, qi: (qi, 0)),
                  pl.BlockSpec((t_len, LANES), lambda b, h, qi: (0, 0)),
                  pl.BlockSpec((t_len, LANES), lambda b, h, qi: (0, 0)),
                  row_vec, row_vec],
        out_specs=pl.BlockSpec((tq, LANES), lambda b, h, qi: (b * nq + qi, h)),
        out_shape=jax.ShapeDtypeStruct((n_seq * t_len, H_B * DV_B), F32),
        scratch_shapes=[pltpu.VMEM((n_past + t_len, LANES), BF16)] * 2,
        compiler_params=_params("arbitrary", "arbitrary", "arbitrary"),
        name="attn_lat",
    )(z, z, z, cache_k, cache_v, cos, sin, cos, sin, lam, subln)


def _rope_tables(t_len):
    t = jnp.arange(t_len)
    pos = jnp.stack([t // GRID_W, t % GRID_W], axis=1).astype(F32)
    nf = DQK_B // 4
    inv_freq = ROPE_BASE ** (-jnp.arange(nf, dtype=F32) / nf)
    lane = jnp.arange(LANES)
    half = (lane % DQK_B) // (DQK_B // 2)
    ang = pos[:, half] * inv_freq[lane % nf][None, :]
    sign = jnp.where((lane % (DQK_B // 2)) < nf, -1.0, 1.0).astype(F32)
    return jnp.cos(ang), jnp.sin(ang) * sign[None, :]


def _merge_body(nct, oac_ref, oal_ref, ga_ref, on_ref, obc_ref, obl_ref, wa_ref, wb_ref, gma_ref, gmb_ref, m_ref,
                a_scr, b_scr):
    i = pl.program_id(0)
    first = pl.program_id(1) == 0

    def prologue(oa_ref, ob_ref):
        for h in range(H_A):
            c = slice(h * DV_A, (h + 1) * DV_A)
            a_scr[:, c] = (_rms_scale(oa_ref[:, c]) * on_ref[...] * _silu(ga_ref[:, c])).astype(BF16)
        b_scr[...] = ob_ref[...].astype(BF16)

    @pl.when(jnp.logical_and(first, i < nct))
    def _():
        prologue(oac_ref, obc_ref)

    @pl.when(jnp.logical_and(first, i >= nct))
    def _():
        prologue(oal_ref, obl_ref)

    ya = jnp.dot(a_scr[...], wa_ref[...], preferred_element_type=F32)
    yb = jnp.dot(b_scr[...], wb_ref[...], preferred_element_type=F32)
    m_ref[...] = (jax.nn.sigmoid(gma_ref[...]) * ya + jax.nn.sigmoid(gmb_ref[...]) * yb).astype(BF16)


def _merge(rows, z, oa_c, oa_l, ob_c, ob_l, onorm, w_up_a, w_up_b):
    n_tok = z.shape[0]
    tm = rows.tm
    d = w_up_a.shape[1]
    ka = w_up_a.shape[0]
    tn = 1024
    nj = d // tn
    ctx_blk = pl.BlockSpec((tm, ka), lambda i, j: (rows.ctx_idx(i), 0))
    lat_blk = pl.BlockSpec((tm, ka), lambda i, j: (rows.lat_idx(i), 0))
    return pl.pallas_call(
        functools.partial(_merge_body, rows.nct),
        grid=(n_tok // tm, nj),
        in_specs=[ctx_blk, lat_blk,
                  pl.BlockSpec((tm, ka), lambda i, j: (i, Z_GATE_A // ka)),
                  pl.BlockSpec((1, DV_A), lambda i, j: (0, 0)),
                  ctx_blk, lat_blk,
                  pl.BlockSpec((ka, tn), lambda i, j: (0, j)),
                  pl.BlockSpec((ka, tn), lambda i, j: (0, j)),
                  pl.BlockSpec((tm, tn), lambda i, j: (i, Z_GM // tn + j)),
                  pl.BlockSpec((tm, tn), lambda i, j: (i, Z_GM // tn + nj + j))],
        out_specs=pl.BlockSpec((tm, tn), lambda i, j: (i, j)),
        out_shape=jax.ShapeDtypeStruct((n_tok, d), BF16),
        scratch_shapes=[pltpu.VMEM((tm, ka), BF16)] * 2,
        compiler_params=_params("arbitrary", "arbitrary"),
        name="merge",
    )(oa_c, oa_l, z, onorm, ob_c, ob_l, w_up_a, w_up_b, z, z)


ROUTE_E = 0
ROUTE_W = TOP_K


def _route_rows(lg):
    lane = lax.broadcasted_iota(jnp.int32, lg.shape, 1)
    neg = -jnp.inf

    def first_max(x):
        m = jnp.max(x, axis=1, keepdims=True)
        return m, jnp.min(jnp.where(x == m, lane, LANES), axis=1, keepdims=True)

    gl = jnp.where(lane < N_GROUPS, lg, neg)
    gmax, g_idx = first_max(gl)
    pg_top = 1.0 / jnp.sum(jnp.exp(gl - gmax), axis=1, keepdims=True)
    lo = N_GROUPS + E_PER_GROUP * g_idx
    el = jnp.where(jnp.logical_and(lane >= lo, lane < lo + E_PER_GROUP), lg, neg)
    emax, i1 = first_max(el)
    esum = jnp.sum(jnp.exp(el - emax), axis=1, keepdims=True)
    e2max, i2 = first_max(jnp.where(lane == i1, neg, el))
    p1 = 1.0 / esum
    p2 = jnp.exp(e2max - emax) / esum
    den = p1 + p2
    vals = [(i1 - N_GROUPS).astype(F32), (i2 - N_GROUPS).astype(F32), pg_top * p1 / den, pg_top * p2 / den]
    out = jnp.zeros(lg.shape, F32)
    for pos, val in enumerate(vals):
        out = jnp.where(lane == pos, val, out)
    return out


def _outproj_body(nct, m_ref, wo_ref, xp_ref, xs_ref, gt_ref, g2_ref, sh_ref, sc_ref, wr_ref, br_ref,
                  x1_ref, h2_ref, rt_ref):
    i = pl.program_id(0)
    y = jnp.dot(m_ref[...], wo_ref[...], preferred_element_type=F32)

    def finish(x_ref):
        x1 = x_ref[...] + gt_ref[0] * y
        x1_ref[...] = x1
        h2 = _rms_scale(x1) * g2_ref[...]
        h2 = h2 * (1.0 + sc_ref[0]) + sh_ref[0]
        h2_ref[...] = h2
        lg = jnp.dot(h2, wr_ref[...], preferred_element_type=F32, precision=lax.Precision.HIGHEST) + br_ref[...]
        rt_ref[...] = _route_rows(lg)

    @pl.when(i < nct)
    def _():
        finish(xp_ref)

    @pl.when(i >= nct)
    def _():
        finish(xs_ref)


def _outproj(rows, mixed, w_o, xp, xs, gt1, gain2, sh2, sc2, w_r, b_r):
    d = xp.shape[1]
    tm = rows.tm
    n_tok = rows.n * tm
    mod = lambda: pl.BlockSpec((1, 1, d), lambda i: (rows.mod_idx(i), 0, 0))
    tok = pl.BlockSpec((tm, d), lambda i: (i, 0))
    return pl.pallas_call(
        functools.partial(_outproj_body, rows.nct),
        grid=(rows.n,),
        in_specs=[tok,
                  pl.BlockSpec((d, d), lambda i: (0, 0)),
                  pl.BlockSpec((tm, d), lambda i: (rows.ctx_idx(i), 0)),
                  pl.BlockSpec((tm, d), lambda i: (rows.lat_idx(i), 0)),
                  mod(),
                  pl.BlockSpec((1, d), lambda i: (0, 0)),
                  mod(), mod(),
                  pl.BlockSpec((d, LANES), lambda i: (0, 0)),
                  pl.BlockSpec((1, LANES), lambda i: (0, 0))],
        out_specs=[tok, tok, pl.BlockSpec((tm, LANES), lambda i: (i, 0))],
        out_shape=[jax.ShapeDtypeStruct((n_tok, d), F32), jax.ShapeDtypeStruct((n_tok, d), F32),
                   jax.ShapeDtypeStruct((n_tok, LANES), F32)],
        compiler_params=_params("arbitrary"),
        name="outproj",
    )(mixed, w_o, xp, xs, gt1, gain2, sh2, sc2, w_r, b_r)


DISPATCH_BLOCK = 256
TBL_EXPERT, TBL_VALID, TBL_COUNT, TBL_OFFSET = 0, 1, 2, 3


def _dispatch_body(n_tok, route_ref, pos_ref, tbl_ref):
    nb = n_tok // DISPATCH_BLOCK
    lane = lax.broadcasted_iota(jnp.int32, (1, LANES), 1)
    lane_f = lane.astype(F32)

    def one_hot(b, k):
        r = pl.ds(pl.multiple_of(b * DISPATCH_BLOCK, DISPATCH_BLOCK), DISPATCH_BLOCK)
        return jnp.where(route_ref[r, ROUTE_E + k:ROUTE_E + k + 1] == lane_f, 1.0, 0.0)

    def count(b, acc):
        return acc + jnp.sum(one_hot(b, 0) + one_hot(b, 1), axis=0, keepdims=True)

    counts = lax.fori_loop(0, nb, count, jnp.zeros((1, LANES), F32))
    padded = jnp.floor((counts + (MOE_TILE - 1)) * (1.0 / MOE_TILE)) * MOE_TILE
    pad_end = padded
    shift = 1
    while shift < LANES:
        pad_end = pad_end + jnp.where(lane >= shift, pltpu.roll(pad_end, shift, 1), 0.0)
        shift *= 2
    pad_off = pad_end - padded

    ri = lax.broadcasted_iota(jnp.int32, (DISPATCH_BLOCK, DISPATCH_BLOCK), 0)
    ci = lax.broadcasted_iota(jnp.int32, (DISPATCH_BLOCK, DISPATCH_BLOCK), 1)
    before = jnp.where(ci < ri, 1.0, 0.0).astype(BF16)

    def place(b, run):
        oh = [one_hot(b, k) for k in range(TOP_K)]
        base = pad_off + run
        out = jnp.zeros((DISPATCH_BLOCK, LANES), F32)
        lane_b = lax.broadcasted_iota(jnp.int32, (DISPATCH_BLOCK, LANES), 1)
        for k in range(TOP_K):
            prior = jnp.dot(before, oh[k].astype(BF16), preferred_element_type=F32)
            pos = jnp.sum(oh[k] * (base + prior), axis=1, keepdims=True)
            out = jnp.where(lane_b == k, pos, out)
            base = base + jnp.sum(oh[k], axis=0, keepdims=True)
        r = pl.ds(pl.multiple_of(b * DISPATCH_BLOCK, DISPATCH_BLOCK), DISPATCH_BLOCK)
        pos_ref[r, :] = out.astype(jnp.int32)
        return base - pad_off

    lax.fori_loop(0, nb, place, jnp.zeros((1, LANES), F32))

    end_col = jnp.transpose(jnp.broadcast_to(pad_end, (8, LANES)))[:, 0:1]
    e_col = lax.broadcasted_iota(jnp.int32, (LANES, 1), 0)
    tile_start = lane_f * MOE_TILE
    passed = jnp.where(jnp.logical_and(end_col <= tile_start, e_col < N_EXPERTS), 1.0, 0.0)
    tile_expert = jnp.minimum(jnp.sum(passed, axis=0, keepdims=True), N_EXPERTS - 1.0)
    total = jnp.sum(jnp.where(lane == N_EXPERTS - 1, pad_end, 0.0), axis=1, keepdims=True)
    tile_valid = jnp.where(tile_start < total, 1.0, 0.0)
    row = lax.broadcasted_iota(jnp.int32, (8, LANES), 0)
    tbl = jnp.zeros((8, LANES), F32)
    for idx, val in ((TBL_EXPERT, tile_expert), (TBL_VALID, tile_valid), (TBL_COUNT, counts), (TBL_OFFSET, pad_off)):
        tbl = jnp.where(row == idx, val, tbl)
    tbl_ref[...] = tbl.astype(jnp.int32)


def _dispatch(route):
    n_tok = route.shape[0]
    return pl.pallas_call(
        functools.partial(_dispatch_body, n_tok),
        out_shape=[jax.ShapeDtypeStruct((n_tok, LANES), jnp.int32), jax.ShapeDtypeStruct((8, LANES), jnp.int32)],
        compiler_params=pltpu.CompilerParams(vmem_limit_bytes=VMEM_LIMIT),
        name="dispatch",
    )(route)


def _scatter_body(tm, n_tiles, pos_ref, cnt_ref, off_ref, h_ref, o_hbm, zbuf, sem):
    i = pl.program_id(0)

    @pl.when(i == 0)
    def _():
        zbuf[...] = jnp.zeros_like(zbuf)
        last = N_EXPERTS - 1
        first_empty = (off_ref[last] + cnt_ref[last] + MOE_TILE - 1) // MOE_TILE

        def tile_copy(t):
            rows = pl.ds(pl.multiple_of(t * MOE_TILE, MOE_TILE), MOE_TILE)
            return pltpu.make_async_copy(zbuf, o_hbm.at[rows, :], sem.at[1])

        def partial_tile(e, start):
            cnt = cnt_ref[e]

            @pl.when(cnt % MOE_TILE != 0)
            def _():
                cp = tile_copy((off_ref[e] + cnt) // MOE_TILE)
                if start:
                    cp.start()
                else:
                    cp.wait()

        def zero_partial(e, carry):
            partial_tile(e, True)
            return carry

        def partial_done(e, carry):
            partial_tile(e, False)
            return carry

        def zero_tile(t, carry):
            tile_copy(t).start()
            return carry

        def zero_done(t, carry):
            tile_copy(t).wait()
            return carry

        lax.fori_loop(0, N_EXPERTS, zero_partial, 0)
        lax.fori_loop(first_empty, n_tiles, zero_tile, 0)
        lax.fori_loop(0, N_EXPERTS, partial_done, 0)
        lax.fori_loop(first_empty, n_tiles, zero_done, 0)

    def row_copies(r):
        return [pltpu.make_async_copy(h_ref.at[pl.ds(r, 1), :],
                                      o_hbm.at[pl.ds(pos_ref[(i * tm + r) * TOP_K + k], 1), :], sem.at[0])
                for k in range(TOP_K)]

    def start(r, carry):
        for cp in row_copies(r):
            cp.start()
        return carry

    def wait(r, carry):
        for cp in row_copies(r):
            cp.wait()
        return carry

    lax.fori_loop(0, tm, start, 0, unroll=8)
    lax.fori_loop(0, tm, wait, 0, unroll=8)


def _scatter_rows(h2, pos_flat, counts, offsets, n_rows):
    n_tok, d = h2.shape
    tm = 256
    grid_spec = pltpu.PrefetchScalarGridSpec(
        num_scalar_prefetch=3,
        grid=(n_tok // tm,),
        in_specs=[pl.BlockSpec((tm, d), lambda i, pos, cnt, off: (i, 0))],
        out_specs=pl.BlockSpec(memory_space=pl.ANY),
        scratch_shapes=[pltpu.VMEM((MOE_TILE, d), F32), pltpu.SemaphoreType.DMA((2,))],
    )
    return pl.pallas_call(
        functools.partial(_scatter_body, tm, n_rows // MOE_TILE),
        grid_spec=grid_spec,
        out_shape=jax.ShapeDtypeStruct((n_rows, d), F32),
        compiler_params=_params("arbitrary"),
        name="scatter_rows",
    )(pos_flat, counts, offsets, h2)


def _moe_body(te_ref, tv_ref, x_ref, wg_ref, wu_ref, wd_ref, y_ref, wg_b, wu_b, wd_b):
    i = pl.program_id(0)
    prev = jnp.maximum(i - 1, 0)

    @pl.when(jnp.logical_or(i == 0, te_ref[i] != te_ref[prev]))
    def _():
        wg_b[...] = wg_ref[...].astype(BF16)
        wu_b[...] = wu_ref[...].astype(BF16)
        wd_b[...] = wd_ref[...].astype(BF16)

    @pl.when(tv_ref[i] == 1)
    def _():
        x = x_ref[...].astype(BF16)
        g = jnp.dot(x, wg_b[...], preferred_element_type=F32)
        u = jnp.dot(x, wu_b[...], preferred_element_type=F32)
        y_ref[...] = jnp.dot((_silu(g) * u).astype(BF16), wd_b[...], preferred_element_type=F32)

    @pl.when(tv_ref[i] == 0)
    def _():
        y_ref[...] = jnp.zeros_like(y_ref)


def _moe(x_sorted, tile_expert, tile_valid, w_g, w_u, w_d):
    n_tiles = tile_expert.shape[0]
    d = x_sorted.shape[1]
    f = w_g.shape[-1]
    grid_spec = pltpu.PrefetchScalarGridSpec(
        num_scalar_prefetch=2,
        grid=(n_tiles,),
        in_specs=[pl.BlockSpec((MOE_TILE, d), lambda i, te, tv: (i * tv[i], 0)),
                  pl.BlockSpec((None, d, f), lambda i, te, tv: (te[i], 0, 0)),
                  pl.BlockSpec((None, d, f), lambda i, te, tv: (te[i], 0, 0)),
                  pl.BlockSpec((None, f, d), lambda i, te, tv: (te[i], 0, 0))],
        out_specs=pl.BlockSpec((MOE_TILE, d), lambda i, te, tv: (i, 0)),
        scratch_shapes=[pltpu.VMEM((d, f), BF16), pltpu.VMEM((d, f), BF16), pltpu.VMEM((f, d), BF16)],
    )
    return pl.pallas_call(
        _moe_body,
        grid_spec=grid_spec,
        out_shape=jax.ShapeDtypeStruct((n_tiles * MOE_TILE, d), F32),
        compiler_params=_params("arbitrary"),
        name="moe",
    )(tile_expert, tile_valid, x_sorted, w_g, w_u, w_d)


def _combine_body(tm, n_tiles, row0, pos_ref, y_hbm, rt_ref, x1_ref, gt_ref, gf_ref, o_ref, ybuf, sem):
    i = pl.program_id(0)
    slot = i % 2

    def gather(tile, dst_slot, start):
        def row(r, carry):
            for k in range(TOP_K):
                p = pos_ref[(row0 + tile * tm + r) * TOP_K + k]
                cp = pltpu.make_async_copy(y_hbm.at[pl.ds(p, 1), :], ybuf.at[dst_slot, k, pl.ds(r, 1), :],
                                           sem.at[dst_slot])
                if start:
                    cp.start()
                else:
                    cp.wait()
            return carry

        lax.fori_loop(0, tm, row, 0, unroll=8)

    @pl.when(i == 0)
    def _():
        gather(0, 0, True)

    @pl.when(i + 1 < n_tiles)
    def _():
        gather(i + 1, 1 - slot, True)

    gather(i, slot, False)
    rt = rt_ref[...]
    y = rt[:, ROUTE_W:ROUTE_W + 1] * ybuf[slot, 0] + rt[:, ROUTE_W + 1:ROUTE_W + 2] * ybuf[slot, 1]
    x2 = x1_ref[...] + gt_ref[0] * y
    o_ref[...] = _rms_scale(x2) * gf_ref[...]


def _combine(y_sorted, pos_flat, route, x1, gt2, gain_f, row0, n_rows, mod_of_tile, tm):
    d = x1.shape[1]
    rb = row0 // tm
    n_tiles = n_rows // tm
    grid_spec = pltpu.PrefetchScalarGridSpec(
        num_scalar_prefetch=1,
        grid=(n_tiles,),
        in_specs=[pl.BlockSpec(memory_space=pl.ANY),
                  pl.BlockSpec((tm, LANES), lambda i, pos: (rb + i, 0)),
                  pl.BlockSpec((tm, d), lambda i, pos: (rb + i, 0)),
                  pl.BlockSpec((1, 1, d), lambda i, pos: (mod_of_tile(i), 0, 0)),
                  pl.BlockSpec((1, d), lambda i, pos: (0, 0))],
        out_specs=pl.BlockSpec((tm, d), lambda i, pos: (i, 0)),
        scratch_shapes=[pltpu.VMEM((2, TOP_K, tm, d), F32), pltpu.SemaphoreType.DMA((2,))],
    )
    return pl.pallas_call(
        functools.partial(_combine_body, tm, n_tiles, row0),
        grid_spec=grid_spec,
        out_shape=jax.ShapeDtypeStruct((n_rows, d), F32),
        compiler_params=_params("arbitrary"),
        name="combine",
    )(pos_flat, y_sorted, route, x1, gt2, gain_f)


GATE_SRC = {"beta": 0, "gc": 2 * H_A, "egc": 4 * H_A, "ekd": 6 * H_A}


def _gates_body(zs_ref, al_ref, dt_ref, sel_ref, o_ref):
    tm = zs_ref.shape[0]
    lane = lax.broadcasted_iota(jnp.int32, (1, LANES), 1)
    zs = zs_ref[...]
    x = zs + dt_ref[...]
    softplus = jnp.maximum(x, 0.0) + jnp.log(1.0 + jnp.exp(-jnp.abs(x)))
    in_g = jnp.logical_and(lane >= GATE_SRC["gc"], lane < GATE_SRC["egc"])
    g = jnp.where(in_g, -jnp.exp(al_ref[...]) * softplus, 0.0)
    beta = jax.nn.sigmoid(zs)
    reversed_lane = lane >= GATE_SRC["gc"] + H_A
    ri = lax.broadcasted_iota(jnp.int32, (CHUNK, CHUNK), 0)
    ci = lax.broadcasted_iota(jnp.int32, (CHUNK, CHUNK), 1)
    prefix = jnp.where(ci <= ri, 1.0, 0.0)
    suffix = jnp.where(ci >= ri, 1.0, 0.0)
    exact = dict(preferred_element_type=F32, precision=lax.Precision.HIGHEST)
    for c in range(tm // CHUNK):
        r = slice(c * CHUNK, (c + 1) * CHUNK)
        gch = g[r]
        gc = jnp.where(reversed_lane, jnp.dot(suffix, gch, **exact), jnp.dot(prefix, gch, **exact))
        tot = jnp.sum(gch, axis=0, keepdims=True)
        src = jnp.where(lane < GATE_SRC["gc"], beta[r],
                        jnp.where(lane < GATE_SRC["egc"], gc,
                                  jnp.where(lane < GATE_SRC["ekd"], pltpu.roll(jnp.exp(gc), 2 * H_A, 1),
                                            jnp.where(lane < GATE_SRC["ekd"] + 2 * H_A,
                                                      pltpu.roll(jnp.exp(tot - gc), 4 * H_A, 1), 0.0))))
        o_ref[r, :] = jnp.dot(src, sel_ref[...], **exact)


def _gate_table(zs, a_log, dt_bias):
    n_tok = zs.shape[0]
    tm = 512
    place = lambda v: jnp.pad(v.reshape(1, 2 * H_A), ((0, 0), (GATE_SRC["gc"], LANES - GATE_SRC["egc"])))
    src = jnp.arange(LANES)[:, None]
    dst = jnp.arange(H_A * LANES)[None, :]
    head, within = dst // LANES, dst % LANES
    kind_src = jnp.array([GATE_SRC["gc"], GATE_SRC["beta"], GATE_SRC["egc"], GATE_SRC["ekd"]])
    want = kind_src[within % GATE_KINDS] + (within // GATE_KINDS) * H_A + head
    sel = jnp.where(jnp.logical_and(within < 2 * GATE_KINDS, src == want), 1.0, 0.0).astype(F32)
    return pl.pallas_call(
        _gates_body,
        grid=(n_tok // tm,),
        in_specs=[pl.BlockSpec((tm, LANES), lambda i: (i, 0)),
                  pl.BlockSpec((1, LANES), lambda i: (0, 0)),
                  pl.BlockSpec((1, LANES), lambda i: (0, 0)),
                  pl.BlockSpec((LANES, H_A * LANES), lambda i: (0, 0))],
        out_specs=pl.BlockSpec((tm, H_A * LANES), lambda i: (i, 0)),
        out_shape=jax.ShapeDtypeStruct((n_tok, H_A * LANES), F32),
        compiler_params=_params("arbitrary"),
        name="gates",
    )(zs, place(a_log), place(dt_bias), sel)


def kernel(x_prompt, x_sample, c, state_delta, cache_k, cache_v, c_ctx, w_ada, b_ada, norm_mix, norm_ffn, w_in,
           conv_a, a_log, dt_bias, onorm_a, lam, subln_b, w_up_a, w_up_b, w_o, w_rg, b_rg, w_re, b_re,
           w_e_gate, w_e_up, w_e_down, norm_final):
    assert w_in.shape[0] == 1, "single trunk layer"
    l = 0
    lam_init = 0.8 - 0.6 * math.exp(-0.3 * l)
    bc, tc, d = x_prompt.shape
    bl, tl, _ = x_sample.shape
    n_ctx, n_lat = bc * tc, bl * tl
    xp = x_prompt.reshape(n_ctx, d)
    xs = x_sample.reshape(n_lat, d)

    cvec = jnp.zeros((N_MOD_ROWS, d), F32).at[0].set(c_ctx).at[1:1 + bl].set(c)
    mod = _adaln(cvec, w_ada[l], b_ada[l][None, :])
    sh1, sc1, gt1, sh2, sc2, gt2 = [m.reshape(N_MOD_ROWS, 1, d) for m in jnp.split(mod, 6, axis=1)]

    w_l = w_in[l]
    small0 = QKV_A + H_A * DV_A
    w_main = jnp.concatenate([w_l[:, :small0], w_l[:, small0 + N_SMALL:]], axis=1).astype(BF16)
    w_small = jnp.pad(w_l[:, small0:small0 + N_SMALL], ((0, 0), (0, LANES - N_SMALL))).astype(BF16)
    rows = _Rows(n_ctx, n_lat, tl, 512)
    z, zs = _inproj(rows, xp, xs, norm_mix[l][None, :], sh1, sc1, w_main, w_small)

    gates = _gate_table(zs, a_log[l], dt_bias[l])
    oa_c, new_state = _deltanet(z, conv_a[l], gates, None, bc, tc, 0, 8)
    oa_l = _deltanet(z, conv_a[l], gates, state_delta, bl, tl, n_ctx, 4)

    lv = lam[l]
    lam_val = jnp.exp(jnp.sum(lv[0] * lv[1])) - jnp.exp(jnp.sum(lv[2] * lv[3])) + lam_init
    lam_row = jnp.full((1, LANES), lam_val, F32)
    sub_row = subln_b[l][None, :]
    ob_c, new_k, new_v = _attn_ctx(z, lam_row, sub_row, bc, tc, lam_init)
    cos, sin = _rope_tables(tl)
    ob_l = _attn_lat(z, cache_k, cache_v, cos, sin, lam_row, sub_row, bl, tl, n_ctx, lam_init)

    mixed = _merge(rows, z, oa_c, oa_l, ob_c, ob_l, onorm_a[l][None, :],
                   w_up_a[l].astype(BF16), w_up_b[l].astype(BF16))
    w_r = jnp.pad(jnp.concatenate([w_rg[l], w_re[l]], axis=1), ((0, 0), (0, LANES - N_GROUPS - N_EXPERTS)))
    b_r = jnp.pad(jnp.concatenate([b_rg[l], b_re[l]]), (0, LANES - N_GROUPS - N_EXPERTS))[None, :]
    x1, h2, route = _outproj(rows, mixed, w_o[l].astype(BF16), xp, xs, gt1, norm_ffn[l][None, :], sh2, sc2, w_r, b_r)

    n_tok = n_ctx + n_lat
    n_tiles = (n_tok * TOP_K) // MOE_TILE + N_EXPERTS
    pos, tbl = _dispatch(route)
    pos_flat = pos[:, :TOP_K].reshape(-1)
    counts, offsets = tbl[TBL_COUNT, :N_EXPERTS], tbl[TBL_OFFSET, :N_EXPERTS]
    x_sorted = _scatter_rows(h2, pos_flat, counts, offsets, n_tiles * MOE_TILE)
    y_sorted = _moe(x_sorted, tbl[TBL_EXPERT, :n_tiles], tbl[TBL_VALID, :n_tiles],
                    w_e_gate[l], w_e_up[l], w_e_down[l])
    gf = norm_final[None, :]
    tm_c = 256
    y_prompt = _combine(y_sorted, pos_flat, route, x1, gt2, gf, 0, n_ctx, lambda i: 0 * i, tm_c)
    y_sample = _combine(y_sorted, pos_flat, route, x1, gt2, gf, n_ctx, n_lat, lambda i: 1 + i // (tl // tm_c), tm_c)
    return (y_prompt.reshape(bc, tc, d), y_sample.reshape(bl, tl, d), new_state, new_k, new_v)
```

```python
import functools
import math

import jax
import jax.numpy as jnp
from jax import lax
from jax.experimental import pallas as pl
from jax.experimental.pallas import tpu as pltpu

F32 = jnp.float32
BF16 = jnp.bfloat16

D_MODEL = 2048
GRID_W = 64
H_A = 8
DK_A = 128
DV_A = 128
CONV_K = 3
CHUNK = 64
H_B = 8
DQK_B = 64
DV_B = 2 * DQK_B
ROPE_BASE = 10000.0
N_GROUPS = 4
E_PER_GROUP = 8
N_EXPERTS = N_GROUPS * E_PER_GROUP
TOP_K = 2
D_FF_E = D_MODEL // 4
EPS = 1e-6

LANES = 128
QKV_A = 2 * H_A * DK_A + H_A * DV_A
N_SMALL = 4 * H_A
Z_GATE_A = QKV_A
Z_QB = Z_GATE_A + H_A * DV_A
Z_KB = Z_QB + H_B * 2 * DQK_B
Z_VB = Z_KB + H_B * 2 * DQK_B
Z_GM = Z_VB + H_B * DV_B
N_MAIN = Z_GM + 2 * D_MODEL

N_MOD_ROWS = 8
MOE_TILE = 256
VMEM_LIMIT = 56 * 1024 * 1024

NT_DIMS = (((1,), (1,)), ((), ()))


def _params(*sem):
    return pltpu.CompilerParams(dimension_semantics=sem, vmem_limit_bytes=VMEM_LIMIT)


def _mm(a, b):
    return jnp.dot(a.astype(BF16), b.astype(BF16), preferred_element_type=F32)


def _mm_nt(a, b):
    return lax.dot_general(a.astype(BF16), b.astype(BF16), NT_DIMS, preferred_element_type=F32)


def _silu(x):
    return x * jax.nn.sigmoid(x)


def _rms_scale(x):
    return x * lax.rsqrt(jnp.mean(x * x, axis=-1, keepdims=True) + EPS)


def _adaln_body(c_ref, w_ref, b_ref, o_ref):
    s = _silu(c_ref[...])
    o_ref[...] = _mm(s, w_ref[...]) + b_ref[...]


def _adaln(cvec, w, b):
    d, n = w.shape
    tn = 1024
    return pl.pallas_call(
        _adaln_body,
        grid=(n // tn,),
        in_specs=[pl.BlockSpec((N_MOD_ROWS, d), lambda j: (0, 0)),
                  pl.BlockSpec((d, tn), lambda j: (0, j)),
                  pl.BlockSpec((1, tn), lambda j: (0, j))],
        out_specs=pl.BlockSpec((N_MOD_ROWS, tn), lambda j: (0, j)),
        out_shape=jax.ShapeDtypeStruct((N_MOD_ROWS, n), F32),
        compiler_params=_params("arbitrary"),
        name="adaln",
    )(cvec, w, b)


class _Rows:
    def __init__(self, n_ctx, n_lat, t_lat, tm):
        assert n_ctx % tm == 0 and n_lat % tm == 0 and t_lat % tm == 0
        self.tm = tm
        self.nct = n_ctx // tm
        self.nlt = n_lat // tm
        self.per_seq = t_lat // tm
        self.n = self.nct + self.nlt

    def ctx_idx(self, i):
        return jnp.minimum(i, self.nct - 1)

    def lat_idx(self, i):
        return jnp.maximum(i - self.nct, 0)

    def mod_idx(self, i):
        return jnp.where(i < self.nct, 0, 1 + (i - self.nct) // self.per_seq)


SMALL_COL0 = QKV_A + H_A * DV_A


def _prenorm_body(nct, xp_ref, xs_ref, g_ref, sh_ref, sc_ref, ws_ref, h_ref, zs_ref):
    i = pl.program_id(0)

    def run(x_ref):
        h = _rms_scale(x_ref[...]) * g_ref[...]
        h = (h * (1.0 + sc_ref[0]) + sh_ref[0]).astype(BF16)
        h_ref[...] = h
        zs_ref[...] = jnp.dot(h, ws_ref[...].astype(BF16), preferred_element_type=F32)

    @pl.when(i < nct)
    def _():
        run(xp_ref)

    @pl.when(i >= nct)
    def _():
        run(xs_ref)


def _prenorm(rows, xp, xs, gain, sh, sc, w_in):
    d = xp.shape[1]
    tm = rows.tm
    n_tok = rows.n * tm
    return pl.pallas_call(
        functools.partial(_prenorm_body, rows.nct),
        grid=(rows.n,),
        in_specs=[pl.BlockSpec((tm, d), lambda i: (rows.ctx_idx(i), 0)),
                  pl.BlockSpec((tm, d), lambda i: (rows.lat_idx(i), 0)),
                  pl.BlockSpec((1, d), lambda i: (0, 0)),
                  pl.BlockSpec((1, 1, d), lambda i: (rows.mod_idx(i), 0, 0)),
                  pl.BlockSpec((1, 1, d), lambda i: (rows.mod_idx(i), 0, 0)),
                  pl.BlockSpec((d, LANES), lambda i: (0, SMALL_COL0 // LANES))],
        out_specs=[pl.BlockSpec((tm, d), lambda i: (i, 0)),
                   pl.BlockSpec((tm, LANES), lambda i: (i, 0))],
        out_shape=[jax.ShapeDtypeStruct((n_tok, d), BF16),
                   jax.ShapeDtypeStruct((n_tok, LANES), F32)],
        compiler_params=_params("arbitrary"),
        name="prenorm",
    )(xp, xs, gain, sh, sc, w_in)


INPROJ_TN = 1024
N_ALIGNED = SMALL_COL0 // INPROJ_TN


def _inproj_body(h_ref, w_ref, wn_ref, z_ref, w_scr):
    j = pl.program_id(0)
    first = pl.program_id(1) == 0

    @pl.when(jnp.logical_and(first, j < N_ALIGNED))
    def _():
        w_scr[...] = w_ref[...].astype(BF16)

    @pl.when(jnp.logical_and(first, j >= N_ALIGNED))
    def _():
        w_scr[:, :INPROJ_TN - N_SMALL] = w_ref[:, N_SMALL:].astype(BF16)
        w_scr[:, INPROJ_TN - N_SMALL:] = wn_ref[:, :N_SMALL].astype(BF16)

    z_ref[...] = jnp.dot(h_ref[...], w_scr[...], preferred_element_type=F32)


def _inproj(h, w_in):
    n_tok, d = h.shape
    tm, tn = 1024, INPROJ_TN
    per = tn // LANES
    return pl.pallas_call(
        _inproj_body,
        grid=(N_MAIN // tn, n_tok // tm),
        in_specs=[pl.BlockSpec((tm, d), lambda j, i: (i, 0)),
                  pl.BlockSpec((d, tn), lambda j, i: (0, j)),
                  pl.BlockSpec((d, LANES), lambda j, i: (0, (j + 1) * per))],
        out_specs=pl.BlockSpec((tm, tn), lambda j, i: (i, j)),
        out_shape=jax.ShapeDtypeStruct((n_tok, N_MAIN), F32),
        scratch_shapes=[pltpu.VMEM((d, tn), BF16)],
        compiler_params=_params("arbitrary", "arbitrary"),
        name="inproj",
    )(h, w_in, w_in)


GATE_KINDS = 4
INV_BLOCK = 16


def _chunk_solve(chains, ii, jj):
    eye = ii == jj
    blk = (ii // INV_BLOCK) == (jj // INV_BLOCK)
    blk2 = (ii // (2 * INV_BLOCK)) == (jj // (2 * INV_BLOCK))
    cols, a_qk, a0, off, d0, rhs = [], [], [], [], [], []
    for kk, qk, v, k, gates, reverse in chains:
        base = GATE_KINDS if reverse else 0
        gc, beta, egc, ekd = (gates[:, base + i:base + i + 1] for i in range(GATE_KINDS))
        incl = (jj >= ii) if reverse else (jj <= ii)
        strict = (jj > ii) if reverse else (jj < ii)
        gc_row = jnp.sum(jnp.where(eye, gc, 0.0), axis=0, keepdims=True)
        dec = jnp.where(incl, jnp.exp(jnp.where(incl, gc - gc_row, 0.0)), 0.0)
        a = jnp.where(strict, kk * dec, 0.0)
        cols.append((egc, ekd))
        a_qk.append(qk * dec)
        a0.append(jnp.where(blk, a, 0.0))
        off.append(jnp.where(blk, 0.0, a))
        d0.append(jnp.where(eye, 1.0, 0.0) - a0[-1])
        rhs.append(jnp.concatenate([v * beta, k * (beta * egc)], axis=1))
    p = [_mm(x, x) for x in a0]
    for _ in range(int(math.log2(INV_BLOCK)) - 2):
        r = [_mm(jnp.concatenate([pi, di], axis=0), pi) for pi, di in zip(p, d0)]
        p = [ri[:CHUNK] for ri in r]
        d0 = [di + ri[CHUNK:] for di, ri in zip(d0, r)]
    d0 = [di + _mm(di, pi) for di, pi in zip(d0, p)]
    wl = [_mm(di, oi) for di, oi in zip(d0, off)]
    yr = [_mm(di, ri) for di, ri in zip(d0, rhs)]
    x1 = [jnp.where(blk2, x, 0.0) for x in wl]
    yl = [jnp.where(blk2, 0.0, x) for x in wl]
    zl = [y - _mm(x, y) for x, y in zip(x1, yl)]
    zr = [y - _mm(x, y) for x, y in zip(x1, yr)]
    sol = [y - _mm(x, y) for x, y in zip(zl, zr)]
    return [(so[:, :DV_A], so[:, DV_A:], aq, egc, ekd) for so, aq, (egc, ekd) in zip(sol, a_qk, cols)]


def _delta_body(t_len, hp, has_s0, q_ref, k_ref, v_ref, cq_ref, ck_ref, cv_ref, g_ref, *rest):
    if has_s0:
        s0_ref, o_ref, qs, ks, vs, u_scr, wq_scr, ak_scr, s_scr = rest
    else:
        o_ref, sfin_ref, qs, ks, vs, u_scr, wq_scr, ak_scr, s_scr = rest
    n = t_len // CHUNK
    tpos = lax.broadcasted_iota(jnp.int32, (t_len, 1), 0)

    def conv_act(x, w):
        x_prev = jnp.where(tpos == 0, 0.0, pltpu.roll(x, 1, 0))
        x_next = jnp.where(tpos == t_len - 1, 0.0, pltpu.roll(x, t_len - 1, 0))
        return _silu(x_prev * w[0:1] + x * w[1:2] + x_next * w[2:3])

    def l2n(x):
        return x * lax.rsqrt(jnp.sum(x * x, axis=-1, keepdims=True) + EPS)

    for hh in range(hp):
        cols = slice(hh * LANES, (hh + 1) * LANES)
        qs[:, cols] = l2n(conv_act(q_ref[:, cols], cq_ref[:, cols])) * (DK_A ** -0.5)
        ks[:, cols] = l2n(conv_act(k_ref[:, cols], ck_ref[:, cols]))
        vs[:, cols] = conv_act(v_ref[:, cols], cv_ref[:, cols])
    if has_s0:
        s_scr[...] = s0_ref[...]
    else:
        s_scr[...] = jnp.zeros_like(s_scr)
    o_ref[...] = jnp.zeros_like(o_ref)
    ii = lax.broadcasted_iota(jnp.int32, (CHUNK, CHUNK), 0)
    jj = lax.broadcasted_iota(jnp.int32, (CHUNK, CHUNK), 1)

    def prep(c, carry):
        r = pl.ds(pl.multiple_of(c * CHUNK, CHUNK), CHUNK)
        heads = []
        for hh in range(hp):
            cols = slice(hh * LANES, (hh + 1) * LANES)
            heads.append((qs[r, cols], ks[r, cols], vs[r, cols], g_ref[r, cols]))
        rr = [_mm_nt(jnp.concatenate([k * g[:, 1:2], k * g[:, GATE_KINDS + 1:GATE_KINDS + 2], q], axis=0), k)
              for q, k, v, g in heads]
        chains = [(rr[hh][d * CHUNK:(d + 1) * CHUNK], rr[hh][2 * CHUNK:], heads[hh][2], heads[hh][1], heads[hh][3],
                   d == 1) for hh in range(hp) for d in (0, 1)]
        solved = _chunk_solve(chains, ii, jj)
        for hh in range(hp):
            cols = slice(hh * LANES, (hh + 1) * LANES)
            q, k = heads[hh][0], heads[hh][1]
            for d in (0, 1):
                u, w, a_qk, egc, ekd = solved[hh * 2 + d]
                slot = (d * hp + hh) * n + c
                u_scr[d, r, cols] = u
                wq_scr[slot] = jnp.concatenate([w, q * egc], axis=0).astype(BF16)
                ak_scr[slot] = jnp.concatenate([a_qk, (k * ekd).T], axis=0).astype(BF16)
        return carry

    lax.fori_loop(0, n, prep, 0)

    def scan(c, carry):
        chains = []
        for hh in range(hp):
            for d in (0, 1):
                cc = c if d == 0 else n - 1 - c
                chains.append((hh, d, pl.multiple_of(cc * CHUNK, CHUNK), (d * hp + hh) * n + cc))
        s = [s_scr[d, hh] for hh, d, r0, slot in chains]
        r1 = [jnp.dot(wq_scr[slot], si.astype(BF16), preferred_element_type=F32)
              for si, (hh, d, r0, slot) in zip(s, chains)]
        v_new = [u_scr[d, pl.ds(r0, CHUNK), hh * LANES:(hh + 1) * LANES] - ri[:CHUNK]
                 for ri, (hh, d, r0, slot) in zip(r1, chains)]
        r2 = [jnp.dot(ak_scr[slot], vi.astype(BF16), preferred_element_type=F32)
              for vi, (hh, d, r0, slot) in zip(v_new, chains)]
        for si, r1i, r2i, (hh, d, r0, slot) in zip(s, r1, r2, chains):
            o_ref[pl.ds(r0, CHUNK), hh * LANES:(hh + 1) * LANES] += r1i[CHUNK:] + r2i[:CHUNK]
            lane = hh * LANES + d * GATE_KINDS + 2
            edge = g_ref[pl.ds(pl.multiple_of(r0 + (0 if d else CHUNK - 8), 8), 8), lane:lane + 1]
            s_scr[d, hh] = si * (edge[0:1] if d else edge[7:8]) + r2i[CHUNK:]
        return carry

    lax.fori_loop(0, n, scan, 0)
    if not has_s0:
        sfin_ref[...] = s_scr[...]


def _deltanet(z, conv_w, gates, s0, n_seq, t_len, row0, hp):
    rb = row0 // t_len
    w = hp * LANES
    n = t_len // CHUNK
    seq_blk = lambda col0: pl.BlockSpec((t_len, w), lambda b, h: (rb + b, col0 // hp + h))
    cw_blk = lambda col0: pl.BlockSpec((CONV_K, w), lambda b, h: (0, col0 // hp + h))
    state_blk = pl.BlockSpec((None, None, 2, hp, DK_A, DV_A), lambda b, h: (b, 0, 0, h, 0, 0))
    in_specs = [seq_blk(0), seq_blk(H_A), seq_blk(2 * H_A), cw_blk(0), cw_blk(H_A), cw_blk(2 * H_A),
                pl.BlockSpec((t_len, w), lambda b, h: (rb + b, h))]
    args = [z, z, z, conv_w, conv_w, conv_w, gates]
    o_spec = pl.BlockSpec((t_len, w), lambda b, h: (b, h))
    o_shape = jax.ShapeDtypeStruct((n_seq * t_len, H_A * DV_A), F32)
    has_s0 = s0 is not None
    if has_s0:
        in_specs += [state_blk]
        args += [s0]
        out_specs, out_shape = o_spec, o_shape
    else:
        out_specs = [o_spec, state_blk]
        out_shape = [o_shape, jax.ShapeDtypeStruct((n_seq, 1, 2, H_A, DK_A, DV_A), F32)]

    return pl.pallas_call(
        functools.partial(_delta_body, t_len, hp, has_s0),
        grid=(n_seq, H_A // hp),
        in_specs=in_specs,
        out_specs=out_specs,
        out_shape=out_shape,
        scratch_shapes=[pltpu.VMEM((t_len, w), F32)] * 3
        + [pltpu.VMEM((2, t_len, w), F32),
           pltpu.VMEM((2 * hp * n, 2 * CHUNK, DV_A), BF16),
           pltpu.VMEM((2 * hp * n, CHUNK + DK_A, CHUNK), BF16),
           pltpu.VMEM((2, hp, DK_A, DV_A), F32)],
        compiler_params=_params("arbitrary", "arbitrary"),
        name="deltanet_lat" if has_s0 else "deltanet_ctx",
    )(*args)


def _diff_attn(q, keys, vals, lam):
    qb = (q * (DQK_B ** -0.5)).astype(BF16)

    def probs(lo):
        s = lax.dot_general(qb[:, lo:lo + DQK_B], keys[:, lo:lo + DQK_B], NT_DIMS, preferred_element_type=F32)
        e = jnp.exp(s - jnp.max(s, axis=-1, keepdims=True))
        return e / jnp.sum(e, axis=-1, keepdims=True)

    a = probs(0) - lam * probs(DQK_B)
    return jnp.dot(a.astype(BF16), vals, preferred_element_type=F32)


def _subln(o, sub_ref, lam_init):
    return _rms_scale(o) * sub_ref[...] * (1.0 - lam_init)


def _attn_ctx_body(lam_init, q_ref, k_ref, v_ref, lam_ref, sub_ref, o_ref, ck_ref, cv_ref):
    heads = [slice(h * LANES, (h + 1) * LANES) for h in range(H_B)]
    lam = lam_ref[0:1, 0:1]
    ks = [k_ref[:, c] for c in heads]
    vs = [v_ref[:, c] for c in heads]
    for h in range(H_B):
        ck_ref[h] = ks[h]
        cv_ref[h] = vs[h]
    qb = [(q_ref[:, c] * (DQK_B ** -0.5)).astype(BF16) for c in heads]
    kb = [k.astype(BF16) for k in ks]
    probs = []
    for lo in (0, DQK_B):
        s = [lax.dot_general(q[:, lo:lo + DQK_B], k[:, lo:lo + DQK_B], NT_DIMS, preferred_element_type=F32)
             for q, k in zip(qb, kb)]
        e = [jnp.exp(x - jnp.max(x, axis=-1, keepdims=True)) for x in s]
        probs.append([x / jnp.sum(x, axis=-1, keepdims=True) for x in e])
    o = [jnp.dot((p1 - lam * p2).astype(BF16), v.astype(BF16), preferred_element_type=F32)
         for p1, p2, v in zip(probs[0], probs[1], vs)]
    for c, oh in zip(heads, o):
        o_ref[:, c] = _subln(oh, sub_ref, lam_init)


def _attn_ctx(z, lam, subln, n_seq, t_len, lam_init):
    n_tok = n_seq * t_len
    w = H_B * LANES
    blk = lambda col0: pl.BlockSpec((t_len, w), lambda b: (b, col0 // w))
    cache_blk = pl.BlockSpec((None, None, H_B, t_len, LANES), lambda b: (b, 0, 0, 0, 0))
    cache_shape = jax.ShapeDtypeStruct((n_seq, 1, H_B, t_len, LANES), F32)
    return pl.pallas_call(
        functools.partial(_attn_ctx_body, lam_init),
        grid=(n_seq,),
        in_specs=[blk(Z_QB), blk(Z_KB), blk(Z_VB),
                  pl.BlockSpec((1, LANES), lambda b: (0, 0)),
                  pl.BlockSpec((1, LANES), lambda b: (0, 0))],
        out_specs=[pl.BlockSpec((t_len, w), lambda b: (b, 0)), cache_blk, cache_blk],
        out_shape=[jax.ShapeDtypeStruct((n_tok, H_B * DV_B), F32), cache_shape, cache_shape],
        compiler_params=_params("arbitrary"),
        name="attn_ctx",
    )(z, z, z, lam, subln)


def _rope(x, cos, sin_signed):
    lane = lax.broadcasted_iota(jnp.int32, (1, LANES), 1)
    first = (lane % 32) < 16
    partner = jnp.where(first, pltpu.roll(x, LANES - 16, 1), pltpu.roll(x, 16, 1))
    return x * cos + partner * sin_signed


def _attn_lat_body(lam_init, n_past, q_ref, k_ref, v_ref, pk_ref, pv_ref, cosq_ref, sinq_ref, cos_ref, sin_ref,
                   lam_ref, sub_ref, o_ref, keys, vals):
    @pl.when(pl.program_id(2) == 0)
    def _():
        keys[0:n_past, :] = pk_ref[...].astype(BF16)
        vals[0:n_past, :] = pv_ref[...].astype(BF16)
        keys[n_past:, :] = _rope(k_ref[...], cos_ref[...], sin_ref[...]).astype(BF16)
        vals[n_past:, :] = v_ref[...].astype(BF16)

    q = _rope(q_ref[...], cosq_ref[...], sinq_ref[...])
    o = _diff_attn(q, keys[...], vals[...], lam_ref[0:1, 0:1])
    o_ref[...] = _subln(o, sub_ref, lam_init)


def _attn_lat(z, cache_k, cache_v, cos, sin, lam, subln, n_seq, t_len, row0, lam_init):
    tq = 256
    nq = t_len // tq
    n_past = cache_k.shape[3]
    rbq = row0 // tq
    rbs = row0 // t_len
    seq_blk = lambda col0: pl.BlockSpec((t_len, LANES), lambda b, h, qi: (rbs + b, col0 // LANES + h))
    past_blk = pl.BlockSpec((None, None, None, n_past, LANES), lambda b, h, qi: (b, 0, h, 0, 0))
    row_vec = pl.BlockSpec((1, LANES), lambda b, h, qi: (0, 0))
    return pl.pallas_call(
        functools.partial(_attn_lat_body, lam_init, n_past),
        grid=(n_seq, H_B, nq),
        in_specs=[pl.BlockSpec((tq, LANES), lambda b, h, qi: (rbq + b * nq + qi, Z_QB // LANES + h)),
                  seq_blk(Z_KB), seq_blk(Z_VB), past_blk, past_blk,
                  pl.BlockSpec((tq, LANES), lambda b, h, qi: (qi, 0)),
                  pl.BlockSpec((tq, LANES), lambda b, h, qi: (qi, 0)),
                  pl.BlockSpec((t_len, LANES), lambda b, h, qi: (0, 0)),
                  pl.BlockSpec((t_len, LANES), lambda b, h, qi: (0, 0)),
                  row_vec, row_vec],
        out_specs=pl.BlockSpec((tq, LANES), lambda b, h, qi: (b * nq + qi, h)),
        out_shape=jax.ShapeDtypeStruct((n_seq * t_len, H_B * DV_B), F32),
        scratch_shapes=[pltpu.VMEM((n_past + t_len, LANES), BF16)] * 2,
        compiler_params=_params("arbitrary", "arbitrary", "arbitrary"),
        name="attn_lat",
    )(z, z, z, cache_k, cache_v, cos, sin, cos, sin, lam, subln)


def _rope_tables(t_len):
    t = jnp.arange(t_len)
    pos = jnp.stack([t // GRID_W, t % GRID_W], axis=1).astype(F32)
    nf = DQK_B // 4
    inv_freq = ROPE_BASE ** (-jnp.arange(nf, dtype=F32) / nf)
    lane = jnp.arange(LANES)
    half = (lane % DQK_B) // (DQK_B // 2)
    ang = pos[:, half] * inv_freq[lane % nf][None, :]
    sign = jnp.where((lane % (DQK_B // 2)) < nf, -1.0, 1.0).astype(F32)
    return jnp.cos(ang), jnp.sin(ang) * sign[None, :]


def _merge_body(nct, oac_ref, oal_ref, ga_ref, on_ref, obc_ref, obl_ref, wa_ref, wb_ref, gma_ref, gmb_ref, m_ref,
                a_scr, b_scr):
    i = pl.program_id(0)
    first = pl.program_id(1) == 0

    def prologue(oa_ref, ob_ref):
        for h in range(H_A):
            c = slice(h * DV_A, (h + 1) * DV_A)
            a_scr[:, c] = (_rms_scale(oa_ref[:, c]) * on_ref[...] * _silu(ga_ref[:, c])).astype(BF16)
        b_scr[...] = ob_ref[...].astype(BF16)

    @pl.when(jnp.logical_and(first, i < nct))
    def _():
        prologue(oac_ref, obc_ref)

    @pl.when(jnp.logical_and(first, i >= nct))
    def _():
        prologue(oal_ref, obl_ref)

    ya = jnp.dot(a_scr[...], wa_ref[...], preferred_element_type=F32)
    yb = jnp.dot(b_scr[...], wb_ref[...], preferred_element_type=F32)
    m_ref[...] = (jax.nn.sigmoid(gma_ref[...]) * ya + jax.nn.sigmoid(gmb_ref[...]) * yb).astype(BF16)


def _merge(rows, z, oa_c, oa_l, ob_c, ob_l, onorm, w_up_a, w_up_b):
    n_tok = z.shape[0]
    tm = rows.tm
    d = w_up_a.shape[1]
    ka = w_up_a.shape[0]
    tn = 1024
    nj = d // tn
    ctx_blk = pl.BlockSpec((tm, ka), lambda i, j: (rows.ctx_idx(i), 0))
    lat_blk = pl.BlockSpec((tm, ka), lambda i, j: (rows.lat_idx(i), 0))
    return pl.pallas_call(
        functools.partial(_merge_body, rows.nct),
        grid=(n_tok // tm, nj),
        in_specs=[ctx_blk, lat_blk,
                  pl.BlockSpec((tm, ka), lambda i, j: (i, Z_GATE_A // ka)),
                  pl.BlockSpec((1, DV_A), lambda i, j: (0, 0)),
                  ctx_blk, lat_blk,
                  pl.BlockSpec((ka, tn), lambda i, j: (0, j)),
                  pl.BlockSpec((ka, tn), lambda i, j: (0, j)),
                  pl.BlockSpec((tm, tn), lambda i, j: (i, Z_GM // tn + j)),
                  pl.BlockSpec((tm, tn), lambda i, j: (i, Z_GM // tn + nj + j))],
        out_specs=pl.BlockSpec((tm, tn), lambda i, j: (i, j)),
        out_shape=jax.ShapeDtypeStruct((n_tok, d), BF16),
        scratch_shapes=[pltpu.VMEM((tm, ka), BF16)] * 2,
        compiler_params=_params("arbitrary", "arbitrary"),
        name="merge",
    )(oa_c, oa_l, z, onorm, ob_c, ob_l, w_up_a, w_up_b, z, z)


ROUTE_E = 0
ROUTE_W = TOP_K


def _route_rows(lg):
    lane = lax.broadcasted_iota(jnp.int32, lg.shape, 1)
    neg = -jnp.inf

    def first_max(x):
        m = jnp.max(x, axis=1, keepdims=True)
        return m, jnp.min(jnp.where(x == m, lane, LANES), axis=1, keepdims=True)

    gl = jnp.where(lane < N_GROUPS, lg, neg)
    gmax, g_idx = first_max(gl)
    pg_top = 1.0 / jnp.sum(jnp.exp(gl - gmax), axis=1, keepdims=True)
    lo = N_GROUPS + E_PER_GROUP * g_idx
    el = jnp.where(jnp.logical_and(lane >= lo, lane < lo + E_PER_GROUP), lg, neg)
    emax, i1 = first_max(el)
    esum = jnp.sum(jnp.exp(el - emax), axis=1, keepdims=True)
    e2max, i2 = first_max(jnp.where(lane == i1, neg, el))
    p1 = 1.0 / esum
    p2 = jnp.exp(e2max - emax) / esum
    den = p1 + p2
    vals = [(i1 - N_GROUPS).astype(F32), (i2 - N_GROUPS).astype(F32), pg_top * p1 / den, pg_top * p2 / den]
    out = jnp.zeros(lg.shape, F32)
    for pos, val in enumerate(vals):
        out = jnp.where(lane == pos, val, out)
    return out


def _outproj_body(nct, m_ref, wo_ref, xp_ref, xs_ref, gt_ref, g2_ref, sh_ref, sc_ref, wr_ref, br_ref,
                  x1_ref, h2_ref, rt_ref):
    i = pl.program_id(0)
    y = jnp.dot(m_ref[...], wo_ref[...], preferred_element_type=F32)

    def finish(x_ref):
        x1 = x_ref[...] + gt_ref[0] * y
        x1_ref[...] = x1
        h2 = _rms_scale(x1) * g2_ref[...]
        h2 = h2 * (1.0 + sc_ref[0]) + sh_ref[0]
        h2_ref[...] = h2
        lg = jnp.dot(h2, wr_ref[...], preferred_element_type=F32, precision=lax.Precision.HIGHEST) + br_ref[...]
        rt_ref[...] = _route_rows(lg)

    @pl.when(i < nct)
    def _():
        finish(xp_ref)

    @pl.when(i >= nct)
    def _():
        finish(xs_ref)


def _outproj(rows, mixed, w_o, xp, xs, gt1, gain2, sh2, sc2, w_r, b_r):
    d = xp.shape[1]
    tm = rows.tm
    n_tok = rows.n * tm
    mod = lambda: pl.BlockSpec((1, 1, d), lambda i: (rows.mod_idx(i), 0, 0))
    tok = pl.BlockSpec((tm, d), lambda i: (i, 0))
    return pl.pallas_call(
        functools.partial(_outproj_body, rows.nct),
        grid=(rows.n,),
        in_specs=[tok,
                  pl.BlockSpec((d, d), lambda i: (0, 0)),
                  pl.BlockSpec((tm, d), lambda i: (rows.ctx_idx(i), 0)),
                  pl.BlockSpec((tm, d), lambda i: (rows.lat_idx(i), 0)),
                  mod(),
                  pl.BlockSpec((1, d), lambda i: (0, 0)),
                  mod(), mod(),
                  pl.BlockSpec((d, LANES), lambda i: (0, 0)),
                  pl.BlockSpec((1, LANES), lambda i: (0, 0))],
        out_specs=[tok, tok, pl.BlockSpec((tm, LANES), lambda i: (i, 0))],
        out_shape=[jax.ShapeDtypeStruct((n_tok, d), F32), jax.ShapeDtypeStruct((n_tok, d), F32),
                   jax.ShapeDtypeStruct((n_tok, LANES), F32)],
        compiler_params=_params("arbitrary"),
        name="outproj",
    )(mixed, w_o, xp, xs, gt1, gain2, sh2, sc2, w_r, b_r)


DISPATCH_BLOCK = 256
TBL_EXPERT, TBL_VALID, TBL_COUNT, TBL_OFFSET = 0, 1, 2, 3


def _dispatch_body(n_tok, route_ref, pos_ref, tbl_ref):
    nb = n_tok // DISPATCH_BLOCK
    lane = lax.broadcasted_iota(jnp.int32, (1, LANES), 1)
    lane_f = lane.astype(F32)

    def one_hot(b, k):
        r = pl.ds(pl.multiple_of(b * DISPATCH_BLOCK, DISPATCH_BLOCK), DISPATCH_BLOCK)
        return jnp.where(route_ref[r, ROUTE_E + k:ROUTE_E + k + 1] == lane_f, 1.0, 0.0)

    def count(b, acc):
        return acc + jnp.sum(one_hot(b, 0) + one_hot(b, 1), axis=0, keepdims=True)

    counts = lax.fori_loop(0, nb, count, jnp.zeros((1, LANES), F32))
    padded = jnp.floor((counts + (MOE_TILE - 1)) * (1.0 / MOE_TILE)) * MOE_TILE
    pad_end = padded
    shift = 1
    while shift < LANES:
        pad_end = pad_end + jnp.where(lane >= shift, pltpu.roll(pad_end, shift, 1), 0.0)
        shift *= 2
    pad_off = pad_end - padded

    ri = lax.broadcasted_iota(jnp.int32, (DISPATCH_BLOCK, DISPATCH_BLOCK), 0)
    ci = lax.broadcasted_iota(jnp.int32, (DISPATCH_BLOCK, DISPATCH_BLOCK), 1)
    before = jnp.where(ci < ri, 1.0, 0.0).astype(BF16)

    def place(b, run):
        oh = [one_hot(b, k) for k in range(TOP_K)]
        base = pad_off + run
        out = jnp.zeros((DISPATCH_BLOCK, LANES), F32)
        lane_b = lax.broadcasted_iota(jnp.int32, (DISPATCH_BLOCK, LANES), 1)
        for k in range(TOP_K):
            prior = jnp.dot(before, oh[k].astype(BF16), preferred_element_type=F32)
            pos = jnp.sum(oh[k] * (base + prior), axis=1, keepdims=True)
            out = jnp.where(lane_b == k, pos, out)
            base = base + jnp.sum(oh[k], axis=0, keepdims=True)
        r = pl.ds(pl.multiple_of(b * DISPATCH_BLOCK, DISPATCH_BLOCK), DISPATCH_BLOCK)
        pos_ref[r, :] = out.astype(jnp.int32)
        return base - pad_off

    lax.fori_loop(0, nb, place, jnp.zeros((1, LANES), F32))

    end_col = jnp.transpose(jnp.broadcast_to(pad_end, (8, LANES)))[:, 0:1]
    e_col = lax.broadcasted_iota(jnp.int32, (LANES, 1), 0)
    tile_start = lane_f * MOE_TILE
    passed = jnp.where(jnp.logical_and(end_col <= tile_start, e_col < N_EXPERTS), 1.0, 0.0)
    tile_expert = jnp.minimum(jnp.sum(passed, axis=0, keepdims=True), N_EXPERTS - 1.0)
    total = jnp.sum(jnp.where(lane == N_EXPERTS - 1, pad_end, 0.0), axis=1, keepdims=True)
    tile_valid = jnp.where(tile_start < total, 1.0, 0.0)
    row = lax.broadcasted_iota(jnp.int32, (8, LANES), 0)
    tbl = jnp.zeros((8, LANES), F32)
    for idx, val in ((TBL_EXPERT, tile_expert), (TBL_VALID, tile_valid), (TBL_COUNT, counts), (TBL_OFFSET, pad_off)):
        tbl = jnp.where(row == idx, val, tbl)
    tbl_ref[...] = tbl.astype(jnp.int32)


def _dispatch(route):
    n_tok = route.shape[0]
    return pl.pallas_call(
        functools.partial(_dispatch_body, n_tok),
        out_shape=[jax.ShapeDtypeStruct((n_tok, LANES), jnp.int32), jax.ShapeDtypeStruct((8, LANES), jnp.int32)],
        compiler_params=pltpu.CompilerParams(vmem_limit_bytes=VMEM_LIMIT),
        name="dispatch",
    )(route)


def _scatter_body(tm, n_tiles, pos_ref, cnt_ref, off_ref, h_ref, o_hbm, zbuf, sem):
    i = pl.program_id(0)

    @pl.when(i == 0)
    def _():
        zbuf[...] = jnp.zeros_like(zbuf)
        last = N_EXPERTS - 1
        first_empty = (off_ref[last] + cnt_ref[last] + MOE_TILE - 1) // MOE_TILE

        def tile_copy(t):
            rows = pl.ds(pl.multiple_of(t * MOE_TILE, MOE_TILE), MOE_TILE)
            return pltpu.make_async_copy(zbuf, o_hbm.at[rows, :], sem.at[1])

        def partial_tile(e, start):
            cnt = cnt_ref[e]

            @pl.when(cnt % MOE_TILE != 0)
            def _():
                cp = tile_copy((off_ref[e] + cnt) // MOE_TILE)
                if start:
                    cp.start()
                else:
                    cp.wait()

        def zero_partial(e, carry):
            partial_tile(e, True)
            return carry

        def partial_done(e, carry):
            partial_tile(e, False)
            return carry

        def zero_tile(t, carry):
            tile_copy(t).start()
            return carry

        def zero_done(t, carry):
            tile_copy(t).wait()
            return carry

        lax.fori_loop(0, N_EXPERTS, zero_partial, 0)
        lax.fori_loop(first_empty, n_tiles, zero_tile, 0)
        lax.fori_loop(0, N_EXPERTS, partial_done, 0)
        lax.fori_loop(first_empty, n_tiles, zero_done, 0)

    def row_copies(r):
        return [pltpu.make_async_copy(h_ref.at[pl.ds(r, 1), :],
                                      o_hbm.at[pl.ds(pos_ref[(i * tm + r) * TOP_K + k], 1), :], sem.at[0])
                for k in range(TOP_K)]

    def start(r, carry):
        for cp in row_copies(r):
            cp.start()
        return carry

    def wait(r, carry):
        for cp in row_copies(r):
            cp.wait()
        return carry

    lax.fori_loop(0, tm, start, 0, unroll=8)
    lax.fori_loop(0, tm, wait, 0, unroll=8)


def _scatter_rows(h2, pos_flat, counts, offsets, n_rows):
    n_tok, d = h2.shape
    tm = 256
    grid_spec = pltpu.PrefetchScalarGridSpec(
        num_scalar_prefetch=3,
        grid=(n_tok // tm,),
        in_specs=[pl.BlockSpec((tm, d), lambda i, pos, cnt, off: (i, 0))],
        out_specs=pl.BlockSpec(memory_space=pl.ANY),
        scratch_shapes=[pltpu.VMEM((MOE_TILE, d), F32), pltpu.SemaphoreType.DMA((2,))],
    )
    return pl.pallas_call(
        functools.partial(_scatter_body, tm, n_rows // MOE_TILE),
        grid_spec=grid_spec,
        out_shape=jax.ShapeDtypeStruct((n_rows, d), F32),
        compiler_params=_params("arbitrary"),
        name="scatter_rows",
    )(pos_flat, counts, offsets, h2)


def _moe_body(n_tiles, te_ref, tv_ref, cnt_ref, x_ref, wg_hbm, wu_hbm, wd_hbm, y_ref,
              wg_f, wu_f, wd_f, wg_b, wu_b, wd_b, slot_ref, sem):
    i = pl.program_id(0)
    prev = jnp.maximum(i - 1, 0)
    valid = tv_ref[i] == 1
    e = te_ref[i]

    def fetch(expert, slot):
        return [pltpu.make_async_copy(src.at[expert], dst.at[slot], sem.at[slot])
                for src, dst in ((wg_hbm, wg_f), (wu_hbm, wu_f), (wd_hbm, wd_f))]

    @pl.when(jnp.logical_and(valid, i == 0))
    def _():
        slot_ref[0] = 1
        for cp in fetch(e, 0):
            cp.start()

    @pl.when(jnp.logical_and(valid, jnp.logical_or(i == 0, e != te_ref[prev])))
    def _():
        slot = 1 - slot_ref[0]
        slot_ref[0] = slot
        for cp in fetch(e, slot):
            cp.wait()
        nxt = jnp.minimum(i + (cnt_ref[e] + MOE_TILE - 1) // MOE_TILE, n_tiles - 1)

        @pl.when(jnp.logical_and(nxt > i, tv_ref[nxt] == 1))
        def _():
            for cp in fetch(te_ref[nxt], 1 - slot):
                cp.start()

        wg_b[...] = wg_f[slot].astype(BF16)
        wu_b[...] = wu_f[slot].astype(BF16)
        wd_b[...] = wd_f[slot].astype(BF16)

    @pl.when(valid)
    def _():
        x = x_ref[...].astype(BF16)
        g = jnp.dot(x, wg_b[...], preferred_element_type=F32)
        u = jnp.dot(x, wu_b[...], preferred_element_type=F32)
        y_ref[...] = jnp.dot((_silu(g) * u).astype(BF16), wd_b[...], preferred_element_type=F32)

    @pl.when(tv_ref[i] == 0)
    def _():
        y_ref[...] = jnp.zeros_like(y_ref)


def _moe(x_sorted, tile_expert, tile_valid, counts, w_g, w_u, w_d):
    n_tiles = tile_expert.shape[0]
    d = x_sorted.shape[1]
    f = w_g.shape[-1]
    hbm = pl.BlockSpec(memory_space=pl.ANY)
    grid_spec = pltpu.PrefetchScalarGridSpec(
        num_scalar_prefetch=3,
        grid=(n_tiles,),
        in_specs=[pl.BlockSpec((MOE_TILE, d), lambda i, te, tv, cnt: (i * tv[i], 0)), hbm, hbm, hbm],
        out_specs=pl.BlockSpec((MOE_TILE, d), lambda i, te, tv, cnt: (i, 0)),
        scratch_shapes=[pltpu.VMEM((2, d, f), F32), pltpu.VMEM((2, d, f), F32), pltpu.VMEM((2, f, d), F32),
                        pltpu.VMEM((d, f), BF16), pltpu.VMEM((d, f), BF16), pltpu.VMEM((f, d), BF16),
                        pltpu.SMEM((1,), jnp.int32), pltpu.SemaphoreType.DMA((2,))],
    )
    return pl.pallas_call(
        functools.partial(_moe_body, n_tiles),
        grid_spec=grid_spec,
        out_shape=jax.ShapeDtypeStruct((n_tiles * MOE_TILE, d), F32),
        compiler_params=_params("arbitrary"),
        name="moe",
    )(tile_expert, tile_valid, counts, x_sorted, w_g, w_u, w_d)


def _combine_body(tm, n_tiles, row0, pos_ref, y_hbm, rt_ref, x1_ref, gt_ref, gf_ref, o_ref, ybuf, sem):
    i = pl.program_id(0)
    slot = i % 2

    def gather(tile, dst_slot, start):
        def row(r, carry):
            for k in range(TOP_K):
                p = pos_ref[(row0 + tile * tm + r) * TOP_K + k]
                cp = pltpu.make_async_copy(y_hbm.at[pl.ds(p, 1), :], ybuf.at[dst_slot, k, pl.ds(r, 1), :],
                                           sem.at[dst_slot])
                if start:
                    cp.start()
                else:
                    cp.wait()
            return carry

        lax.fori_loop(0, tm, row, 0, unroll=8)

    @pl.when(i == 0)
    def _():
        gather(0, 0, True)

    @pl.when(i + 1 < n_tiles)
    def _():
        gather(i + 1, 1 - slot, True)

    gather(i, slot, False)
    rt = rt_ref[...]
    y = rt[:, ROUTE_W:ROUTE_W + 1] * ybuf[slot, 0] + rt[:, ROUTE_W + 1:ROUTE_W + 2] * ybuf[slot, 1]
    x2 = x1_ref[...] + gt_ref[0] * y
    o_ref[...] = _rms_scale(x2) * gf_ref[...]


def _combine(y_sorted, pos_flat, route, x1, gt2, gain_f, row0, n_rows, mod_of_tile, tm):
    d = x1.shape[1]
    rb = row0 // tm
    n_tiles = n_rows // tm
    grid_spec = pltpu.PrefetchScalarGridSpec(
        num_scalar_prefetch=1,
        grid=(n_tiles,),
        in_specs=[pl.BlockSpec(memory_space=pl.ANY),
                  pl.BlockSpec((tm, LANES), lambda i, pos: (rb + i, 0)),
                  pl.BlockSpec((tm, d), lambda i, pos: (rb + i, 0)),
                  pl.BlockSpec((1, 1, d), lambda i, pos: (mod_of_tile(i), 0, 0)),
                  pl.BlockSpec((1, d), lambda i, pos: (0, 0))],
        out_specs=pl.BlockSpec((tm, d), lambda i, pos: (i, 0)),
        scratch_shapes=[pltpu.VMEM((2, TOP_K, tm, d), F32), pltpu.SemaphoreType.DMA((2,))],
    )
    return pl.pallas_call(
        functools.partial(_combine_body, tm, n_tiles, row0),
        grid_spec=grid_spec,
        out_shape=jax.ShapeDtypeStruct((n_rows, d), F32),
        compiler_params=_params("arbitrary"),
        name="combine",
    )(pos_flat, y_sorted, route, x1, gt2, gain_f)


GATE_SRC = {"beta": 0, "gc": 2 * H_A, "egc": 4 * H_A, "ekd": 6 * H_A}


def _gates_body(zs_ref, al_ref, dt_ref, sel_ref, o_ref):
    tm = zs_ref.shape[0]
    lane = lax.broadcasted_iota(jnp.int32, (1, LANES), 1)
    zs = zs_ref[...]
    x = zs + dt_ref[...]
    softplus = jnp.maximum(x, 0.0) + jnp.log(1.0 + jnp.exp(-jnp.abs(x)))
    in_g = jnp.logical_and(lane >= GATE_SRC["gc"], lane < GATE_SRC["egc"])
    g = jnp.where(in_g, -jnp.exp(al_ref[...]) * softplus, 0.0)
    beta = jax.nn.sigmoid(zs)
    reversed_lane = lane >= GATE_SRC["gc"] + H_A
    ri = lax.broadcasted_iota(jnp.int32, (CHUNK, CHUNK), 0)
    ci = lax.broadcasted_iota(jnp.int32, (CHUNK, CHUNK), 1)
    prefix = jnp.where(ci <= ri, 1.0, 0.0)
    suffix = jnp.where(ci >= ri, 1.0, 0.0)
    exact = dict(preferred_element_type=F32, precision=lax.Precision.HIGHEST)
    for c in range(tm // CHUNK):
        r = slice(c * CHUNK, (c + 1) * CHUNK)
        gch = g[r]
        gc = jnp.where(reversed_lane, jnp.dot(suffix, gch, **exact), jnp.dot(prefix, gch, **exact))
        tot = jnp.sum(gch, axis=0, keepdims=True)
        src = jnp.where(lane < GATE_SRC["gc"], beta[r],
                        jnp.where(lane < GATE_SRC["egc"], gc,
                                  jnp.where(lane < GATE_SRC["ekd"], pltpu.roll(jnp.exp(gc), 2 * H_A, 1),
                                            jnp.where(lane < GATE_SRC["ekd"] + 2 * H_A,
                                                      pltpu.roll(jnp.exp(tot - gc), 4 * H_A, 1), 0.0))))
        o_ref[r, :] = jnp.dot(src, sel_ref[...], **exact)


def _gate_table(zs, a_log, dt_bias):
    n_tok = zs.shape[0]
    tm = 512
    place = lambda v: jnp.pad(v.reshape(1, 2 * H_A), ((0, 0), (GATE_SRC["gc"], LANES - GATE_SRC["egc"])))
    src = jnp.arange(LANES)[:, None]
    dst = jnp.arange(H_A * LANES)[None, :]
    head, within = dst // LANES, dst % LANES
    kind_src = jnp.array([GATE_SRC["gc"], GATE_SRC["beta"], GATE_SRC["egc"], GATE_SRC["ekd"]])
    want = kind_src[within % GATE_KINDS] + (within // GATE_KINDS) * H_A + head
    sel = jnp.where(jnp.logical_and(within < 2 * GATE_KINDS, src == want), 1.0, 0.0).astype(F32)
    return pl.pallas_call(
        _gates_body,
        grid=(n_tok // tm,),
        in_specs=[pl.BlockSpec((tm, LANES), lambda i: (i, 0)),
                  pl.BlockSpec((1, LANES), lambda i: (0, 0)),
                  pl.BlockSpec((1, LANES), lambda i: (0, 0)),
                  pl.BlockSpec((LANES, H_A * LANES), lambda i: (0, 0))],
        out_specs=pl.BlockSpec((tm, H_A * LANES), lambda i: (i, 0)),
        out_shape=jax.ShapeDtypeStruct((n_tok, H_A * LANES), F32),
        compiler_params=_params("arbitrary"),
        name="gates",
    )(zs, place(a_log), place(dt_bias), sel)


def kernel(x_prompt, x_sample, c, state_delta, cache_k, cache_v, c_ctx, w_ada, b_ada, norm_mix, norm_ffn, w_in,
           conv_a, a_log, dt_bias, onorm_a, lam, subln_b, w_up_a, w_up_b, w_o, w_rg, b_rg, w_re, b_re,
           w_e_gate, w_e_up, w_e_down, norm_final):
    assert w_in.shape[0] == 1, "single trunk layer"
    l = 0
    lam_init = 0.8 - 0.6 * math.exp(-0.3 * l)
    bc, tc, d = x_prompt.shape
    bl, tl, _ = x_sample.shape
    n_ctx, n_lat = bc * tc, bl * tl
    xp = x_prompt.reshape(n_ctx, d)
    xs = x_sample.reshape(n_lat, d)

    cvec = jnp.zeros((N_MOD_ROWS, d), F32).at[0].set(c_ctx).at[1:1 + bl].set(c)
    mod = _adaln(cvec, w_ada[l], b_ada[l][None, :])
    sh1, sc1, gt1, sh2, sc2, gt2 = [m.reshape(N_MOD_ROWS, 1, d) for m in jnp.split(mod, 6, axis=1)]

    rows = _Rows(n_ctx, n_lat, tl, 512)
    h1, zs = _prenorm(rows, xp, xs, norm_mix[l][None, :], sh1, sc1, w_in[l])
    z = _inproj(h1, w_in[l])

    gates = _gate_table(zs, a_log[l], dt_bias[l])
    oa_c, new_state = _deltanet(z, conv_a[l], gates, None, bc, tc, 0, 8)
    oa_l = _deltanet(z, conv_a[l], gates, state_delta, bl, tl, n_ctx, 4)

    lv = lam[l]
    lam_val = jnp.exp(jnp.sum(lv[0] * lv[1])) - jnp.exp(jnp.sum(lv[2] * lv[3])) + lam_init
    lam_row = jnp.full((1, LANES), lam_val, F32)
    sub_row = subln_b[l][None, :]
    ob_c, new_k, new_v = _attn_ctx(z, lam_row, sub_row, bc, tc, lam_init)
    cos, sin = _rope_tables(tl)
    ob_l = _attn_lat(z, cache_k, cache_v, cos, sin, lam_row, sub_row, bl, tl, n_ctx, lam_init)

    mixed = _merge(rows, z, oa_c, oa_l, ob_c, ob_l, onorm_a[l][None, :],
                   w_up_a[l].astype(BF16), w_up_b[l].astype(BF16))
    w_r = jnp.pad(jnp.concatenate([w_rg[l], w_re[l]], axis=1), ((0, 0), (0, LANES - N_GROUPS - N_EXPERTS)))
    b_r = jnp.pad(jnp.concatenate([b_rg[l], b_re[l]]), (0, LANES - N_GROUPS - N_EXPERTS))[None, :]
    x1, h2, route = _outproj(rows, mixed, w_o[l].astype(BF16), xp, xs, gt1, norm_ffn[l][None, :], sh2, sc2, w_r, b_r)

    n_tok = n_ctx + n_lat
    n_tiles = (n_tok * TOP_K) // MOE_TILE + N_EXPERTS
    pos, tbl = _dispatch(route)
    pos_flat = pos[:, :TOP_K].reshape(-1)
    counts, offsets = tbl[TBL_COUNT, :N_EXPERTS], tbl[TBL_OFFSET, :N_EXPERTS]
    x_sorted = _scatter_rows(h2, pos_flat, counts, offsets, n_tiles * MOE_TILE)
    y_sorted = _moe(x_sorted, tbl[TBL_EXPERT, :n_tiles], tbl[TBL_VALID, :n_tiles], counts,
                    w_e_gate[l], w_e_up[l], w_e_down[l])
    gf = norm_final[None, :]
    tm_c = 256
    y_prompt = _combine(y_sorted, pos_flat, route, x1, gt2, gf, 0, n_ctx, lambda i: 0 * i, tm_c)
    y_sample = _combine(y_sorted, pos_flat, route, x1, gt2, gf, n_ctx, n_lat, lambda i: 1 + i // (tl // tm_c), tm_c)
    return (y_prompt.reshape(bc, tc, d), y_sample.reshape(bl, tl, d), new_state, new_k, new_v)
```

```python
import functools
import math

import jax
import jax.numpy as jnp
from jax import lax
from jax.experimental import pallas as pl
from jax.experimental.pallas import tpu as pltpu

F32 = jnp.float32
BF16 = jnp.bfloat16

D_MODEL = 2048
GRID_W = 64
H_A = 8
DK_A = 128
DV_A = 128
CONV_K = 3
CHUNK = 64
H_B = 8
DQK_B = 64
DV_B = 2 * DQK_B
ROPE_BASE = 10000.0
N_GROUPS = 4
E_PER_GROUP = 8
N_EXPERTS = N_GROUPS * E_PER_GROUP
TOP_K = 2
D_FF_E = D_MODEL // 4
EPS = 1e-6

LANES = 128
QKV_A = 2 * H_A * DK_A + H_A * DV_A
N_SMALL = 4 * H_A
Z_GATE_A = QKV_A
Z_QB = Z_GATE_A + H_A * DV_A
Z_KB = Z_QB + H_B * 2 * DQK_B
Z_VB = Z_KB + H_B * 2 * DQK_B
Z_GM = Z_VB + H_B * DV_B
N_MAIN = Z_GM + 2 * D_MODEL

N_MOD_ROWS = 8
MOE_TILE = 256
VMEM_LIMIT = 56 * 1024 * 1024

NT_DIMS = (((1,), (1,)), ((), ()))


def _params(*sem):
    return pltpu.CompilerParams(dimension_semantics=sem, vmem_limit_bytes=VMEM_LIMIT)


def _mm(a, b):
    return jnp.dot(a.astype(BF16), b.astype(BF16), preferred_element_type=F32)


def _mm_nt(a, b):
    return lax.dot_general(a.astype(BF16), b.astype(BF16), NT_DIMS, preferred_element_type=F32)


def _silu(x):
    return x * jax.nn.sigmoid(x)


def _rms_scale(x):
    return x * lax.rsqrt(jnp.mean(x * x, axis=-1, keepdims=True) + EPS)


def _adaln_body(c_ref, w_ref, b_ref, o_ref):
    s = _silu(c_ref[...])
    o_ref[...] = _mm(s, w_ref[...]) + b_ref[...]


def _adaln(cvec, w, b):
    d, n = w.shape
    tn = 1024
    return pl.pallas_call(
        _adaln_body,
        grid=(n // tn,),
        in_specs=[pl.BlockSpec((N_MOD_ROWS, d), lambda j: (0, 0)),
                  pl.BlockSpec((d, tn), lambda j: (0, j)),
                  pl.BlockSpec((1, tn), lambda j: (0, j))],
        out_specs=pl.BlockSpec((N_MOD_ROWS, tn), lambda j: (0, j)),
        out_shape=jax.ShapeDtypeStruct((N_MOD_ROWS, n), F32),
        compiler_params=_params("arbitrary"),
        name="adaln",
    )(cvec, w, b)


class _Rows:
    def __init__(self, n_ctx, n_lat, t_lat, tm):
        assert n_ctx % tm == 0 and n_lat % tm == 0 and t_lat % tm == 0
        self.tm = tm
        self.nct = n_ctx // tm
        self.nlt = n_lat // tm
        self.per_seq = t_lat // tm
        self.n = self.nct + self.nlt

    def ctx_idx(self, i):
        return jnp.minimum(i, self.nct - 1)

    def lat_idx(self, i):
        return jnp.maximum(i - self.nct, 0)

    def mod_idx(self, i):
        return jnp.where(i < self.nct, 0, 1 + (i - self.nct) // self.per_seq)


SMALL_COL0 = QKV_A + H_A * DV_A


def _prenorm_body(nct, xp_ref, xs_ref, g_ref, sh_ref, sc_ref, ws_ref, h_ref, zs_ref):
    i = pl.program_id(0)

    def run(x_ref):
        h = _rms_scale(x_ref[...]) * g_ref[...]
        h = (h * (1.0 + sc_ref[0]) + sh_ref[0]).astype(BF16)
        h_ref[...] = h
        zs_ref[...] = lax.dot_general(h, ws_ref[...].astype(BF16), NT_DIMS, preferred_element_type=F32)

    @pl.when(i < nct)
    def _():
        run(xp_ref)

    @pl.when(i >= nct)
    def _():
        run(xs_ref)


def _prenorm(rows, xp, xs, gain, sh, sc, w_in_t):
    d = xp.shape[1]
    tm = rows.tm
    n_tok = rows.n * tm
    return pl.pallas_call(
        functools.partial(_prenorm_body, rows.nct),
        grid=(rows.n,),
        in_specs=[pl.BlockSpec((tm, d), lambda i: (rows.ctx_idx(i), 0)),
                  pl.BlockSpec((tm, d), lambda i: (rows.lat_idx(i), 0)),
                  pl.BlockSpec((1, d), lambda i: (0, 0)),
                  pl.BlockSpec((1, 1, d), lambda i: (rows.mod_idx(i), 0, 0)),
                  pl.BlockSpec((1, 1, d), lambda i: (rows.mod_idx(i), 0, 0)),
                  pl.BlockSpec((LANES, d), lambda i: (SMALL_COL0 // LANES, 0))],
        out_specs=[pl.BlockSpec((tm, d), lambda i: (i, 0)),
                   pl.BlockSpec((tm, LANES), lambda i: (i, 0))],
        out_shape=[jax.ShapeDtypeStruct((n_tok, d), BF16),
                   jax.ShapeDtypeStruct((n_tok, LANES), F32)],
        compiler_params=_params("arbitrary"),
        name="prenorm",
    )(xp, xs, gain, sh, sc, w_in_t)


INPROJ_TN = 1024
N_ALIGNED = SMALL_COL0 // INPROJ_TN


def _inproj_body(h_ref, wt_ref, z_ref, w_scr):
    @pl.when(pl.program_id(1) == 0)
    def _():
        w_scr[...] = wt_ref[...].astype(BF16)

    z_ref[...] = lax.dot_general(h_ref[...], w_scr[...], NT_DIMS, preferred_element_type=F32)


def _inproj(h, w_in_t):
    n_tok, d = h.shape
    tm, tn = 1024, INPROJ_TN
    first_row = lambda j: pl.multiple_of(j * tn + N_SMALL * jnp.minimum(j // N_ALIGNED, 1), N_SMALL)
    return pl.pallas_call(
        _inproj_body,
        grid=(N_MAIN // tn, n_tok // tm),
        in_specs=[pl.BlockSpec((tm, d), lambda j, i: (i, 0)),
                  pl.BlockSpec((pl.Element(tn), pl.Element(d)), lambda j, i: (first_row(j), 0))],
        out_specs=pl.BlockSpec((tm, tn), lambda j, i: (i, j)),
        out_shape=jax.ShapeDtypeStruct((n_tok, N_MAIN), F32),
        scratch_shapes=[pltpu.VMEM((tn, d), BF16)],
        compiler_params=_params("arbitrary", "arbitrary"),
        name="inproj",
    )(h, w_in_t)


GATE_KINDS = 4
INV_BLOCK = 16


def _chunk_solve(chains, ii, jj):
    eye = ii == jj
    blk = (ii // INV_BLOCK) == (jj // INV_BLOCK)
    blk2 = (ii // (2 * INV_BLOCK)) == (jj // (2 * INV_BLOCK))
    cols, a_qk, a0, off, d0, rhs = [], [], [], [], [], []
    for kk, qk, v, k, gates, reverse in chains:
        base = GATE_KINDS if reverse else 0
        gc, beta, egc, ekd = (gates[:, base + i:base + i + 1] for i in range(GATE_KINDS))
        incl = (jj >= ii) if reverse else (jj <= ii)
        strict = (jj > ii) if reverse else (jj < ii)
        gc_row = jnp.sum(jnp.where(eye, gc, 0.0), axis=0, keepdims=True)
        dec = jnp.where(incl, jnp.exp(jnp.where(incl, gc - gc_row, 0.0)), 0.0)
        a = jnp.where(strict, kk * dec, 0.0)
        cols.append((egc, ekd))
        a_qk.append(qk * dec)
        a0.append(jnp.where(blk, a, 0.0))
        off.append(jnp.where(blk, 0.0, a))
        d0.append(jnp.where(eye, 1.0, 0.0) - a0[-1])
        rhs.append(jnp.concatenate([v * beta, k * (beta * egc)], axis=1))
    p = [_mm(x, x) for x in a0]
    for _ in range(int(math.log2(INV_BLOCK)) - 2):
        r = [_mm(jnp.concatenate([pi, di], axis=0), pi) for pi, di in zip(p, d0)]
        p = [ri[:CHUNK] for ri in r]
        d0 = [di + ri[CHUNK:] for di, ri in zip(d0, r)]
    d0 = [di + _mm(di, pi) for di, pi in zip(d0, p)]
    wl = [_mm(di, oi) for di, oi in zip(d0, off)]
    yr = [_mm(di, ri) for di, ri in zip(d0, rhs)]
    x1 = [jnp.where(blk2, x, 0.0) for x in wl]
    yl = [jnp.where(blk2, 0.0, x) for x in wl]
    zl = [y - _mm(x, y) for x, y in zip(x1, yl)]
    zr = [y - _mm(x, y) for x, y in zip(x1, yr)]
    sol = [y - _mm(x, y) for x, y in zip(zl, zr)]
    return [(so[:, :DV_A], so[:, DV_A:], aq, egc, ekd) for so, aq, (egc, ekd) in zip(sol, a_qk, cols)]


def _delta_body(t_len, hp, has_s0, q_ref, k_ref, v_ref, cq_ref, ck_ref, cv_ref, g_ref, *rest):
    if has_s0:
        s0_ref, o_ref, qs, ks, vs, u_scr, wq_scr, ak_scr, s_scr = rest
    else:
        o_ref, sfin_ref, qs, ks, vs, u_scr, wq_scr, ak_scr, s_scr = rest
    n = t_len // CHUNK
    tpos = lax.broadcasted_iota(jnp.int32, (t_len, 1), 0)

    def conv_act(x, w):
        x_prev = jnp.where(tpos == 0, 0.0, pltpu.roll(x, 1, 0))
        x_next = jnp.where(tpos == t_len - 1, 0.0, pltpu.roll(x, t_len - 1, 0))
        return _silu(x_prev * w[0:1] + x * w[1:2] + x_next * w[2:3])

    def l2n(x):
        return x * lax.rsqrt(jnp.sum(x * x, axis=-1, keepdims=True) + EPS)

    for hh in range(hp):
        cols = slice(hh * LANES, (hh + 1) * LANES)
        qs[:, cols] = l2n(conv_act(q_ref[:, cols], cq_ref[:, cols])) * (DK_A ** -0.5)
        ks[:, cols] = l2n(conv_act(k_ref[:, cols], ck_ref[:, cols]))
        vs[:, cols] = conv_act(v_ref[:, cols], cv_ref[:, cols])
    if has_s0:
        s_scr[...] = s0_ref[...]
    else:
        s_scr[...] = jnp.zeros_like(s_scr)
    o_ref[...] = jnp.zeros_like(o_ref)
    ii = lax.broadcasted_iota(jnp.int32, (CHUNK, CHUNK), 0)
    jj = lax.broadcasted_iota(jnp.int32, (CHUNK, CHUNK), 1)

    def prep(c, carry):
        r = pl.ds(pl.multiple_of(c * CHUNK, CHUNK), CHUNK)
        heads = []
        for hh in range(hp):
            cols = slice(hh * LANES, (hh + 1) * LANES)
            heads.append((qs[r, cols], ks[r, cols], vs[r, cols], g_ref[r, cols]))
        rr = [_mm_nt(jnp.concatenate([k * g[:, 1:2], k * g[:, GATE_KINDS + 1:GATE_KINDS + 2], q], axis=0), k)
              for q, k, v, g in heads]
        chains = [(rr[hh][d * CHUNK:(d + 1) * CHUNK], rr[hh][2 * CHUNK:], heads[hh][2], heads[hh][1], heads[hh][3],
                   d == 1) for hh in range(hp) for d in (0, 1)]
        solved = _chunk_solve(chains, ii, jj)
        for hh in range(hp):
            cols = slice(hh * LANES, (hh + 1) * LANES)
            q, k = heads[hh][0], heads[hh][1]
            for d in (0, 1):
                u, w, a_qk, egc, ekd = solved[hh * 2 + d]
                slot = (d * hp + hh) * n + c
                u_scr[d, r, cols] = u
                wq_scr[slot] = jnp.concatenate([w, q * egc], axis=0).astype(BF16)
                ak_scr[slot] = jnp.concatenate([a_qk, (k * ekd).T], axis=0).astype(BF16)
        return carry

    lax.fori_loop(0, n, prep, 0)

    def scan(c, carry):
        chains = []
        for hh in range(hp):
            for d in (0, 1):
                cc = c if d == 0 else n - 1 - c
                chains.append((hh, d, pl.multiple_of(cc * CHUNK, CHUNK), (d * hp + hh) * n + cc))
        s = [s_scr[d, hh] for hh, d, r0, slot in chains]
        r1 = [jnp.dot(wq_scr[slot], si.astype(BF16), preferred_element_type=F32)
              for si, (hh, d, r0, slot) in zip(s, chains)]
        v_new = [u_scr[d, pl.ds(r0, CHUNK), hh * LANES:(hh + 1) * LANES] - ri[:CHUNK]
                 for ri, (hh, d, r0, slot) in zip(r1, chains)]
        r2 = [jnp.dot(ak_scr[slot], vi.astype(BF16), preferred_element_type=F32)
              for vi, (hh, d, r0, slot) in zip(v_new, chains)]
        for si, r1i, r2i, (hh, d, r0, slot) in zip(s, r1, r2, chains):
            o_ref[pl.ds(r0, CHUNK), hh * LANES:(hh + 1) * LANES] += r1i[CHUNK:] + r2i[:CHUNK]
            lane = hh * LANES + d * GATE_KINDS + 2
            edge = g_ref[pl.ds(pl.multiple_of(r0 + (0 if d else CHUNK - 8), 8), 8), lane:lane + 1]
            s_scr[d, hh] = si * (edge[0:1] if d else edge[7:8]) + r2i[CHUNK:]
        return carry

    lax.fori_loop(0, n, scan, 0)
    if not has_s0:
        sfin_ref[...] = s_scr[...]


def _deltanet(z, conv_w, gates, s0, n_seq, t_len, row0, hp):
    rb = row0 // t_len
    w = hp * LANES
    n = t_len // CHUNK
    seq_blk = lambda col0: pl.BlockSpec((t_len, w), lambda b, h: (rb + b, col0 // hp + h))
    cw_blk = lambda col0: pl.BlockSpec((CONV_K, w), lambda b, h: (0, col0 // hp + h))
    state_blk = pl.BlockSpec((None, None, 2, hp, DK_A, DV_A), lambda b, h: (b, 0, 0, h, 0, 0))
    in_specs = [seq_blk(0), seq_blk(H_A), seq_blk(2 * H_A), cw_blk(0), cw_blk(H_A), cw_blk(2 * H_A),
                pl.BlockSpec((t_len, w), lambda b, h: (rb + b, h))]
    args = [z, z, z, conv_w, conv_w, conv_w, gates]
    o_spec = pl.BlockSpec((t_len, w), lambda b, h: (b, h))
    o_shape = jax.ShapeDtypeStruct((n_seq * t_len, H_A * DV_A), F32)
    has_s0 = s0 is not None
    if has_s0:
        in_specs += [state_blk]
        args += [s0]
        out_specs, out_shape = o_spec, o_shape
    else:
        out_specs = [o_spec, state_blk]
        out_shape = [o_shape, jax.ShapeDtypeStruct((n_seq, 1, 2, H_A, DK_A, DV_A), F32)]

    return pl.pallas_call(
        functools.partial(_delta_body, t_len, hp, has_s0),
        grid=(n_seq, H_A // hp),
        in_specs=in_specs,
        out_specs=out_specs,
        out_shape=out_shape,
        scratch_shapes=[pltpu.VMEM((t_len, w), F32)] * 3
        + [pltpu.VMEM((2, t_len, w), F32),
           pltpu.VMEM((2 * hp * n, 2 * CHUNK, DV_A), BF16),
           pltpu.VMEM((2 * hp * n, CHUNK + DK_A, CHUNK), BF16),
           pltpu.VMEM((2, hp, DK_A, DV_A), F32)],
        compiler_params=_params("arbitrary", "arbitrary"),
        name="deltanet_lat" if has_s0 else "deltanet_ctx",
    )(*args)


def _diff_attn(q, keys, vals, lam):
    qb = (q * (DQK_B ** -0.5)).astype(BF16)

    def probs(lo):
        s = lax.dot_general(qb[:, lo:lo + DQK_B], keys[:, lo:lo + DQK_B], NT_DIMS, preferred_element_type=F32)
        e = jnp.exp(s - jnp.max(s, axis=-1, keepdims=True))
        return e / jnp.sum(e, axis=-1, keepdims=True)

    a = probs(0) - lam * probs(DQK_B)
    return jnp.dot(a.astype(BF16), vals, preferred_element_type=F32)


def _subln(o, sub_ref, lam_init):
    return _rms_scale(o) * sub_ref[...] * (1.0 - lam_init)


def _attn_ctx_body(lam_init, q_ref, k_ref, v_ref, lam_ref, sub_ref, o_ref, ck_ref, cv_ref):
    heads = [slice(h * LANES, (h + 1) * LANES) for h in range(H_B)]
    lam = lam_ref[0:1, 0:1]
    ks = [k_ref[:, c] for c in heads]
    vs = [v_ref[:, c] for c in heads]
    for h in range(H_B):
        ck_ref[h] = ks[h]
        cv_ref[h] = vs[h]
    qb = [(q_ref[:, c] * (DQK_B ** -0.5)).astype(BF16) for c in heads]
    kb = [k.astype(BF16) for k in ks]
    probs = []
    for lo in (0, DQK_B):
        s = [lax.dot_general(q[:, lo:lo + DQK_B], k[:, lo:lo + DQK_B], NT_DIMS, preferred_element_type=F32)
             for q, k in zip(qb, kb)]
        e = [jnp.exp(x - jnp.max(x, axis=-1, keepdims=True)) for x in s]
        probs.append([x / jnp.sum(x, axis=-1, keepdims=True) for x in e])
    o = [jnp.dot((p1 - lam * p2).astype(BF16), v.astype(BF16), preferred_element_type=F32)
         for p1, p2, v in zip(probs[0], probs[1], vs)]
    for c, oh in zip(heads, o):
        o_ref[:, c] = _subln(oh, sub_ref, lam_init)


def _attn_ctx(z, lam, subln, n_seq, t_len, lam_init):
    n_tok = n_seq * t_len
    w = H_B * LANES
    blk = lambda col0: pl.BlockSpec((t_len, w), lambda b: (b, col0 // w))
    cache_blk = pl.BlockSpec((None, None, H_B, t_len, LANES), lambda b: (b, 0, 0, 0, 0))
    cache_shape = jax.ShapeDtypeStruct((n_seq, 1, H_B, t_len, LANES), F32)
    return pl.pallas_call(
        functools.partial(_attn_ctx_body, lam_init),
        grid=(n_seq,),
        in_specs=[blk(Z_QB), blk(Z_KB), blk(Z_VB),
                  pl.BlockSpec((1, LANES), lambda b: (0, 0)),
                  pl.BlockSpec((1, LANES), lambda b: (0, 0))],
        out_specs=[pl.BlockSpec((t_len, w), lambda b: (b, 0)), cache_blk, cache_blk],
        out_shape=[jax.ShapeDtypeStruct((n_tok, H_B * DV_B), F32), cache_shape, cache_shape],
        compiler_params=_params("arbitrary"),
        name="attn_ctx",
    )(z, z, z, lam, subln)


def _rope(x, cos, sin_signed):
    lane = lax.broadcasted_iota(jnp.int32, (1, LANES), 1)
    first = (lane % 32) < 16
    partner = jnp.where(first, pltpu.roll(x, LANES - 16, 1), pltpu.roll(x, 16, 1))
    return x * cos + partner * sin_signed


def _attn_lat_body(lam_init, n_past, q_ref, k_ref, v_ref, pk_ref, pv_ref, cosq_ref, sinq_ref, cos_ref, sin_ref,
                   lam_ref, sub_ref, o_ref, keys, vals):
    @pl.when(pl.program_id(2) == 0)
    def _():
        keys[0:n_past, :] = pk_ref[...].astype(BF16)
        vals[0:n_past, :] = pv_ref[...].astype(BF16)
        keys[n_past:, :] = _rope(k_ref[...], cos_ref[...], sin_ref[...]).astype(BF16)
        vals[n_past:, :] = v_ref[...].astype(BF16)

    q = _rope(q_ref[...], cosq_ref[...], sinq_ref[...])
    o = _diff_attn(q, keys[...], vals[...], lam_ref[0:1, 0:1])
    o_ref[...] = _subln(o, sub_ref, lam_init)


def _attn_lat(z, cache_k, cache_v, cos, sin, lam, subln, n_seq, t_len, row0, lam_init):
    tq = 256
    nq = t_len // tq
    n_past = cache_k.shape[3]
    rbq = row0 // tq
    rbs = row0 // t_len
    seq_blk = lambda col0: pl.BlockSpec((t_len, LANES), lambda b, h, qi: (rbs + b, col0 // LANES + h))
    past_blk = pl.BlockSpec((None, None, None, n_past, LANES), lambda b, h, qi: (b, 0, h, 0, 0))
    row_vec = pl.BlockSpec((1, LANES), lambda b, h, qi: (0, 0))
    return pl.pallas_call(
        functools.partial(_attn_lat_body, lam_init, n_past),
        grid=(n_seq, H_B, nq),
        in_specs=[pl.BlockSpec((tq, LANES), lambda b, h, qi: (rbq + b * nq + qi, Z_QB // LANES + h)),
                  seq_blk(Z_KB), seq_blk(Z_VB), past_blk, past_blk,
                  pl.BlockSpec((tq, LANES), lambda b, h, qi: (qi, 0)),
                  pl.BlockSpec((tq, LANES), lambda b, h, qi: (qi, 0)),
                  pl.BlockSpec((t_len, LANES), lambda b, h, qi: (0, 0)),
                  pl.BlockSpec((t_len, LANES), lambda b, h, qi: (0, 0)),
                  row_vec, row_vec],
        out_specs=pl.BlockSpec((tq, LANES), lambda b, h, qi: (b * nq + qi, h)),
        out_shape=jax.ShapeDtypeStruct((n_seq * t_len, H_B * DV_B), F32),
        scratch_shapes=[pltpu.VMEM((n_past + t_len, LANES), BF16)] * 2,
        compiler_params=_params("arbitrary", "arbitrary", "arbitrary"),
        name="attn_lat",
    )(z, z, z, cache_k, cache_v, cos, sin, cos, sin, lam, subln)


def _rope_tables(t_len):
    t = jnp.arange(t_len)
    pos = jnp.stack([t // GRID_W, t % GRID_W], axis=1).astype(F32)
    nf = DQK_B // 4
    inv_freq = ROPE_BASE ** (-jnp.arange(nf, dtype=F32) / nf)
    lane = jnp.arange(LANES)
    half = (lane % DQK_B) // (DQK_B // 2)
    ang = pos[:, half] * inv_freq[lane % nf][None, :]
    sign = jnp.where((lane % (DQK_B // 2)) < nf, -1.0, 1.0).astype(F32)
    return jnp.cos(ang), jnp.sin(ang) * sign[None, :]


def _merge_body(nct, oac_ref, oal_ref, ga_ref, on_ref, obc_ref, obl_ref, wa_ref, wb_ref, gma_ref, gmb_ref, m_ref,
                a_scr, b_scr):
    i = pl.program_id(0)
    first = pl.program_id(1) == 0

    def prologue(oa_ref, ob_ref):
        for h in range(H_A):
            c = slice(h * DV_A, (h + 1) * DV_A)
            a_scr[:, c] = (_rms_scale(oa_ref[:, c]) * on_ref[...] * _silu(ga_ref[:, c])).astype(BF16)
        b_scr[...] = ob_ref[...].astype(BF16)

    @pl.when(jnp.logical_and(first, i < nct))
    def _():
        prologue(oac_ref, obc_ref)

    @pl.when(jnp.logical_and(first, i >= nct))
    def _():
        prologue(oal_ref, obl_ref)

    ya = jnp.dot(a_scr[...], wa_ref[...], preferred_element_type=F32)
    yb = jnp.dot(b_scr[...], wb_ref[...], preferred_element_type=F32)
    m_ref[...] = (jax.nn.sigmoid(gma_ref[...]) * ya + jax.nn.sigmoid(gmb_ref[...]) * yb).astype(BF16)


def _merge(rows, z, oa_c, oa_l, ob_c, ob_l, onorm, w_up_a, w_up_b):
    n_tok = z.shape[0]
    tm = rows.tm
    d = w_up_a.shape[1]
    ka = w_up_a.shape[0]
    tn = 1024
    nj = d // tn
    ctx_blk = pl.BlockSpec((tm, ka), lambda i, j: (rows.ctx_idx(i), 0))
    lat_blk = pl.BlockSpec((tm, ka), lambda i, j: (rows.lat_idx(i), 0))
    return pl.pallas_call(
        functools.partial(_merge_body, rows.nct),
        grid=(n_tok // tm, nj),
        in_specs=[ctx_blk, lat_blk,
                  pl.BlockSpec((tm, ka), lambda i, j: (i, Z_GATE_A // ka)),
                  pl.BlockSpec((1, DV_A), lambda i, j: (0, 0)),
                  ctx_blk, lat_blk,
                  pl.BlockSpec((ka, tn), lambda i, j: (0, j)),
                  pl.BlockSpec((ka, tn), lambda i, j: (0, j)),
                  pl.BlockSpec((tm, tn), lambda i, j: (i, Z_GM // tn + j)),
                  pl.BlockSpec((tm, tn), lambda i, j: (i, Z_GM // tn + nj + j))],
        out_specs=pl.BlockSpec((tm, tn), lambda i, j: (i, j)),
        out_shape=jax.ShapeDtypeStruct((n_tok, d), BF16),
        scratch_shapes=[pltpu.VMEM((tm, ka), BF16)] * 2,
        compiler_params=_params("arbitrary", "arbitrary"),
        name="merge",
    )(oa_c, oa_l, z, onorm, ob_c, ob_l, w_up_a, w_up_b, z, z)


ROUTE_E = 0
ROUTE_W = TOP_K


def _route_rows(lg):
    lane = lax.broadcasted_iota(jnp.int32, lg.shape, 1)
    neg = -jnp.inf

    def first_max(x):
        m = jnp.max(x, axis=1, keepdims=True)
        return m, jnp.min(jnp.where(x == m, lane, LANES), axis=1, keepdims=True)

    gl = jnp.where(lane < N_GROUPS, lg, neg)
    gmax, g_idx = first_max(gl)
    pg_top = 1.0 / jnp.sum(jnp.exp(gl - gmax), axis=1, keepdims=True)
    lo = N_GROUPS + E_PER_GROUP * g_idx
    el = jnp.where(jnp.logical_and(lane >= lo, lane < lo + E_PER_GROUP), lg, neg)
    emax, i1 = first_max(el)
    esum = jnp.sum(jnp.exp(el - emax), axis=1, keepdims=True)
    e2max, i2 = first_max(jnp.where(lane == i1, neg, el))
    p1 = 1.0 / esum
    p2 = jnp.exp(e2max - emax) / esum
    den = p1 + p2
    vals = [(i1 - N_GROUPS).astype(F32), (i2 - N_GROUPS).astype(F32), pg_top * p1 / den, pg_top * p2 / den]
    out = jnp.zeros(lg.shape, F32)
    for pos, val in enumerate(vals):
        out = jnp.where(lane == pos, val, out)
    return out


def _outproj_body(nct, m_ref, wo_ref, xp_ref, xs_ref, gt_ref, g2_ref, sh_ref, sc_ref, wr_ref, br_ref,
                  x1_ref, h2_ref, rt_ref):
    i = pl.program_id(0)
    y = jnp.dot(m_ref[...], wo_ref[...], preferred_element_type=F32)

    def finish(x_ref):
        x1 = x_ref[...] + gt_ref[0] * y
        x1_ref[...] = x1
        h2 = _rms_scale(x1) * g2_ref[...]
        h2 = h2 * (1.0 + sc_ref[0]) + sh_ref[0]
        h2_ref[...] = h2
        lg = jnp.dot(h2, wr_ref[...], preferred_element_type=F32, precision=lax.Precision.HIGHEST) + br_ref[...]
        rt_ref[...] = _route_rows(lg)

    @pl.when(i < nct)
    def _():
        finish(xp_ref)

    @pl.when(i >= nct)
    def _():
        finish(xs_ref)


def _outproj(rows, mixed, w_o, xp, xs, gt1, gain2, sh2, sc2, w_r, b_r):
    d = xp.shape[1]
    tm = rows.tm
    n_tok = rows.n * tm
    mod = lambda: pl.BlockSpec((1, 1, d), lambda i: (rows.mod_idx(i), 0, 0))
    tok = pl.BlockSpec((tm, d), lambda i: (i, 0))
    return pl.pallas_call(
        functools.partial(_outproj_body, rows.nct),
        grid=(rows.n,),
        in_specs=[tok,
                  pl.BlockSpec((d, d), lambda i: (0, 0)),
                  pl.BlockSpec((tm, d), lambda i: (rows.ctx_idx(i), 0)),
                  pl.BlockSpec((tm, d), lambda i: (rows.lat_idx(i), 0)),
                  mod(),
                  pl.BlockSpec((1, d), lambda i: (0, 0)),
                  mod(), mod(),
                  pl.BlockSpec((d, LANES), lambda i: (0, 0)),
                  pl.BlockSpec((1, LANES), lambda i: (0, 0))],
        out_specs=[tok, tok, pl.BlockSpec((tm, LANES), lambda i: (i, 0))],
        out_shape=[jax.ShapeDtypeStruct((n_tok, d), F32), jax.ShapeDtypeStruct((n_tok, d), F32),
                   jax.ShapeDtypeStruct((n_tok, LANES), F32)],
        compiler_params=_params("arbitrary"),
        name="outproj",
    )(mixed, w_o, xp, xs, gt1, gain2, sh2, sc2, w_r, b_r)


DISPATCH_BLOCK = 256
TBL_EXPERT, TBL_VALID, TBL_COUNT, TBL_OFFSET = 0, 1, 2, 3


def _dispatch_body(n_tok, route_ref, pos_ref, tbl_ref):
    nb = n_tok // DISPATCH_BLOCK
    lane = lax.broadcasted_iota(jnp.int32, (1, LANES), 1)
    lane_f = lane.astype(F32)

    def one_hot(b, k):
        r = pl.ds(pl.multiple_of(b * DISPATCH_BLOCK, DISPATCH_BLOCK), DISPATCH_BLOCK)
        return jnp.where(route_ref[r, ROUTE_E + k:ROUTE_E + k + 1] == lane_f, 1.0, 0.0)

    def count(b, acc):
        return acc + jnp.sum(one_hot(b, 0) + one_hot(b, 1), axis=0, keepdims=True)

    counts = lax.fori_loop(0, nb, count, jnp.zeros((1, LANES), F32))
    padded = jnp.floor((counts + (MOE_TILE - 1)) * (1.0 / MOE_TILE)) * MOE_TILE
    pad_end = padded
    shift = 1
    while shift < LANES:
        pad_end = pad_end + jnp.where(lane >= shift, pltpu.roll(pad_end, shift, 1), 0.0)
        shift *= 2
    pad_off = pad_end - padded

    ri = lax.broadcasted_iota(jnp.int32, (DISPATCH_BLOCK, DISPATCH_BLOCK), 0)
    ci = lax.broadcasted_iota(jnp.int32, (DISPATCH_BLOCK, DISPATCH_BLOCK), 1)
    before = jnp.where(ci < ri, 1.0, 0.0).astype(BF16)

    def place(b, run):
        oh = [one_hot(b, k) for k in range(TOP_K)]
        base = pad_off + run
        out = jnp.zeros((DISPATCH_BLOCK, LANES), F32)
        lane_b = lax.broadcasted_iota(jnp.int32, (DISPATCH_BLOCK, LANES), 1)
        for k in range(TOP_K):
            prior = jnp.dot(before, oh[k].astype(BF16), preferred_element_type=F32)
            pos = jnp.sum(oh[k] * (base + prior), axis=1, keepdims=True)
            out = jnp.where(lane_b == k, pos, out)
            base = base + jnp.sum(oh[k], axis=0, keepdims=True)
        r = pl.ds(pl.multiple_of(b * DISPATCH_BLOCK, DISPATCH_BLOCK), DISPATCH_BLOCK)
        pos_ref[r, :] = out.astype(jnp.int32)
        return base - pad_off

    lax.fori_loop(0, nb, place, jnp.zeros((1, LANES), F32))

    end_col = jnp.transpose(jnp.broadcast_to(pad_end, (8, LANES)))[:, 0:1]
    e_col = lax.broadcasted_iota(jnp.int32, (LANES, 1), 0)
    tile_start = lane_f * MOE_TILE
    passed = jnp.where(jnp.logical_and(end_col <= tile_start, e_col < N_EXPERTS), 1.0, 0.0)
    tile_expert = jnp.minimum(jnp.sum(passed, axis=0, keepdims=True), N_EXPERTS - 1.0)
    total = jnp.sum(jnp.where(lane == N_EXPERTS - 1, pad_end, 0.0), axis=1, keepdims=True)
    tile_valid = jnp.where(tile_start < total, 1.0, 0.0)
    row = lax.broadcasted_iota(jnp.int32, (8, LANES), 0)
    tbl = jnp.zeros((8, LANES), F32)
    for idx, val in ((TBL_EXPERT, tile_expert), (TBL_VALID, tile_valid), (TBL_COUNT, counts), (TBL_OFFSET, pad_off)):
        tbl = jnp.where(row == idx, val, tbl)
    tbl_ref[...] = tbl.astype(jnp.int32)


def _dispatch(route):
    n_tok = route.shape[0]
    return pl.pallas_call(
        functools.partial(_dispatch_body, n_tok),
        out_shape=[jax.ShapeDtypeStruct((n_tok, LANES), jnp.int32), jax.ShapeDtypeStruct((8, LANES), jnp.int32)],
        compiler_params=pltpu.CompilerParams(vmem_limit_bytes=VMEM_LIMIT),
        name="dispatch",
    )(route)


def _scatter_body(tm, n_tiles, pos_ref, cnt_ref, off_ref, h_ref, o_hbm, zbuf, sem):
    i = pl.program_id(0)

    @pl.when(i == 0)
    def _():
        zbuf[...] = jnp.zeros_like(zbuf)
        last = N_EXPERTS - 1
        first_empty = (off_ref[last] + cnt_ref[last] + MOE_TILE - 1) // MOE_TILE

        def tile_copy(t):
            rows = pl.ds(pl.multiple_of(t * MOE_TILE, MOE_TILE), MOE_TILE)
            return pltpu.make_async_copy(zbuf, o_hbm.at[rows, :], sem.at[1])

        def partial_tile(e, start):
            cnt = cnt_ref[e]

            @pl.when(cnt % MOE_TILE != 0)
            def _():
                cp = tile_copy((off_ref[e] + cnt) // MOE_TILE)
                if start:
                    cp.start()
                else:
                    cp.wait()

        def zero_partial(e, carry):
            partial_tile(e, True)
            return carry

        def partial_done(e, carry):
            partial_tile(e, False)
            return carry

        def zero_tile(t, carry):
            tile_copy(t).start()
            return carry

        def zero_done(t, carry):
            tile_copy(t).wait()
            return carry

        lax.fori_loop(0, N_EXPERTS, zero_partial, 0)
        lax.fori_loop(first_empty, n_tiles, zero_tile, 0)
        lax.fori_loop(0, N_EXPERTS, partial_done, 0)
        lax.fori_loop(first_empty, n_tiles, zero_done, 0)

    def row_copies(r):
        return [pltpu.make_async_copy(h_ref.at[pl.ds(r, 1), :],
                                      o_hbm.at[pl.ds(pos_ref[(i * tm + r) * TOP_K + k], 1), :], sem.at[0])
                for k in range(TOP_K)]

    def start(r, carry):
        for cp in row_copies(r):
            cp.start()
        return carry

    def wait(r, carry):
        for cp in row_copies(r):
            cp.wait()
        return carry

    lax.fori_loop(0, tm, start, 0, unroll=8)
    lax.fori_loop(0, tm, wait, 0, unroll=8)


def _scatter_rows(h2, pos_flat, counts, offsets, n_rows):
    n_tok, d = h2.shape
    tm = 256
    grid_spec = pltpu.PrefetchScalarGridSpec(
        num_scalar_prefetch=3,
        grid=(n_tok // tm,),
        in_specs=[pl.BlockSpec((tm, d), lambda i, pos, cnt, off: (i, 0))],
        out_specs=pl.BlockSpec(memory_space=pl.ANY),
        scratch_shapes=[pltpu.VMEM((MOE_TILE, d), F32), pltpu.SemaphoreType.DMA((2,))],
    )
    return pl.pallas_call(
        functools.partial(_scatter_body, tm, n_rows // MOE_TILE),
        grid_spec=grid_spec,
        out_shape=jax.ShapeDtypeStruct((n_rows, d), F32),
        compiler_params=_params("arbitrary"),
        name="scatter_rows",
    )(pos_flat, counts, offsets, h2)


def _moe_body(n_tiles, te_ref, tv_ref, cnt_ref, x_ref, wg_hbm, wu_hbm, wd_hbm, y_ref,
              wg_f, wu_f, wd_f, wg_b, wu_b, wd_b, slot_ref, sem):
    i = pl.program_id(0)
    prev = jnp.maximum(i - 1, 0)
    valid = tv_ref[i] == 1
    e = te_ref[i]

    def fetch(expert, slot):
        return [pltpu.make_async_copy(src.at[expert], dst.at[slot], sem.at[slot])
                for src, dst in ((wg_hbm, wg_f), (wu_hbm, wu_f), (wd_hbm, wd_f))]

    @pl.when(jnp.logical_and(valid, i == 0))
    def _():
        slot_ref[0] = 1
        for cp in fetch(e, 0):
            cp.start()

    @pl.when(jnp.logical_and(valid, jnp.logical_or(i == 0, e != te_ref[prev])))
    def _():
        slot = 1 - slot_ref[0]
        slot_ref[0] = slot
        for cp in fetch(e, slot):
            cp.wait()
        nxt = jnp.minimum(i + (cnt_ref[e] + MOE_TILE - 1) // MOE_TILE, n_tiles - 1)

        @pl.when(jnp.logical_and(nxt > i, tv_ref[nxt] == 1))
        def _():
            for cp in fetch(te_ref[nxt], 1 - slot):
                cp.start()

        wg_b[...] = wg_f[slot].astype(BF16)
        wu_b[...] = wu_f[slot].astype(BF16)
        wd_b[...] = wd_f[slot].astype(BF16)

    @pl.when(valid)
    def _():
        x = x_ref[...].astype(BF16)
        g = jnp.dot(x, wg_b[...], preferred_element_type=F32)
        u = jnp.dot(x, wu_b[...], preferred_element_type=F32)
        y_ref[...] = jnp.dot((_silu(g) * u).astype(BF16), wd_b[...], preferred_element_type=F32)

    @pl.when(tv_ref[i] == 0)
    def _():
        y_ref[...] = jnp.zeros_like(y_ref)


def _moe(x_sorted, tile_expert, tile_valid, counts, w_g, w_u, w_d):
    n_tiles = tile_expert.shape[0]
    d = x_sorted.shape[1]
    f = w_g.shape[-1]
    hbm = pl.BlockSpec(memory_space=pl.ANY)
    grid_spec = pltpu.PrefetchScalarGridSpec(
        num_scalar_prefetch=3,
        grid=(n_tiles,),
        in_specs=[pl.BlockSpec((MOE_TILE, d), lambda i, te, tv, cnt: (i * tv[i], 0)), hbm, hbm, hbm],
        out_specs=pl.BlockSpec((MOE_TILE, d), lambda i, te, tv, cnt: (i, 0)),
        scratch_shapes=[pltpu.VMEM((2, d, f), F32), pltpu.VMEM((2, d, f), F32), pltpu.VMEM((2, f, d), F32),
                        pltpu.VMEM((d, f), BF16), pltpu.VMEM((d, f), BF16), pltpu.VMEM((f, d), BF16),
                        pltpu.SMEM((1,), jnp.int32), pltpu.SemaphoreType.DMA((2,))],
    )
    return pl.pallas_call(
        functools.partial(_moe_body, n_tiles),
        grid_spec=grid_spec,
        out_shape=jax.ShapeDtypeStruct((n_tiles * MOE_TILE, d), F32),
        compiler_params=_params("arbitrary"),
        name="moe",
    )(tile_expert, tile_valid, counts, x_sorted, w_g, w_u, w_d)


def _combine_body(tm, n_tiles, row0, pos_ref, y_hbm, rt_ref, x1_ref, gt_ref, gf_ref, o_ref, ybuf, sem):
    i = pl.program_id(0)
    slot = i % 2

    def gather(tile, dst_slot, start):
        def row(r, carry):
            for k in range(TOP_K):
                p = pos_ref[(row0 + tile * tm + r) * TOP_K + k]
                cp = pltpu.make_async_copy(y_hbm.at[pl.ds(p, 1), :], ybuf.at[dst_slot, k, pl.ds(r, 1), :],
                                           sem.at[dst_slot])
                if start:
                    cp.start()
                else:
                    cp.wait()
            return carry

        lax.fori_loop(0, tm, row, 0, unroll=8)

    @pl.when(i == 0)
    def _():
        gather(0, 0, True)

    @pl.when(i + 1 < n_tiles)
    def _():
        gather(i + 1, 1 - slot, True)

    gather(i, slot, False)
    rt = rt_ref[...]
    y = rt[:, ROUTE_W:ROUTE_W + 1] * ybuf[slot, 0] + rt[:, ROUTE_W + 1:ROUTE_W + 2] * ybuf[slot, 1]
    x2 = x1_ref[...] + gt_ref[0] * y
    o_ref[...] = _rms_scale(x2) * gf_ref[...]


def _combine(y_sorted, pos_flat, route, x1, gt2, gain_f, row0, n_rows, mod_of_tile, tm):
    d = x1.shape[1]
    rb = row0 // tm
    n_tiles = n_rows // tm
    grid_spec = pltpu.PrefetchScalarGridSpec(
        num_scalar_prefetch=1,
        grid=(n_tiles,),
        in_specs=[pl.BlockSpec(memory_space=pl.ANY),
                  pl.BlockSpec((tm, LANES), lambda i, pos: (rb + i, 0)),
                  pl.BlockSpec((tm, d), lambda i, pos: (rb + i, 0)),
                  pl.BlockSpec((1, 1, d), lambda i, pos: (mod_of_tile(i), 0, 0)),
                  pl.BlockSpec((1, d), lambda i, pos: (0, 0))],
        out_specs=pl.BlockSpec((tm, d), lambda i, pos: (i, 0)),
        scratch_shapes=[pltpu.VMEM((2, TOP_K, tm, d), F32), pltpu.SemaphoreType.DMA((2,))],
    )
    return pl.pallas_call(
        functools.partial(_combine_body, tm, n_tiles, row0),
        grid_spec=grid_spec,
        out_shape=jax.ShapeDtypeStruct((n_rows, d), F32),
        compiler_params=_params("arbitrary"),
        name="combine",
    )(pos_flat, y_sorted, route, x1, gt2, gain_f)


GATE_SRC = {"beta": 0, "gc": 2 * H_A, "egc": 4 * H_A, "ekd": 6 * H_A}


def _gates_body(zs_ref, al_ref, dt_ref, sel_ref, o_ref):
    tm = zs_ref.shape[0]
    lane = lax.broadcasted_iota(jnp.int32, (1, LANES), 1)
    zs = zs_ref[...]
    x = zs + dt_ref[...]
    softplus = jnp.maximum(x, 0.0) + jnp.log(1.0 + jnp.exp(-jnp.abs(x)))
    in_g = jnp.logical_and(lane >= GATE_SRC["gc"], lane < GATE_SRC["egc"])
    g = jnp.where(in_g, -jnp.exp(al_ref[...]) * softplus, 0.0)
    beta = jax.nn.sigmoid(zs)
    reversed_lane = lane >= GATE_SRC["gc"] + H_A
    ri = lax.broadcasted_iota(jnp.int32, (CHUNK, CHUNK), 0)
    ci = lax.broadcasted_iota(jnp.int32, (CHUNK, CHUNK), 1)
    prefix = jnp.where(ci <= ri, 1.0, 0.0)
    suffix = jnp.where(ci >= ri, 1.0, 0.0)
    exact = dict(preferred_element_type=F32, precision=lax.Precision.HIGHEST)
    for c in range(tm // CHUNK):
        r = slice(c * CHUNK, (c + 1) * CHUNK)
        gch = g[r]
        gc = jnp.where(reversed_lane, jnp.dot(suffix, gch, **exact), jnp.dot(prefix, gch, **exact))
        tot = jnp.sum(gch, axis=0, keepdims=True)
        src = jnp.where(lane < GATE_SRC["gc"], beta[r],
                        jnp.where(lane < GATE_SRC["egc"], gc,
                                  jnp.where(lane < GATE_SRC["ekd"], pltpu.roll(jnp.exp(gc), 2 * H_A, 1),
                                            jnp.where(lane < GATE_SRC["ekd"] + 2 * H_A,
                                                      pltpu.roll(jnp.exp(tot - gc), 4 * H_A, 1), 0.0))))
        o_ref[r, :] = jnp.dot(src, sel_ref[...], **exact)


def _gate_table(zs, a_log, dt_bias):
    n_tok = zs.shape[0]
    tm = 512
    place = lambda v: jnp.pad(v.reshape(1, 2 * H_A), ((0, 0), (GATE_SRC["gc"], LANES - GATE_SRC["egc"])))
    src = jnp.arange(LANES)[:, None]
    dst = jnp.arange(H_A * LANES)[None, :]
    head, within = dst // LANES, dst % LANES
    kind_src = jnp.array([GATE_SRC["gc"], GATE_SRC["beta"], GATE_SRC["egc"], GATE_SRC["ekd"]])
    want = kind_src[within % GATE_KINDS] + (within // GATE_KINDS) * H_A + head
    sel = jnp.where(jnp.logical_and(within < 2 * GATE_KINDS, src == want), 1.0, 0.0).astype(F32)
    return pl.pallas_call(
        _gates_body,
        grid=(n_tok // tm,),
        in_specs=[pl.BlockSpec((tm, LANES), lambda i: (i, 0)),
                  pl.BlockSpec((1, LANES), lambda i: (0, 0)),
                  pl.BlockSpec((1, LANES), lambda i: (0, 0)),
                  pl.BlockSpec((LANES, H_A * LANES), lambda i: (0, 0))],
        out_specs=pl.BlockSpec((tm, H_A * LANES), lambda i: (i, 0)),
        out_shape=jax.ShapeDtypeStruct((n_tok, H_A * LANES), F32),
        compiler_params=_params("arbitrary"),
        name="gates",
    )(zs, place(a_log), place(dt_bias), sel)


def kernel(x_prompt, x_sample, c, state_delta, cache_k, cache_v, c_ctx, w_ada, b_ada, norm_mix, norm_ffn, w_in,
           conv_a, a_log, dt_bias, onorm_a, lam, subln_b, w_up_a, w_up_b, w_o, w_rg, b_rg, w_re, b_re,
           w_e_gate, w_e_up, w_e_down, norm_final):
    assert w_in.shape[0] == 1, "single trunk layer"
    l = 0
    lam_init = 0.8 - 0.6 * math.exp(-0.3 * l)
    bc, tc, d = x_prompt.shape
    bl, tl, _ = x_sample.shape
    n_ctx, n_lat = bc * tc, bl * tl
    xp = x_prompt.reshape(n_ctx, d)
    xs = x_sample.reshape(n_lat, d)

    cvec = jnp.zeros((N_MOD_ROWS, d), F32).at[0].set(c_ctx).at[1:1 + bl].set(c)
    mod = _adaln(cvec, w_ada[l], b_ada[l][None, :])
    sh1, sc1, gt1, sh2, sc2, gt2 = [m.reshape(N_MOD_ROWS, 1, d) for m in jnp.split(mod, 6, axis=1)]

    rows = _Rows(n_ctx, n_lat, tl, 512)
    w_in_t = jnp.swapaxes(w_in[l], 0, 1)
    h1, zs = _prenorm(rows, xp, xs, norm_mix[l][None, :], sh1, sc1, w_in_t)
    z = _inproj(h1, w_in_t)

    gates = _gate_table(zs, a_log[l], dt_bias[l])
    oa_c, new_state = _deltanet(z, conv_a[l], gates, None, bc, tc, 0, 8)
    oa_l = _deltanet(z, conv_a[l], gates, state_delta, bl, tl, n_ctx, 4)

    lv = lam[l]
    lam_val = jnp.exp(jnp.sum(lv[0] * lv[1])) - jnp.exp(jnp.sum(lv[2] * lv[3])) + lam_init
    lam_row = jnp.full((1, LANES), lam_val, F32)
    sub_row = subln_b[l][None, :]
    ob_c, new_k, new_v = _attn_ctx(z, lam_row, sub_row, bc, tc, lam_init)
    cos, sin = _rope_tables(tl)
    ob_l = _attn_lat(z, cache_k, cache_v, cos, sin, lam_row, sub_row, bl, tl, n_ctx, lam_init)

    mixed = _merge(rows, z, oa_c, oa_l, ob_c, ob_l, onorm_a[l][None, :],
                   w_up_a[l].astype(BF16), w_up_b[l].astype(BF16))
    w_r = jnp.pad(jnp.concatenate([w_rg[l], w_re[l]], axis=1), ((0, 0), (0, LANES - N_GROUPS - N_EXPERTS)))
    b_r = jnp.pad(jnp.concatenate([b_rg[l], b_re[l]]), (0, LANES - N_GROUPS - N_EXPERTS))[None, :]
    x1, h2, route = _outproj(rows, mixed, w_o[l].astype(BF16), xp, xs, gt1, norm_ffn[l][None, :], sh2, sc2, w_r, b_r)

    n_tok = n_ctx + n_lat
    n_tiles = (n_tok * TOP_K) // MOE_TILE + N_EXPERTS
    pos, tbl = _dispatch(route)
    pos_flat = pos[:, :TOP_K].reshape(-1)
    counts, offsets = tbl[TBL_COUNT, :N_EXPERTS], tbl[TBL_OFFSET, :N_EXPERTS]
    x_sorted = _scatter_rows(h2, pos_flat, counts, offsets, n_tiles * MOE_TILE)
    y_sorted = _moe(x_sorted, tbl[TBL_EXPERT, :n_tiles], tbl[TBL_VALID, :n_tiles], counts,
                    w_e_gate[l], w_e_up[l], w_e_down[l])
    gf = norm_final[None, :]
    tm_c = 256
    y_prompt = _combine(y_sorted, pos_flat, route, x1, gt2, gf, 0, n_ctx, lambda i: 0 * i, tm_c)
    y_sample = _combine(y_sorted, pos_flat, route, x1, gt2, gf, n_ctx, n_lat, lambda i: 1 + i // (tl // tm_c), tm_c)
    return (y_prompt.reshape(bc, tc, d), y_sample.reshape(bl, tl, d), new_state, new_k, new_v)
```

```python
import functools
import math

import jax
import jax.numpy as jnp
from jax import lax
from jax.experimental import pallas as pl
from jax.experimental.pallas import tpu as pltpu

F32 = jnp.float32
BF16 = jnp.bfloat16

D_MODEL = 2048
GRID_W = 64
H_A = 8
DK_A = 128
DV_A = 128
CONV_K = 3
CHUNK = 64
H_B = 8
DQK_B = 64
DV_B = 2 * DQK_B
ROPE_BASE = 10000.0
N_GROUPS = 4
E_PER_GROUP = 8
N_EXPERTS = N_GROUPS * E_PER_GROUP
TOP_K = 2
D_FF_E = D_MODEL // 4
EPS = 1e-6

LANES = 128
QKV_A = 2 * H_A * DK_A + H_A * DV_A
N_SMALL = 4 * H_A
Z_GATE_A = QKV_A
Z_QB = Z_GATE_A + H_A * DV_A
Z_KB = Z_QB + H_B * 2 * DQK_B
Z_VB = Z_KB + H_B * 2 * DQK_B
Z_GM = Z_VB + H_B * DV_B
N_MAIN = Z_GM + 2 * D_MODEL

N_MOD_ROWS = 8
MOE_TILE = 256
VMEM_LIMIT = 56 * 1024 * 1024

NT_DIMS = (((1,), (1,)), ((), ()))


def _params(*sem):
    return pltpu.CompilerParams(dimension_semantics=sem, vmem_limit_bytes=VMEM_LIMIT)


def _mm(a, b):
    return jnp.dot(a.astype(BF16), b.astype(BF16), preferred_element_type=F32)


def _mm_nt(a, b):
    return lax.dot_general(a.astype(BF16), b.astype(BF16), NT_DIMS, preferred_element_type=F32)


def _silu(x):
    return x * jax.nn.sigmoid(x)


def _rms_scale(x):
    return x * lax.rsqrt(jnp.mean(x * x, axis=-1, keepdims=True) + EPS)


def _adaln_body(c_ref, w_ref, b_ref, o_ref):
    s = _silu(c_ref[...])
    o_ref[...] = _mm(s, w_ref[...]) + b_ref[...]


def _adaln(cvec, w, b):
    d, n = w.shape
    tn = 1024
    return pl.pallas_call(
        _adaln_body,
        grid=(n // tn,),
        in_specs=[pl.BlockSpec((N_MOD_ROWS, d), lambda j: (0, 0)),
                  pl.BlockSpec((d, tn), lambda j: (0, j)),
                  pl.BlockSpec((1, tn), lambda j: (0, j))],
        out_specs=pl.BlockSpec((N_MOD_ROWS, tn), lambda j: (0, j)),
        out_shape=jax.ShapeDtypeStruct((N_MOD_ROWS, n), F32),
        compiler_params=_params("arbitrary"),
        name="adaln",
    )(cvec, w, b)


class _Rows:
    def __init__(self, n_ctx, n_lat, t_lat, tm):
        assert n_ctx % tm == 0 and n_lat % tm == 0 and t_lat % tm == 0
        self.tm = tm
        self.nct = n_ctx // tm
        self.nlt = n_lat // tm
        self.per_seq = t_lat // tm
        self.n = self.nct + self.nlt

    def ctx_idx(self, i):
        return jnp.minimum(i, self.nct - 1)

    def lat_idx(self, i):
        return jnp.maximum(i - self.nct, 0)

    def mod_idx(self, i):
        return jnp.where(i < self.nct, 0, 1 + (i - self.nct) // self.per_seq)


SMALL_COL0 = QKV_A + H_A * DV_A


def _prenorm_body(nct, xp_ref, xs_ref, g_ref, sh_ref, sc_ref, ws_ref, h_ref, zs_ref):
    i = pl.program_id(0)

    def run(x_ref):
        h = _rms_scale(x_ref[...]) * g_ref[...]
        h = (h * (1.0 + sc_ref[0]) + sh_ref[0]).astype(BF16)
        h_ref[...] = h
        zs_ref[...] = lax.dot_general(h, ws_ref[...].astype(BF16), NT_DIMS, preferred_element_type=F32)

    @pl.when(i < nct)
    def _():
        run(xp_ref)

    @pl.when(i >= nct)
    def _():
        run(xs_ref)


def _prenorm(rows, xp, xs, gain, sh, sc, w_in_t):
    d = xp.shape[1]
    tm = rows.tm
    n_tok = rows.n * tm
    return pl.pallas_call(
        functools.partial(_prenorm_body, rows.nct),
        grid=(rows.n,),
        in_specs=[pl.BlockSpec((tm, d), lambda i: (rows.ctx_idx(i), 0)),
                  pl.BlockSpec((tm, d), lambda i: (rows.lat_idx(i), 0)),
                  pl.BlockSpec((1, d), lambda i: (0, 0)),
                  pl.BlockSpec((1, 1, d), lambda i: (rows.mod_idx(i), 0, 0)),
                  pl.BlockSpec((1, 1, d), lambda i: (rows.mod_idx(i), 0, 0)),
                  pl.BlockSpec((LANES, d), lambda i: (SMALL_COL0 // LANES, 0))],
        out_specs=[pl.BlockSpec((tm, d), lambda i: (i, 0)),
                   pl.BlockSpec((tm, LANES), lambda i: (i, 0))],
        out_shape=[jax.ShapeDtypeStruct((n_tok, d), BF16),
                   jax.ShapeDtypeStruct((n_tok, LANES), F32)],
        compiler_params=_params("arbitrary"),
        name="prenorm",
    )(xp, xs, gain, sh, sc, w_in_t)


INPROJ_TN = 1024
N_ALIGNED = SMALL_COL0 // INPROJ_TN


def _inproj_body(h_ref, wt_ref, z_ref, w_scr):
    @pl.when(pl.program_id(1) == 0)
    def _():
        w_scr[...] = wt_ref[...].astype(BF16)

    z_ref[...] = lax.dot_general(h_ref[...], w_scr[...], NT_DIMS, preferred_element_type=F32)


def _inproj(h, w_in_t):
    n_tok, d = h.shape
    tm, tn = 1024, INPROJ_TN
    first_row = lambda j: pl.multiple_of(j * tn + N_SMALL * jnp.minimum(j // N_ALIGNED, 1), N_SMALL)
    return pl.pallas_call(
        _inproj_body,
        grid=(N_MAIN // tn, n_tok // tm),
        in_specs=[pl.BlockSpec((tm, d), lambda j, i: (i, 0)),
                  pl.BlockSpec((pl.Element(tn), pl.Element(d)), lambda j, i: (first_row(j), 0))],
        out_specs=pl.BlockSpec((tm, tn), lambda j, i: (i, j)),
        out_shape=jax.ShapeDtypeStruct((n_tok, N_MAIN), F32),
        scratch_shapes=[pltpu.VMEM((tn, d), BF16)],
        compiler_params=_params("arbitrary", "arbitrary"),
        name="inproj",
    )(h, w_in_t)


GATE_KINDS = 4
INV_BLOCK = 16


def _chunk_solve(chains, ii, jj):
    eye = ii == jj
    blk = (ii // INV_BLOCK) == (jj // INV_BLOCK)
    blk2 = (ii // (2 * INV_BLOCK)) == (jj // (2 * INV_BLOCK))
    cols, a_qk, a0, off, d0, rhs = [], [], [], [], [], []
    for kk, qk, v, k, gates, reverse in chains:
        base = GATE_KINDS if reverse else 0
        gc, beta, egc, ekd = (gates[:, base + i:base + i + 1] for i in range(GATE_KINDS))
        incl = (jj >= ii) if reverse else (jj <= ii)
        strict = (jj > ii) if reverse else (jj < ii)
        gc_row = jnp.sum(jnp.where(eye, gc, 0.0), axis=0, keepdims=True)
        dec = jnp.where(incl, jnp.exp(jnp.where(incl, gc - gc_row, 0.0)), 0.0)
        a = jnp.where(strict, kk * dec, 0.0)
        cols.append((egc, ekd))
        a_qk.append(qk * dec)
        a0.append(jnp.where(blk, a, 0.0))
        off.append(jnp.where(blk, 0.0, a))
        d0.append(jnp.where(eye, 1.0, 0.0) - a0[-1])
        rhs.append(jnp.concatenate([v * beta, k * (beta * egc)], axis=1))
    p = [_mm(x, x) for x in a0]
    for _ in range(int(math.log2(INV_BLOCK)) - 2):
        r = [_mm(jnp.concatenate([pi, di], axis=0), pi) for pi, di in zip(p, d0)]
        p = [ri[:CHUNK] for ri in r]
        d0 = [di + ri[CHUNK:] for di, ri in zip(d0, r)]
    d0 = [di + _mm(di, pi) for di, pi in zip(d0, p)]
    wl = [_mm(di, oi) for di, oi in zip(d0, off)]
    yr = [_mm(di, ri) for di, ri in zip(d0, rhs)]
    x1 = [jnp.where(blk2, x, 0.0) for x in wl]
    yl = [jnp.where(blk2, 0.0, x) for x in wl]
    zl = [y - _mm(x, y) for x, y in zip(x1, yl)]
    zr = [y - _mm(x, y) for x, y in zip(x1, yr)]
    sol = [y - _mm(x, y) for x, y in zip(zl, zr)]
    return [(so[:, :DV_A], so[:, DV_A:], aq, egc, ekd) for so, aq, (egc, ekd) in zip(sol, a_qk, cols)]


def _delta_body(t_len, hp, has_s0, q_ref, k_ref, v_ref, cq_ref, ck_ref, cv_ref, g_ref, *rest):
    if has_s0:
        s0_ref, o_ref, qs, ks, vs, u_scr, wq_scr, ak_scr, s_scr = rest
    else:
        o_ref, sfin_ref, qs, ks, vs, u_scr, wq_scr, ak_scr, s_scr = rest
    n = t_len // CHUNK
    tpos = lax.broadcasted_iota(jnp.int32, (t_len, 1), 0)

    def conv_act(x, w):
        x_prev = jnp.where(tpos == 0, 0.0, pltpu.roll(x, 1, 0))
        x_next = jnp.where(tpos == t_len - 1, 0.0, pltpu.roll(x, t_len - 1, 0))
        return _silu(x_prev * w[0:1] + x * w[1:2] + x_next * w[2:3])

    def l2n(x):
        return x * lax.rsqrt(jnp.sum(x * x, axis=-1, keepdims=True) + EPS)

    for hh in range(hp):
        cols = slice(hh * LANES, (hh + 1) * LANES)
        qs[:, cols] = l2n(conv_act(q_ref[:, cols], cq_ref[:, cols])) * (DK_A ** -0.5)
        ks[:, cols] = l2n(conv_act(k_ref[:, cols], ck_ref[:, cols]))
        vs[:, cols] = conv_act(v_ref[:, cols], cv_ref[:, cols])
    if has_s0:
        s_scr[...] = s0_ref[...]
    else:
        s_scr[...] = jnp.zeros_like(s_scr)
    o_ref[...] = jnp.zeros_like(o_ref)
    ii = lax.broadcasted_iota(jnp.int32, (CHUNK, CHUNK), 0)
    jj = lax.broadcasted_iota(jnp.int32, (CHUNK, CHUNK), 1)

    def prep(c, carry):
        r = pl.ds(pl.multiple_of(c * CHUNK, CHUNK), CHUNK)
        heads = []
        for hh in range(hp):
            cols = slice(hh * LANES, (hh + 1) * LANES)
            heads.append((qs[r, cols], ks[r, cols], vs[r, cols], g_ref[r, cols]))
        rr = [_mm_nt(jnp.concatenate([k * g[:, 1:2], k * g[:, GATE_KINDS + 1:GATE_KINDS + 2], q], axis=0), k)
              for q, k, v, g in heads]
        chains = [(rr[hh][d * CHUNK:(d + 1) * CHUNK], rr[hh][2 * CHUNK:], heads[hh][2], heads[hh][1], heads[hh][3],
                   d == 1) for hh in range(hp) for d in (0, 1)]
        solved = _chunk_solve(chains, ii, jj)
        for hh in range(hp):
            cols = slice(hh * LANES, (hh + 1) * LANES)
            q, k = heads[hh][0], heads[hh][1]
            for d in (0, 1):
                u, w, a_qk, egc, ekd = solved[hh * 2 + d]
                slot = (d * hp + hh) * n + c
                u_scr[d, r, cols] = u
                wq_scr[slot] = jnp.concatenate([w, q * egc], axis=0).astype(BF16)
                ak_scr[slot] = jnp.concatenate([a_qk, (k * ekd).T], axis=0).astype(BF16)
        return carry

    lax.fori_loop(0, n, prep, 0)

    def scan(c, carry):
        chains = []
        for hh in range(hp):
            for d in (0, 1):
                cc = c if d == 0 else n - 1 - c
                chains.append((hh, d, pl.multiple_of(cc * CHUNK, CHUNK), (d * hp + hh) * n + cc))
        s = [s_scr[d, hh] for hh, d, r0, slot in chains]
        r1 = [jnp.dot(wq_scr[slot], si.astype(BF16), preferred_element_type=F32)
              for si, (hh, d, r0, slot) in zip(s, chains)]
        v_new = [u_scr[d, pl.ds(r0, CHUNK), hh * LANES:(hh + 1) * LANES] - ri[:CHUNK]
                 for ri, (hh, d, r0, slot) in zip(r1, chains)]
        r2 = [jnp.dot(ak_scr[slot], vi.astype(BF16), preferred_element_type=F32)
              for vi, (hh, d, r0, slot) in zip(v_new, chains)]
        for si, r1i, r2i, (hh, d, r0, slot) in zip(s, r1, r2, chains):
            o_ref[pl.ds(r0, CHUNK), hh * LANES:(hh + 1) * LANES] += r1i[CHUNK:] + r2i[:CHUNK]
            lane = hh * LANES + d * GATE_KINDS + 2
            edge = g_ref[pl.ds(pl.multiple_of(r0 + (0 if d else CHUNK - 8), 8), 8), lane:lane + 1]
            s_scr[d, hh] = si * (edge[0:1] if d else edge[7:8]) + r2i[CHUNK:]
        return carry

    lax.fori_loop(0, n, scan, 0)
    if not has_s0:
        sfin_ref[...] = s_scr[...]


def _deltanet(z, conv_w, gates, s0, n_seq, t_len, row0, hp):
    rb = row0 // t_len
    w = hp * LANES
    n = t_len // CHUNK
    seq_blk = lambda col0: pl.BlockSpec((t_len, w), lambda b, h: (rb + b, col0 // hp + h))
    cw_blk = lambda col0: pl.BlockSpec((CONV_K, w), lambda b, h: (0, col0 // hp + h))
    state_blk = pl.BlockSpec((None, None, 2, hp, DK_A, DV_A), lambda b, h: (b, 0, 0, h, 0, 0))
    in_specs = [seq_blk(0), seq_blk(H_A), seq_blk(2 * H_A), cw_blk(0), cw_blk(H_A), cw_blk(2 * H_A),
                pl.BlockSpec((t_len, w), lambda b, h: (rb + b, h))]
    args = [z, z, z, conv_w, conv_w, conv_w, gates]
    o_spec = pl.BlockSpec((t_len, w), lambda b, h: (b, h))
    o_shape = jax.ShapeDtypeStruct((n_seq * t_len, H_A * DV_A), F32)
    has_s0 = s0 is not None
    if has_s0:
        in_specs += [state_blk]
        args += [s0]
        out_specs, out_shape = o_spec, o_shape
    else:
        out_specs = [o_spec, state_blk]
        out_shape = [o_shape, jax.ShapeDtypeStruct((n_seq, 1, 2, H_A, DK_A, DV_A), F32)]

    return pl.pallas_call(
        functools.partial(_delta_body, t_len, hp, has_s0),
        grid=(n_seq, H_A // hp),
        in_specs=in_specs,
        out_specs=out_specs,
        out_shape=out_shape,
        scratch_shapes=[pltpu.VMEM((t_len, w), F32)] * 3
        + [pltpu.VMEM((2, t_len, w), F32),
           pltpu.VMEM((2 * hp * n, 2 * CHUNK, DV_A), BF16),
           pltpu.VMEM((2 * hp * n, CHUNK + DK_A, CHUNK), BF16),
           pltpu.VMEM((2, hp, DK_A, DV_A), F32)],
        compiler_params=_params("arbitrary", "arbitrary"),
        name="deltanet_lat" if has_s0 else "deltanet_ctx",
    )(*args)


def _subln(o, sub_ref, lam_init):
    return _rms_scale(o) * sub_ref[...] * (1.0 - lam_init)


def _attn_ctx_body(lam_init, q_ref, k_ref, v_ref, lam_ref, sub_ref, o_ref, ck_ref, cv_ref):
    heads = [slice(h * LANES, (h + 1) * LANES) for h in range(H_B)]
    lam = lam_ref[0:1, 0:1]
    ks = [k_ref[:, c] for c in heads]
    vs = [v_ref[:, c] for c in heads]
    for h in range(H_B):
        ck_ref[h] = ks[h]
        cv_ref[h] = vs[h]
    qb = [(q_ref[:, c] * (DQK_B ** -0.5)).astype(BF16) for c in heads]
    kb = [k.astype(BF16) for k in ks]
    probs = []
    for lo in (0, DQK_B):
        s = [lax.dot_general(q[:, lo:lo + DQK_B], k[:, lo:lo + DQK_B], NT_DIMS, preferred_element_type=F32)
             for q, k in zip(qb, kb)]
        e = [jnp.exp(x - jnp.max(x, axis=-1, keepdims=True)) for x in s]
        probs.append([x / jnp.sum(x, axis=-1, keepdims=True) for x in e])
    o = [jnp.dot((p1 - lam * p2).astype(BF16), v.astype(BF16), preferred_element_type=F32)
         for p1, p2, v in zip(probs[0], probs[1], vs)]
    for c, oh in zip(heads, o):
        o_ref[:, c] = _subln(oh, sub_ref, lam_init)


def _attn_ctx(z, lam, subln, n_seq, t_len, lam_init):
    n_tok = n_seq * t_len
    w = H_B * LANES
    blk = lambda col0: pl.BlockSpec((t_len, w), lambda b: (b, col0 // w))
    cache_blk = pl.BlockSpec((None, None, H_B, t_len, LANES), lambda b: (b, 0, 0, 0, 0))
    cache_shape = jax.ShapeDtypeStruct((n_seq, 1, H_B, t_len, LANES), F32)
    return pl.pallas_call(
        functools.partial(_attn_ctx_body, lam_init),
        grid=(n_seq,),
        in_specs=[blk(Z_QB), blk(Z_KB), blk(Z_VB),
                  pl.BlockSpec((1, LANES), lambda b: (0, 0)),
                  pl.BlockSpec((1, LANES), lambda b: (0, 0))],
        out_specs=[pl.BlockSpec((t_len, w), lambda b: (b, 0)), cache_blk, cache_blk],
        out_shape=[jax.ShapeDtypeStruct((n_tok, H_B * DV_B), F32), cache_shape, cache_shape],
        compiler_params=_params("arbitrary"),
        name="attn_ctx",
    )(z, z, z, lam, subln)


def _rope(x, cos, sin_signed):
    lane = lax.broadcasted_iota(jnp.int32, (1, LANES), 1)
    first = (lane % 32) < 16
    partner = jnp.where(first, pltpu.roll(x, LANES - 16, 1), pltpu.roll(x, 16, 1))
    return x * cos + partner * sin_signed


LAT_HEADS = 2


def _attn_lat_body(lam_init, n_past, q_ref, k_ref, v_ref, pk_ref, pv_ref, cosq_ref, sinq_ref, cos_ref, sin_ref,
                   lam_ref, sub_ref, o_ref, keys, vals):
    heads = [slice(h * LANES, (h + 1) * LANES) for h in range(LAT_HEADS)]

    @pl.when(pl.program_id(2) == 0)
    def _():
        for h, c in enumerate(heads):
            keys[h, 0:n_past, :] = pk_ref[h].astype(BF16)
            vals[h, 0:n_past, :] = pv_ref[h].astype(BF16)
            keys[h, n_past:, :] = _rope(k_ref[:, c], cos_ref[...], sin_ref[...]).astype(BF16)
            vals[h, n_past:, :] = v_ref[:, c].astype(BF16)

    lam = lam_ref[0:1, 0:1]
    qb = [(_rope(q_ref[:, c], cosq_ref[...], sinq_ref[...]) * (DQK_B ** -0.5)).astype(BF16) for c in heads]
    probs = []
    for lo in (0, DQK_B):
        s = [lax.dot_general(q[:, lo:lo + DQK_B], keys[h, :, lo:lo + DQK_B], NT_DIMS, preferred_element_type=F32)
             for h, q in enumerate(qb)]
        e = [jnp.exp(x - jnp.max(x, axis=-1, keepdims=True)) for x in s]
        probs.append([x / jnp.sum(x, axis=-1, keepdims=True) for x in e])
    for h, c in enumerate(heads):
        a = (probs[0][h] - lam * probs[1][h]).astype(BF16)
        o_ref[:, c] = _subln(jnp.dot(a, vals[h], preferred_element_type=F32), sub_ref, lam_init)


def _attn_lat(z, cache_k, cache_v, cos, sin, lam, subln, n_seq, t_len, row0, lam_init):
    tq = 256
    nq = t_len // tq
    n_past = cache_k.shape[3]
    rbq = row0 // tq
    rbs = row0 // t_len
    w = LAT_HEADS * LANES
    seq_blk = lambda col0: pl.BlockSpec((t_len, w), lambda b, h, qi: (rbs + b, col0 // w + h))
    past_blk = pl.BlockSpec((None, None, LAT_HEADS, n_past, LANES), lambda b, h, qi: (b, 0, h, 0, 0))
    row_vec = pl.BlockSpec((1, LANES), lambda b, h, qi: (0, 0))
    return pl.pallas_call(
        functools.partial(_attn_lat_body, lam_init, n_past),
        grid=(n_seq, H_B // LAT_HEADS, nq),
        in_specs=[pl.BlockSpec((tq, w), lambda b, h, qi: (rbq + b * nq + qi, Z_QB // w + h)),
                  seq_blk(Z_KB), seq_blk(Z_VB), past_blk, past_blk,
                  pl.BlockSpec((tq, LANES), lambda b, h, qi: (qi, 0)),
                  pl.BlockSpec((tq, LANES), lambda b, h, qi: (qi, 0)),
                  pl.BlockSpec((t_len, LANES), lambda b, h, qi: (0, 0)),
                  pl.BlockSpec((t_len, LANES), lambda b, h, qi: (0, 0)),
                  row_vec, row_vec],
        out_specs=pl.BlockSpec((tq, w), lambda b, h, qi: (b * nq + qi, h)),
        out_shape=jax.ShapeDtypeStruct((n_seq * t_len, H_B * DV_B), F32),
        scratch_shapes=[pltpu.VMEM((LAT_HEADS, n_past + t_len, LANES), BF16)] * 2,
        compiler_params=_params("arbitrary", "arbitrary", "arbitrary"),
        name="attn_lat",
    )(z, z, z, cache_k, cache_v, cos, sin, cos, sin, lam, subln)


def _rope_tables(t_len):
    t = jnp.arange(t_len)
    pos = jnp.stack([t // GRID_W, t % GRID_W], axis=1).astype(F32)
    nf = DQK_B // 4
    inv_freq = ROPE_BASE ** (-jnp.arange(nf, dtype=F32) / nf)
    lane = jnp.arange(LANES)
    half = (lane % DQK_B) // (DQK_B // 2)
    ang = pos[:, half] * inv_freq[lane % nf][None, :]
    sign = jnp.where((lane % (DQK_B // 2)) < nf, -1.0, 1.0).astype(F32)
    return jnp.cos(ang), jnp.sin(ang) * sign[None, :]


def _merge_body(nct, oac_ref, oal_ref, ga_ref, on_ref, obc_ref, obl_ref, wa_ref, wb_ref, *rest):
    gm_refs, m_ref = rest[:-1], rest[-1]
    i = pl.program_id(0)
    n_blk = len(gm_refs) // 2

    def run(oa_ref, ob_ref):
        a = jnp.concatenate(
            [(_rms_scale(oa_ref[:, c]) * on_ref[...] * _silu(ga_ref[:, c])).astype(BF16)
             for c in (slice(h * DV_A, (h + 1) * DV_A) for h in range(H_A))], axis=1)
        ya = jnp.dot(a, wa_ref[...], preferred_element_type=F32)
        yb = jnp.dot(ob_ref[...].astype(BF16), wb_ref[...], preferred_element_type=F32)
        tn = gm_refs[0].shape[1]
        for j in range(n_blk):
            c = slice(j * tn, (j + 1) * tn)
            m_ref[:, c] = (jax.nn.sigmoid(gm_refs[j][...]) * ya[:, c]
                           + jax.nn.sigmoid(gm_refs[n_blk + j][...]) * yb[:, c]).astype(BF16)

    @pl.when(i < nct)
    def _():
        run(oac_ref, obc_ref)

    @pl.when(i >= nct)
    def _():
        run(oal_ref, obl_ref)


def _merge(rows, z, oa_c, oa_l, ob_c, ob_l, onorm, w_up_a, w_up_b):
    n_tok = z.shape[0]
    tm = rows.tm
    ka, d = w_up_a.shape
    tn = 1024
    nj = d // tn
    ctx_blk = pl.BlockSpec((tm, ka), lambda i: (rows.ctx_idx(i), 0))
    lat_blk = pl.BlockSpec((tm, ka), lambda i: (rows.lat_idx(i), 0))
    weight = pl.BlockSpec((ka, d), lambda i: (0, 0))
    gate_cols = [pl.BlockSpec((tm, tn), functools.partial(lambda i, c: (i, c), c=Z_GM // tn + j)) for j in range(2 * nj)]
    return pl.pallas_call(
        functools.partial(_merge_body, rows.nct),
        grid=(n_tok // tm,),
        in_specs=[ctx_blk, lat_blk,
                  pl.BlockSpec((tm, ka), lambda i: (i, Z_GATE_A // ka)),
                  pl.BlockSpec((1, DV_A), lambda i: (0, 0)),
                  ctx_blk, lat_blk, weight, weight] + gate_cols,
        out_specs=pl.BlockSpec((tm, d), lambda i: (i, 0)),
        out_shape=jax.ShapeDtypeStruct((n_tok, d), BF16),
        compiler_params=_params("arbitrary"),
        name="merge",
    )(oa_c, oa_l, z, onorm, ob_c, ob_l, w_up_a, w_up_b, *([z] * (2 * nj)))


ROUTE_E = 0
ROUTE_W = TOP_K


def _route_rows(lg):
    lane = lax.broadcasted_iota(jnp.int32, lg.shape, 1)
    neg = -jnp.inf

    def first_max(x):
        m = jnp.max(x, axis=1, keepdims=True)
        return m, jnp.min(jnp.where(x == m, lane, LANES), axis=1, keepdims=True)

    gl = jnp.where(lane < N_GROUPS, lg, neg)
    gmax, g_idx = first_max(gl)
    pg_top = 1.0 / jnp.sum(jnp.exp(gl - gmax), axis=1, keepdims=True)
    lo = N_GROUPS + E_PER_GROUP * g_idx
    el = jnp.where(jnp.logical_and(lane >= lo, lane < lo + E_PER_GROUP), lg, neg)
    emax, i1 = first_max(el)
    esum = jnp.sum(jnp.exp(el - emax), axis=1, keepdims=True)
    e2max, i2 = first_max(jnp.where(lane == i1, neg, el))
    p1 = 1.0 / esum
    p2 = jnp.exp(e2max - emax) / esum
    den = p1 + p2
    vals = [(i1 - N_GROUPS).astype(F32), (i2 - N_GROUPS).astype(F32), pg_top * p1 / den, pg_top * p2 / den]
    out = jnp.zeros(lg.shape, F32)
    for pos, val in enumerate(vals):
        out = jnp.where(lane == pos, val, out)
    return out


def _outproj_body(nct, m_ref, wo_ref, xp_ref, xs_ref, gt_ref, g2_ref, sh_ref, sc_ref, wr_ref, br_ref,
                  x1_ref, h2_ref, rt_ref):
    i = pl.program_id(0)
    y = jnp.dot(m_ref[...], wo_ref[...], preferred_element_type=F32)

    def finish(x_ref):
        x1 = x_ref[...] + gt_ref[0] * y
        x1_ref[...] = x1
        h2 = _rms_scale(x1) * g2_ref[...]
        h2 = h2 * (1.0 + sc_ref[0]) + sh_ref[0]
        h2_ref[...] = h2
        hi = h2.astype(BF16)
        lo = (h2 - hi.astype(F32)).astype(BF16)
        p_hi = jnp.dot(hi, wr_ref[...], preferred_element_type=F32)
        p_lo = jnp.dot(lo, wr_ref[...], preferred_element_type=F32)
        lg = p_hi[:, :LANES] + p_hi[:, LANES:] + p_lo[:, :LANES] + p_lo[:, LANES:] + br_ref[...]
        rt_ref[...] = _route_rows(lg)

    @pl.when(i < nct)
    def _():
        finish(xp_ref)

    @pl.when(i >= nct)
    def _():
        finish(xs_ref)


def _outproj(rows, mixed, w_o, xp, xs, gt1, gain2, sh2, sc2, w_r, b_r):
    d = xp.shape[1]
    tm = rows.tm
    n_tok = rows.n * tm
    mod = lambda: pl.BlockSpec((1, 1, d), lambda i: (rows.mod_idx(i), 0, 0))
    tok = pl.BlockSpec((tm, d), lambda i: (i, 0))
    return pl.pallas_call(
        functools.partial(_outproj_body, rows.nct),
        grid=(rows.n,),
        in_specs=[tok,
                  pl.BlockSpec((d, d), lambda i: (0, 0)),
                  pl.BlockSpec((tm, d), lambda i: (rows.ctx_idx(i), 0)),
                  pl.BlockSpec((tm, d), lambda i: (rows.lat_idx(i), 0)),
                  mod(),
                  pl.BlockSpec((1, d), lambda i: (0, 0)),
                  mod(), mod(),
                  pl.BlockSpec((d, 2 * LANES), lambda i: (0, 0)),
                  pl.BlockSpec((1, LANES), lambda i: (0, 0))],
        out_specs=[tok, tok, pl.BlockSpec((tm, LANES), lambda i: (i, 0))],
        out_shape=[jax.ShapeDtypeStruct((n_tok, d), F32), jax.ShapeDtypeStruct((n_tok, d), F32),
                   jax.ShapeDtypeStruct((n_tok, LANES), F32)],
        compiler_params=_params("arbitrary"),
        name="outproj",
    )(mixed, w_o, xp, xs, gt1, gain2, sh2, sc2, w_r, b_r)


DISPATCH_BLOCK = 256
TBL_EXPERT, TBL_VALID, TBL_COUNT, TBL_OFFSET = 0, 1, 2, 3


def _dispatch_body(n_tok, route_ref, pos_ref, tbl_ref):
    nb = n_tok // DISPATCH_BLOCK
    lane = lax.broadcasted_iota(jnp.int32, (1, LANES), 1)
    lane_f = lane.astype(F32)

    def one_hot(b, k):
        r = pl.ds(pl.multiple_of(b * DISPATCH_BLOCK, DISPATCH_BLOCK), DISPATCH_BLOCK)
        return jnp.where(route_ref[r, ROUTE_E + k:ROUTE_E + k + 1] == lane_f, 1.0, 0.0)

    def count(b, acc):
        return acc + jnp.sum(one_hot(b, 0) + one_hot(b, 1), axis=0, keepdims=True)

    counts = lax.fori_loop(0, nb, count, jnp.zeros((1, LANES), F32))
    padded = jnp.floor((counts + (MOE_TILE - 1)) * (1.0 / MOE_TILE)) * MOE_TILE
    pad_end = padded
    shift = 1
    while shift < LANES:
        pad_end = pad_end + jnp.where(lane >= shift, pltpu.roll(pad_end, shift, 1), 0.0)
        shift *= 2
    pad_off = pad_end - padded

    ri = lax.broadcasted_iota(jnp.int32, (DISPATCH_BLOCK, DISPATCH_BLOCK), 0)
    ci = lax.broadcasted_iota(jnp.int32, (DISPATCH_BLOCK, DISPATCH_BLOCK), 1)
    before = jnp.where(ci < ri, 1.0, 0.0).astype(BF16)

    def place(b, run):
        oh = [one_hot(b, k) for k in range(TOP_K)]
        base = pad_off + run
        out = jnp.zeros((DISPATCH_BLOCK, LANES), F32)
        lane_b = lax.broadcasted_iota(jnp.int32, (DISPATCH_BLOCK, LANES), 1)
        for k in range(TOP_K):
            prior = jnp.dot(before, oh[k].astype(BF16), preferred_element_type=F32)
            pos = jnp.sum(oh[k] * (base + prior), axis=1, keepdims=True)
            out = jnp.where(lane_b == k, pos, out)
            base = base + jnp.sum(oh[k], axis=0, keepdims=True)
        r = pl.ds(pl.multiple_of(b * DISPATCH_BLOCK, DISPATCH_BLOCK), DISPATCH_BLOCK)
        pos_ref[r, :] = out.astype(jnp.int32)
        return base - pad_off

    lax.fori_loop(0, nb, place, jnp.zeros((1, LANES), F32))

    end_col = jnp.transpose(jnp.broadcast_to(pad_end, (8, LANES)))[:, 0:1]
    e_col = lax.broadcasted_iota(jnp.int32, (LANES, 1), 0)
    tile_start = lane_f * MOE_TILE
    passed = jnp.where(jnp.logical_and(end_col <= tile_start, e_col < N_EXPERTS), 1.0, 0.0)
    tile_expert = jnp.minimum(jnp.sum(passed, axis=0, keepdims=True), N_EXPERTS - 1.0)
    total = jnp.sum(jnp.where(lane == N_EXPERTS - 1, pad_end, 0.0), axis=1, keepdims=True)
    tile_valid = jnp.where(tile_start < total, 1.0, 0.0)
    row = lax.broadcasted_iota(jnp.int32, (8, LANES), 0)
    tbl = jnp.zeros((8, LANES), F32)
    for idx, val in ((TBL_EXPERT, tile_expert), (TBL_VALID, tile_valid), (TBL_COUNT, counts), (TBL_OFFSET, pad_off)):
        tbl = jnp.where(row == idx, val, tbl)
    tbl_ref[...] = tbl.astype(jnp.int32)


def _dispatch(route):
    n_tok = route.shape[0]
    return pl.pallas_call(
        functools.partial(_dispatch_body, n_tok),
        out_shape=[jax.ShapeDtypeStruct((n_tok, LANES), jnp.int32), jax.ShapeDtypeStruct((8, LANES), jnp.int32)],
        compiler_params=pltpu.CompilerParams(vmem_limit_bytes=VMEM_LIMIT),
        name="dispatch",
    )(route)


def _scatter_body(tm, n_tiles, pos_ref, cnt_ref, off_ref, h_ref, o_hbm, zbuf, sem):
    i = pl.program_id(0)

    @pl.when(i == 0)
    def _():
        zbuf[...] = jnp.zeros_like(zbuf)
        last = N_EXPERTS - 1
        first_empty = (off_ref[last] + cnt_ref[last] + MOE_TILE - 1) // MOE_TILE

        def tile_copy(t):
            rows = pl.ds(pl.multiple_of(t * MOE_TILE, MOE_TILE), MOE_TILE)
            return pltpu.make_async_copy(zbuf, o_hbm.at[rows, :], sem.at[1])

        def partial_tile(e, start):
            cnt = cnt_ref[e]

            @pl.when(cnt % MOE_TILE != 0)
            def _():
                cp = tile_copy((off_ref[e] + cnt) // MOE_TILE)
                if start:
                    cp.start()
                else:
                    cp.wait()

        def zero_partial(e, carry):
            partial_tile(e, True)
            return carry

        def partial_done(e, carry):
            partial_tile(e, False)
            return carry

        def zero_tile(t, carry):
            tile_copy(t).start()
            return carry

        def zero_done(t, carry):
            tile_copy(t).wait()
            return carry

        lax.fori_loop(0, N_EXPERTS, zero_partial, 0)
        lax.fori_loop(first_empty, n_tiles, zero_tile, 0)
        lax.fori_loop(0, N_EXPERTS, partial_done, 0)
        lax.fori_loop(first_empty, n_tiles, zero_done, 0)

    def row_copies(r):
        return [pltpu.make_async_copy(h_ref.at[pl.ds(r, 1), :],
                                      o_hbm.at[pl.ds(pos_ref[(i * tm + r) * TOP_K + k], 1), :], sem.at[0])
                for k in range(TOP_K)]

    def start(r, carry):
        for cp in row_copies(r):
            cp.start()
        return carry

    def wait(r, carry):
        for cp in row_copies(r):
            cp.wait()
        return carry

    lax.fori_loop(0, tm, start, 0, unroll=8)
    lax.fori_loop(0, tm, wait, 0, unroll=8)


def _scatter_rows(h2, pos_flat, counts, offsets, n_rows):
    n_tok, d = h2.shape
    tm = 256
    grid_spec = pltpu.PrefetchScalarGridSpec(
        num_scalar_prefetch=3,
        grid=(n_tok // tm,),
        in_specs=[pl.BlockSpec((tm, d), lambda i, pos, cnt, off: (i, 0))],
        out_specs=pl.BlockSpec(memory_space=pl.ANY),
        scratch_shapes=[pltpu.VMEM((MOE_TILE, d), F32), pltpu.SemaphoreType.DMA((2,))],
    )
    return pl.pallas_call(
        functools.partial(_scatter_body, tm, n_rows // MOE_TILE),
        grid_spec=grid_spec,
        out_shape=jax.ShapeDtypeStruct((n_rows, d), F32),
        compiler_params=_params("arbitrary"),
        name="scatter_rows",
    )(pos_flat, counts, offsets, h2)


def _moe_body(n_tiles, te_ref, tv_ref, cnt_ref, x_ref, wg_hbm, wu_hbm, wd_hbm, y_ref,
              wg_f, wu_f, wd_f, wg_b, wu_b, wd_b, slot_ref, sem):
    i = pl.program_id(0)
    prev = jnp.maximum(i - 1, 0)
    valid = tv_ref[i] == 1
    e = te_ref[i]

    def fetch(expert, slot):
        return [pltpu.make_async_copy(src.at[expert], dst.at[slot], sem.at[slot])
                for src, dst in ((wg_hbm, wg_f), (wu_hbm, wu_f), (wd_hbm, wd_f))]

    @pl.when(jnp.logical_and(valid, i == 0))
    def _():
        slot_ref[0] = 1
        for cp in fetch(e, 0):
            cp.start()

    @pl.when(jnp.logical_and(valid, jnp.logical_or(i == 0, e != te_ref[prev])))
    def _():
        slot = 1 - slot_ref[0]
        slot_ref[0] = slot
        for cp in fetch(e, slot):
            cp.wait()
        nxt = jnp.minimum(i + (cnt_ref[e] + MOE_TILE - 1) // MOE_TILE, n_tiles - 1)

        @pl.when(jnp.logical_and(nxt > i, tv_ref[nxt] == 1))
        def _():
            for cp in fetch(te_ref[nxt], 1 - slot):
                cp.start()

        wg_b[...] = wg_f[slot].astype(BF16)
        wu_b[...] = wu_f[slot].astype(BF16)
        wd_b[...] = wd_f[slot].astype(BF16)

    @pl.when(valid)
    def _():
        x = x_ref[...].astype(BF16)
        g = jnp.dot(x, wg_b[...], preferred_element_type=F32)
        u = jnp.dot(x, wu_b[...], preferred_element_type=F32)
        y_ref[...] = jnp.dot((_silu(g) * u).astype(BF16), wd_b[...], preferred_element_type=F32)

    @pl.when(tv_ref[i] == 0)
    def _():
        y_ref[...] = jnp.zeros_like(y_ref)


def _moe(x_sorted, tile_expert, tile_valid, counts, w_g, w_u, w_d):
    n_tiles = tile_expert.shape[0]
    d = x_sorted.shape[1]
    f = w_g.shape[-1]
    hbm = pl.BlockSpec(memory_space=pl.ANY)
    grid_spec = pltpu.PrefetchScalarGridSpec(
        num_scalar_prefetch=3,
        grid=(n_tiles,),
        in_specs=[pl.BlockSpec((MOE_TILE, d), lambda i, te, tv, cnt: (i * tv[i], 0)), hbm, hbm, hbm],
        out_specs=pl.BlockSpec((MOE_TILE, d), lambda i, te, tv, cnt: (i, 0)),
        scratch_shapes=[pltpu.VMEM((2, d, f), F32), pltpu.VMEM((2, d, f), F32), pltpu.VMEM((2, f, d), F32),
                        pltpu.VMEM((d, f), BF16), pltpu.VMEM((d, f), BF16), pltpu.VMEM((f, d), BF16),
                        pltpu.SMEM((1,), jnp.int32), pltpu.SemaphoreType.DMA((2,))],
    )
    return pl.pallas_call(
        functools.partial(_moe_body, n_tiles),
        grid_spec=grid_spec,
        out_shape=jax.ShapeDtypeStruct((n_tiles * MOE_TILE, d), F32),
        compiler_params=_params("arbitrary"),
        name="moe",
    )(tile_expert, tile_valid, counts, x_sorted, w_g, w_u, w_d)


def _combine_body(tm, n_tiles, row0, pos_ref, y_hbm, rt_ref, x1_ref, gt_ref, gf_ref, o_ref, ybuf, sem):
    i = pl.program_id(0)
    slot = i % 2

    def gather(tile, dst_slot, start):
        def row(r, carry):
            for k in range(TOP_K):
                p = pos_ref[(row0 + tile * tm + r) * TOP_K + k]
                cp = pltpu.make_async_copy(y_hbm.at[pl.ds(p, 1), :], ybuf.at[dst_slot, k, pl.ds(r, 1), :],
                                           sem.at[dst_slot])
                if start:
                    cp.start()
                else:
                    cp.wait()
            return carry

        lax.fori_loop(0, tm, row, 0, unroll=8)

    @pl.when(i == 0)
    def _():
        gather(0, 0, True)

    @pl.when(i + 1 < n_tiles)
    def _():
        gather(i + 1, 1 - slot, True)

    gather(i, slot, False)
    rt = rt_ref[...]
    y = rt[:, ROUTE_W:ROUTE_W + 1] * ybuf[slot, 0] + rt[:, ROUTE_W + 1:ROUTE_W + 2] * ybuf[slot, 1]
    x2 = x1_ref[...] + gt_ref[0] * y
    o_ref[...] = _rms_scale(x2) * gf_ref[...]


def _combine(y_sorted, pos_flat, route, x1, gt2, gain_f, row0, n_rows, mod_of_tile, tm):
    d = x1.shape[1]
    rb = row0 // tm
    n_tiles = n_rows // tm
    grid_spec = pltpu.PrefetchScalarGridSpec(
        num_scalar_prefetch=1,
        grid=(n_tiles,),
        in_specs=[pl.BlockSpec(memory_space=pl.ANY),
                  pl.BlockSpec((tm, LANES), lambda i, pos: (rb + i, 0)),
                  pl.BlockSpec((tm, d), lambda i, pos: (rb + i, 0)),
                  pl.BlockSpec((1, 1, d), lambda i, pos: (mod_of_tile(i), 0, 0)),
                  pl.BlockSpec((1, d), lambda i, pos: (0, 0))],
        out_specs=pl.BlockSpec((tm, d), lambda i, pos: (i, 0)),
        scratch_shapes=[pltpu.VMEM((2, TOP_K, tm, d), F32), pltpu.SemaphoreType.DMA((2,))],
    )
    return pl.pallas_call(
        functools.partial(_combine_body, tm, n_tiles, row0),
        grid_spec=grid_spec,
        out_shape=jax.ShapeDtypeStruct((n_rows, d), F32),
        compiler_params=_params("arbitrary"),
        name="combine",
    )(pos_flat, y_sorted, route, x1, gt2, gain_f)


GATE_SRC = {"beta": 0, "gc": 2 * H_A, "egc": 4 * H_A, "ekd": 6 * H_A}


def _gates_body(zs_ref, al_ref, dt_ref, sel_ref, o_ref):
    tm = zs_ref.shape[0]
    lane = lax.broadcasted_iota(jnp.int32, (1, LANES), 1)
    zs = zs_ref[...]
    x = zs + dt_ref[...]
    softplus = jnp.maximum(x, 0.0) + jnp.log(1.0 + jnp.exp(-jnp.abs(x)))
    in_g = jnp.logical_and(lane >= GATE_SRC["gc"], lane < GATE_SRC["egc"])
    g = jnp.where(in_g, -jnp.exp(al_ref[...]) * softplus, 0.0)
    beta = jax.nn.sigmoid(zs)
    reversed_lane = lane >= GATE_SRC["gc"] + H_A
    ri = lax.broadcasted_iota(jnp.int32, (CHUNK, CHUNK), 0)
    ci = lax.broadcasted_iota(jnp.int32, (CHUNK, CHUNK), 1)
    prefix = jnp.where(ci <= ri, 1.0, 0.0)
    suffix = jnp.where(ci >= ri, 1.0, 0.0)
    exact = dict(preferred_element_type=F32, precision=lax.Precision.HIGHEST)
    for c in range(tm // CHUNK):
        r = slice(c * CHUNK, (c + 1) * CHUNK)
        gch = g[r]
        gc = jnp.where(reversed_lane, jnp.dot(suffix, gch, **exact), jnp.dot(prefix, gch, **exact))
        tot = jnp.sum(gch, axis=0, keepdims=True)
        src = jnp.where(lane < GATE_SRC["gc"], beta[r],
                        jnp.where(lane < GATE_SRC["egc"], gc,
                                  jnp.where(lane < GATE_SRC["ekd"], pltpu.roll(jnp.exp(gc), 2 * H_A, 1),
                                            jnp.where(lane < GATE_SRC["ekd"] + 2 * H_A,
                                                      pltpu.roll(jnp.exp(tot - gc), 4 * H_A, 1), 0.0))))
        o_ref[r, :] = jnp.dot(src, sel_ref[...], **exact)


def _gate_table(zs, a_log, dt_bias):
    n_tok = zs.shape[0]
    tm = 512
    place = lambda v: jnp.pad(v.reshape(1, 2 * H_A), ((0, 0), (GATE_SRC["gc"], LANES - GATE_SRC["egc"])))
    src = jnp.arange(LANES)[:, None]
    dst = jnp.arange(H_A * LANES)[None, :]
    head, within = dst // LANES, dst % LANES
    kind_src = jnp.array([GATE_SRC["gc"], GATE_SRC["beta"], GATE_SRC["egc"], GATE_SRC["ekd"]])
    want = kind_src[within % GATE_KINDS] + (within // GATE_KINDS) * H_A + head
    sel = jnp.where(jnp.logical_and(within < 2 * GATE_KINDS, src == want), 1.0, 0.0).astype(F32)
    return pl.pallas_call(
        _gates_body,
        grid=(n_tok // tm,),
        in_specs=[pl.BlockSpec((tm, LANES), lambda i: (i, 0)),
                  pl.BlockSpec((1, LANES), lambda i: (0, 0)),
                  pl.BlockSpec((1, LANES), lambda i: (0, 0)),
                  pl.BlockSpec((LANES, H_A * LANES), lambda i: (0, 0))],
        out_specs=pl.BlockSpec((tm, H_A * LANES), lambda i: (i, 0)),
        out_shape=jax.ShapeDtypeStruct((n_tok, H_A * LANES), F32),
        compiler_params=_params("arbitrary"),
        name="gates",
    )(zs, place(a_log), place(dt_bias), sel)


def kernel(x_prompt, x_sample, c, state_delta, cache_k, cache_v, c_ctx, w_ada, b_ada, norm_mix, norm_ffn, w_in,
           conv_a, a_log, dt_bias, onorm_a, lam, subln_b, w_up_a, w_up_b, w_o, w_rg, b_rg, w_re, b_re,
           w_e_gate, w_e_up, w_e_down, norm_final):
    assert w_in.shape[0] == 1, "single trunk layer"
    l = 0
    lam_init = 0.8 - 0.6 * math.exp(-0.3 * l)
    bc, tc, d = x_prompt.shape
    bl, tl, _ = x_sample.shape
    n_ctx, n_lat = bc * tc, bl * tl
    xp = x_prompt.reshape(n_ctx, d)
    xs = x_sample.reshape(n_lat, d)

    cvec = jnp.zeros((N_MOD_ROWS, d), F32).at[0].set(c_ctx).at[1:1 + bl].set(c)
    mod = _adaln(cvec, w_ada[l], b_ada[l][None, :])
    sh1, sc1, gt1, sh2, sc2, gt2 = [m.reshape(N_MOD_ROWS, 1, d) for m in jnp.split(mod, 6, axis=1)]

    rows = _Rows(n_ctx, n_lat, tl, 512)
    w_in_t = jnp.swapaxes(w_in[l], 0, 1)
    h1, zs = _prenorm(rows, xp, xs, norm_mix[l][None, :], sh1, sc1, w_in_t)
    z = _inproj(h1, w_in_t)

    gates = _gate_table(zs, a_log[l], dt_bias[l])
    oa_c, new_state = _deltanet(z, conv_a[l], gates, None, bc, tc, 0, 8)
    oa_l = _deltanet(z, conv_a[l], gates, state_delta, bl, tl, n_ctx, 4)

    lv = lam[l]
    lam_val = jnp.exp(jnp.sum(lv[0] * lv[1])) - jnp.exp(jnp.sum(lv[2] * lv[3])) + lam_init
    lam_row = jnp.full((1, LANES), lam_val, F32)
    sub_row = subln_b[l][None, :]
    ob_c, new_k, new_v = _attn_ctx(z, lam_row, sub_row, bc, tc, lam_init)
    cos, sin = _rope_tables(tl)
    ob_l = _attn_lat(z, cache_k, cache_v, cos, sin, lam_row, sub_row, bl, tl, n_ctx, lam_init)

    mixed = _merge(_Rows(n_ctx, n_lat, tl, 256), z, oa_c, oa_l, ob_c, ob_l, onorm_a[l][None, :],
                   w_up_a[l].astype(BF16), w_up_b[l].astype(BF16))
    w_r = jnp.pad(jnp.concatenate([w_rg[l], w_re[l]], axis=1), ((0, 0), (0, LANES - N_GROUPS - N_EXPERTS)))
    w_r_hi = w_r.astype(BF16)
    w_r = jnp.concatenate([w_r_hi, (w_r - w_r_hi.astype(F32)).astype(BF16)], axis=1)
    b_r = jnp.pad(jnp.concatenate([b_rg[l], b_re[l]]), (0, LANES - N_GROUPS - N_EXPERTS))[None, :]
    x1, h2, route = _outproj(rows, mixed, w_o[l].astype(BF16), xp, xs, gt1, norm_ffn[l][None, :], sh2, sc2, w_r, b_r)

    n_tok = n_ctx + n_lat
    n_tiles = (n_tok * TOP_K) // MOE_TILE + N_EXPERTS
    pos, tbl = _dispatch(route)
    pos_flat = pos[:, :TOP_K].reshape(-1)
    counts, offsets = tbl[TBL_COUNT, :N_EXPERTS], tbl[TBL_OFFSET, :N_EXPERTS]
    x_sorted = _scatter_rows(h2, pos_flat, counts, offsets, n_tiles * MOE_TILE)
    y_sorted = _moe(x_sorted, tbl[TBL_EXPERT, :n_tiles], tbl[TBL_VALID, :n_tiles], counts,
                    w_e_gate[l], w_e_up[l], w_e_down[l])
    gf = norm_final[None, :]
    tm_c = 256
    y_prompt = _combine(y_sorted, pos_flat, route, x1, gt2, gf, 0, n_ctx, lambda i: 0 * i, tm_c)
    y_sample = _combine(y_sorted, pos_flat, route, x1, gt2, gf, n_ctx, n_lat, lambda i: 1 + i // (tl // tm_c), tm_c)
    return (y_prompt.reshape(bc, tc, d), y_sample.reshape(bl, tl, d), new_state, new_k, new_v)
```

```python
import functools
import math

import jax
import jax.numpy as jnp
from jax import lax
from jax.experimental import pallas as pl
from jax.experimental.pallas import tpu as pltpu

F32 = jnp.float32
BF16 = jnp.bfloat16

D_MODEL = 2048
GRID_W = 64
H_A = 8
DK_A = 128
DV_A = 128
CONV_K = 3
CHUNK = 64
H_B = 8
DQK_B = 64
DV_B = 2 * DQK_B
ROPE_BASE = 10000.0
N_GROUPS = 4
E_PER_GROUP = 8
N_EXPERTS = N_GROUPS * E_PER_GROUP
TOP_K = 2
D_FF_E = D_MODEL // 4
EPS = 1e-6

LANES = 128
QKV_A = 2 * H_A * DK_A + H_A * DV_A
N_SMALL = 4 * H_A
Z_GATE_A = QKV_A
Z_QB = Z_GATE_A + H_A * DV_A
Z_KB = Z_QB + H_B * 2 * DQK_B
Z_VB = Z_KB + H_B * 2 * DQK_B
Z_GM = Z_VB + H_B * DV_B
N_MAIN = Z_GM + 2 * D_MODEL

N_MOD_ROWS = 8
MOE_TILE = 256
VMEM_LIMIT = 56 * 1024 * 1024

NT_DIMS = (((1,), (1,)), ((), ()))


def _params(*sem):
    return pltpu.CompilerParams(dimension_semantics=sem, vmem_limit_bytes=VMEM_LIMIT)


def _mm(a, b):
    return jnp.dot(a.astype(BF16), b.astype(BF16), preferred_element_type=F32)


def _mm_nt(a, b):
    return lax.dot_general(a.astype(BF16), b.astype(BF16), NT_DIMS, preferred_element_type=F32)


def _silu(x):
    return x * jax.nn.sigmoid(x)


def _rms_scale(x):
    return x * lax.rsqrt(jnp.mean(x * x, axis=-1, keepdims=True) + EPS)


def _adaln_body(c_ref, w_ref, b_ref, o_ref):
    s = _silu(c_ref[...])
    o_ref[...] = _mm(s, w_ref[...]) + b_ref[...]


def _adaln(cvec, w, b):
    d, n = w.shape
    tn = 1024
    return pl.pallas_call(
        _adaln_body,
        grid=(n // tn,),
        in_specs=[pl.BlockSpec((N_MOD_ROWS, d), lambda j: (0, 0)),
                  pl.BlockSpec((d, tn), lambda j: (0, j)),
                  pl.BlockSpec((1, tn), lambda j: (0, j))],
        out_specs=pl.BlockSpec((N_MOD_ROWS, tn), lambda j: (0, j)),
        out_shape=jax.ShapeDtypeStruct((N_MOD_ROWS, n), F32),
        compiler_params=_params("arbitrary"),
        name="adaln",
    )(cvec, w, b)


class _Rows:
    def __init__(self, n_ctx, n_lat, t_lat, tm):
        assert n_ctx % tm == 0 and n_lat % tm == 0 and t_lat % tm == 0
        self.tm = tm
        self.nct = n_ctx // tm
        self.nlt = n_lat // tm
        self.per_seq = t_lat // tm
        self.n = self.nct + self.nlt

    def ctx_idx(self, i):
        return jnp.minimum(i, self.nct - 1)

    def lat_idx(self, i):
        return jnp.maximum(i - self.nct, 0)

    def mod_idx(self, i):
        return jnp.where(i < self.nct, 0, 1 + (i - self.nct) // self.per_seq)


SMALL_COL0 = QKV_A + H_A * DV_A


def _prenorm_body(nct, xp_ref, xs_ref, g_ref, sh_ref, sc_ref, ws_ref, h_ref, zs_ref):
    i = pl.program_id(0)

    def run(x_ref):
        h = _rms_scale(x_ref[...]) * g_ref[...]
        h = (h * (1.0 + sc_ref[0]) + sh_ref[0]).astype(BF16)
        h_ref[...] = h
        zs_ref[...] = lax.dot_general(h, ws_ref[...].astype(BF16), NT_DIMS, preferred_element_type=F32)

    @pl.when(i < nct)
    def _():
        run(xp_ref)

    @pl.when(i >= nct)
    def _():
        run(xs_ref)


def _prenorm(rows, xp, xs, gain, sh, sc, w_in_t):
    d = xp.shape[1]
    tm = rows.tm
    n_tok = rows.n * tm
    return pl.pallas_call(
        functools.partial(_prenorm_body, rows.nct),
        grid=(rows.n,),
        in_specs=[pl.BlockSpec((tm, d), lambda i: (rows.ctx_idx(i), 0)),
                  pl.BlockSpec((tm, d), lambda i: (rows.lat_idx(i), 0)),
                  pl.BlockSpec((1, d), lambda i: (0, 0)),
                  pl.BlockSpec((1, 1, d), lambda i: (rows.mod_idx(i), 0, 0)),
                  pl.BlockSpec((1, 1, d), lambda i: (rows.mod_idx(i), 0, 0)),
                  pl.BlockSpec((LANES, d), lambda i: (SMALL_COL0 // LANES, 0))],
        out_specs=[pl.BlockSpec((tm, d), lambda i: (i, 0)),
                   pl.BlockSpec((tm, LANES), lambda i: (i, 0))],
        out_shape=[jax.ShapeDtypeStruct((n_tok, d), BF16),
                   jax.ShapeDtypeStruct((n_tok, LANES), F32)],
        compiler_params=_params("arbitrary"),
        name="prenorm",
    )(xp, xs, gain, sh, sc, w_in_t)


INPROJ_TN = 1024
N_ALIGNED = SMALL_COL0 // INPROJ_TN


def _inproj_body(h_ref, wt_ref, z_ref, w_scr):
    @pl.when(pl.program_id(1) == 0)
    def _():
        w_scr[...] = wt_ref[...].astype(BF16)

    z_ref[...] = lax.dot_general(h_ref[...], w_scr[...], NT_DIMS, preferred_element_type=F32).astype(BF16)


def _inproj(h, w_in_t):
    n_tok, d = h.shape
    tm, tn = 1024, INPROJ_TN
    first_row = lambda j: pl.multiple_of(j * tn + N_SMALL * jnp.minimum(j // N_ALIGNED, 1), N_SMALL)
    return pl.pallas_call(
        _inproj_body,
        grid=(N_MAIN // tn, n_tok // tm),
        in_specs=[pl.BlockSpec((tm, d), lambda j, i: (i, 0)),
                  pl.BlockSpec((pl.Element(tn), pl.Element(d)), lambda j, i: (first_row(j), 0))],
        out_specs=pl.BlockSpec((tm, tn), lambda j, i: (i, j)),
        out_shape=jax.ShapeDtypeStruct((n_tok, N_MAIN), BF16),
        scratch_shapes=[pltpu.VMEM((tn, d), BF16)],
        compiler_params=_params("arbitrary", "arbitrary"),
        name="inproj",
    )(h, w_in_t)


GATE_KINDS = 4
INV_BLOCK = 16


def _chunk_solve(chains, ii, jj):
    eye = ii == jj
    blk = (ii // INV_BLOCK) == (jj // INV_BLOCK)
    blk2 = (ii // (2 * INV_BLOCK)) == (jj // (2 * INV_BLOCK))
    cols, a_qk, a0, off, d0, rhs = [], [], [], [], [], []
    for kk, qk, v, k, gates, reverse in chains:
        base = GATE_KINDS if reverse else 0
        gc, beta, egc, ekd = (gates[:, base + i:base + i + 1] for i in range(GATE_KINDS))
        incl = (jj >= ii) if reverse else (jj <= ii)
        strict = (jj > ii) if reverse else (jj < ii)
        gc_row = jnp.sum(jnp.where(eye, gc, 0.0), axis=0, keepdims=True)
        dec = jnp.where(incl, jnp.exp(jnp.where(incl, gc - gc_row, 0.0)), 0.0)
        a = jnp.where(strict, kk * dec, 0.0)
        cols.append((egc, ekd))
        a_qk.append(qk * dec)
        a0.append(jnp.where(blk, a, 0.0))
        off.append(jnp.where(blk, 0.0, a))
        d0.append(jnp.where(eye, 1.0, 0.0) - a0[-1])
        rhs.append(jnp.concatenate([v * beta, k * (beta * egc)], axis=1))
    p = [_mm(x, x) for x in a0]
    for _ in range(int(math.log2(INV_BLOCK)) - 2):
        r = [_mm(jnp.concatenate([pi, di], axis=0), pi) for pi, di in zip(p, d0)]
        p = [ri[:CHUNK] for ri in r]
        d0 = [di + ri[CHUNK:] for di, ri in zip(d0, r)]
    d0 = [di + _mm(di, pi) for di, pi in zip(d0, p)]
    wl = [_mm(di, oi) for di, oi in zip(d0, off)]
    yr = [_mm(di, ri) for di, ri in zip(d0, rhs)]
    x1 = [jnp.where(blk2, x, 0.0) for x in wl]
    yl = [jnp.where(blk2, 0.0, x) for x in wl]
    zl = [y - _mm(x, y) for x, y in zip(x1, yl)]
    zr = [y - _mm(x, y) for x, y in zip(x1, yr)]
    sol = [y - _mm(x, y) for x, y in zip(zl, zr)]
    return [(so[:, :DV_A], so[:, DV_A:], aq, egc, ekd) for so, aq, (egc, ekd) in zip(sol, a_qk, cols)]


def _delta_body(t_len, hp, has_s0, q_ref, k_ref, v_ref, cq_ref, ck_ref, cv_ref, g_ref, *rest):
    if has_s0:
        s0_ref, o_ref, qs, ks, vs, u_scr, wq_scr, ak_scr, s_scr = rest
    else:
        o_ref, sfin_ref, qs, ks, vs, u_scr, wq_scr, ak_scr, s_scr = rest
    n = t_len // CHUNK
    tpos = lax.broadcasted_iota(jnp.int32, (t_len, 1), 0)

    def conv_act(x, w):
        x_prev = jnp.where(tpos == 0, 0.0, pltpu.roll(x, 1, 0))
        x_next = jnp.where(tpos == t_len - 1, 0.0, pltpu.roll(x, t_len - 1, 0))
        return _silu(x_prev * w[0:1] + x * w[1:2] + x_next * w[2:3])

    def l2n(x):
        return x * lax.rsqrt(jnp.sum(x * x, axis=-1, keepdims=True) + EPS)

    for hh in range(hp):
        cols = slice(hh * LANES, (hh + 1) * LANES)
        qs[:, cols] = l2n(conv_act(q_ref[:, cols].astype(F32), cq_ref[:, cols])) * (DK_A ** -0.5)
        ks[:, cols] = l2n(conv_act(k_ref[:, cols].astype(F32), ck_ref[:, cols]))
        vs[:, cols] = conv_act(v_ref[:, cols].astype(F32), cv_ref[:, cols])
    if has_s0:
        s_scr[...] = s0_ref[...]
    else:
        s_scr[...] = jnp.zeros_like(s_scr)
    o_ref[...] = jnp.zeros_like(o_ref)
    ii = lax.broadcasted_iota(jnp.int32, (CHUNK, CHUNK), 0)
    jj = lax.broadcasted_iota(jnp.int32, (CHUNK, CHUNK), 1)

    def prep(c, carry):
        r = pl.ds(pl.multiple_of(c * CHUNK, CHUNK), CHUNK)
        heads = []
        for hh in range(hp):
            cols = slice(hh * LANES, (hh + 1) * LANES)
            heads.append((qs[r, cols], ks[r, cols], vs[r, cols], g_ref[r, cols]))
        rr = [_mm_nt(jnp.concatenate([k * g[:, 1:2], k * g[:, GATE_KINDS + 1:GATE_KINDS + 2], q], axis=0), k)
              for q, k, v, g in heads]
        chains = [(rr[hh][d * CHUNK:(d + 1) * CHUNK], rr[hh][2 * CHUNK:], heads[hh][2], heads[hh][1], heads[hh][3],
                   d == 1) for hh in range(hp) for d in (0, 1)]
        solved = _chunk_solve(chains, ii, jj)
        for hh in range(hp):
            cols = slice(hh * LANES, (hh + 1) * LANES)
            q, k = heads[hh][0], heads[hh][1]
            for d in (0, 1):
                u, w, a_qk, egc, ekd = solved[hh * 2 + d]
                slot = (d * hp + hh) * n + c
                u_scr[d, r, cols] = u
                wq_scr[slot] = jnp.concatenate([w, q * egc], axis=0).astype(BF16)
                ak_scr[slot] = jnp.concatenate([a_qk, (k * ekd).T], axis=0).astype(BF16)
        return carry

    lax.fori_loop(0, n, prep, 0)

    def scan(c, carry):
        chains = []
        for hh in range(hp):
            for d in (0, 1):
                cc = c if d == 0 else n - 1 - c
                chains.append((hh, d, pl.multiple_of(cc * CHUNK, CHUNK), (d * hp + hh) * n + cc))
        s = [s_scr[d, hh] for hh, d, r0, slot in chains]
        r1 = [jnp.dot(wq_scr[slot], si.astype(BF16), preferred_element_type=F32)
              for si, (hh, d, r0, slot) in zip(s, chains)]
        v_new = [u_scr[d, pl.ds(r0, CHUNK), hh * LANES:(hh + 1) * LANES] - ri[:CHUNK]
                 for ri, (hh, d, r0, slot) in zip(r1, chains)]
        r2 = [jnp.dot(ak_scr[slot], vi.astype(BF16), preferred_element_type=F32)
              for vi, (hh, d, r0, slot) in zip(v_new, chains)]
        for si, r1i, r2i, (hh, d, r0, slot) in zip(s, r1, r2, chains):
            o_ref[pl.ds(r0, CHUNK), hh * LANES:(hh + 1) * LANES] += r1i[CHUNK:] + r2i[:CHUNK]
            lane = hh * LANES + d * GATE_KINDS + 2
            edge = g_ref[pl.ds(pl.multiple_of(r0 + (0 if d else CHUNK - 8), 8), 8), lane:lane + 1]
            s_scr[d, hh] = si * (edge[0:1] if d else edge[7:8]) + r2i[CHUNK:]
        return carry

    lax.fori_loop(0, n, scan, 0)
    if not has_s0:
        sfin_ref[...] = s_scr[...]


def _deltanet(z, conv_w, gates, s0, n_seq, t_len, row0, hp):
    rb = row0 // t_len
    w = hp * LANES
    n = t_len // CHUNK
    seq_blk = lambda col0: pl.BlockSpec((t_len, w), lambda b, h: (rb + b, col0 // hp + h))
    cw_blk = lambda col0: pl.BlockSpec((CONV_K, w), lambda b, h: (0, col0 // hp + h))
    state_blk = pl.BlockSpec((None, None, 2, hp, DK_A, DV_A), lambda b, h: (b, 0, 0, h, 0, 0))
    in_specs = [seq_blk(0), seq_blk(H_A), seq_blk(2 * H_A), cw_blk(0), cw_blk(H_A), cw_blk(2 * H_A),
                pl.BlockSpec((t_len, w), lambda b, h: (rb + b, h))]
    args = [z, z, z, conv_w, conv_w, conv_w, gates]
    o_spec = pl.BlockSpec((t_len, w), lambda b, h: (b, h))
    o_shape = jax.ShapeDtypeStruct((n_seq * t_len, H_A * DV_A), F32)
    has_s0 = s0 is not None
    if has_s0:
        in_specs += [state_blk]
        args += [s0]
        out_specs, out_shape = o_spec, o_shape
    else:
        out_specs = [o_spec, state_blk]
        out_shape = [o_shape, jax.ShapeDtypeStruct((n_seq, 1, 2, H_A, DK_A, DV_A), F32)]

    return pl.pallas_call(
        functools.partial(_delta_body, t_len, hp, has_s0),
        grid=(n_seq, H_A // hp),
        in_specs=in_specs,
        out_specs=out_specs,
        out_shape=out_shape,
        scratch_shapes=[pltpu.VMEM((t_len, w), F32)] * 3
        + [pltpu.VMEM((2, t_len, w), F32),
           pltpu.VMEM((2 * hp * n, 2 * CHUNK, DV_A), BF16),
           pltpu.VMEM((2 * hp * n, CHUNK + DK_A, CHUNK), BF16),
           pltpu.VMEM((2, hp, DK_A, DV_A), F32)],
        compiler_params=_params("arbitrary", "arbitrary"),
        name="deltanet_lat" if has_s0 else "deltanet_ctx",
    )(*args)


def _subln(o, sub_ref, lam_init):
    return _rms_scale(o) * sub_ref[...] * (1.0 - lam_init)


def _attn_ctx_body(lam_init, q_ref, k_ref, v_ref, lam_ref, sub_ref, o_ref, ck_ref, cv_ref):
    heads = [slice(h * LANES, (h + 1) * LANES) for h in range(H_B)]
    lam = lam_ref[0:1, 0:1]
    ks = [k_ref[:, c] for c in heads]
    vs = [v_ref[:, c] for c in heads]
    for h in range(H_B):
        ck_ref[h] = ks[h].astype(F32)
        cv_ref[h] = vs[h].astype(F32)
    qb = [q_ref[:, c] * (DQK_B ** -0.5) for c in heads]
    kb = ks
    probs = []
    for lo in (0, DQK_B):
        s = [lax.dot_general(q[:, lo:lo + DQK_B], k[:, lo:lo + DQK_B], NT_DIMS, preferred_element_type=F32)
             for q, k in zip(qb, kb)]
        e = [jnp.exp(x - jnp.max(x, axis=-1, keepdims=True)) for x in s]
        probs.append([x / jnp.sum(x, axis=-1, keepdims=True) for x in e])
    o = [jnp.dot((p1 - lam * p2).astype(BF16), v, preferred_element_type=F32)
         for p1, p2, v in zip(probs[0], probs[1], vs)]
    for c, oh in zip(heads, o):
        o_ref[:, c] = _subln(oh, sub_ref, lam_init).astype(BF16)


def _attn_ctx(z, lam, subln, n_seq, t_len, lam_init):
    n_tok = n_seq * t_len
    w = H_B * LANES
    blk = lambda col0: pl.BlockSpec((t_len, w), lambda b: (b, col0 // w))
    cache_blk = pl.BlockSpec((None, None, H_B, t_len, LANES), lambda b: (b, 0, 0, 0, 0))
    cache_shape = jax.ShapeDtypeStruct((n_seq, 1, H_B, t_len, LANES), F32)
    return pl.pallas_call(
        functools.partial(_attn_ctx_body, lam_init),
        grid=(n_seq,),
        in_specs=[blk(Z_QB), blk(Z_KB), blk(Z_VB),
                  pl.BlockSpec((1, LANES), lambda b: (0, 0)),
                  pl.BlockSpec((1, LANES), lambda b: (0, 0))],
        out_specs=[pl.BlockSpec((t_len, w), lambda b: (b, 0)), cache_blk, cache_blk],
        out_shape=[jax.ShapeDtypeStruct((n_tok, H_B * DV_B), BF16), cache_shape, cache_shape],
        compiler_params=_params("arbitrary"),
        name="attn_ctx",
    )(z, z, z, lam, subln)


def _rope(x, cos, sin_signed):
    lane = lax.broadcasted_iota(jnp.int32, (1, LANES), 1)
    first = (lane % 32) < 16
    partner = jnp.where(first, pltpu.roll(x, LANES - 16, 1), pltpu.roll(x, 16, 1))
    return x * cos + partner * sin_signed


LAT_HEADS = 2


def _attn_lat_body(lam_init, n_past, q_ref, k_ref, v_ref, pk_ref, pv_ref, cosq_ref, sinq_ref, cos_ref, sin_ref,
                   lam_ref, sub_ref, o_ref, keys, vals):
    heads = [slice(h * LANES, (h + 1) * LANES) for h in range(LAT_HEADS)]

    @pl.when(pl.program_id(2) == 0)
    def _():
        for h, c in enumerate(heads):
            keys[h, 0:n_past, :] = pk_ref[h].astype(BF16)
            vals[h, 0:n_past, :] = pv_ref[h].astype(BF16)
            keys[h, n_past:, :] = _rope(k_ref[:, c].astype(F32), cos_ref[...], sin_ref[...]).astype(BF16)
            vals[h, n_past:, :] = v_ref[:, c]

    lam = lam_ref[0:1, 0:1]
    qb = [(_rope(q_ref[:, c].astype(F32), cosq_ref[...], sinq_ref[...]) * (DQK_B ** -0.5)).astype(BF16)
          for c in heads]
    probs = []
    for lo in (0, DQK_B):
        s = [lax.dot_general(q[:, lo:lo + DQK_B], keys[h, :, lo:lo + DQK_B], NT_DIMS, preferred_element_type=F32)
             for h, q in enumerate(qb)]
        e = [jnp.exp(x - jnp.max(x, axis=-1, keepdims=True)) for x in s]
        probs.append([x / jnp.sum(x, axis=-1, keepdims=True) for x in e])
    for h, c in enumerate(heads):
        a = (probs[0][h] - lam * probs[1][h]).astype(BF16)
        o_ref[:, c] = _subln(jnp.dot(a, vals[h], preferred_element_type=F32), sub_ref, lam_init).astype(BF16)


def _attn_lat(z, cache_k, cache_v, cos, sin, lam, subln, n_seq, t_len, row0, lam_init):
    tq = 256
    nq = t_len // tq
    n_past = cache_k.shape[3]
    rbq = row0 // tq
    rbs = row0 // t_len
    w = LAT_HEADS * LANES
    seq_blk = lambda col0: pl.BlockSpec((t_len, w), lambda b, h, qi: (rbs + b, col0 // w + h))
    past_blk = pl.BlockSpec((None, None, LAT_HEADS, n_past, LANES), lambda b, h, qi: (b, 0, h, 0, 0))
    row_vec = pl.BlockSpec((1, LANES), lambda b, h, qi: (0, 0))
    return pl.pallas_call(
        functools.partial(_attn_lat_body, lam_init, n_past),
        grid=(n_seq, H_B // LAT_HEADS, nq),
        in_specs=[pl.BlockSpec((tq, w), lambda b, h, qi: (rbq + b * nq + qi, Z_QB // w + h)),
                  seq_blk(Z_KB), seq_blk(Z_VB), past_blk, past_blk,
                  pl.BlockSpec((tq, LANES), lambda b, h, qi: (qi, 0)),
                  pl.BlockSpec((tq, LANES), lambda b, h, qi: (qi, 0)),
                  pl.BlockSpec((t_len, LANES), lambda b, h, qi: (0, 0)),
                  pl.BlockSpec((t_len, LANES), lambda b, h, qi: (0, 0)),
                  row_vec, row_vec],
        out_specs=pl.BlockSpec((tq, w), lambda b, h, qi: (b * nq + qi, h)),
        out_shape=jax.ShapeDtypeStruct((n_seq * t_len, H_B * DV_B), BF16),
        scratch_shapes=[pltpu.VMEM((LAT_HEADS, n_past + t_len, LANES), BF16)] * 2,
        compiler_params=_params("arbitrary", "arbitrary", "arbitrary"),
        name="attn_lat",
    )(z, z, z, cache_k, cache_v, cos, sin, cos, sin, lam, subln)


def _rope_tables(t_len):
    t = jnp.arange(t_len)
    pos = jnp.stack([t // GRID_W, t % GRID_W], axis=1).astype(F32)
    nf = DQK_B // 4
    inv_freq = ROPE_BASE ** (-jnp.arange(nf, dtype=F32) / nf)
    lane = jnp.arange(LANES)
    half = (lane % DQK_B) // (DQK_B // 2)
    ang = pos[:, half] * inv_freq[lane % nf][None, :]
    sign = jnp.where((lane % (DQK_B // 2)) < nf, -1.0, 1.0).astype(F32)
    return jnp.cos(ang), jnp.sin(ang) * sign[None, :]


def _merge_body(nct, oac_ref, oal_ref, ga_ref, on_ref, obc_ref, obl_ref, wa_ref, wb_ref, *rest):
    gm_refs, m_ref = rest[:-1], rest[-1]
    i = pl.program_id(0)
    n_blk = len(gm_refs) // 2

    def run(oa_ref, ob_ref):
        a = jnp.concatenate(
            [(_rms_scale(oa_ref[:, c]) * on_ref[...] * _silu(ga_ref[:, c].astype(F32))).astype(BF16)
             for c in (slice(h * DV_A, (h + 1) * DV_A) for h in range(H_A))], axis=1)
        ya = jnp.dot(a, wa_ref[...], preferred_element_type=F32)
        yb = jnp.dot(ob_ref[...], wb_ref[...], preferred_element_type=F32)
        tn = gm_refs[0].shape[1]
        for j in range(n_blk):
            c = slice(j * tn, (j + 1) * tn)
            m_ref[:, c] = (jax.nn.sigmoid(gm_refs[j][...].astype(F32)) * ya[:, c]
                           + jax.nn.sigmoid(gm_refs[n_blk + j][...].astype(F32)) * yb[:, c]).astype(BF16)

    @pl.when(i < nct)
    def _():
        run(oac_ref, obc_ref)

    @pl.when(i >= nct)
    def _():
        run(oal_ref, obl_ref)


def _merge(rows, z, oa_c, oa_l, ob_c, ob_l, onorm, w_up_a, w_up_b):
    n_tok = z.shape[0]
    tm = rows.tm
    ka, d = w_up_a.shape
    tn = 1024
    nj = d // tn
    ctx_blk = pl.BlockSpec((tm, ka), lambda i: (rows.ctx_idx(i), 0))
    lat_blk = pl.BlockSpec((tm, ka), lambda i: (rows.lat_idx(i), 0))
    weight = pl.BlockSpec((ka, d), lambda i: (0, 0))
    gate_cols = [pl.BlockSpec((tm, tn), functools.partial(lambda i, c: (i, c), c=Z_GM // tn + j)) for j in range(2 * nj)]
    return pl.pallas_call(
        functools.partial(_merge_body, rows.nct),
        grid=(n_tok // tm,),
        in_specs=[ctx_blk, lat_blk,
                  pl.BlockSpec((tm, ka), lambda i: (i, Z_GATE_A // ka)),
                  pl.BlockSpec((1, DV_A), lambda i: (0, 0)),
                  ctx_blk, lat_blk, weight, weight] + gate_cols,
        out_specs=pl.BlockSpec((tm, d), lambda i: (i, 0)),
        out_shape=jax.ShapeDtypeStruct((n_tok, d), BF16),
        compiler_params=_params("arbitrary"),
        name="merge",
    )(oa_c, oa_l, z, onorm, ob_c, ob_l, w_up_a, w_up_b, *([z] * (2 * nj)))


ROUTE_E = 0
ROUTE_W = TOP_K


def _route_rows(lg):
    lane = lax.broadcasted_iota(jnp.int32, lg.shape, 1)
    neg = -jnp.inf

    def first_max(x):
        m = jnp.max(x, axis=1, keepdims=True)
        return m, jnp.min(jnp.where(x == m, lane, LANES), axis=1, keepdims=True)

    gl = jnp.where(lane < N_GROUPS, lg, neg)
    gmax, g_idx = first_max(gl)
    pg_top = 1.0 / jnp.sum(jnp.exp(gl - gmax), axis=1, keepdims=True)
    lo = N_GROUPS + E_PER_GROUP * g_idx
    el = jnp.where(jnp.logical_and(lane >= lo, lane < lo + E_PER_GROUP), lg, neg)
    emax, i1 = first_max(el)
    esum = jnp.sum(jnp.exp(el - emax), axis=1, keepdims=True)
    e2max, i2 = first_max(jnp.where(lane == i1, neg, el))
    p1 = 1.0 / esum
    p2 = jnp.exp(e2max - emax) / esum
    den = p1 + p2
    vals = [(i1 - N_GROUPS).astype(F32), (i2 - N_GROUPS).astype(F32), pg_top * p1 / den, pg_top * p2 / den]
    out = jnp.zeros(lg.shape, F32)
    for pos, val in enumerate(vals):
        out = jnp.where(lane == pos, val, out)
    return out


def _outproj_body(nct, m_ref, wo_ref, xp_ref, xs_ref, gt_ref, g2_ref, sh_ref, sc_ref, wr_ref, br_ref,
                  x1_ref, h2_ref, rt_ref):
    i = pl.program_id(0)
    y = jnp.dot(m_ref[...], wo_ref[...], preferred_element_type=F32)

    def finish(x_ref):
        x1 = x_ref[...] + gt_ref[0] * y
        x1_ref[...] = x1
        h2 = _rms_scale(x1) * g2_ref[...]
        h2 = h2 * (1.0 + sc_ref[0]) + sh_ref[0]
        h2_ref[...] = h2
        hi = h2.astype(BF16)
        lo = (h2 - hi.astype(F32)).astype(BF16)
        p_hi = jnp.dot(hi, wr_ref[...], preferred_element_type=F32)
        p_lo = jnp.dot(lo, wr_ref[...], preferred_element_type=F32)
        lg = p_hi[:, :LANES] + p_hi[:, LANES:] + p_lo[:, :LANES] + p_lo[:, LANES:] + br_ref[...]
        rt_ref[...] = _route_rows(lg)

    @pl.when(i < nct)
    def _():
        finish(xp_ref)

    @pl.when(i >= nct)
    def _():
        finish(xs_ref)


def _outproj(rows, mixed, w_o, xp, xs, gt1, gain2, sh2, sc2, w_r, b_r):
    d = xp.shape[1]
    tm = rows.tm
    n_tok = rows.n * tm
    mod = lambda: pl.BlockSpec((1, 1, d), lambda i: (rows.mod_idx(i), 0, 0))
    tok = pl.BlockSpec((tm, d), lambda i: (i, 0))
    return pl.pallas_call(
        functools.partial(_outproj_body, rows.nct),
        grid=(rows.n,),
        in_specs=[tok,
                  pl.BlockSpec((d, d), lambda i: (0, 0)),
                  pl.BlockSpec((tm, d), lambda i: (rows.ctx_idx(i), 0)),
                  pl.BlockSpec((tm, d), lambda i: (rows.lat_idx(i), 0)),
                  mod(),
                  pl.BlockSpec((1, d), lambda i: (0, 0)),
                  mod(), mod(),
                  pl.BlockSpec((d, 2 * LANES), lambda i: (0, 0)),
                  pl.BlockSpec((1, LANES), lambda i: (0, 0))],
        out_specs=[tok, tok, pl.BlockSpec((tm, LANES), lambda i: (i, 0))],
        out_shape=[jax.ShapeDtypeStruct((n_tok, d), F32), jax.ShapeDtypeStruct((n_tok, d), F32),
                   jax.ShapeDtypeStruct((n_tok, LANES), F32)],
        compiler_params=_params("arbitrary"),
        name="outproj",
    )(mixed, w_o, xp, xs, gt1, gain2, sh2, sc2, w_r, b_r)


DISPATCH_BLOCK = 256
TBL_EXPERT, TBL_VALID, TBL_COUNT, TBL_OFFSET = 0, 1, 2, 3


def _dispatch_body(n_tok, route_ref, pos_ref, tbl_ref):
    nb = n_tok // DISPATCH_BLOCK
    lane = lax.broadcasted_iota(jnp.int32, (1, LANES), 1)
    lane_f = lane.astype(F32)

    def one_hot(b, k):
        r = pl.ds(pl.multiple_of(b * DISPATCH_BLOCK, DISPATCH_BLOCK), DISPATCH_BLOCK)
        return jnp.where(route_ref[r, ROUTE_E + k:ROUTE_E + k + 1] == lane_f, 1.0, 0.0)

    def count(b, acc):
        return acc + jnp.sum(one_hot(b, 0) + one_hot(b, 1), axis=0, keepdims=True)

    counts = lax.fori_loop(0, nb, count, jnp.zeros((1, LANES), F32))
    padded = jnp.floor((counts + (MOE_TILE - 1)) * (1.0 / MOE_TILE)) * MOE_TILE
    pad_end = padded
    shift = 1
    while shift < LANES:
        pad_end = pad_end + jnp.where(lane >= shift, pltpu.roll(pad_end, shift, 1), 0.0)
        shift *= 2
    pad_off = pad_end - padded

    ri = lax.broadcasted_iota(jnp.int32, (DISPATCH_BLOCK, DISPATCH_BLOCK), 0)
    ci = lax.broadcasted_iota(jnp.int32, (DISPATCH_BLOCK, DISPATCH_BLOCK), 1)
    before = jnp.where(ci < ri, 1.0, 0.0).astype(BF16)

    def place(b, run):
        oh = [one_hot(b, k) for k in range(TOP_K)]
        base = pad_off + run
        out = jnp.zeros((DISPATCH_BLOCK, LANES), F32)
        lane_b = lax.broadcasted_iota(jnp.int32, (DISPATCH_BLOCK, LANES), 1)
        for k in range(TOP_K):
            prior = jnp.dot(before, oh[k].astype(BF16), preferred_element_type=F32)
            pos = jnp.sum(oh[k] * (base + prior), axis=1, keepdims=True)
            out = jnp.where(lane_b == k, pos, out)
            base = base + jnp.sum(oh[k], axis=0, keepdims=True)
        r = pl.ds(pl.multiple_of(b * DISPATCH_BLOCK, DISPATCH_BLOCK), DISPATCH_BLOCK)
        pos_ref[r, :] = out.astype(jnp.int32)
        return base - pad_off

    lax.fori_loop(0, nb, place, jnp.zeros((1, LANES), F32))

    end_col = jnp.transpose(jnp.broadcast_to(pad_end, (8, LANES)))[:, 0:1]
    e_col = lax.broadcasted_iota(jnp.int32, (LANES, 1), 0)
    tile_start = lane_f * MOE_TILE
    passed = jnp.where(jnp.logical_and(end_col <= tile_start, e_col < N_EXPERTS), 1.0, 0.0)
    tile_expert = jnp.minimum(jnp.sum(passed, axis=0, keepdims=True), N_EXPERTS - 1.0)
    total = jnp.sum(jnp.where(lane == N_EXPERTS - 1, pad_end, 0.0), axis=1, keepdims=True)
    tile_valid = jnp.where(tile_start < total, 1.0, 0.0)
    row = lax.broadcasted_iota(jnp.int32, (8, LANES), 0)
    tbl = jnp.zeros((8, LANES), F32)
    for idx, val in ((TBL_EXPERT, tile_expert), (TBL_VALID, tile_valid), (TBL_COUNT, counts), (TBL_OFFSET, pad_off)):
        tbl = jnp.where(row == idx, val, tbl)
    tbl_ref[...] = tbl.astype(jnp.int32)


def _dispatch(route):
    n_tok = route.shape[0]
    return pl.pallas_call(
        functools.partial(_dispatch_body, n_tok),
        out_shape=[jax.ShapeDtypeStruct((n_tok, LANES), jnp.int32), jax.ShapeDtypeStruct((8, LANES), jnp.int32)],
        compiler_params=pltpu.CompilerParams(vmem_limit_bytes=VMEM_LIMIT),
        name="dispatch",
    )(route)


def _scatter_body(tm, n_tiles, pos_ref, cnt_ref, off_ref, h_ref, o_hbm, zbuf, sem):
    i = pl.program_id(0)

    @pl.when(i == 0)
    def _():
        zbuf[...] = jnp.zeros_like(zbuf)
        last = N_EXPERTS - 1
        first_empty = (off_ref[last] + cnt_ref[last] + MOE_TILE - 1) // MOE_TILE

        def tile_copy(t):
            rows = pl.ds(pl.multiple_of(t * MOE_TILE, MOE_TILE), MOE_TILE)
            return pltpu.make_async_copy(zbuf, o_hbm.at[rows, :], sem.at[1])

        def partial_tile(e, start):
            cnt = cnt_ref[e]

            @pl.when(cnt % MOE_TILE != 0)
            def _():
                cp = tile_copy((off_ref[e] + cnt) // MOE_TILE)
                if start:
                    cp.start()
                else:
                    cp.wait()

        def zero_partial(e, carry):
            partial_tile(e, True)
            return carry

        def partial_done(e, carry):
            partial_tile(e, False)
            return carry

        def zero_tile(t, carry):
            tile_copy(t).start()
            return carry

        def zero_done(t, carry):
            tile_copy(t).wait()
            return carry

        lax.fori_loop(0, N_EXPERTS, zero_partial, 0)
        lax.fori_loop(first_empty, n_tiles, zero_tile, 0)
        lax.fori_loop(0, N_EXPERTS, partial_done, 0)
        lax.fori_loop(first_empty, n_tiles, zero_done, 0)

    def row_copies(r):
        return [pltpu.make_async_copy(h_ref.at[pl.ds(r, 1), :],
                                      o_hbm.at[pl.ds(pos_ref[(i * tm + r) * TOP_K + k], 1), :], sem.at[0])
                for k in range(TOP_K)]

    def start(r, carry):
        for cp in row_copies(r):
            cp.start()
        return carry

    def wait(r, carry):
        for cp in row_copies(r):
            cp.wait()
        return carry

    lax.fori_loop(0, tm, start, 0, unroll=8)
    lax.fori_loop(0, tm, wait, 0, unroll=8)


def _scatter_rows(h2, pos_flat, counts, offsets, n_rows):
    n_tok, d = h2.shape
    tm = 512
    grid_spec = pltpu.PrefetchScalarGridSpec(
        num_scalar_prefetch=3,
        grid=(n_tok // tm,),
        in_specs=[pl.BlockSpec((tm, d), lambda i, pos, cnt, off: (i, 0))],
        out_specs=pl.BlockSpec(memory_space=pl.ANY),
        scratch_shapes=[pltpu.VMEM((MOE_TILE, d), F32), pltpu.SemaphoreType.DMA((2,))],
    )
    return pl.pallas_call(
        functools.partial(_scatter_body, tm, n_rows // MOE_TILE),
        grid_spec=grid_spec,
        out_shape=jax.ShapeDtypeStruct((n_rows, d), F32),
        compiler_params=_params("arbitrary"),
        name="scatter_rows",
    )(pos_flat, counts, offsets, h2)


def _moe_body(n_tiles, te_ref, tv_ref, cnt_ref, x_ref, wg_hbm, wu_hbm, wd_hbm, y_ref,
              wg_f, wu_f, wd_f, wg_b, wu_b, wd_b, slot_ref, sem):
    i = pl.program_id(0)
    prev = jnp.maximum(i - 1, 0)
    valid = tv_ref[i] == 1
    e = te_ref[i]

    def fetch(expert, slot):
        return [pltpu.make_async_copy(src.at[expert], dst.at[slot], sem.at[slot])
                for src, dst in ((wg_hbm, wg_f), (wu_hbm, wu_f), (wd_hbm, wd_f))]

    @pl.when(jnp.logical_and(valid, i == 0))
    def _():
        slot_ref[0] = 1
        for cp in fetch(e, 0):
            cp.start()

    @pl.when(jnp.logical_and(valid, jnp.logical_or(i == 0, e != te_ref[prev])))
    def _():
        slot = 1 - slot_ref[0]
        slot_ref[0] = slot
        for cp in fetch(e, slot):
            cp.wait()
        nxt = jnp.minimum(i + (cnt_ref[e] + MOE_TILE - 1) // MOE_TILE, n_tiles - 1)

        @pl.when(jnp.logical_and(nxt > i, tv_ref[nxt] == 1))
        def _():
            for cp in fetch(te_ref[nxt], 1 - slot):
                cp.start()

        wg_b[...] = wg_f[slot].astype(BF16)
        wu_b[...] = wu_f[slot].astype(BF16)
        wd_b[...] = wd_f[slot].astype(BF16)

    @pl.when(valid)
    def _():
        x = x_ref[...].astype(BF16)
        g = jnp.dot(x, wg_b[...], preferred_element_type=F32)
        u = jnp.dot(x, wu_b[...], preferred_element_type=F32)
        y_ref[...] = jnp.dot((_silu(g) * u).astype(BF16), wd_b[...], preferred_element_type=F32)

    @pl.when(tv_ref[i] == 0)
    def _():
        y_ref[...] = jnp.zeros_like(y_ref)


def _moe(x_sorted, tile_expert, tile_valid, counts, w_g, w_u, w_d):
    n_tiles = tile_expert.shape[0]
    d = x_sorted.shape[1]
    f = w_g.shape[-1]
    hbm = pl.BlockSpec(memory_space=pl.ANY)
    grid_spec = pltpu.PrefetchScalarGridSpec(
        num_scalar_prefetch=3,
        grid=(n_tiles,),
        in_specs=[pl.BlockSpec((MOE_TILE, d), lambda i, te, tv, cnt: (i * tv[i], 0)), hbm, hbm, hbm],
        out_specs=pl.BlockSpec((MOE_TILE, d), lambda i, te, tv, cnt: (i, 0)),
        scratch_shapes=[pltpu.VMEM((2, d, f), F32), pltpu.VMEM((2, d, f), F32), pltpu.VMEM((2, f, d), F32),
                        pltpu.VMEM((d, f), BF16), pltpu.VMEM((d, f), BF16), pltpu.VMEM((f, d), BF16),
                        pltpu.SMEM((1,), jnp.int32), pltpu.SemaphoreType.DMA((2,))],
    )
    return pl.pallas_call(
        functools.partial(_moe_body, n_tiles),
        grid_spec=grid_spec,
        out_shape=jax.ShapeDtypeStruct((n_tiles * MOE_TILE, d), F32),
        compiler_params=_params("arbitrary"),
        name="moe",
    )(tile_expert, tile_valid, counts, x_sorted, w_g, w_u, w_d)


def _combine_body(tm, n_tiles, row0, pos_ref, y_hbm, rt_ref, x1_ref, gt_ref, gf_ref, o_ref, ybuf, sem):
    i = pl.program_id(0)
    slot = i % 2

    def gather(tile, dst_slot, start):
        def row(r, carry):
            for k in range(TOP_K):
                p = pos_ref[(row0 + tile * tm + r) * TOP_K + k]
                cp = pltpu.make_async_copy(y_hbm.at[pl.ds(p, 1), :], ybuf.at[dst_slot, k, pl.ds(r, 1), :],
                                           sem.at[dst_slot])
                if start:
                    cp.start()
                else:
                    cp.wait()
            return carry

        lax.fori_loop(0, tm, row, 0, unroll=8)

    @pl.when(i == 0)
    def _():
        gather(0, 0, True)

    @pl.when(i + 1 < n_tiles)
    def _():
        gather(i + 1, 1 - slot, True)

    gather(i, slot, False)
    rt = rt_ref[...]
    y = rt[:, ROUTE_W:ROUTE_W + 1] * ybuf[slot, 0] + rt[:, ROUTE_W + 1:ROUTE_W + 2] * ybuf[slot, 1]
    x2 = x1_ref[...] + gt_ref[0] * y
    o_ref[...] = _rms_scale(x2) * gf_ref[...]


def _combine(y_sorted, pos_flat, route, x1, gt2, gain_f, row0, n_rows, mod_of_tile, tm):
    d = x1.shape[1]
    rb = row0 // tm
    n_tiles = n_rows // tm
    grid_spec = pltpu.PrefetchScalarGridSpec(
        num_scalar_prefetch=1,
        grid=(n_tiles,),
        in_specs=[pl.BlockSpec(memory_space=pl.ANY),
                  pl.BlockSpec((tm, LANES), lambda i, pos: (rb + i, 0)),
                  pl.BlockSpec((tm, d), lambda i, pos: (rb + i, 0)),
                  pl.BlockSpec((1, 1, d), lambda i, pos: (mod_of_tile(i), 0, 0)),
                  pl.BlockSpec((1, d), lambda i, pos: (0, 0))],
        out_specs=pl.BlockSpec((tm, d), lambda i, pos: (i, 0)),
        scratch_shapes=[pltpu.VMEM((2, TOP_K, tm, d), F32), pltpu.SemaphoreType.DMA((2,))],
    )
    return pl.pallas_call(
        functools.partial(_combine_body, tm, n_tiles, row0),
        grid_spec=grid_spec,
        out_shape=jax.ShapeDtypeStruct((n_rows, d), F32),
        compiler_params=_params("arbitrary"),
        name="combine",
    )(pos_flat, y_sorted, route, x1, gt2, gain_f)


GATE_SRC = {"beta": 0, "gc": 2 * H_A, "egc": 4 * H_A, "ekd": 6 * H_A}


def _gates_body(zs_ref, al_ref, dt_ref, sel_ref, o_ref):
    tm = zs_ref.shape[0]
    lane = lax.broadcasted_iota(jnp.int32, (1, LANES), 1)
    zs = zs_ref[...]
    x = zs + dt_ref[...]
    softplus = jnp.maximum(x, 0.0) + jnp.log(1.0 + jnp.exp(-jnp.abs(x)))
    in_g = jnp.logical_and(lane >= GATE_SRC["gc"], lane < GATE_SRC["egc"])
    g = jnp.where(in_g, -jnp.exp(al_ref[...]) * softplus, 0.0)
    beta = jax.nn.sigmoid(zs)
    reversed_lane = lane >= GATE_SRC["gc"] + H_A
    ri = lax.broadcasted_iota(jnp.int32, (CHUNK, CHUNK), 0)
    ci = lax.broadcasted_iota(jnp.int32, (CHUNK, CHUNK), 1)
    prefix = jnp.where(ci <= ri, 1.0, 0.0)
    suffix = jnp.where(ci >= ri, 1.0, 0.0)
    exact = dict(preferred_element_type=F32, precision=lax.Precision.HIGHEST)
    for c in range(tm // CHUNK):
        r = slice(c * CHUNK, (c + 1) * CHUNK)
        gch = g[r]
        gc = jnp.where(reversed_lane, jnp.dot(suffix, gch, **exact), jnp.dot(prefix, gch, **exact))
        tot = jnp.sum(gch, axis=0, keepdims=True)
        src = jnp.where(lane < GATE_SRC["gc"], beta[r],
                        jnp.where(lane < GATE_SRC["egc"], gc,
                                  jnp.where(lane < GATE_SRC["ekd"], pltpu.roll(jnp.exp(gc), 2 * H_A, 1),
                                            jnp.where(lane < GATE_SRC["ekd"] + 2 * H_A,
                                                      pltpu.roll(jnp.exp(tot - gc), 4 * H_A, 1), 0.0))))
        o_ref[r, :] = jnp.dot(src, sel_ref[...], **exact)


def _gate_table(zs, a_log, dt_bias):
    n_tok = zs.shape[0]
    tm = 512
    place = lambda v: jnp.pad(v.reshape(1, 2 * H_A), ((0, 0), (GATE_SRC["gc"], LANES - GATE_SRC["egc"])))
    src = jnp.arange(LANES)[:, None]
    dst = jnp.arange(H_A * LANES)[None, :]
    head, within = dst // LANES, dst % LANES
    kind_src = jnp.array([GATE_SRC["gc"], GATE_SRC["beta"], GATE_SRC["egc"], GATE_SRC["ekd"]])
    want = kind_src[within % GATE_KINDS] + (within // GATE_KINDS) * H_A + head
    sel = jnp.where(jnp.logical_and(within < 2 * GATE_KINDS, src == want), 1.0, 0.0).astype(F32)
    return pl.pallas_call(
        _gates_body,
        grid=(n_tok // tm,),
        in_specs=[pl.BlockSpec((tm, LANES), lambda i: (i, 0)),
                  pl.BlockSpec((1, LANES), lambda i: (0, 0)),
                  pl.BlockSpec((1, LANES), lambda i: (0, 0)),
                  pl.BlockSpec((LANES, H_A * LANES), lambda i: (0, 0))],
        out_specs=pl.BlockSpec((tm, H_A * LANES), lambda i: (i, 0)),
        out_shape=jax.ShapeDtypeStruct((n_tok, H_A * LANES), F32),
        compiler_params=_params("arbitrary"),
        name="gates",
    )(zs, place(a_log), place(dt_bias), sel)


def kernel(x_prompt, x_sample, c, state_delta, cache_k, cache_v, c_ctx, w_ada, b_ada, norm_mix, norm_ffn, w_in,
           conv_a, a_log, dt_bias, onorm_a, lam, subln_b, w_up_a, w_up_b, w_o, w_rg, b_rg, w_re, b_re,
           w_e_gate, w_e_up, w_e_down, norm_final):
    assert w_in.shape[0] == 1, "single trunk layer"
    l = 0
    lam_init = 0.8 - 0.6 * math.exp(-0.3 * l)
    bc, tc, d = x_prompt.shape
    bl, tl, _ = x_sample.shape
    n_ctx, n_lat = bc * tc, bl * tl
    xp = x_prompt.reshape(n_ctx, d)
    xs = x_sample.reshape(n_lat, d)

    cvec = jnp.zeros((N_MOD_ROWS, d), F32).at[0].set(c_ctx).at[1:1 + bl].set(c)
    mod = _adaln(cvec, w_ada[l], b_ada[l][None, :])
    sh1, sc1, gt1, sh2, sc2, gt2 = [m.reshape(N_MOD_ROWS, 1, d) for m in jnp.split(mod, 6, axis=1)]

    rows = _Rows(n_ctx, n_lat, tl, 512)
    w_in_t = jnp.swapaxes(w_in[l], 0, 1)
    h1, zs = _prenorm(rows, xp, xs, norm_mix[l][None, :], sh1, sc1, w_in_t)
    z = _inproj(h1, w_in_t)

    gates = _gate_table(zs, a_log[l], dt_bias[l])
    oa_c, new_state = _deltanet(z, conv_a[l], gates, None, bc, tc, 0, 8)
    oa_l = _deltanet(z, conv_a[l], gates, state_delta, bl, tl, n_ctx, 4)

    lv = lam[l]
    lam_val = jnp.exp(jnp.sum(lv[0] * lv[1])) - jnp.exp(jnp.sum(lv[2] * lv[3])) + lam_init
    lam_row = jnp.full((1, LANES), lam_val, F32)
    sub_row = subln_b[l][None, :]
    ob_c, new_k, new_v = _attn_ctx(z, lam_row, sub_row, bc, tc, lam_init)
    cos, sin = _rope_tables(tl)
    ob_l = _attn_lat(z, cache_k, cache_v, cos, sin, lam_row, sub_row, bl, tl, n_ctx, lam_init)

    mixed = _merge(_Rows(n_ctx, n_lat, tl, 256), z, oa_c, oa_l, ob_c, ob_l, onorm_a[l][None, :],
                   w_up_a[l].astype(BF16), w_up_b[l].astype(BF16))
    w_r = jnp.pad(jnp.concatenate([w_rg[l], w_re[l]], axis=1), ((0, 0), (0, LANES - N_GROUPS - N_EXPERTS)))
    w_r_hi = w_r.astype(BF16)
    w_r = jnp.concatenate([w_r_hi, (w_r - w_r_hi.astype(F32)).astype(BF16)], axis=1)
    b_r = jnp.pad(jnp.concatenate([b_rg[l], b_re[l]]), (0, LANES - N_GROUPS - N_EXPERTS))[None, :]
    x1, h2, route = _outproj(rows, mixed, w_o[l].astype(BF16), xp, xs, gt1, norm_ffn[l][None, :], sh2, sc2, w_r, b_r)

    n_tok = n_ctx + n_lat
    n_tiles = (n_tok * TOP_K) // MOE_TILE + N_EXPERTS
    pos, tbl = _dispatch(route)
    pos_flat = pos[:, :TOP_K].reshape(-1)
    counts, offsets = tbl[TBL_COUNT, :N_EXPERTS], tbl[TBL_OFFSET, :N_EXPERTS]
    x_sorted = _scatter_rows(h2, pos_flat, counts, offsets, n_tiles * MOE_TILE)
    y_sorted = _moe(x_sorted, tbl[TBL_EXPERT, :n_tiles], tbl[TBL_VALID, :n_tiles], counts,
                    w_e_gate[l], w_e_up[l], w_e_down[l])
    gf = norm_final[None, :]
    tm_c = 256
    y_prompt = _combine(y_sorted, pos_flat, route, x1, gt2, gf, 0, n_ctx, lambda i: 0 * i, tm_c)
    y_sample = _combine(y_sorted, pos_flat, route, x1, gt2, gf, n_ctx, n_lat, lambda i: 1 + i // (tl // tm_c), tm_c)
    return (y_prompt.reshape(bc, tc, d), y_sample.reshape(bl, tl, d), new_state, new_k, new_v)
```

```python
import functools
import math

import jax
import jax.numpy as jnp
from jax import lax
from jax.experimental import pallas as pl
from jax.experimental.pallas import tpu as pltpu

F32 = jnp.float32
BF16 = jnp.bfloat16

D_MODEL = 2048
GRID_W = 64
H_A = 8
DK_A = 128
DV_A = 128
CONV_K = 3
CHUNK = 64
H_B = 8
DQK_B = 64
DV_B = 2 * DQK_B
ROPE_BASE = 10000.0
N_GROUPS = 4
E_PER_GROUP = 8
N_EXPERTS = N_GROUPS * E_PER_GROUP
TOP_K = 2
D_FF_E = D_MODEL // 4
EPS = 1e-6

LANES = 128
QKV_A = 2 * H_A * DK_A + H_A * DV_A
N_SMALL = 4 * H_A
Z_GATE_A = QKV_A
Z_QB = Z_GATE_A + H_A * DV_A
Z_KB = Z_QB + H_B * 2 * DQK_B
Z_VB = Z_KB + H_B * 2 * DQK_B
Z_GM = Z_VB + H_B * DV_B
N_MAIN = Z_GM + 2 * D_MODEL

N_MOD_ROWS = 8
MOE_TILE = 256
VMEM_LIMIT = 56 * 1024 * 1024

NT_DIMS = (((1,), (1,)), ((), ()))


def _params(*sem):
    return pltpu.CompilerParams(dimension_semantics=sem, vmem_limit_bytes=VMEM_LIMIT)


def _mm(a, b):
    return jnp.dot(a.astype(BF16), b.astype(BF16), preferred_element_type=F32)


def _mm_nt(a, b):
    return lax.dot_general(a.astype(BF16), b.astype(BF16), NT_DIMS, preferred_element_type=F32)


def _silu(x):
    return x * jax.nn.sigmoid(x)


def _rms_scale(x):
    return x * lax.rsqrt(jnp.mean(x * x, axis=-1, keepdims=True) + EPS)


def _adaln_body(c_ref, w_ref, b_ref, o_ref):
    s = _silu(c_ref[...])
    o_ref[...] = _mm(s, w_ref[...]) + b_ref[...]


def _adaln(cvec, w, b):
    d, n = w.shape
    tn = 1024
    return pl.pallas_call(
        _adaln_body,
        grid=(n // tn,),
        in_specs=[pl.BlockSpec((N_MOD_ROWS, d), lambda j: (0, 0)),
                  pl.BlockSpec((d, tn), lambda j: (0, j)),
                  pl.BlockSpec((1, tn), lambda j: (0, j))],
        out_specs=pl.BlockSpec((N_MOD_ROWS, tn), lambda j: (0, j)),
        out_shape=jax.ShapeDtypeStruct((N_MOD_ROWS, n), F32),
        compiler_params=_params("arbitrary"),
        name="adaln",
    )(cvec, w, b)


class _Rows:
    def __init__(self, n_ctx, n_lat, t_lat, tm):
        assert n_ctx % tm == 0 and n_lat % tm == 0 and t_lat % tm == 0
        self.tm = tm
        self.nct = n_ctx // tm
        self.nlt = n_lat // tm
        self.per_seq = t_lat // tm
        self.n = self.nct + self.nlt

    def ctx_idx(self, i):
        return jnp.minimum(i, self.nct - 1)

    def lat_idx(self, i):
        return jnp.maximum(i - self.nct, 0)

    def mod_idx(self, i):
        return jnp.where(i < self.nct, 0, 1 + (i - self.nct) // self.per_seq)


SMALL_COL0 = QKV_A + H_A * DV_A
GATE_KINDS = 4
GATE_BETA, GATE_GC, GATE_EGC, GATE_EKD = (kind * 2 * H_A for kind in range(GATE_KINDS))


def _gate_rows(zs, a_log_row, dt_row):
    lane = lax.broadcasted_iota(jnp.int32, (1, LANES), 1)
    x = zs + dt_row
    softplus = jnp.maximum(x, 0.0) + jnp.log(1.0 + jnp.exp(-jnp.abs(x)))
    in_g = jnp.logical_and(lane >= GATE_GC, lane < GATE_EGC)
    g = jnp.where(in_g, -jnp.exp(a_log_row) * softplus, 0.0)
    beta = jax.nn.sigmoid(zs)
    reversed_lane = lane >= GATE_GC + H_A
    ri = lax.broadcasted_iota(jnp.int32, (CHUNK, CHUNK), 0)
    ci = lax.broadcasted_iota(jnp.int32, (CHUNK, CHUNK), 1)
    prefix = jnp.where(ci <= ri, 1.0, 0.0)
    suffix = jnp.where(ci >= ri, 1.0, 0.0)
    exact = dict(preferred_element_type=F32, precision=lax.Precision.HIGHEST)
    rows = []
    for c in range(zs.shape[0] // CHUNK):
        r = slice(c * CHUNK, (c + 1) * CHUNK)
        gch = g[r]
        gc = jnp.where(reversed_lane, jnp.dot(suffix, gch, **exact), jnp.dot(prefix, gch, **exact))
        tot = jnp.sum(gch, axis=0, keepdims=True)
        rows.append(jnp.where(lane < GATE_GC, beta[r],
                              jnp.where(lane < GATE_EGC, gc,
                                        jnp.where(lane < GATE_EKD, pltpu.roll(jnp.exp(gc), GATE_EGC - GATE_GC, 1),
                                                  jnp.where(lane < GATE_EKD + 2 * H_A,
                                                            pltpu.roll(jnp.exp(tot - gc), GATE_EKD - GATE_GC, 1),
                                                            0.0)))))
    return jnp.concatenate(rows, axis=0)


def _prenorm_body(nct, xp_ref, xs_ref, g_ref, sh_ref, sc_ref, ws_ref, al_ref, dt_ref, h_ref, gates_ref):
    i = pl.program_id(0)

    def run(x_ref):
        h = _rms_scale(x_ref[...]) * g_ref[...]
        h = (h * (1.0 + sc_ref[0]) + sh_ref[0]).astype(BF16)
        h_ref[...] = h
        zs = lax.dot_general(h, ws_ref[...].astype(BF16), NT_DIMS, preferred_element_type=F32)
        gates_ref[...] = _gate_rows(zs, al_ref[...], dt_ref[...])

    @pl.when(i < nct)
    def _():
        run(xp_ref)

    @pl.when(i >= nct)
    def _():
        run(xs_ref)


def _prenorm(rows, xp, xs, gain, sh, sc, w_in_t, a_log, dt_bias):
    place = lambda v: jnp.pad(v.reshape(1, 2 * H_A), ((0, 0), (GATE_GC, LANES - GATE_EGC)))
    d = xp.shape[1]
    tm = rows.tm
    n_tok = rows.n * tm
    return pl.pallas_call(
        functools.partial(_prenorm_body, rows.nct),
        grid=(rows.n,),
        in_specs=[pl.BlockSpec((tm, d), lambda i: (rows.ctx_idx(i), 0)),
                  pl.BlockSpec((tm, d), lambda i: (rows.lat_idx(i), 0)),
                  pl.BlockSpec((1, d), lambda i: (0, 0)),
                  pl.BlockSpec((1, 1, d), lambda i: (rows.mod_idx(i), 0, 0)),
                  pl.BlockSpec((1, 1, d), lambda i: (rows.mod_idx(i), 0, 0)),
                  pl.BlockSpec((LANES, d), lambda i: (SMALL_COL0 // LANES, 0)),
                  pl.BlockSpec((1, LANES), lambda i: (0, 0)),
                  pl.BlockSpec((1, LANES), lambda i: (0, 0))],
        out_specs=[pl.BlockSpec((tm, d), lambda i: (i, 0)),
                   pl.BlockSpec((tm, LANES), lambda i: (i, 0))],
        out_shape=[jax.ShapeDtypeStruct((n_tok, d), BF16),
                   jax.ShapeDtypeStruct((n_tok, LANES), F32)],
        compiler_params=_params("arbitrary"),
        name="prenorm",
    )(xp, xs, gain, sh, sc, w_in_t, place(a_log), place(dt_bias))


INPROJ_TN = 1024
N_ALIGNED = SMALL_COL0 // INPROJ_TN


def _inproj_body(h_ref, wt_ref, z_ref, w_scr):
    @pl.when(pl.program_id(1) == 0)
    def _():
        w_scr[...] = wt_ref[...].astype(BF16)

    z_ref[...] = lax.dot_general(h_ref[...], w_scr[...], NT_DIMS, preferred_element_type=F32).astype(BF16)


def _inproj(h, w_in_t):
    n_tok, d = h.shape
    tm, tn = 2048, INPROJ_TN
    first_row = lambda j: pl.multiple_of(j * tn + N_SMALL * jnp.minimum(j // N_ALIGNED, 1), N_SMALL)
    return pl.pallas_call(
        _inproj_body,
        grid=(N_MAIN // tn, n_tok // tm),
        in_specs=[pl.BlockSpec((tm, d), lambda j, i: (i, 0)),
                  pl.BlockSpec((pl.Element(tn), pl.Element(d)), lambda j, i: (first_row(j), 0))],
        out_specs=pl.BlockSpec((tm, tn), lambda j, i: (i, j)),
        out_shape=jax.ShapeDtypeStruct((n_tok, N_MAIN), BF16),
        scratch_shapes=[pltpu.VMEM((tn, d), BF16)],
        compiler_params=_params("arbitrary", "arbitrary"),
        name="inproj",
    )(h, w_in_t)


INV_BLOCK = 16


def _chunk_solve(chains, ii, jj):
    eye = ii == jj
    blk = (ii // INV_BLOCK) == (jj // INV_BLOCK)
    blk2 = (ii // (2 * INV_BLOCK)) == (jj // (2 * INV_BLOCK))
    cols, a_qk, a0, off, d0, rhs = [], [], [], [], [], []
    for kk, qk, v, k, gates, reverse, lane in chains:
        beta, gc, egc, ekd = (gates[:, b + lane:b + lane + 1] for b in (GATE_BETA, GATE_GC, GATE_EGC, GATE_EKD))
        incl = (jj >= ii) if reverse else (jj <= ii)
        strict = (jj > ii) if reverse else (jj < ii)
        gc_row = jnp.sum(jnp.where(eye, gc, 0.0), axis=0, keepdims=True)
        dec = jnp.where(incl, jnp.exp(jnp.where(incl, gc - gc_row, 0.0)), 0.0)
        a = jnp.where(strict, kk * dec, 0.0)
        cols.append((egc, ekd))
        a_qk.append(qk * dec)
        a0.append(jnp.where(blk, a, 0.0))
        off.append(jnp.where(blk, 0.0, a))
        d0.append(jnp.where(eye, 1.0, 0.0) - a0[-1])
        rhs.append(jnp.concatenate([v * beta, k * (beta * egc)], axis=1))
    p = [_mm(x, x) for x in a0]
    for _ in range(int(math.log2(INV_BLOCK)) - 2):
        r = [_mm(jnp.concatenate([pi, di], axis=0), pi) for pi, di in zip(p, d0)]
        p = [ri[:CHUNK] for ri in r]
        d0 = [di + ri[CHUNK:] for di, ri in zip(d0, r)]
    d0 = [di + _mm(di, pi) for di, pi in zip(d0, p)]
    wl = [_mm(di, oi) for di, oi in zip(d0, off)]
    yr = [_mm(di, ri) for di, ri in zip(d0, rhs)]
    x1 = [jnp.where(blk2, x, 0.0) for x in wl]
    yl = [jnp.where(blk2, 0.0, x) for x in wl]
    zl = [y - _mm(x, y) for x, y in zip(x1, yl)]
    zr = [y - _mm(x, y) for x, y in zip(x1, yr)]
    sol = [y - _mm(x, y) for x, y in zip(zl, zr)]
    return [(so[:, :DV_A], so[:, DV_A:], aq, egc, ekd) for so, aq, (egc, ekd) in zip(sol, a_qk, cols)]


def _delta_body(t_len, hp, has_s0, q_ref, k_ref, v_ref, cq_ref, ck_ref, cv_ref, g_ref, *rest):
    if has_s0:
        s0_ref, o_ref, qs, ks, vs, g_scr, u_scr, wq_scr, ak_scr, s_scr = rest
    else:
        o_ref, sfin_ref, qs, ks, vs, g_scr, u_scr, wq_scr, ak_scr, s_scr = rest
    n = t_len // CHUNK
    tpos = lax.broadcasted_iota(jnp.int32, (t_len, 1), 0)

    def conv_act(x, w):
        x_prev = jnp.where(tpos == 0, 0.0, pltpu.roll(x, 1, 0))
        x_next = jnp.where(tpos == t_len - 1, 0.0, pltpu.roll(x, t_len - 1, 0))
        return _silu(x_prev * w[0:1] + x * w[1:2] + x_next * w[2:3])

    def l2n(x):
        return x * lax.rsqrt(jnp.sum(x * x, axis=-1, keepdims=True) + EPS)

    for hh in range(hp):
        cols = slice(hh * LANES, (hh + 1) * LANES)
        qs[:, cols] = l2n(conv_act(q_ref[:, cols].astype(F32), cq_ref[:, cols])) * (DK_A ** -0.5)
        ks[:, cols] = l2n(conv_act(k_ref[:, cols].astype(F32), ck_ref[:, cols]))
        vs[:, cols] = conv_act(v_ref[:, cols].astype(F32), cv_ref[:, cols])
    if hp == H_A:
        g_scr[...] = g_ref[...]
    else:
        g_scr[...] = pltpu.roll(g_ref[...], (LANES - pl.program_id(1) * hp) % LANES, 1)
    if has_s0:
        s_scr[...] = s0_ref[...]
    else:
        s_scr[...] = jnp.zeros_like(s_scr)
    o_ref[...] = jnp.zeros_like(o_ref)
    ii = lax.broadcasted_iota(jnp.int32, (CHUNK, CHUNK), 0)
    jj = lax.broadcasted_iota(jnp.int32, (CHUNK, CHUNK), 1)

    def prep(c, carry):
        r = pl.ds(pl.multiple_of(c * CHUNK, CHUNK), CHUNK)
        g = g_scr[r, :]
        heads = []
        for hh in range(hp):
            cols = slice(hh * LANES, (hh + 1) * LANES)
            heads.append((qs[r, cols], ks[r, cols], vs[r, cols]))
        beta = lambda d, hh: g[:, GATE_BETA + d * H_A + hh:GATE_BETA + d * H_A + hh + 1]
        rr = [_mm_nt(jnp.concatenate([k * beta(0, hh), k * beta(1, hh), q], axis=0), k)
              for hh, (q, k, v) in enumerate(heads)]
        chains = [(rr[hh][d * CHUNK:(d + 1) * CHUNK], rr[hh][2 * CHUNK:], heads[hh][2], heads[hh][1], g,
                   d == 1, d * H_A + hh) for hh in range(hp) for d in (0, 1)]
        solved = _chunk_solve(chains, ii, jj)
        for hh in range(hp):
            cols = slice(hh * LANES, (hh + 1) * LANES)
            q, k = heads[hh][0], heads[hh][1]
            for d in (0, 1):
                u, w, a_qk, egc, ekd = solved[hh * 2 + d]
                slot = (d * hp + hh) * n + c
                u_scr[d, r, cols] = u
                wq_scr[slot] = jnp.concatenate([w, q * egc], axis=0).astype(BF16)
                ak_scr[slot] = jnp.concatenate([a_qk, (k * ekd).T], axis=0).astype(BF16)
        return carry

    lax.fori_loop(0, n, prep, 0)

    def scan(c, carry):
        chains = []
        for hh in range(hp):
            for d in (0, 1):
                cc = c if d == 0 else n - 1 - c
                chains.append((hh, d, pl.multiple_of(cc * CHUNK, CHUNK), (d * hp + hh) * n + cc))
        s = [s_scr[d, hh] for hh, d, r0, slot in chains]
        r1 = [jnp.dot(wq_scr[slot], si.astype(BF16), preferred_element_type=F32)
              for si, (hh, d, r0, slot) in zip(s, chains)]
        v_new = [u_scr[d, pl.ds(r0, CHUNK), hh * LANES:(hh + 1) * LANES] - ri[:CHUNK]
                 for ri, (hh, d, r0, slot) in zip(r1, chains)]
        r2 = [jnp.dot(ak_scr[slot], vi.astype(BF16), preferred_element_type=F32)
              for vi, (hh, d, r0, slot) in zip(v_new, chains)]
        for si, r1i, r2i, (hh, d, r0, slot) in zip(s, r1, r2, chains):
            o_ref[pl.ds(r0, CHUNK), hh * LANES:(hh + 1) * LANES] += r1i[CHUNK:] + r2i[:CHUNK]
            lane = GATE_EGC + d * H_A + hh
            edge = g_scr[pl.ds(pl.multiple_of(r0 + (0 if d else CHUNK - 8), 8), 8), lane:lane + 1]
            s_scr[d, hh] = si * (edge[0:1] if d else edge[7:8]) + r2i[CHUNK:]
        return carry

    lax.fori_loop(0, n, scan, 0)
    if not has_s0:
        sfin_ref[...] = s_scr[...]


def _deltanet(z, conv_w, gates, s0, n_seq, t_len, row0, hp):
    rb = row0 // t_len
    w = hp * LANES
    n = t_len // CHUNK
    seq_blk = lambda col0: pl.BlockSpec((t_len, w), lambda b, h: (rb + b, col0 // hp + h))
    cw_blk = lambda col0: pl.BlockSpec((CONV_K, w), lambda b, h: (0, col0 // hp + h))
    state_blk = pl.BlockSpec((None, None, 2, hp, DK_A, DV_A), lambda b, h: (b, 0, 0, h, 0, 0))
    in_specs = [seq_blk(0), seq_blk(H_A), seq_blk(2 * H_A), cw_blk(0), cw_blk(H_A), cw_blk(2 * H_A),
                pl.BlockSpec((t_len, LANES), lambda b, h: (rb + b, 0))]
    args = [z, z, z, conv_w, conv_w, conv_w, gates]
    o_spec = pl.BlockSpec((t_len, w), lambda b, h: (b, h))
    o_shape = jax.ShapeDtypeStruct((n_seq * t_len, H_A * DV_A), F32)
    has_s0 = s0 is not None
    if has_s0:
        in_specs += [state_blk]
        args += [s0]
        out_specs, out_shape = o_spec, o_shape
    else:
        out_specs = [o_spec, state_blk]
        out_shape = [o_shape, jax.ShapeDtypeStruct((n_seq, 1, 2, H_A, DK_A, DV_A), F32)]

    return pl.pallas_call(
        functools.partial(_delta_body, t_len, hp, has_s0),
        grid=(n_seq, H_A // hp),
        in_specs=in_specs,
        out_specs=out_specs,
        out_shape=out_shape,
        scratch_shapes=[pltpu.VMEM((t_len, w), F32)] * 3
        + [pltpu.VMEM((t_len, LANES), F32),
           pltpu.VMEM((2, t_len, w), F32),
           pltpu.VMEM((2 * hp * n, 2 * CHUNK, DV_A), BF16),
           pltpu.VMEM((2 * hp * n, CHUNK + DK_A, CHUNK), BF16),
           pltpu.VMEM((2, hp, DK_A, DV_A), F32)],
        compiler_params=_params("arbitrary", "arbitrary"),
        name="deltanet_lat" if has_s0 else "deltanet_ctx",
    )(*args)


def _subln(o, sub_ref, lam_init):
    return _rms_scale(o) * sub_ref[...] * (1.0 - lam_init)


def _attn_ctx_body(lam_init, q_ref, k_ref, v_ref, lam_ref, sub_ref, o_ref, ck_ref, cv_ref):
    heads = [slice(h * LANES, (h + 1) * LANES) for h in range(H_B)]
    lam = lam_ref[0:1, 0:1]
    ks = [k_ref[:, c] for c in heads]
    vs = [v_ref[:, c] for c in heads]
    for h in range(H_B):
        ck_ref[h] = ks[h].astype(F32)
        cv_ref[h] = vs[h].astype(F32)
    qb = [q_ref[:, c] * (DQK_B ** -0.5) for c in heads]
    kb = ks
    probs = []
    for lo in (0, DQK_B):
        s = [lax.dot_general(q[:, lo:lo + DQK_B], k[:, lo:lo + DQK_B], NT_DIMS, preferred_element_type=F32)
             for q, k in zip(qb, kb)]
        e = [jnp.exp(x - jnp.max(x, axis=-1, keepdims=True)) for x in s]
        probs.append([x / jnp.sum(x, axis=-1, keepdims=True) for x in e])
    o = [jnp.dot((p1 - lam * p2).astype(BF16), v, preferred_element_type=F32)
         for p1, p2, v in zip(probs[0], probs[1], vs)]
    for c, oh in zip(heads, o):
        o_ref[:, c] = _subln(oh, sub_ref, lam_init).astype(BF16)


def _attn_ctx(z, lam, subln, n_seq, t_len, lam_init):
    n_tok = n_seq * t_len
    w = H_B * LANES
    blk = lambda col0: pl.BlockSpec((t_len, w), lambda b: (b, col0 // w))
    cache_blk = pl.BlockSpec((None, None, H_B, t_len, LANES), lambda b: (b, 0, 0, 0, 0))
    cache_shape = jax.ShapeDtypeStruct((n_seq, 1, H_B, t_len, LANES), F32)
    return pl.pallas_call(
        functools.partial(_attn_ctx_body, lam_init),
        grid=(n_seq,),
        in_specs=[blk(Z_QB), blk(Z_KB), blk(Z_VB),
                  pl.BlockSpec((1, LANES), lambda b: (0, 0)),
                  pl.BlockSpec((1, LANES), lambda b: (0, 0))],
        out_specs=[pl.BlockSpec((t_len, w), lambda b: (b, 0)), cache_blk, cache_blk],
        out_shape=[jax.ShapeDtypeStruct((n_tok, H_B * DV_B), BF16), cache_shape, cache_shape],
        compiler_params=_params("arbitrary"),
        name="attn_ctx",
    )(z, z, z, lam, subln)


def _rope(x, cos, sin_signed):
    lane = lax.broadcasted_iota(jnp.int32, (1, LANES), 1)
    first = (lane % 32) < 16
    partner = jnp.where(first, pltpu.roll(x, LANES - 16, 1), pltpu.roll(x, 16, 1))
    return x * cos + partner * sin_signed


LAT_HEADS = 2


def _attn_lat_body(lam_init, n_past, q_ref, k_ref, v_ref, pk_ref, pv_ref, cosq_ref, sinq_ref, cos_ref, sin_ref,
                   lam_ref, sub_ref, o_ref, keys, vals):
    heads = [slice(h * LANES, (h + 1) * LANES) for h in range(LAT_HEADS)]

    @pl.when(pl.program_id(2) == 0)
    def _():
        for h, c in enumerate(heads):
            keys[h, 0:n_past, :] = pk_ref[h].astype(BF16)
            vals[h, 0:n_past, :] = pv_ref[h].astype(BF16)
            keys[h, n_past:, :] = _rope(k_ref[:, c].astype(F32), cos_ref[...], sin_ref[...]).astype(BF16)
            vals[h, n_past:, :] = v_ref[:, c]

    lam = lam_ref[0:1, 0:1]
    qb = [(_rope(q_ref[:, c].astype(F32), cosq_ref[...], sinq_ref[...]) * (DQK_B ** -0.5)).astype(BF16)
          for c in heads]
    probs = []
    for lo in (0, DQK_B):
        s = [lax.dot_general(q[:, lo:lo + DQK_B], keys[h, :, lo:lo + DQK_B], NT_DIMS, preferred_element_type=F32)
             for h, q in enumerate(qb)]
        e = [jnp.exp(x - jnp.max(x, axis=-1, keepdims=True)) for x in s]
        probs.append([x / jnp.sum(x, axis=-1, keepdims=True) for x in e])
    for h, c in enumerate(heads):
        a = (probs[0][h] - lam * probs[1][h]).astype(BF16)
        o_ref[:, c] = _subln(jnp.dot(a, vals[h], preferred_element_type=F32), sub_ref, lam_init).astype(BF16)


def _attn_lat(z, cache_k, cache_v, cos, sin, lam, subln, n_seq, t_len, row0, lam_init):
    tq = 256
    nq = t_len // tq
    n_past = cache_k.shape[3]
    rbq = row0 // tq
    rbs = row0 // t_len
    w = LAT_HEADS * LANES
    seq_blk = lambda col0: pl.BlockSpec((t_len, w), lambda b, h, qi: (rbs + b, col0 // w + h))
    past_blk = pl.BlockSpec((None, None, LAT_HEADS, n_past, LANES), lambda b, h, qi: (b, 0, h, 0, 0))
    row_vec = pl.BlockSpec((1, LANES), lambda b, h, qi: (0, 0))
    return pl.pallas_call(
        functools.partial(_attn_lat_body, lam_init, n_past),
        grid=(n_seq, H_B // LAT_HEADS, nq),
        in_specs=[pl.BlockSpec((tq, w), lambda b, h, qi: (rbq + b * nq + qi, Z_QB // w + h)),
                  seq_blk(Z_KB), seq_blk(Z_VB), past_blk, past_blk,
                  pl.BlockSpec((tq, LANES), lambda b, h, qi: (qi, 0)),
                  pl.BlockSpec((tq, LANES), lambda b, h, qi: (qi, 0)),
                  pl.BlockSpec((t_len, LANES), lambda b, h, qi: (0, 0)),
                  pl.BlockSpec((t_len, LANES), lambda b, h, qi: (0, 0)),
                  row_vec, row_vec],
        out_specs=pl.BlockSpec((tq, w), lambda b, h, qi: (b * nq + qi, h)),
        out_shape=jax.ShapeDtypeStruct((n_seq * t_len, H_B * DV_B), BF16),
        scratch_shapes=[pltpu.VMEM((LAT_HEADS, n_past + t_len, LANES), BF16)] * 2,
        compiler_params=_params("arbitrary", "arbitrary", "arbitrary"),
        name="attn_lat",
    )(z, z, z, cache_k, cache_v, cos, sin, cos, sin, lam, subln)


def _rope_tables(t_len):
    t = jnp.arange(t_len)
    pos = jnp.stack([t // GRID_W, t % GRID_W], axis=1).astype(F32)
    nf = DQK_B // 4
    inv_freq = ROPE_BASE ** (-jnp.arange(nf, dtype=F32) / nf)
    lane = jnp.arange(LANES)
    half = (lane % DQK_B) // (DQK_B // 2)
    ang = pos[:, half] * inv_freq[lane % nf][None, :]
    sign = jnp.where((lane % (DQK_B // 2)) < nf, -1.0, 1.0).astype(F32)
    return jnp.cos(ang), jnp.sin(ang) * sign[None, :]


def _merge_body(nct, oac_ref, oal_ref, ga_ref, on_ref, obc_ref, obl_ref, wa_ref, wb_ref, *rest):
    gm_refs, m_ref = rest[:-1], rest[-1]
    i = pl.program_id(0)
    n_blk = len(gm_refs) // 2

    def run(oa_ref, ob_ref):
        a = jnp.concatenate(
            [(_rms_scale(oa_ref[:, c]) * on_ref[...] * _silu(ga_ref[:, c].astype(F32))).astype(BF16)
             for c in (slice(h * DV_A, (h + 1) * DV_A) for h in range(H_A))], axis=1)
        ya = jnp.dot(a, wa_ref[...], preferred_element_type=F32)
        yb = jnp.dot(ob_ref[...], wb_ref[...], preferred_element_type=F32)
        tn = gm_refs[0].shape[1]
        for j in range(n_blk):
            c = slice(j * tn, (j + 1) * tn)
            m_ref[:, c] = (jax.nn.sigmoid(gm_refs[j][...].astype(F32)) * ya[:, c]
                           + jax.nn.sigmoid(gm_refs[n_blk + j][...].astype(F32)) * yb[:, c]).astype(BF16)

    @pl.when(i < nct)
    def _():
        run(oac_ref, obc_ref)

    @pl.when(i >= nct)
    def _():
        run(oal_ref, obl_ref)


def _merge(rows, z, oa_c, oa_l, ob_c, ob_l, onorm, w_up_a, w_up_b):
    n_tok = z.shape[0]
    tm = rows.tm
    ka, d = w_up_a.shape
    tn = 1024
    nj = d // tn
    ctx_blk = pl.BlockSpec((tm, ka), lambda i: (rows.ctx_idx(i), 0))
    lat_blk = pl.BlockSpec((tm, ka), lambda i: (rows.lat_idx(i), 0))
    weight = pl.BlockSpec((ka, d), lambda i: (0, 0))
    gate_cols = [pl.BlockSpec((tm, tn), functools.partial(lambda i, c: (i, c), c=Z_GM // tn + j)) for j in range(2 * nj)]
    return pl.pallas_call(
        functools.partial(_merge_body, rows.nct),
        grid=(n_tok // tm,),
        in_specs=[ctx_blk, lat_blk,
                  pl.BlockSpec((tm, ka), lambda i: (i, Z_GATE_A // ka)),
                  pl.BlockSpec((1, DV_A), lambda i: (0, 0)),
                  ctx_blk, lat_blk, weight, weight] + gate_cols,
        out_specs=pl.BlockSpec((tm, d), lambda i: (i, 0)),
        out_shape=jax.ShapeDtypeStruct((n_tok, d), BF16),
        compiler_params=_params("arbitrary"),
        name="merge",
    )(oa_c, oa_l, z, onorm, ob_c, ob_l, w_up_a, w_up_b, *([z] * (2 * nj)))


ROUTE_E = 0
ROUTE_W = TOP_K


def _route_rows(lg):
    lane = lax.broadcasted_iota(jnp.int32, lg.shape, 1)
    neg = -jnp.inf

    def first_max(x):
        m = jnp.max(x, axis=1, keepdims=True)
        return m, jnp.min(jnp.where(x == m, lane, LANES), axis=1, keepdims=True)

    gl = jnp.where(lane < N_GROUPS, lg, neg)
    gmax, g_idx = first_max(gl)
    pg_top = 1.0 / jnp.sum(jnp.exp(gl - gmax), axis=1, keepdims=True)
    lo = N_GROUPS + E_PER_GROUP * g_idx
    el = jnp.where(jnp.logical_and(lane >= lo, lane < lo + E_PER_GROUP), lg, neg)
    emax, i1 = first_max(el)
    esum = jnp.sum(jnp.exp(el - emax), axis=1, keepdims=True)
    e2max, i2 = first_max(jnp.where(lane == i1, neg, el))
    p1 = 1.0 / esum
    p2 = jnp.exp(e2max - emax) / esum
    den = p1 + p2
    vals = [(i1 - N_GROUPS).astype(F32), (i2 - N_GROUPS).astype(F32), pg_top * p1 / den, pg_top * p2 / den]
    out = jnp.zeros(lg.shape, F32)
    for pos, val in enumerate(vals):
        out = jnp.where(lane == pos, val, out)
    return out


def _outproj_body(nct, m_ref, wo_ref, xp_ref, xs_ref, gt_ref, g2_ref, sh_ref, sc_ref, wr_ref, br_ref,
                  x1_ref, h2_ref, rt_ref):
    i = pl.program_id(0)
    y = jnp.dot(m_ref[...], wo_ref[...], preferred_element_type=F32)

    def finish(x_ref):
        x1 = x_ref[...] + gt_ref[0] * y
        x1_ref[...] = x1
        h2 = _rms_scale(x1) * g2_ref[...]
        h2 = h2 * (1.0 + sc_ref[0]) + sh_ref[0]
        h2_ref[...] = h2
        hi = h2.astype(BF16)
        lo = (h2 - hi.astype(F32)).astype(BF16)
        p_hi = jnp.dot(hi, wr_ref[...], preferred_element_type=F32)
        p_lo = jnp.dot(lo, wr_ref[...], preferred_element_type=F32)
        lg = p_hi[:, :LANES] + p_hi[:, LANES:] + p_lo[:, :LANES] + p_lo[:, LANES:] + br_ref[...]
        rt_ref[...] = _route_rows(lg)

    @pl.when(i < nct)
    def _():
        finish(xp_ref)

    @pl.when(i >= nct)
    def _():
        finish(xs_ref)


def _outproj(rows, mixed, w_o, xp, xs, gt1, gain2, sh2, sc2, w_r, b_r):
    d = xp.shape[1]
    tm = rows.tm
    n_tok = rows.n * tm
    mod = lambda: pl.BlockSpec((1, 1, d), lambda i: (rows.mod_idx(i), 0, 0))
    tok = pl.BlockSpec((tm, d), lambda i: (i, 0))
    return pl.pallas_call(
        functools.partial(_outproj_body, rows.nct),
        grid=(rows.n,),
        in_specs=[tok,
                  pl.BlockSpec((d, d), lambda i: (0, 0)),
                  pl.BlockSpec((tm, d), lambda i: (rows.ctx_idx(i), 0)),
                  pl.BlockSpec((tm, d), lambda i: (rows.lat_idx(i), 0)),
                  mod(),
                  pl.BlockSpec((1, d), lambda i: (0, 0)),
                  mod(), mod(),
                  pl.BlockSpec((d, 2 * LANES), lambda i: (0, 0)),
                  pl.BlockSpec((1, LANES), lambda i: (0, 0))],
        out_specs=[tok, tok, pl.BlockSpec((tm, LANES), lambda i: (i, 0))],
        out_shape=[jax.ShapeDtypeStruct((n_tok, d), F32), jax.ShapeDtypeStruct((n_tok, d), F32),
                   jax.ShapeDtypeStruct((n_tok, LANES), F32)],
        compiler_params=_params("arbitrary"),
        name="outproj",
    )(mixed, w_o, xp, xs, gt1, gain2, sh2, sc2, w_r, b_r)


DISPATCH_BLOCK = 256
TBL_EXPERT, TBL_VALID, TBL_COUNT, TBL_OFFSET = 0, 1, 2, 3


def _dispatch_body(n_tok, route_ref, pos_ref, tbl_ref):
    nb = n_tok // DISPATCH_BLOCK
    lane = lax.broadcasted_iota(jnp.int32, (1, LANES), 1)
    lane_f = lane.astype(F32)

    def one_hot(b, k):
        r = pl.ds(pl.multiple_of(b * DISPATCH_BLOCK, DISPATCH_BLOCK), DISPATCH_BLOCK)
        return jnp.where(route_ref[r, ROUTE_E + k:ROUTE_E + k + 1] == lane_f, 1.0, 0.0)

    def count(b, acc):
        return acc + jnp.sum(one_hot(b, 0) + one_hot(b, 1), axis=0, keepdims=True)

    counts = lax.fori_loop(0, nb, count, jnp.zeros((1, LANES), F32))
    padded = jnp.floor((counts + (MOE_TILE - 1)) * (1.0 / MOE_TILE)) * MOE_TILE
    pad_end = padded
    shift = 1
    while shift < LANES:
        pad_end = pad_end + jnp.where(lane >= shift, pltpu.roll(pad_end, shift, 1), 0.0)
        shift *= 2
    pad_off = pad_end - padded

    ri = lax.broadcasted_iota(jnp.int32, (DISPATCH_BLOCK, DISPATCH_BLOCK), 0)
    ci = lax.broadcasted_iota(jnp.int32, (DISPATCH_BLOCK, DISPATCH_BLOCK), 1)
    before = jnp.where(ci < ri, 1.0, 0.0).astype(BF16)

    def place(b, run):
        oh = [one_hot(b, k) for k in range(TOP_K)]
        base = pad_off + run
        out = jnp.zeros((DISPATCH_BLOCK, LANES), F32)
        lane_b = lax.broadcasted_iota(jnp.int32, (DISPATCH_BLOCK, LANES), 1)
        for k in range(TOP_K):
            prior = jnp.dot(before, oh[k].astype(BF16), preferred_element_type=F32)
            pos = jnp.sum(oh[k] * (base + prior), axis=1, keepdims=True)
            out = jnp.where(lane_b == k, pos, out)
            base = base + jnp.sum(oh[k], axis=0, keepdims=True)
        r = pl.ds(pl.multiple_of(b * DISPATCH_BLOCK, DISPATCH_BLOCK), DISPATCH_BLOCK)
        pos_ref[r, :] = out.astype(jnp.int32)
        return base - pad_off

    lax.fori_loop(0, nb, place, jnp.zeros((1, LANES), F32))

    end_col = jnp.transpose(jnp.broadcast_to(pad_end, (8, LANES)))[:, 0:1]
    e_col = lax.broadcasted_iota(jnp.int32, (LANES, 1), 0)
    tile_start = lane_f * MOE_TILE
    passed = jnp.where(jnp.logical_and(end_col <= tile_start, e_col < N_EXPERTS), 1.0, 0.0)
    tile_expert = jnp.minimum(jnp.sum(passed, axis=0, keepdims=True), N_EXPERTS - 1.0)
    total = jnp.sum(jnp.where(lane == N_EXPERTS - 1, pad_end, 0.0), axis=1, keepdims=True)
    tile_valid = jnp.where(tile_start < total, 1.0, 0.0)
    row = lax.broadcasted_iota(jnp.int32, (8, LANES), 0)
    tbl = jnp.zeros((8, LANES), F32)
    for idx, val in ((TBL_EXPERT, tile_expert), (TBL_VALID, tile_valid), (TBL_COUNT, counts), (TBL_OFFSET, pad_off)):
        tbl = jnp.where(row == idx, val, tbl)
    tbl_ref[...] = tbl.astype(jnp.int32)


def _dispatch(route):
    n_tok = route.shape[0]
    return pl.pallas_call(
        functools.partial(_dispatch_body, n_tok),
        out_shape=[jax.ShapeDtypeStruct((n_tok, LANES), jnp.int32), jax.ShapeDtypeStruct((8, LANES), jnp.int32)],
        compiler_params=pltpu.CompilerParams(vmem_limit_bytes=VMEM_LIMIT),
        name="dispatch",
    )(route)


def _scatter_body(tm, n_tiles, pos_ref, cnt_ref, off_ref, h_ref, o_hbm, zbuf, sem):
    i = pl.program_id(0)

    @pl.when(i == 0)
    def _():
        zbuf[...] = jnp.zeros_like(zbuf)
        last = N_EXPERTS - 1
        first_empty = (off_ref[last] + cnt_ref[last] + MOE_TILE - 1) // MOE_TILE

        def tile_copy(t):
            rows = pl.ds(pl.multiple_of(t * MOE_TILE, MOE_TILE), MOE_TILE)
            return pltpu.make_async_copy(zbuf, o_hbm.at[rows, :], sem.at[1])

        def partial_tile(e, start):
            cnt = cnt_ref[e]

            @pl.when(cnt % MOE_TILE != 0)
            def _():
                cp = tile_copy((off_ref[e] + cnt) // MOE_TILE)
                if start:
                    cp.start()
                else:
                    cp.wait()

        def zero_partial(e, carry):
            partial_tile(e, True)
            return carry

        def partial_done(e, carry):
            partial_tile(e, False)
            return carry

        def zero_tile(t, carry):
            tile_copy(t).start()
            return carry

        def zero_done(t, carry):
            tile_copy(t).wait()
            return carry

        lax.fori_loop(0, N_EXPERTS, zero_partial, 0)
        lax.fori_loop(first_empty, n_tiles, zero_tile, 0)
        lax.fori_loop(0, N_EXPERTS, partial_done, 0)
        lax.fori_loop(first_empty, n_tiles, zero_done, 0)

    def row_copies(r):
        return [pltpu.make_async_copy(h_ref.at[pl.ds(r, 1), :],
                                      o_hbm.at[pl.ds(pos_ref[(i * tm + r) * TOP_K + k], 1), :], sem.at[0])
                for k in range(TOP_K)]

    def start(r, carry):
        for cp in row_copies(r):
            cp.start()
        return carry

    def wait(r, carry):
        for cp in row_copies(r):
            cp.wait()
        return carry

    lax.fori_loop(0, tm, start, 0, unroll=8)
    lax.fori_loop(0, tm, wait, 0, unroll=8)


def _scatter_rows(h2, pos_flat, counts, offsets, n_rows):
    n_tok, d = h2.shape
    tm = 512
    grid_spec = pltpu.PrefetchScalarGridSpec(
        num_scalar_prefetch=3,
        grid=(n_tok // tm,),
        in_specs=[pl.BlockSpec((tm, d), lambda i, pos, cnt, off: (i, 0))],
        out_specs=pl.BlockSpec(memory_space=pl.ANY),
        scratch_shapes=[pltpu.VMEM((MOE_TILE, d), F32), pltpu.SemaphoreType.DMA((2,))],
    )
    return pl.pallas_call(
        functools.partial(_scatter_body, tm, n_rows // MOE_TILE),
        grid_spec=grid_spec,
        out_shape=jax.ShapeDtypeStruct((n_rows, d), F32),
        compiler_params=_params("arbitrary"),
        name="scatter_rows",
    )(pos_flat, counts, offsets, h2)


def _moe_body(n_tiles, te_ref, tv_ref, cnt_ref, x_ref, wg_hbm, wu_hbm, wd_hbm, y_ref,
              wg_f, wu_f, wd_f, wg_b, wu_b, wd_b, slot_ref, sem):
    i = pl.program_id(0)
    prev = jnp.maximum(i - 1, 0)
    valid = tv_ref[i] == 1
    e = te_ref[i]

    def fetch(expert, slot):
        return [pltpu.make_async_copy(src.at[expert], dst.at[slot], sem.at[slot])
                for src, dst in ((wg_hbm, wg_f), (wu_hbm, wu_f), (wd_hbm, wd_f))]

    @pl.when(jnp.logical_and(valid, i == 0))
    def _():
        slot_ref[0] = 1
        for cp in fetch(e, 0):
            cp.start()

    @pl.when(jnp.logical_and(valid, jnp.logical_or(i == 0, e != te_ref[prev])))
    def _():
        slot = 1 - slot_ref[0]
        slot_ref[0] = slot
        for cp in fetch(e, slot):
            cp.wait()
        nxt = jnp.minimum(i + (cnt_ref[e] + MOE_TILE - 1) // MOE_TILE, n_tiles - 1)

        @pl.when(jnp.logical_and(nxt > i, tv_ref[nxt] == 1))
        def _():
            for cp in fetch(te_ref[nxt], 1 - slot):
                cp.start()

        wg_b[...] = wg_f[slot].astype(BF16)
        wu_b[...] = wu_f[slot].astype(BF16)
        wd_b[...] = wd_f[slot].astype(BF16)

    @pl.when(valid)
    def _():
        x = x_ref[...].astype(BF16)
        g = jnp.dot(x, wg_b[...], preferred_element_type=F32)
        u = jnp.dot(x, wu_b[...], preferred_element_type=F32)
        y_ref[...] = jnp.dot((_silu(g) * u).astype(BF16), wd_b[...], preferred_element_type=F32)

    @pl.when(tv_ref[i] == 0)
    def _():
        y_ref[...] = jnp.zeros_like(y_ref)


def _moe(x_sorted, tile_expert, tile_valid, counts, w_g, w_u, w_d):
    n_tiles = tile_expert.shape[0]
    d = x_sorted.shape[1]
    f = w_g.shape[-1]
    hbm = pl.BlockSpec(memory_space=pl.ANY)
    grid_spec = pltpu.PrefetchScalarGridSpec(
        num_scalar_prefetch=3,
        grid=(n_tiles,),
        in_specs=[pl.BlockSpec((MOE_TILE, d), lambda i, te, tv, cnt: (i * tv[i], 0)), hbm, hbm, hbm],
        out_specs=pl.BlockSpec((MOE_TILE, d), lambda i, te, tv, cnt: (i, 0)),
        scratch_shapes=[pltpu.VMEM((2, d, f), F32), pltpu.VMEM((2, d, f), F32), pltpu.VMEM((2, f, d), F32),
                        pltpu.VMEM((d, f), BF16), pltpu.VMEM((d, f), BF16), pltpu.VMEM((f, d), BF16),
                        pltpu.SMEM((1,), jnp.int32), pltpu.SemaphoreType.DMA((2,))],
    )
    return pl.pallas_call(
        functools.partial(_moe_body, n_tiles),
        grid_spec=grid_spec,
        out_shape=jax.ShapeDtypeStruct((n_tiles * MOE_TILE, d), F32),
        compiler_params=_params("arbitrary"),
        name="moe",
    )(tile_expert, tile_valid, counts, x_sorted, w_g, w_u, w_d)


def _combine_body(tm, n_tiles, row0, pos_ref, y_hbm, rt_ref, x1_ref, gt_ref, gf_ref, o_ref, ybuf, sem):
    i = pl.program_id(0)
    slot = i % 2

    def gather(tile, dst_slot, start):
        def row(r, carry):
            for k in range(TOP_K):
                p = pos_ref[(row0 + tile * tm + r) * TOP_K + k]
                cp = pltpu.make_async_copy(y_hbm.at[pl.ds(p, 1), :], ybuf.at[dst_slot, k, pl.ds(r, 1), :],
                                           sem.at[dst_slot])
                if start:
                    cp.start()
                else:
                    cp.wait()
            return carry

        lax.fori_loop(0, tm, row, 0, unroll=8)

    @pl.when(i == 0)
    def _():
        gather(0, 0, True)

    @pl.when(i + 1 < n_tiles)
    def _():
        gather(i + 1, 1 - slot, True)

    gather(i, slot, False)
    rt = rt_ref[...]
    y = rt[:, ROUTE_W:ROUTE_W + 1] * ybuf[slot, 0] + rt[:, ROUTE_W + 1:ROUTE_W + 2] * ybuf[slot, 1]
    x2 = x1_ref[...] + gt_ref[0] * y
    o_ref[...] = _rms_scale(x2) * gf_ref[...]


def _combine(y_sorted, pos_flat, route, x1, gt2, gain_f, row0, n_rows, mod_of_tile, tm):
    d = x1.shape[1]
    rb = row0 // tm
    n_tiles = n_rows // tm
    grid_spec = pltpu.PrefetchScalarGridSpec(
        num_scalar_prefetch=1,
        grid=(n_tiles,),
        in_specs=[pl.BlockSpec(memory_space=pl.ANY),
                  pl.BlockSpec((tm, LANES), lambda i, pos: (rb + i, 0)),
                  pl.BlockSpec((tm, d), lambda i, pos: (rb + i, 0)),
                  pl.BlockSpec((1, 1, d), lambda i, pos: (mod_of_tile(i), 0, 0)),
                  pl.BlockSpec((1, d), lambda i, pos: (0, 0))],
        out_specs=pl.BlockSpec((tm, d), lambda i, pos: (i, 0)),
        scratch_shapes=[pltpu.VMEM((2, TOP_K, tm, d), F32), pltpu.SemaphoreType.DMA((2,))],
    )
    return pl.pallas_call(
        functools.partial(_combine_body, tm, n_tiles, row0),
        grid_spec=grid_spec,
        out_shape=jax.ShapeDtypeStruct((n_rows, d), F32),
        compiler_params=_params("arbitrary"),
        name="combine",
    )(pos_flat, y_sorted, route, x1, gt2, gain_f)


def kernel(x_prompt, x_sample, c, state_delta, cache_k, cache_v, c_ctx, w_ada, b_ada, norm_mix, norm_ffn, w_in,
           conv_a, a_log, dt_bias, onorm_a, lam, subln_b, w_up_a, w_up_b, w_o, w_rg, b_rg, w_re, b_re,
           w_e_gate, w_e_up, w_e_down, norm_final):
    assert w_in.shape[0] == 1, "single trunk layer"
    l = 0
    lam_init = 0.8 - 0.6 * math.exp(-0.3 * l)
    bc, tc, d = x_prompt.shape
    bl, tl, _ = x_sample.shape
    n_ctx, n_lat = bc * tc, bl * tl
    xp = x_prompt.reshape(n_ctx, d)
    xs = x_sample.reshape(n_lat, d)

    cvec = jnp.zeros((N_MOD_ROWS, d), F32).at[0].set(c_ctx).at[1:1 + bl].set(c)
    mod = _adaln(cvec, w_ada[l], b_ada[l][None, :])
    sh1, sc1, gt1, sh2, sc2, gt2 = [m.reshape(N_MOD_ROWS, 1, d) for m in jnp.split(mod, 6, axis=1)]

    rows = _Rows(n_ctx, n_lat, tl, 512)
    w_in_t = jnp.swapaxes(w_in[l], 0, 1)
    h1, gates = _prenorm(rows, xp, xs, norm_mix[l][None, :], sh1, sc1, w_in_t, a_log[l], dt_bias[l])
    z = _inproj(h1, w_in_t)

    oa_c, new_state = _deltanet(z, conv_a[l], gates, None, bc, tc, 0, 8)
    oa_l = _deltanet(z, conv_a[l], gates, state_delta, bl, tl, n_ctx, 4)

    lv = lam[l]
    lam_val = jnp.exp(jnp.sum(lv[0] * lv[1])) - jnp.exp(jnp.sum(lv[2] * lv[3])) + lam_init
    lam_row = jnp.full((1, LANES), lam_val, F32)
    sub_row = subln_b[l][None, :]
    ob_c, new_k, new_v = _attn_ctx(z, lam_row, sub_row, bc, tc, lam_init)
    cos, sin = _rope_tables(tl)
    ob_l = _attn_lat(z, cache_k, cache_v, cos, sin, lam_row, sub_row, bl, tl, n_ctx, lam_init)

    mixed = _merge(_Rows(n_ctx, n_lat, tl, 256), z, oa_c, oa_l, ob_c, ob_l, onorm_a[l][None, :],
                   w_up_a[l].astype(BF16), w_up_b[l].astype(BF16))
    w_r = jnp.pad(jnp.concatenate([w_rg[l], w_re[l]], axis=1), ((0, 0), (0, LANES - N_GROUPS - N_EXPERTS)))
    w_r_hi = w_r.astype(BF16)
    w_r = jnp.concatenate([w_r_hi, (w_r - w_r_hi.astype(F32)).astype(BF16)], axis=1)
    b_r = jnp.pad(jnp.concatenate([b_rg[l], b_re[l]]), (0, LANES - N_GROUPS - N_EXPERTS))[None, :]
    x1, h2, route = _outproj(rows, mixed, w_o[l].astype(BF16), xp, xs, gt1, norm_ffn[l][None, :], sh2, sc2, w_r, b_r)

    n_tok = n_ctx + n_lat
    n_tiles = (n_tok * TOP_K) // MOE_TILE + N_EXPERTS
    pos, tbl = _dispatch(route)
    pos_flat = pos[:, :TOP_K].reshape(-1)
    counts, offsets = tbl[TBL_COUNT, :N_EXPERTS], tbl[TBL_OFFSET, :N_EXPERTS]
    x_sorted = _scatter_rows(h2, pos_flat, counts, offsets, n_tiles * MOE_TILE)
    y_sorted = _moe(x_sorted, tbl[TBL_EXPERT, :n_tiles], tbl[TBL_VALID, :n_tiles], counts,
                    w_e_gate[l], w_e_up[l], w_e_down[l])
    gf = norm_final[None, :]
    tm_c = 256
    y_prompt = _combine(y_sorted, pos_flat, route, x1, gt2, gf, 0, n_ctx, lambda i: 0 * i, tm_c)
    y_sample = _combine(y_sorted, pos_flat, route, x1, gt2, gf, n_ctx, n_lat, lambda i: 1 + i // (tl // tm_c), tm_c)
    return (y_prompt.reshape(bc, tc, d), y_sample.reshape(bl, tl, d), new_state, new_k, new_v)
```

```python
import functools
import math

import jax
import jax.numpy as jnp
import numpy as np
from jax import lax
from jax.experimental import pallas as pl
from jax.experimental.pallas import tpu as pltpu

F32 = jnp.float32
BF16 = jnp.bfloat16

D_MODEL = 2048
GRID_W = 64
H_A = 8
DK_A = 128
DV_A = 128
CONV_K = 3
CHUNK = 64
H_B = 8
DQK_B = 64
DV_B = 2 * DQK_B
ROPE_BASE = 10000.0
N_GROUPS = 4
E_PER_GROUP = 8
N_EXPERTS = N_GROUPS * E_PER_GROUP
TOP_K = 2
D_FF_E = D_MODEL // 4
EPS = 1e-6

LANES = 128
QKV_A = 2 * H_A * DK_A + H_A * DV_A
N_SMALL = 4 * H_A
Z_GATE_A = QKV_A
Z_QB = Z_GATE_A + H_A * DV_A
Z_KB = Z_QB + H_B * 2 * DQK_B
Z_VB = Z_KB + H_B * 2 * DQK_B
Z_GM = Z_VB + H_B * DV_B
N_MAIN = Z_GM + 2 * D_MODEL

N_MOD_ROWS = 8
MOE_TILE = 256
VMEM_LIMIT = 56 * 1024 * 1024

NT_DIMS = (((1,), (1,)), ((), ()))


def _params(*sem):
    return pltpu.CompilerParams(dimension_semantics=sem, vmem_limit_bytes=VMEM_LIMIT)


def _mm(a, b):
    return jnp.dot(a.astype(BF16), b.astype(BF16), preferred_element_type=F32)


def _mm_nt(a, b):
    return lax.dot_general(a.astype(BF16), b.astype(BF16), NT_DIMS, preferred_element_type=F32)


def _silu(x):
    return x * jax.nn.sigmoid(x)


def _rms_scale(x):
    return x * lax.rsqrt(jnp.mean(x * x, axis=-1, keepdims=True) + EPS)


def _adaln_body(c_ref, w_ref, b_ref, o_ref):
    s = _silu(c_ref[...])
    o_ref[...] = _mm(s, w_ref[...]) + b_ref[...]


def _adaln(cvec, w, b):
    d, n = w.shape
    tn = 1024
    return pl.pallas_call(
        _adaln_body,
        grid=(n // tn,),
        in_specs=[pl.BlockSpec((N_MOD_ROWS, d), lambda j: (0, 0)),
                  pl.BlockSpec((d, tn), lambda j: (0, j)),
                  pl.BlockSpec((1, tn), lambda j: (0, j))],
        out_specs=pl.BlockSpec((N_MOD_ROWS, tn), lambda j: (0, j)),
        out_shape=jax.ShapeDtypeStruct((N_MOD_ROWS, n), F32),
        compiler_params=_params("arbitrary"),
        name="adaln",
    )(cvec, w, b)


class _Rows:
    def __init__(self, n_ctx, n_lat, t_lat, tm):
        assert n_ctx % tm == 0 and n_lat % tm == 0 and t_lat % tm == 0
        self.tm = tm
        self.nct = n_ctx // tm
        self.nlt = n_lat // tm
        self.per_seq = t_lat // tm
        self.n = self.nct + self.nlt

    def ctx_idx(self, i):
        return jnp.minimum(i, self.nct - 1)

    def lat_idx(self, i):
        return jnp.maximum(i - self.nct, 0)

    def mod_idx(self, i):
        return jnp.where(i < self.nct, 0, 1 + (i - self.nct) // self.per_seq)


SMALL_COL0 = QKV_A + H_A * DV_A
GATE_KINDS = 4
GATE_BETA, GATE_GC, GATE_EGC, GATE_EKD = (kind * 2 * H_A for kind in range(GATE_KINDS))


def _gate_rows(zs, a_log_row, dt_row):
    lane = lax.broadcasted_iota(jnp.int32, (1, LANES), 1)
    x = zs + dt_row
    softplus = jnp.maximum(x, 0.0) + jnp.log(1.0 + jnp.exp(-jnp.abs(x)))
    in_g = jnp.logical_and(lane >= GATE_GC, lane < GATE_EGC)
    g = jnp.where(in_g, -jnp.exp(a_log_row) * softplus, 0.0)
    beta = jax.nn.sigmoid(zs)
    reversed_lane = lane >= GATE_GC + H_A
    ri = lax.broadcasted_iota(jnp.int32, (CHUNK, CHUNK), 0)
    ci = lax.broadcasted_iota(jnp.int32, (CHUNK, CHUNK), 1)
    prefix = jnp.where(ci <= ri, 1.0, 0.0)
    suffix = jnp.where(ci >= ri, 1.0, 0.0)
    exact = dict(preferred_element_type=F32, precision=lax.Precision.HIGHEST)
    rows = []
    for c in range(zs.shape[0] // CHUNK):
        r = slice(c * CHUNK, (c + 1) * CHUNK)
        gch = g[r]
        gc = jnp.where(reversed_lane, jnp.dot(suffix, gch, **exact), jnp.dot(prefix, gch, **exact))
        tot = jnp.sum(gch, axis=0, keepdims=True)
        rows.append(jnp.where(lane < GATE_GC, beta[r],
                              jnp.where(lane < GATE_EGC, gc,
                                        jnp.where(lane < GATE_EKD, pltpu.roll(jnp.exp(gc), GATE_EGC - GATE_GC, 1),
                                                  jnp.where(lane < GATE_EKD + 2 * H_A,
                                                            pltpu.roll(jnp.exp(tot - gc), GATE_EKD - GATE_GC, 1),
                                                            0.0)))))
    return jnp.concatenate(rows, axis=0)


def _prenorm_body(nct, xp_ref, xs_ref, g_ref, sh_ref, sc_ref, ws_ref, al_ref, dt_ref, h_ref, gates_ref):
    i = pl.program_id(0)

    def run(x_ref):
        h = _rms_scale(x_ref[...]) * g_ref[...]
        h = (h * (1.0 + sc_ref[0]) + sh_ref[0]).astype(BF16)
        h_ref[...] = h
        zs = lax.dot_general(h, ws_ref[...].astype(BF16), NT_DIMS, preferred_element_type=F32)
        gates_ref[...] = _gate_rows(zs, al_ref[...], dt_ref[...])

    @pl.when(i < nct)
    def _():
        run(xp_ref)

    @pl.when(i >= nct)
    def _():
        run(xs_ref)


def _prenorm(rows, xp, xs, gain, sh, sc, w_in_t, a_log, dt_bias):
    place = lambda v: jnp.pad(v.reshape(1, 2 * H_A), ((0, 0), (GATE_GC, LANES - GATE_EGC)))
    d = xp.shape[1]
    tm = rows.tm
    n_tok = rows.n * tm
    return pl.pallas_call(
        functools.partial(_prenorm_body, rows.nct),
        grid=(rows.n,),
        in_specs=[pl.BlockSpec((tm, d), lambda i: (rows.ctx_idx(i), 0)),
                  pl.BlockSpec((tm, d), lambda i: (rows.lat_idx(i), 0)),
                  pl.BlockSpec((1, d), lambda i: (0, 0)),
                  pl.BlockSpec((1, 1, d), lambda i: (rows.mod_idx(i), 0, 0)),
                  pl.BlockSpec((1, 1, d), lambda i: (rows.mod_idx(i), 0, 0)),
                  pl.BlockSpec((LANES, d), lambda i: (SMALL_COL0 // LANES, 0)),
                  pl.BlockSpec((1, LANES), lambda i: (0, 0)),
                  pl.BlockSpec((1, LANES), lambda i: (0, 0))],
        out_specs=[pl.BlockSpec((tm, d), lambda i: (i, 0)),
                   pl.BlockSpec((tm, LANES), lambda i: (i, 0))],
        out_shape=[jax.ShapeDtypeStruct((n_tok, d), BF16),
                   jax.ShapeDtypeStruct((n_tok, LANES), F32)],
        compiler_params=_params("arbitrary"),
        name="prenorm",
    )(xp, xs, gain, sh, sc, w_in_t, place(a_log), place(dt_bias))


INPROJ_TN = 1024
N_ALIGNED = SMALL_COL0 // INPROJ_TN


def _inproj_body(h_ref, wt_ref, z_ref, w_scr):
    @pl.when(pl.program_id(1) == 0)
    def _():
        w_scr[...] = wt_ref[...].astype(BF16)

    z_ref[...] = lax.dot_general(h_ref[...], w_scr[...], NT_DIMS, preferred_element_type=F32).astype(BF16)


def _inproj(h, w_in_t):
    n_tok, d = h.shape
    tm, tn = 2048, INPROJ_TN
    first_row = lambda j: pl.multiple_of(j * tn + N_SMALL * jnp.minimum(j // N_ALIGNED, 1), N_SMALL)
    return pl.pallas_call(
        _inproj_body,
        grid=(N_MAIN // tn, n_tok // tm),
        in_specs=[pl.BlockSpec((tm, d), lambda j, i: (i, 0)),
                  pl.BlockSpec((pl.Element(tn), pl.Element(d)), lambda j, i: (first_row(j), 0))],
        out_specs=pl.BlockSpec((tm, tn), lambda j, i: (i, j)),
        out_shape=jax.ShapeDtypeStruct((n_tok, N_MAIN), BF16),
        scratch_shapes=[pltpu.VMEM((tn, d), BF16)],
        compiler_params=_params("arbitrary", "arbitrary"),
        name="inproj",
    )(h, w_in_t)


INV_BLOCK = 16


def _chunk_solve(chains, ii, jj):
    eye = ii == jj
    blk = (ii // INV_BLOCK) == (jj // INV_BLOCK)
    blk2 = (ii // (2 * INV_BLOCK)) == (jj // (2 * INV_BLOCK))
    cols, a_qk, a0, off, d0, rhs = [], [], [], [], [], []
    for kk, qk, v, k, gates, reverse, lane in chains:
        beta, gc, egc, ekd = (gates[:, b + lane:b + lane + 1] for b in (GATE_BETA, GATE_GC, GATE_EGC, GATE_EKD))
        incl = (jj >= ii) if reverse else (jj <= ii)
        strict = (jj > ii) if reverse else (jj < ii)
        gc_row = jnp.sum(jnp.where(eye, gc, 0.0), axis=0, keepdims=True)
        dec = jnp.where(incl, jnp.exp(jnp.where(incl, gc - gc_row, 0.0)), 0.0)
        a = jnp.where(strict, kk * dec, 0.0)
        cols.append((egc, ekd))
        a_qk.append(qk * dec)
        a0.append(jnp.where(blk, a, 0.0))
        off.append(jnp.where(blk, 0.0, a))
        d0.append(jnp.where(eye, 1.0, 0.0) - a0[-1])
        rhs.append(jnp.concatenate([v * beta, k * (beta * egc)], axis=1))
    p = [_mm(x, x) for x in a0]
    for _ in range(int(math.log2(INV_BLOCK)) - 2):
        r = [_mm(jnp.concatenate([pi, di], axis=0), pi) for pi, di in zip(p, d0)]
        p = [ri[:CHUNK] for ri in r]
        d0 = [di + ri[CHUNK:] for di, ri in zip(d0, r)]
    d0 = [di + _mm(di, pi) for di, pi in zip(d0, p)]
    wl = [_mm(di, oi) for di, oi in zip(d0, off)]
    yr = [_mm(di, ri) for di, ri in zip(d0, rhs)]
    x1 = [jnp.where(blk2, x, 0.0) for x in wl]
    yl = [jnp.where(blk2, 0.0, x) for x in wl]
    zl = [y - _mm(x, y) for x, y in zip(x1, yl)]
    zr = [y - _mm(x, y) for x, y in zip(x1, yr)]
    sol = [y - _mm(x, y) for x, y in zip(zl, zr)]
    return [(so[:, :DV_A], so[:, DV_A:], aq, egc, ekd) for so, aq, (egc, ekd) in zip(sol, a_qk, cols)]


def _delta_body(t_len, hp, has_s0, q_ref, k_ref, v_ref, cq_ref, ck_ref, cv_ref, g_ref, *rest):
    if has_s0:
        s0_ref, o_ref, qs, ks, vs, g_scr, u_scr, wq_scr, ak_scr, s_scr = rest
    else:
        o_ref, sfin_ref, qs, ks, vs, g_scr, u_scr, wq_scr, ak_scr, s_scr = rest
    n = t_len // CHUNK
    tpos = lax.broadcasted_iota(jnp.int32, (t_len, 1), 0)

    def conv_act(x, w):
        x_prev = jnp.where(tpos == 0, 0.0, pltpu.roll(x, 1, 0))
        x_next = jnp.where(tpos == t_len - 1, 0.0, pltpu.roll(x, t_len - 1, 0))
        return _silu(x_prev * w[0:1] + x * w[1:2] + x_next * w[2:3])

    def l2n(x):
        return x * lax.rsqrt(jnp.sum(x * x, axis=-1, keepdims=True) + EPS)

    for hh in range(hp):
        cols = slice(hh * LANES, (hh + 1) * LANES)
        qs[:, cols] = l2n(conv_act(q_ref[:, cols].astype(F32), cq_ref[:, cols])) * (DK_A ** -0.5)
        ks[:, cols] = l2n(conv_act(k_ref[:, cols].astype(F32), ck_ref[:, cols]))
        vs[:, cols] = conv_act(v_ref[:, cols].astype(F32), cv_ref[:, cols])
    if hp == H_A:
        g_scr[...] = g_ref[...]
    else:
        g_scr[...] = pltpu.roll(g_ref[...], (LANES - pl.program_id(1) * hp) % LANES, 1)
    if has_s0:
        s_scr[...] = s0_ref[...]
    else:
        s_scr[...] = jnp.zeros_like(s_scr)
    o_ref[...] = jnp.zeros_like(o_ref)
    ii = lax.broadcasted_iota(jnp.int32, (CHUNK, CHUNK), 0)
    jj = lax.broadcasted_iota(jnp.int32, (CHUNK, CHUNK), 1)

    def prep(c, carry):
        r = pl.ds(pl.multiple_of(c * CHUNK, CHUNK), CHUNK)
        g = g_scr[r, :]
        heads = []
        for hh in range(hp):
            cols = slice(hh * LANES, (hh + 1) * LANES)
            heads.append((qs[r, cols], ks[r, cols], vs[r, cols]))
        beta = lambda d, hh: g[:, GATE_BETA + d * H_A + hh:GATE_BETA + d * H_A + hh + 1]
        rr = [_mm_nt(jnp.concatenate([k * beta(0, hh), k * beta(1, hh), q], axis=0), k)
              for hh, (q, k, v) in enumerate(heads)]
        chains = [(rr[hh][d * CHUNK:(d + 1) * CHUNK], rr[hh][2 * CHUNK:], heads[hh][2], heads[hh][1], g,
                   d == 1, d * H_A + hh) for hh in range(hp) for d in (0, 1)]
        solved = _chunk_solve(chains, ii, jj)
        for hh in range(hp):
            cols = slice(hh * LANES, (hh + 1) * LANES)
            q, k = heads[hh][0], heads[hh][1]
            for d in (0, 1):
                u, w, a_qk, egc, ekd = solved[hh * 2 + d]
                slot = (d * hp + hh) * n + c
                u_scr[d, r, cols] = u
                wq_scr[slot] = jnp.concatenate([w, q * egc], axis=0).astype(BF16)
                ak_scr[slot] = jnp.concatenate([a_qk, (k * ekd).T], axis=0).astype(BF16)
        return carry

    lax.fori_loop(0, n, prep, 0)

    def scan(c, carry):
        chains = []
        for hh in range(hp):
            for d in (0, 1):
                cc = c if d == 0 else n - 1 - c
                chains.append((hh, d, pl.multiple_of(cc * CHUNK, CHUNK), (d * hp + hh) * n + cc))
        s = [s_scr[d, hh] for hh, d, r0, slot in chains]
        r1 = [jnp.dot(wq_scr[slot], si.astype(BF16), preferred_element_type=F32)
              for si, (hh, d, r0, slot) in zip(s, chains)]
        v_new = [u_scr[d, pl.ds(r0, CHUNK), hh * LANES:(hh + 1) * LANES] - ri[:CHUNK]
                 for ri, (hh, d, r0, slot) in zip(r1, chains)]
        r2 = [jnp.dot(ak_scr[slot], vi.astype(BF16), preferred_element_type=F32)
              for vi, (hh, d, r0, slot) in zip(v_new, chains)]
        for si, r1i, r2i, (hh, d, r0, slot) in zip(s, r1, r2, chains):
            o_ref[pl.ds(r0, CHUNK), hh * LANES:(hh + 1) * LANES] += r1i[CHUNK:] + r2i[:CHUNK]
            lane = GATE_EGC + d * H_A + hh
            edge = g_scr[pl.ds(pl.multiple_of(r0 + (0 if d else CHUNK - 8), 8), 8), lane:lane + 1]
            s_scr[d, hh] = si * (edge[0:1] if d else edge[7:8]) + r2i[CHUNK:]
        return carry

    lax.fori_loop(0, n, scan, 0)
    if not has_s0:
        sfin_ref[...] = s_scr[...]


def _deltanet(z, conv_w, gates, s0, n_seq, t_len, row0, hp):
    rb = row0 // t_len
    w = hp * LANES
    n = t_len // CHUNK
    seq_blk = lambda col0: pl.BlockSpec((t_len, w), lambda b, h: (rb + b, col0 // hp + h))
    cw_blk = lambda col0: pl.BlockSpec((CONV_K, w), lambda b, h: (0, col0 // hp + h))
    state_blk = pl.BlockSpec((None, None, 2, hp, DK_A, DV_A), lambda b, h: (b, 0, 0, h, 0, 0))
    in_specs = [seq_blk(0), seq_blk(H_A), seq_blk(2 * H_A), cw_blk(0), cw_blk(H_A), cw_blk(2 * H_A),
                pl.BlockSpec((t_len, LANES), lambda b, h: (rb + b, 0))]
    args = [z, z, z, conv_w, conv_w, conv_w, gates]
    o_spec = pl.BlockSpec((t_len, w), lambda b, h: (b, h))
    o_shape = jax.ShapeDtypeStruct((n_seq * t_len, H_A * DV_A), F32)
    has_s0 = s0 is not None
    if has_s0:
        in_specs += [state_blk]
        args += [s0]
        out_specs, out_shape = o_spec, o_shape
    else:
        out_specs = [o_spec, state_blk]
        out_shape = [o_shape, jax.ShapeDtypeStruct((n_seq, 1, 2, H_A, DK_A, DV_A), F32)]

    return pl.pallas_call(
        functools.partial(_delta_body, t_len, hp, has_s0),
        grid=(n_seq, H_A // hp),
        in_specs=in_specs,
        out_specs=out_specs,
        out_shape=out_shape,
        scratch_shapes=[pltpu.VMEM((t_len, w), F32)] * 3
        + [pltpu.VMEM((t_len, LANES), F32),
           pltpu.VMEM((2, t_len, w), F32),
           pltpu.VMEM((2 * hp * n, 2 * CHUNK, DV_A), BF16),
           pltpu.VMEM((2 * hp * n, CHUNK + DK_A, CHUNK), BF16),
           pltpu.VMEM((2, hp, DK_A, DV_A), F32)],
        compiler_params=_params("arbitrary", "arbitrary"),
        name="deltanet_lat" if has_s0 else "deltanet_ctx",
    )(*args)


def _subln(o, sub_ref, lam_init):
    return _rms_scale(o) * sub_ref[...] * (1.0 - lam_init)


def _attn_ctx_body(lam_init, q_ref, k_ref, v_ref, lam_ref, sub_ref, o_ref, ck_ref, cv_ref):
    heads = [slice(h * LANES, (h + 1) * LANES) for h in range(H_B)]
    lam = lam_ref[0:1, 0:1]
    ks = [k_ref[:, c] for c in heads]
    vs = [v_ref[:, c] for c in heads]
    for h in range(H_B):
        ck_ref[h] = ks[h].astype(F32)
        cv_ref[h] = vs[h].astype(F32)
    qb = [q_ref[:, c] * (DQK_B ** -0.5) for c in heads]
    kb = ks
    probs = []
    for lo in (0, DQK_B):
        s = [lax.dot_general(q[:, lo:lo + DQK_B], k[:, lo:lo + DQK_B], NT_DIMS, preferred_element_type=F32)
             for q, k in zip(qb, kb)]
        e = [jnp.exp(x - jnp.max(x, axis=-1, keepdims=True)) for x in s]
        probs.append([x * (1.0 / jnp.sum(x, axis=-1, keepdims=True)) for x in e])
    o = [jnp.dot((p1 - lam * p2).astype(BF16), v, preferred_element_type=F32)
         for p1, p2, v in zip(probs[0], probs[1], vs)]
    for c, oh in zip(heads, o):
        o_ref[:, c] = _subln(oh, sub_ref, lam_init).astype(BF16)


def _attn_ctx(z, lam, subln, n_seq, t_len, lam_init):
    n_tok = n_seq * t_len
    w = H_B * LANES
    blk = lambda col0: pl.BlockSpec((t_len, w), lambda b: (b, col0 // w))
    cache_blk = pl.BlockSpec((None, None, H_B, t_len, LANES), lambda b: (b, 0, 0, 0, 0))
    cache_shape = jax.ShapeDtypeStruct((n_seq, 1, H_B, t_len, LANES), F32)
    return pl.pallas_call(
        functools.partial(_attn_ctx_body, lam_init),
        grid=(n_seq,),
        in_specs=[blk(Z_QB), blk(Z_KB), blk(Z_VB),
                  pl.BlockSpec((1, LANES), lambda b: (0, 0)),
                  pl.BlockSpec((1, LANES), lambda b: (0, 0))],
        out_specs=[pl.BlockSpec((t_len, w), lambda b: (b, 0)), cache_blk, cache_blk],
        out_shape=[jax.ShapeDtypeStruct((n_tok, H_B * DV_B), BF16), cache_shape, cache_shape],
        compiler_params=_params("arbitrary"),
        name="attn_ctx",
    )(z, z, z, lam, subln)


def _rope(x, cos, sin_signed):
    lane = lax.broadcasted_iota(jnp.int32, (1, LANES), 1)
    first = (lane % 32) < 16
    partner = jnp.where(first, pltpu.roll(x, LANES - 16, 1), pltpu.roll(x, 16, 1))
    return x * cos + partner * sin_signed


LAT_HEADS = 2


def _attn_lat_body(lam_init, n_past, q_ref, k_ref, v_ref, pk_ref, pv_ref, cosq_ref, sinq_ref, cos_ref, sin_ref,
                   lam_ref, sub_ref, o_ref, keys, vals):
    heads = [slice(h * LANES, (h + 1) * LANES) for h in range(LAT_HEADS)]

    @pl.when(pl.program_id(2) == 0)
    def _():
        for h, c in enumerate(heads):
            keys[h, 0:n_past, :] = pk_ref[h].astype(BF16)
            vals[h, 0:n_past, :] = pv_ref[h].astype(BF16)
            keys[h, n_past:, :] = _rope(k_ref[:, c].astype(F32), cos_ref[...], sin_ref[...]).astype(BF16)
            vals[h, n_past:, :] = v_ref[:, c]

    lam = lam_ref[0:1, 0:1]
    qb = [(_rope(q_ref[:, c].astype(F32), cosq_ref[...], sinq_ref[...]) * (DQK_B ** -0.5)).astype(BF16)
          for c in heads]
    probs = []
    for lo in (0, DQK_B):
        s = [lax.dot_general(q[:, lo:lo + DQK_B], keys[h, :, lo:lo + DQK_B], NT_DIMS, preferred_element_type=F32)
             for h, q in enumerate(qb)]
        e = [jnp.exp(x - jnp.max(x, axis=-1, keepdims=True)) for x in s]
        probs.append([x * (1.0 / jnp.sum(x, axis=-1, keepdims=True)) for x in e])
    for h, c in enumerate(heads):
        a = (probs[0][h] - lam * probs[1][h]).astype(BF16)
        o_ref[:, c] = _subln(jnp.dot(a, vals[h], preferred_element_type=F32), sub_ref, lam_init).astype(BF16)


def _attn_lat(z, cache_k, cache_v, cos, sin, lam, subln, n_seq, t_len, row0, lam_init):
    tq = 256
    nq = t_len // tq
    n_past = cache_k.shape[3]
    rbq = row0 // tq
    rbs = row0 // t_len
    w = LAT_HEADS * LANES
    seq_blk = lambda col0: pl.BlockSpec((t_len, w), lambda b, h, qi: (rbs + b, col0 // w + h))
    past_blk = pl.BlockSpec((None, None, LAT_HEADS, n_past, LANES), lambda b, h, qi: (b, 0, h, 0, 0))
    row_vec = pl.BlockSpec((1, LANES), lambda b, h, qi: (0, 0))
    return pl.pallas_call(
        functools.partial(_attn_lat_body, lam_init, n_past),
        grid=(n_seq, H_B // LAT_HEADS, nq),
        in_specs=[pl.BlockSpec((tq, w), lambda b, h, qi: (rbq + b * nq + qi, Z_QB // w + h)),
                  seq_blk(Z_KB), seq_blk(Z_VB), past_blk, past_blk,
                  pl.BlockSpec((tq, LANES), lambda b, h, qi: (qi, 0)),
                  pl.BlockSpec((tq, LANES), lambda b, h, qi: (qi, 0)),
                  pl.BlockSpec((t_len, LANES), lambda b, h, qi: (0, 0)),
                  pl.BlockSpec((t_len, LANES), lambda b, h, qi: (0, 0)),
                  row_vec, row_vec],
        out_specs=pl.BlockSpec((tq, w), lambda b, h, qi: (b * nq + qi, h)),
        out_shape=jax.ShapeDtypeStruct((n_seq * t_len, H_B * DV_B), BF16),
        scratch_shapes=[pltpu.VMEM((LAT_HEADS, n_past + t_len, LANES), BF16)] * 2,
        compiler_params=_params("arbitrary", "arbitrary", "arbitrary"),
        name="attn_lat",
    )(z, z, z, cache_k, cache_v, cos, sin, cos, sin, lam, subln)


def _rope_tables(t_len):
    t = np.arange(t_len)
    pos = np.stack([t // GRID_W, t % GRID_W], axis=1).astype(np.float32)
    nf = DQK_B // 4
    inv_freq = np.float32(ROPE_BASE) ** (-np.arange(nf, dtype=np.float32) / np.float32(nf))
    lane = np.arange(LANES)
    half = (lane % DQK_B) // (DQK_B // 2)
    ang = (pos[:, half] * inv_freq[lane % nf][None, :]).astype(np.float32)
    sign = np.where((lane % (DQK_B // 2)) < nf, -1.0, 1.0).astype(np.float32)
    return jnp.asarray(np.cos(ang), F32), jnp.asarray(np.sin(ang) * sign[None, :], F32)


def _merge_body(nct, oac_ref, oal_ref, ga_ref, on_ref, obc_ref, obl_ref, wa_f32, wb_f32, *rest):
    gm_refs, m_ref, wa_ref, wb_ref = rest[:-3], rest[-3], rest[-2], rest[-1]
    i = pl.program_id(0)
    n_blk = len(gm_refs) // 2

    @pl.when(i == 0)
    def _():
        wa_ref[...] = wa_f32[...].astype(BF16)
        wb_ref[...] = wb_f32[...].astype(BF16)

    def run(oa_ref, ob_ref):
        a = jnp.concatenate(
            [(_rms_scale(oa_ref[:, c]) * on_ref[...] * _silu(ga_ref[:, c].astype(F32))).astype(BF16)
             for c in (slice(h * DV_A, (h + 1) * DV_A) for h in range(H_A))], axis=1)
        ya = jnp.dot(a, wa_ref[...], preferred_element_type=F32)
        yb = jnp.dot(ob_ref[...], wb_ref[...], preferred_element_type=F32)
        tn = gm_refs[0].shape[1]
        for j in range(n_blk):
            c = slice(j * tn, (j + 1) * tn)
            m_ref[:, c] = (jax.nn.sigmoid(gm_refs[j][...].astype(F32)) * ya[:, c]
                           + jax.nn.sigmoid(gm_refs[n_blk + j][...].astype(F32)) * yb[:, c]).astype(BF16)

    @pl.when(i < nct)
    def _():
        run(oac_ref, obc_ref)

    @pl.when(i >= nct)
    def _():
        run(oal_ref, obl_ref)


def _merge(rows, z, oa_c, oa_l, ob_c, ob_l, onorm, w_up_a, w_up_b):
    n_tok = z.shape[0]
    tm = rows.tm
    ka, d = w_up_a.shape
    tn = 1024
    nj = d // tn
    ctx_blk = pl.BlockSpec((tm, ka), lambda i: (rows.ctx_idx(i), 0))
    lat_blk = pl.BlockSpec((tm, ka), lambda i: (rows.lat_idx(i), 0))
    weight = pl.BlockSpec((ka, d), lambda i: (0, 0), pipeline_mode=pl.Buffered(1))
    gate_cols = [pl.BlockSpec((tm, tn), functools.partial(lambda i, c: (i, c), c=Z_GM // tn + j)) for j in range(2 * nj)]
    return pl.pallas_call(
        functools.partial(_merge_body, rows.nct),
        grid=(n_tok // tm,),
        in_specs=[ctx_blk, lat_blk,
                  pl.BlockSpec((tm, ka), lambda i: (i, Z_GATE_A // ka)),
                  pl.BlockSpec((1, DV_A), lambda i: (0, 0)),
                  ctx_blk, lat_blk, weight, weight] + gate_cols,
        out_specs=pl.BlockSpec((tm, d), lambda i: (i, 0)),
        out_shape=jax.ShapeDtypeStruct((n_tok, d), BF16),
        scratch_shapes=[pltpu.VMEM((ka, d), BF16)] * 2,
        compiler_params=_params("arbitrary"),
        name="merge",
    )(oa_c, oa_l, z, onorm, ob_c, ob_l, w_up_a, w_up_b, *([z] * (2 * nj)))


ROUTE_E = 0
ROUTE_W = TOP_K


def _route_rows(lg):
    lane = lax.broadcasted_iota(jnp.int32, lg.shape, 1)
    neg = -jnp.inf

    def first_max(x):
        m = jnp.max(x, axis=1, keepdims=True)
        return m, jnp.min(jnp.where(x == m, lane, LANES), axis=1, keepdims=True)

    gl = jnp.where(lane < N_GROUPS, lg, neg)
    gmax, g_idx = first_max(gl)
    pg_top = 1.0 / jnp.sum(jnp.exp(gl - gmax), axis=1, keepdims=True)
    lo = N_GROUPS + E_PER_GROUP * g_idx
    el = jnp.where(jnp.logical_and(lane >= lo, lane < lo + E_PER_GROUP), lg, neg)
    emax, i1 = first_max(el)
    esum = jnp.sum(jnp.exp(el - emax), axis=1, keepdims=True)
    e2max, i2 = first_max(jnp.where(lane == i1, neg, el))
    p1 = 1.0 / esum
    p2 = jnp.exp(e2max - emax) / esum
    den = p1 + p2
    vals = [(i1 - N_GROUPS).astype(F32), (i2 - N_GROUPS).astype(F32), pg_top * p1 / den, pg_top * p2 / den]
    out = jnp.zeros(lg.shape, F32)
    for pos, val in enumerate(vals):
        out = jnp.where(lane == pos, val, out)
    return out


def _outproj_body(nct, m_ref, wo_ref, xp_ref, xs_ref, gt_ref, g2_ref, sh_ref, sc_ref, wr_ref, br_ref,
                  x1_ref, h2_ref, rt_ref):
    i = pl.program_id(0)
    half = m_ref.shape[0] // 2

    def finish(x_ref):
        for r in (slice(0, half), slice(half, 2 * half)):
            y = jnp.dot(m_ref[r, :], wo_ref[...], preferred_element_type=F32)
            x1 = x_ref[r, :] + gt_ref[0] * y
            x1_ref[r, :] = x1
            h2 = _rms_scale(x1) * g2_ref[...]
            h2 = h2 * (1.0 + sc_ref[0]) + sh_ref[0]
            h2_ref[r, :] = h2
            hi = h2.astype(BF16)
            lo = (h2 - hi.astype(F32)).astype(BF16)
            p_hi = jnp.dot(hi, wr_ref[...], preferred_element_type=F32)
            p_lo = jnp.dot(lo, wr_ref[...], preferred_element_type=F32)
            lg = p_hi[:, :LANES] + p_hi[:, LANES:] + p_lo[:, :LANES] + p_lo[:, LANES:] + br_ref[...]
            rt_ref[r, :] = _route_rows(lg)

    @pl.when(i < nct)
    def _():
        finish(xp_ref)

    @pl.when(i >= nct)
    def _():
        finish(xs_ref)


def _outproj(rows, mixed, w_o, xp, xs, gt1, gain2, sh2, sc2, w_r, b_r):
    d = xp.shape[1]
    tm = rows.tm
    n_tok = rows.n * tm
    mod = lambda: pl.BlockSpec((1, 1, d), lambda i: (rows.mod_idx(i), 0, 0))
    tok = pl.BlockSpec((tm, d), lambda i: (i, 0))
    return pl.pallas_call(
        functools.partial(_outproj_body, rows.nct),
        grid=(rows.n,),
        in_specs=[tok,
                  pl.BlockSpec((d, d), lambda i: (0, 0)),
                  pl.BlockSpec((tm, d), lambda i: (rows.ctx_idx(i), 0)),
                  pl.BlockSpec((tm, d), lambda i: (rows.lat_idx(i), 0)),
                  mod(),
                  pl.BlockSpec((1, d), lambda i: (0, 0)),
                  mod(), mod(),
                  pl.BlockSpec((d, 2 * LANES), lambda i: (0, 0)),
                  pl.BlockSpec((1, LANES), lambda i: (0, 0))],
        out_specs=[tok, tok, pl.BlockSpec((tm, LANES), lambda i: (i, 0))],
        out_shape=[jax.ShapeDtypeStruct((n_tok, d), F32), jax.ShapeDtypeStruct((n_tok, d), F32),
                   jax.ShapeDtypeStruct((n_tok, LANES), F32)],
        compiler_params=_params("arbitrary"),
        name="outproj",
    )(mixed, w_o, xp, xs, gt1, gain2, sh2, sc2, w_r, b_r)


DISPATCH_BLOCK = 256
TBL_EXPERT, TBL_VALID, TBL_COUNT, TBL_OFFSET = 0, 1, 2, 3


def _dispatch_body(n_tok, route_ref, pos_ref, tbl_ref):
    nb = n_tok // DISPATCH_BLOCK
    lane = lax.broadcasted_iota(jnp.int32, (1, LANES), 1)
    lane_f = lane.astype(F32)

    def one_hot(b, k):
        r = pl.ds(pl.multiple_of(b * DISPATCH_BLOCK, DISPATCH_BLOCK), DISPATCH_BLOCK)
        return jnp.where(route_ref[r, ROUTE_E + k:ROUTE_E + k + 1] == lane_f, 1.0, 0.0)

    def count(b, acc):
        return acc + jnp.sum(one_hot(b, 0) + one_hot(b, 1), axis=0, keepdims=True)

    counts = lax.fori_loop(0, nb, count, jnp.zeros((1, LANES), F32))
    padded = jnp.floor((counts + (MOE_TILE - 1)) * (1.0 / MOE_TILE)) * MOE_TILE
    pad_end = padded
    shift = 1
    while shift < LANES:
        pad_end = pad_end + jnp.where(lane >= shift, pltpu.roll(pad_end, shift, 1), 0.0)
        shift *= 2
    pad_off = pad_end - padded

    ri = lax.broadcasted_iota(jnp.int32, (DISPATCH_BLOCK, DISPATCH_BLOCK), 0)
    ci = lax.broadcasted_iota(jnp.int32, (DISPATCH_BLOCK, DISPATCH_BLOCK), 1)
    before = jnp.where(ci < ri, 1.0, 0.0).astype(BF16)

    def place(b, run):
        oh = [one_hot(b, k) for k in range(TOP_K)]
        base = pad_off + run
        out = jnp.zeros((DISPATCH_BLOCK, LANES), F32)
        lane_b = lax.broadcasted_iota(jnp.int32, (DISPATCH_BLOCK, LANES), 1)
        for k in range(TOP_K):
            prior = jnp.dot(before, oh[k].astype(BF16), preferred_element_type=F32)
            pos = jnp.sum(oh[k] * (base + prior), axis=1, keepdims=True)
            out = jnp.where(lane_b == k, pos, out)
            base = base + jnp.sum(oh[k], axis=0, keepdims=True)
        r = pl.ds(pl.multiple_of(b * DISPATCH_BLOCK, DISPATCH_BLOCK), DISPATCH_BLOCK)
        pos_ref[r, :] = out.astype(jnp.int32)
        return base - pad_off

    lax.fori_loop(0, nb, place, jnp.zeros((1, LANES), F32))

    end_col = jnp.transpose(jnp.broadcast_to(pad_end, (8, LANES)))[:, 0:1]
    e_col = lax.broadcasted_iota(jnp.int32, (LANES, 1), 0)
    tile_start = lane_f * MOE_TILE
    passed = jnp.where(jnp.logical_and(end_col <= tile_start, e_col < N_EXPERTS), 1.0, 0.0)
    tile_expert = jnp.minimum(jnp.sum(passed, axis=0, keepdims=True), N_EXPERTS - 1.0)
    total = jnp.sum(jnp.where(lane == N_EXPERTS - 1, pad_end, 0.0), axis=1, keepdims=True)
    tile_valid = jnp.where(tile_start < total, 1.0, 0.0)
    row = lax.broadcasted_iota(jnp.int32, (8, LANES), 0)
    tbl = jnp.zeros((8, LANES), F32)
    for idx, val in ((TBL_EXPERT, tile_expert), (TBL_VALID, tile_valid), (TBL_COUNT, counts), (TBL_OFFSET, pad_off)):
        tbl = jnp.where(row == idx, val, tbl)
    tbl_ref[...] = tbl.astype(jnp.int32)


def _dispatch(route):
    n_tok = route.shape[0]
    return pl.pallas_call(
        functools.partial(_dispatch_body, n_tok),
        out_shape=[jax.ShapeDtypeStruct((n_tok, LANES), jnp.int32), jax.ShapeDtypeStruct((8, LANES), jnp.int32)],
        compiler_params=pltpu.CompilerParams(vmem_limit_bytes=VMEM_LIMIT),
        name="dispatch",
    )(route)


def _scatter_body(tm, n_tiles, pos_ref, cnt_ref, off_ref, h_ref, o_hbm, zbuf, sem):
    i = pl.program_id(0)

    @pl.when(i == 0)
    def _():
        zbuf[...] = jnp.zeros_like(zbuf)
        last = N_EXPERTS - 1
        first_empty = (off_ref[last] + cnt_ref[last] + MOE_TILE - 1) // MOE_TILE

        def tile_copy(t):
            rows = pl.ds(pl.multiple_of(t * MOE_TILE, MOE_TILE), MOE_TILE)
            return pltpu.make_async_copy(zbuf, o_hbm.at[rows, :], sem.at[1])

        def partial_tile(e, start):
            cnt = cnt_ref[e]

            @pl.when(cnt % MOE_TILE != 0)
            def _():
                cp = tile_copy((off_ref[e] + cnt) // MOE_TILE)
                if start:
                    cp.start()
                else:
                    cp.wait()

        def zero_partial(e, carry):
            partial_tile(e, True)
            return carry

        def partial_done(e, carry):
            partial_tile(e, False)
            return carry

        def zero_tile(t, carry):
            tile_copy(t).start()
            return carry

        def zero_done(t, carry):
            tile_copy(t).wait()
            return carry

        lax.fori_loop(0, N_EXPERTS, zero_partial, 0)
        lax.fori_loop(first_empty, n_tiles, zero_tile, 0)
        lax.fori_loop(0, N_EXPERTS, partial_done, 0)
        lax.fori_loop(first_empty, n_tiles, zero_done, 0)

    def row_copies(r):
        return [pltpu.make_async_copy(h_ref.at[pl.ds(r, 1), :],
                                      o_hbm.at[pl.ds(pos_ref[(i * tm + r) * TOP_K + k], 1), :], sem.at[0])
                for k in range(TOP_K)]

    def start(r, carry):
        for k, cp in enumerate(row_copies(r)):
            cp.start(priority=k)
        return carry

    def wait(r, carry):
        for cp in row_copies(r):
            cp.wait()
        return carry

    lax.fori_loop(0, tm, start, 0, unroll=8)
    lax.fori_loop(0, tm, wait, 0, unroll=8)


def _scatter_rows(h2, pos_flat, counts, offsets, n_rows):
    n_tok, d = h2.shape
    tm = 512
    grid_spec = pltpu.PrefetchScalarGridSpec(
        num_scalar_prefetch=3,
        grid=(n_tok // tm,),
        in_specs=[pl.BlockSpec((tm, d), lambda i, pos, cnt, off: (i, 0))],
        out_specs=pl.BlockSpec(memory_space=pl.ANY),
        scratch_shapes=[pltpu.VMEM((MOE_TILE, d), F32), pltpu.SemaphoreType.DMA((2,))],
    )
    return pl.pallas_call(
        functools.partial(_scatter_body, tm, n_rows // MOE_TILE),
        grid_spec=grid_spec,
        out_shape=jax.ShapeDtypeStruct((n_rows, d), F32),
        compiler_params=_params("arbitrary"),
        name="scatter_rows",
    )(pos_flat, counts, offsets, h2)


def _moe_body(n_tiles, te_ref, tv_ref, cnt_ref, x_ref, wg_hbm, wu_hbm, wd_hbm, y_ref,
              wg_f, wu_f, wd_f, wg_b, wu_b, wd_b, slot_ref, sem):
    i = pl.program_id(0)
    prev = jnp.maximum(i - 1, 0)
    valid = tv_ref[i] == 1
    e = te_ref[i]

    def fetch(expert, slot):
        return [pltpu.make_async_copy(src.at[expert], dst.at[slot], sem.at[slot])
                for src, dst in ((wg_hbm, wg_f), (wu_hbm, wu_f), (wd_hbm, wd_f))]

    @pl.when(jnp.logical_and(valid, i == 0))
    def _():
        slot_ref[0] = 1
        for cp in fetch(e, 0):
            cp.start()

    @pl.when(jnp.logical_and(valid, jnp.logical_or(i == 0, e != te_ref[prev])))
    def _():
        slot = 1 - slot_ref[0]
        slot_ref[0] = slot
        for cp in fetch(e, slot):
            cp.wait()
        nxt = jnp.minimum(i + (cnt_ref[e] + MOE_TILE - 1) // MOE_TILE, n_tiles - 1)

        @pl.when(jnp.logical_and(nxt > i, tv_ref[nxt] == 1))
        def _():
            for cp in fetch(te_ref[nxt], 1 - slot):
                cp.start()

        wg_b[...] = wg_f[slot].astype(BF16)
        wu_b[...] = wu_f[slot].astype(BF16)
        wd_b[...] = wd_f[slot].astype(BF16)

    @pl.when(valid)
    def _():
        x = x_ref[...].astype(BF16)
        g = jnp.dot(x, wg_b[...], preferred_element_type=F32)
        u = jnp.dot(x, wu_b[...], preferred_element_type=F32)
        y_ref[...] = jnp.dot((_silu(g) * u).astype(BF16), wd_b[...], preferred_element_type=F32)

    @pl.when(tv_ref[i] == 0)
    def _():
        y_ref[...] = jnp.zeros_like(y_ref)


def _moe(x_sorted, tile_expert, tile_valid, counts, w_g, w_u, w_d):
    n_tiles = tile_expert.shape[0]
    d = x_sorted.shape[1]
    f = w_g.shape[-1]
    hbm = pl.BlockSpec(memory_space=pl.ANY)
    grid_spec = pltpu.PrefetchScalarGridSpec(
        num_scalar_prefetch=3,
        grid=(n_tiles,),
        in_specs=[pl.BlockSpec((MOE_TILE, d), lambda i, te, tv, cnt: (i * tv[i], 0)), hbm, hbm, hbm],
        out_specs=pl.BlockSpec((MOE_TILE, d), lambda i, te, tv, cnt: (i, 0)),
        scratch_shapes=[pltpu.VMEM((2, d, f), F32), pltpu.VMEM((2, d, f), F32), pltpu.VMEM((2, f, d), F32),
                        pltpu.VMEM((d, f), BF16), pltpu.VMEM((d, f), BF16), pltpu.VMEM((f, d), BF16),
                        pltpu.SMEM((1,), jnp.int32), pltpu.SemaphoreType.DMA((2,))],
    )
    return pl.pallas_call(
        functools.partial(_moe_body, n_tiles),
        grid_spec=grid_spec,
        out_shape=jax.ShapeDtypeStruct((n_tiles * MOE_TILE, d), F32),
        compiler_params=_params("arbitrary"),
        name="moe",
    )(tile_expert, tile_valid, counts, x_sorted, w_g, w_u, w_d)


def _combine_body(tm, n_tiles, row0, pos_ref, y_hbm, rt_ref, x1_ref, gt_ref, gf_ref, o_ref, ybuf, sem):
    i = pl.program_id(0)
    slot = i % 2

    def gather(tile, dst_slot, start):
        def row(r, carry):
            for k in range(TOP_K):
                p = pos_ref[(row0 + tile * tm + r) * TOP_K + k]
                cp = pltpu.make_async_copy(y_hbm.at[pl.ds(p, 1), :], ybuf.at[dst_slot, k, pl.ds(r, 1), :],
                                           sem.at[dst_slot])
                if start:
                    cp.start(priority=k)
                else:
                    cp.wait()
            return carry

        lax.fori_loop(0, tm, row, 0, unroll=8)

    @pl.when(i == 0)
    def _():
        gather(0, 0, True)

    @pl.when(i + 1 < n_tiles)
    def _():
        gather(i + 1, 1 - slot, True)

    gather(i, slot, False)
    rt = rt_ref[...]
    y = rt[:, ROUTE_W:ROUTE_W + 1] * ybuf[slot, 0] + rt[:, ROUTE_W + 1:ROUTE_W + 2] * ybuf[slot, 1]
    x2 = x1_ref[...] + gt_ref[0] * y
    o_ref[...] = _rms_scale(x2) * gf_ref[...]


def _combine(y_sorted, pos_flat, route, x1, gt2, gain_f, row0, n_rows, mod_of_tile, tm):
    d = x1.shape[1]
    rb = row0 // tm
    n_tiles = n_rows // tm
    grid_spec = pltpu.PrefetchScalarGridSpec(
        num_scalar_prefetch=1,
        grid=(n_tiles,),
        in_specs=[pl.BlockSpec(memory_space=pl.ANY),
                  pl.BlockSpec((tm, LANES), lambda i, pos: (rb + i, 0)),
                  pl.BlockSpec((tm, d), lambda i, pos: (rb + i, 0)),
                  pl.BlockSpec((1, 1, d), lambda i, pos: (mod_of_tile(i), 0, 0)),
                  pl.BlockSpec((1, d), lambda i, pos: (0, 0))],
        out_specs=pl.BlockSpec((tm, d), lambda i, pos: (i, 0)),
        scratch_shapes=[pltpu.VMEM((2, TOP_K, tm, d), F32), pltpu.SemaphoreType.DMA((2,))],
    )
    return pl.pallas_call(
        functools.partial(_combine_body, tm, n_tiles, row0),
        grid_spec=grid_spec,
        out_shape=jax.ShapeDtypeStruct((n_rows, d), F32),
        compiler_params=_params("arbitrary"),
        name="combine",
    )(pos_flat, y_sorted, route, x1, gt2, gain_f)


def kernel(x_prompt, x_sample, c, state_delta, cache_k, cache_v, c_ctx, w_ada, b_ada, norm_mix, norm_ffn, w_in,
           conv_a, a_log, dt_bias, onorm_a, lam, subln_b, w_up_a, w_up_b, w_o, w_rg, b_rg, w_re, b_re,
           w_e_gate, w_e_up, w_e_down, norm_final):
    assert w_in.shape[0] == 1, "single trunk layer"
    l = 0
    lam_init = 0.8 - 0.6 * math.exp(-0.3 * l)
    bc, tc, d = x_prompt.shape
    bl, tl, _ = x_sample.shape
    n_ctx, n_lat = bc * tc, bl * tl
    xp = x_prompt.reshape(n_ctx, d)
    xs = x_sample.reshape(n_lat, d)

    cvec = jnp.zeros((N_MOD_ROWS, d), F32).at[0].set(c_ctx).at[1:1 + bl].set(c)
    mod = _adaln(cvec, w_ada[l], b_ada[l][None, :])
    sh1, sc1, gt1, sh2, sc2, gt2 = [m.reshape(N_MOD_ROWS, 1, d) for m in jnp.split(mod, 6, axis=1)]

    rows = _Rows(n_ctx, n_lat, tl, 512)
    w_in_t = jnp.swapaxes(w_in[l], 0, 1)
    h1, gates = _prenorm(rows, xp, xs, norm_mix[l][None, :], sh1, sc1, w_in_t, a_log[l], dt_bias[l])
    z = _inproj(h1, w_in_t)

    oa_c, new_state = _deltanet(z, conv_a[l], gates, None, bc, tc, 0, 8)
    oa_l = _deltanet(z, conv_a[l], gates, state_delta, bl, tl, n_ctx, 4)

    lv = lam[l]
    lam_val = jnp.exp(jnp.sum(lv[0] * lv[1])) - jnp.exp(jnp.sum(lv[2] * lv[3])) + lam_init
    lam_row = jnp.full((1, LANES), lam_val, F32)
    sub_row = subln_b[l][None, :]
    ob_c, new_k, new_v = _attn_ctx(z, lam_row, sub_row, bc, tc, lam_init)
    cos, sin = _rope_tables(tl)
    ob_l = _attn_lat(z, cache_k, cache_v, cos, sin, lam_row, sub_row, bl, tl, n_ctx, lam_init)

    mixed = _merge(_Rows(n_ctx, n_lat, tl, 256), z, oa_c, oa_l, ob_c, ob_l, onorm_a[l][None, :],
                   w_up_a[l], w_up_b[l])
    w_r = jnp.pad(jnp.concatenate([w_rg[l], w_re[l]], axis=1), ((0, 0), (0, LANES - N_GROUPS - N_EXPERTS)))
    w_r_hi = w_r.astype(BF16)
    w_r = jnp.concatenate([w_r_hi, (w_r - w_r_hi.astype(F32)).astype(BF16)], axis=1)
    b_r = jnp.pad(jnp.concatenate([b_rg[l], b_re[l]]), (0, LANES - N_GROUPS - N_EXPERTS))[None, :]
    x1, h2, route = _outproj(rows, mixed, w_o[l].astype(BF16), xp, xs, gt1, norm_ffn[l][None, :], sh2, sc2, w_r, b_r)

    n_tok = n_ctx + n_lat
    n_tiles = (n_tok * TOP_K) // MOE_TILE + N_EXPERTS
    pos, tbl = _dispatch(route)
    pos_flat = pos[:, :TOP_K].reshape(-1)
    counts, offsets = tbl[TBL_COUNT, :N_EXPERTS], tbl[TBL_OFFSET, :N_EXPERTS]
    x_sorted = _scatter_rows(h2, pos_flat, counts, offsets, n_tiles * MOE_TILE)
    y_sorted = _moe(x_sorted, tbl[TBL_EXPERT, :n_tiles], tbl[TBL_VALID, :n_tiles], counts,
                    w_e_gate[l], w_e_up[l], w_e_down[l])
    gf = norm_final[None, :]
    tm_c = 256
    y_prompt = _combine(y_sorted, pos_flat, route, x1, gt2, gf, 0, n_ctx, lambda i: 0 * i, tm_c)
    y_sample = _combine(y_sorted, pos_flat, route, x1, gt2, gf, n_ctx, n_lat, lambda i: 1 + i // (tl // tm_c), tm_c)
    return (y_prompt.reshape(bc, tc, d), y_sample.reshape(bl, tl, d), new_state, new_k, new_v)
```

```python
import functools
import math

import jax
import jax.numpy as jnp
import numpy as np
from jax import lax
from jax.experimental import pallas as pl
from jax.experimental.pallas import tpu as pltpu

F32 = jnp.float32
BF16 = jnp.bfloat16

D_MODEL = 2048
GRID_W = 64
H_A = 8
DK_A = 128
DV_A = 128
CONV_K = 3
CHUNK = 64
H_B = 8
DQK_B = 64
DV_B = 2 * DQK_B
ROPE_BASE = 10000.0
N_GROUPS = 4
E_PER_GROUP = 8
N_EXPERTS = N_GROUPS * E_PER_GROUP
TOP_K = 2
D_FF_E = D_MODEL // 4
EPS = 1e-6

LANES = 128
QKV_A = 2 * H_A * DK_A + H_A * DV_A
N_SMALL = 4 * H_A
Z_GATE_A = QKV_A
Z_QB = Z_GATE_A + H_A * DV_A
Z_KB = Z_QB + H_B * 2 * DQK_B
Z_VB = Z_KB + H_B * 2 * DQK_B
Z_GM = Z_VB + H_B * DV_B
N_MAIN = Z_GM + 2 * D_MODEL

N_MOD_ROWS = 8
MOE_TILE = 256
VMEM_LIMIT = 56 * 1024 * 1024

NT_DIMS = (((1,), (1,)), ((), ()))


def _params(*sem):
    return pltpu.CompilerParams(dimension_semantics=sem, vmem_limit_bytes=VMEM_LIMIT)


def _mm(a, b):
    return jnp.dot(a.astype(BF16), b.astype(BF16), preferred_element_type=F32)


def _mm_nt(a, b):
    return lax.dot_general(a.astype(BF16), b.astype(BF16), NT_DIMS, preferred_element_type=F32)


def _silu(x):
    return x * jax.nn.sigmoid(x)


def _rms_scale(x):
    return x * lax.rsqrt(jnp.mean(x * x, axis=-1, keepdims=True) + EPS)


def _adaln_body(c_ref, w_ref, b_ref, o_ref):
    s = _silu(c_ref[...])
    o_ref[...] = _mm(s, w_ref[...]) + b_ref[...]


def _adaln(cvec, w, b):
    d, n = w.shape
    tn = 1024
    return pl.pallas_call(
        _adaln_body,
        grid=(n // tn,),
        in_specs=[pl.BlockSpec((N_MOD_ROWS, d), lambda j: (0, 0)),
                  pl.BlockSpec((d, tn), lambda j: (0, j)),
                  pl.BlockSpec((1, tn), lambda j: (0, j))],
        out_specs=pl.BlockSpec((N_MOD_ROWS, tn), lambda j: (0, j)),
        out_shape=jax.ShapeDtypeStruct((N_MOD_ROWS, n), F32),
        compiler_params=_params("arbitrary"),
        name="adaln",
    )(cvec, w, b)


class _Rows:
    def __init__(self, n_ctx, n_lat, t_lat, tm):
        assert n_ctx % tm == 0 and n_lat % tm == 0 and t_lat % tm == 0
        self.tm = tm
        self.nct = n_ctx // tm
        self.nlt = n_lat // tm
        self.per_seq = t_lat // tm
        self.n = self.nct + self.nlt

    def ctx_idx(self, i):
        return jnp.minimum(i, self.nct - 1)

    def lat_idx(self, i):
        return jnp.maximum(i - self.nct, 0)

    def mod_idx(self, i):
        return jnp.where(i < self.nct, 0, 1 + (i - self.nct) // self.per_seq)


SMALL_COL0 = QKV_A + H_A * DV_A
GATE_KINDS = 4
GATE_BETA, GATE_GC, GATE_EGC, GATE_EKD = (kind * 2 * H_A for kind in range(GATE_KINDS))


def _gate_rows(zs, a_log_row, dt_row):
    lane = lax.broadcasted_iota(jnp.int32, (1, LANES), 1)
    x = zs + dt_row
    softplus = jnp.maximum(x, 0.0) + jnp.log(1.0 + jnp.exp(-jnp.abs(x)))
    in_g = jnp.logical_and(lane >= GATE_GC, lane < GATE_EGC)
    g = jnp.where(in_g, -jnp.exp(a_log_row) * softplus, 0.0)
    beta = jax.nn.sigmoid(zs)
    reversed_lane = lane >= GATE_GC + H_A
    ri = lax.broadcasted_iota(jnp.int32, (CHUNK, CHUNK), 0)
    ci = lax.broadcasted_iota(jnp.int32, (CHUNK, CHUNK), 1)
    prefix = jnp.where(ci <= ri, 1.0, 0.0)
    suffix = jnp.where(ci >= ri, 1.0, 0.0)
    exact = dict(preferred_element_type=F32, precision=lax.Precision.HIGHEST)
    rows = []
    for c in range(zs.shape[0] // CHUNK):
        r = slice(c * CHUNK, (c + 1) * CHUNK)
        gch = g[r]
        gc = jnp.where(reversed_lane, jnp.dot(suffix, gch, **exact), jnp.dot(prefix, gch, **exact))
        tot = jnp.sum(gch, axis=0, keepdims=True)
        rows.append(jnp.where(lane < GATE_GC, beta[r],
                              jnp.where(lane < GATE_EGC, gc,
                                        jnp.where(lane < GATE_EKD, pltpu.roll(jnp.exp(gc), GATE_EGC - GATE_GC, 1),
                                                  jnp.where(lane < GATE_EKD + 2 * H_A,
                                                            pltpu.roll(jnp.exp(tot - gc), GATE_EKD - GATE_GC, 1),
                                                            0.0)))))
    return jnp.concatenate(rows, axis=0)


def _prenorm_body(nct, xp_ref, xs_ref, g_ref, sh_ref, sc_ref, ws_ref, al_ref, dt_ref, h_ref, gates_ref):
    i = pl.program_id(0)

    def run(x_ref):
        h = _rms_scale(x_ref[...]) * g_ref[...]
        h = (h * (1.0 + sc_ref[0]) + sh_ref[0]).astype(BF16)
        h_ref[...] = h
        zs = lax.dot_general(h, ws_ref[...].astype(BF16), NT_DIMS, preferred_element_type=F32)
        gates_ref[...] = _gate_rows(zs, al_ref[...], dt_ref[...])

    @pl.when(i < nct)
    def _():
        run(xp_ref)

    @pl.when(i >= nct)
    def _():
        run(xs_ref)


def _prenorm(rows, xp, xs, gain, sh, sc, w_in_t, a_log, dt_bias):
    place = lambda v: jnp.pad(v.reshape(1, 2 * H_A), ((0, 0), (GATE_GC, LANES - GATE_EGC)))
    d = xp.shape[1]
    tm = rows.tm
    n_tok = rows.n * tm
    return pl.pallas_call(
        functools.partial(_prenorm_body, rows.nct),
        grid=(rows.n,),
        in_specs=[pl.BlockSpec((tm, d), lambda i: (rows.ctx_idx(i), 0)),
                  pl.BlockSpec((tm, d), lambda i: (rows.lat_idx(i), 0)),
                  pl.BlockSpec((1, d), lambda i: (0, 0)),
                  pl.BlockSpec((1, 1, d), lambda i: (rows.mod_idx(i), 0, 0)),
                  pl.BlockSpec((1, 1, d), lambda i: (rows.mod_idx(i), 0, 0)),
                  pl.BlockSpec((LANES, d), lambda i: (SMALL_COL0 // LANES, 0)),
                  pl.BlockSpec((1, LANES), lambda i: (0, 0)),
                  pl.BlockSpec((1, LANES), lambda i: (0, 0))],
        out_specs=[pl.BlockSpec((tm, d), lambda i: (i, 0)),
                   pl.BlockSpec((tm, LANES), lambda i: (i, 0))],
        out_shape=[jax.ShapeDtypeStruct((n_tok, d), BF16),
                   jax.ShapeDtypeStruct((n_tok, LANES), F32)],
        compiler_params=_params("arbitrary"),
        name="prenorm",
    )(xp, xs, gain, sh, sc, w_in_t, place(a_log), place(dt_bias))


INPROJ_TN = 1024
N_ALIGNED = SMALL_COL0 // INPROJ_TN


def _inproj_body(h_ref, wt_ref, z_ref, w_scr):
    @pl.when(pl.program_id(1) == 0)
    def _():
        w_scr[...] = wt_ref[...].astype(BF16)

    z_ref[...] = lax.dot_general(h_ref[...], w_scr[...], NT_DIMS, preferred_element_type=F32).astype(BF16)


def _inproj(h, w_in_t):
    n_tok, d = h.shape
    tm, tn = 2048, INPROJ_TN
    first_row = lambda j: pl.multiple_of(j * tn + N_SMALL * jnp.minimum(j // N_ALIGNED, 1), N_SMALL)
    return pl.pallas_call(
        _inproj_body,
        grid=(N_MAIN // tn, n_tok // tm),
        in_specs=[pl.BlockSpec((tm, d), lambda j, i: (i, 0)),
                  pl.BlockSpec((pl.Element(tn), pl.Element(d)), lambda j, i: (first_row(j), 0))],
        out_specs=pl.BlockSpec((tm, tn), lambda j, i: (i, j)),
        out_shape=jax.ShapeDtypeStruct((n_tok, N_MAIN), BF16),
        scratch_shapes=[pltpu.VMEM((tn, d), BF16)],
        compiler_params=_params("arbitrary", "arbitrary"),
        name="inproj",
    )(h, w_in_t)


INV_BLOCK = 16


def _chunk_solve(chains, ii, jj):
    eye = ii == jj
    blk = (ii // INV_BLOCK) == (jj // INV_BLOCK)
    blk2 = (ii // (2 * INV_BLOCK)) == (jj // (2 * INV_BLOCK))
    cols, a_qk, a0, off, d0, rhs = [], [], [], [], [], []
    for kk, qk, v, k, gates, reverse, lane in chains:
        beta, gc, egc, ekd = (gates[:, b + lane:b + lane + 1] for b in (GATE_BETA, GATE_GC, GATE_EGC, GATE_EKD))
        incl = (jj >= ii) if reverse else (jj <= ii)
        strict = (jj > ii) if reverse else (jj < ii)
        gc_row = jnp.sum(jnp.where(eye, gc, 0.0), axis=0, keepdims=True)
        dec = jnp.where(incl, jnp.exp(jnp.where(incl, gc - gc_row, 0.0)), 0.0)
        a = jnp.where(strict, kk * dec, 0.0)
        cols.append((egc, ekd))
        a_qk.append(qk * dec)
        a0.append(jnp.where(blk, a, 0.0))
        off.append(jnp.where(blk, 0.0, a))
        d0.append(jnp.where(eye, 1.0, 0.0) - a0[-1])
        rhs.append(jnp.concatenate([v * beta, k * (beta * egc)], axis=1))
    p = [_mm(x, x) for x in a0]
    for _ in range(int(math.log2(INV_BLOCK)) - 2):
        r = [_mm(jnp.concatenate([pi, di], axis=0), pi) for pi, di in zip(p, d0)]
        p = [ri[:CHUNK] for ri in r]
        d0 = [di + ri[CHUNK:] for di, ri in zip(d0, r)]
    d0 = [di + _mm(di, pi) for di, pi in zip(d0, p)]
    wl = [_mm(di, oi) for di, oi in zip(d0, off)]
    yr = [_mm(di, ri) for di, ri in zip(d0, rhs)]
    x1 = [jnp.where(blk2, x, 0.0) for x in wl]
    yl = [jnp.where(blk2, 0.0, x) for x in wl]
    zl = [y - _mm(x, y) for x, y in zip(x1, yl)]
    zr = [y - _mm(x, y) for x, y in zip(x1, yr)]
    sol = [y - _mm(x, y) for x, y in zip(zl, zr)]
    return [(so[:, :DV_A], so[:, DV_A:], aq, egc, ekd) for so, aq, (egc, ekd) in zip(sol, a_qk, cols)]


def _delta_body(t_len, hp, has_s0, q_ref, k_ref, v_ref, cq_ref, ck_ref, cv_ref, g_ref, *rest):
    if has_s0:
        s0_ref, o_ref, qs, ks, vs, g_scr, u_scr, wq_scr, ak_scr, s_scr = rest
    else:
        o_ref, sfin_ref, qs, ks, vs, g_scr, u_scr, wq_scr, ak_scr, s_scr = rest
    n = t_len // CHUNK
    tpos = lax.broadcasted_iota(jnp.int32, (t_len, 1), 0)

    def conv_act(x, w):
        x_prev = jnp.where(tpos == 0, 0.0, pltpu.roll(x, 1, 0))
        x_next = jnp.where(tpos == t_len - 1, 0.0, pltpu.roll(x, t_len - 1, 0))
        return _silu(x_prev * w[0:1] + x * w[1:2] + x_next * w[2:3])

    def l2n(x):
        return x * lax.rsqrt(jnp.sum(x * x, axis=-1, keepdims=True) + EPS)

    for hh in range(hp):
        cols = slice(hh * LANES, (hh + 1) * LANES)
        qs[:, cols] = l2n(conv_act(q_ref[:, cols].astype(F32), cq_ref[:, cols])) * (DK_A ** -0.5)
        ks[:, cols] = l2n(conv_act(k_ref[:, cols].astype(F32), ck_ref[:, cols]))
        vs[:, cols] = conv_act(v_ref[:, cols].astype(F32), cv_ref[:, cols])
    if hp == H_A:
        g_scr[...] = g_ref[...]
    else:
        g_scr[...] = pltpu.roll(g_ref[...], (LANES - pl.program_id(1) * hp) % LANES, 1)
    if has_s0:
        s_scr[...] = s0_ref[...]
    else:
        s_scr[...] = jnp.zeros_like(s_scr)
    o_ref[...] = jnp.zeros_like(o_ref)
    ii = lax.broadcasted_iota(jnp.int32, (CHUNK, CHUNK), 0)
    jj = lax.broadcasted_iota(jnp.int32, (CHUNK, CHUNK), 1)

    def prep(c, carry):
        r = pl.ds(pl.multiple_of(c * CHUNK, CHUNK), CHUNK)
        g = g_scr[r, :]
        heads = []
        for hh in range(hp):
            cols = slice(hh * LANES, (hh + 1) * LANES)
            heads.append((qs[r, cols], ks[r, cols], vs[r, cols]))
        beta = lambda d, hh: g[:, GATE_BETA + d * H_A + hh:GATE_BETA + d * H_A + hh + 1]
        rr = [_mm_nt(jnp.concatenate([k * beta(0, hh), k * beta(1, hh), q], axis=0), k)
              for hh, (q, k, v) in enumerate(heads)]
        chains = [(rr[hh][d * CHUNK:(d + 1) * CHUNK], rr[hh][2 * CHUNK:], heads[hh][2], heads[hh][1], g,
                   d == 1, d * H_A + hh) for hh in range(hp) for d in (0, 1)]
        solved = _chunk_solve(chains, ii, jj)
        for hh in range(hp):
            cols = slice(hh * LANES, (hh + 1) * LANES)
            q, k = heads[hh][0], heads[hh][1]
            for d in (0, 1):
                u, w, a_qk, egc, ekd = solved[hh * 2 + d]
                slot = (d * hp + hh) * n + c
                u_scr[d, r, cols] = u
                wq_scr[slot] = jnp.concatenate([w, q * egc], axis=0).astype(BF16)
                ak_scr[slot] = jnp.concatenate([a_qk, (k * ekd).T], axis=0).astype(BF16)
        return carry

    lax.fori_loop(0, n, prep, 0, unroll=4)

    def scan(c, carry):
        chains = []
        for hh in range(hp):
            for d in (0, 1):
                cc = c if d == 0 else n - 1 - c
                chains.append((hh, d, pl.multiple_of(cc * CHUNK, CHUNK), (d * hp + hh) * n + cc))
        s = [s_scr[d, hh] for hh, d, r0, slot in chains]
        r1 = [jnp.dot(wq_scr[slot], si.astype(BF16), preferred_element_type=F32)
              for si, (hh, d, r0, slot) in zip(s, chains)]
        v_new = [u_scr[d, pl.ds(r0, CHUNK), hh * LANES:(hh + 1) * LANES] - ri[:CHUNK]
                 for ri, (hh, d, r0, slot) in zip(r1, chains)]
        r2 = [jnp.dot(ak_scr[slot], vi.astype(BF16), preferred_element_type=F32)
              for vi, (hh, d, r0, slot) in zip(v_new, chains)]
        for si, r1i, r2i, (hh, d, r0, slot) in zip(s, r1, r2, chains):
            o_ref[pl.ds(r0, CHUNK), hh * LANES:(hh + 1) * LANES] += r1i[CHUNK:] + r2i[:CHUNK]
            lane = GATE_EGC + d * H_A + hh
            edge = g_scr[pl.ds(pl.multiple_of(r0 + (0 if d else CHUNK - 8), 8), 8), lane:lane + 1]
            s_scr[d, hh] = si * (edge[0:1] if d else edge[7:8]) + r2i[CHUNK:]
        return carry

    lax.fori_loop(0, n, scan, 0, unroll=2)
    if not has_s0:
        sfin_ref[...] = s_scr[...]


def _deltanet(z, conv_w, gates, s0, n_seq, t_len, row0, hp):
    rb = row0 // t_len
    w = hp * LANES
    n = t_len // CHUNK
    seq_blk = lambda col0: pl.BlockSpec((t_len, w), lambda b, h: (rb + b, col0 // hp + h))
    cw_blk = lambda col0: pl.BlockSpec((CONV_K, w), lambda b, h: (0, col0 // hp + h))
    state_blk = pl.BlockSpec((None, None, 2, hp, DK_A, DV_A), lambda b, h: (b, 0, 0, h, 0, 0))
    in_specs = [seq_blk(0), seq_blk(H_A), seq_blk(2 * H_A), cw_blk(0), cw_blk(H_A), cw_blk(2 * H_A),
                pl.BlockSpec((t_len, LANES), lambda b, h: (rb + b, 0))]
    args = [z, z, z, conv_w, conv_w, conv_w, gates]
    o_spec = pl.BlockSpec((t_len, w), lambda b, h: (b, h))
    o_shape = jax.ShapeDtypeStruct((n_seq * t_len, H_A * DV_A), F32)
    has_s0 = s0 is not None
    if has_s0:
        in_specs += [state_blk]
        args += [s0]
        out_specs, out_shape = o_spec, o_shape
    else:
        out_specs = [o_spec, state_blk]
        out_shape = [o_shape, jax.ShapeDtypeStruct((n_seq, 1, 2, H_A, DK_A, DV_A), F32)]

    return pl.pallas_call(
        functools.partial(_delta_body, t_len, hp, has_s0),
        grid=(n_seq, H_A // hp),
        in_specs=in_specs,
        out_specs=out_specs,
        out_shape=out_shape,
        scratch_shapes=[pltpu.VMEM((t_len, w), F32)] * 3
        + [pltpu.VMEM((t_len, LANES), F32),
           pltpu.VMEM((2, t_len, w), F32),
           pltpu.VMEM((2 * hp * n, 2 * CHUNK, DV_A), BF16),
           pltpu.VMEM((2 * hp * n, CHUNK + DK_A, CHUNK), BF16),
           pltpu.VMEM((2, hp, DK_A, DV_A), F32)],
        compiler_params=_params("arbitrary", "arbitrary"),
        name="deltanet_lat" if has_s0 else "deltanet_ctx",
    )(*args)


def _subln(o, sub_ref, lam_init):
    return _rms_scale(o) * sub_ref[...] * (1.0 - lam_init)


def _attn_ctx_body(lam_init, q_ref, k_ref, v_ref, lam_ref, sub_ref, o_ref, ck_ref, cv_ref):
    heads = [slice(h * LANES, (h + 1) * LANES) for h in range(H_B)]
    lam = lam_ref[0:1, 0:1]
    ks = [k_ref[:, c] for c in heads]
    vs = [v_ref[:, c] for c in heads]
    for h in range(H_B):
        ck_ref[h] = ks[h].astype(F32)
        cv_ref[h] = vs[h].astype(F32)
    qb = [q_ref[:, c] * (DQK_B ** -0.5) for c in heads]
    kb = ks
    probs = []
    for lo in (0, DQK_B):
        s = [lax.dot_general(q[:, lo:lo + DQK_B], k[:, lo:lo + DQK_B], NT_DIMS, preferred_element_type=F32)
             for q, k in zip(qb, kb)]
        e = [jnp.exp(x - jnp.max(x, axis=-1, keepdims=True)) for x in s]
        probs.append([x * (1.0 / jnp.sum(x, axis=-1, keepdims=True)) for x in e])
    o = [jnp.dot((p1 - lam * p2).astype(BF16), v, preferred_element_type=F32)
         for p1, p2, v in zip(probs[0], probs[1], vs)]
    for c, oh in zip(heads, o):
        o_ref[:, c] = _subln(oh, sub_ref, lam_init).astype(BF16)


def _attn_ctx(z, lam, subln, n_seq, t_len, lam_init):
    n_tok = n_seq * t_len
    w = H_B * LANES
    blk = lambda col0: pl.BlockSpec((t_len, w), lambda b: (b, col0 // w))
    cache_blk = pl.BlockSpec((None, None, H_B, t_len, LANES), lambda b: (b, 0, 0, 0, 0))
    cache_shape = jax.ShapeDtypeStruct((n_seq, 1, H_B, t_len, LANES), F32)
    return pl.pallas_call(
        functools.partial(_attn_ctx_body, lam_init),
        grid=(n_seq,),
        in_specs=[blk(Z_QB), blk(Z_KB), blk(Z_VB),
                  pl.BlockSpec((1, LANES), lambda b: (0, 0)),
                  pl.BlockSpec((1, LANES), lambda b: (0, 0))],
        out_specs=[pl.BlockSpec((t_len, w), lambda b: (b, 0)), cache_blk, cache_blk],
        out_shape=[jax.ShapeDtypeStruct((n_tok, H_B * DV_B), BF16), cache_shape, cache_shape],
        compiler_params=_params("arbitrary"),
        name="attn_ctx",
    )(z, z, z, lam, subln)


def _rope(x, cos, sin_signed):
    lane = lax.broadcasted_iota(jnp.int32, (1, LANES), 1)
    first = (lane % 32) < 16
    partner = jnp.where(first, pltpu.roll(x, LANES - 16, 1), pltpu.roll(x, 16, 1))
    return x * cos + partner * sin_signed


LAT_HEADS = 2


def _attn_lat_body(lam_init, n_past, q_ref, k_ref, v_ref, pk_ref, pv_ref, cosq_ref, sinq_ref, cos_ref, sin_ref,
                   lam_ref, sub_ref, o_ref, keys, vals):
    heads = [slice(h * LANES, (h + 1) * LANES) for h in range(LAT_HEADS)]

    @pl.when(pl.program_id(2) == 0)
    def _():
        for h, c in enumerate(heads):
            keys[h, 0:n_past, :] = pk_ref[h].astype(BF16)
            vals[h, 0:n_past, :] = pv_ref[h].astype(BF16)
            keys[h, n_past:, :] = _rope(k_ref[:, c].astype(F32), cos_ref[...], sin_ref[...]).astype(BF16)
            vals[h, n_past:, :] = v_ref[:, c]

    lam = lam_ref[0:1, 0:1]
    qb = [(_rope(q_ref[:, c].astype(F32), cosq_ref[...], sinq_ref[...]) * (DQK_B ** -0.5)).astype(BF16)
          for c in heads]
    maps = []
    for lo in (0, DQK_B):
        s = [lax.dot_general(q[:, lo:lo + DQK_B], keys[h, :, lo:lo + DQK_B], NT_DIMS, preferred_element_type=F32)
             for h, q in enumerate(qb)]
        e = [jnp.exp(x - jnp.max(x, axis=-1, keepdims=True)) for x in s]
        inv = [1.0 / jnp.sum(x, axis=-1, keepdims=True) for x in e]
        maps.append([jnp.dot(x.astype(BF16), vals[h], preferred_element_type=F32) * r
                     for h, (x, r) in enumerate(zip(e, inv))])
    for h, c in enumerate(heads):
        o_ref[:, c] = _subln(maps[0][h] - lam * maps[1][h], sub_ref, lam_init).astype(BF16)


def _attn_lat(z, cache_k, cache_v, cos, sin, lam, subln, n_seq, t_len, row0, lam_init):
    tq = 256
    nq = t_len // tq
    n_past = cache_k.shape[3]
    rbq = row0 // tq
    rbs = row0 // t_len
    w = LAT_HEADS * LANES
    seq_blk = lambda col0: pl.BlockSpec((t_len, w), lambda b, h, qi: (rbs + b, col0 // w + h))
    past_blk = pl.BlockSpec((None, None, LAT_HEADS, n_past, LANES), lambda b, h, qi: (b, 0, h, 0, 0))
    row_vec = pl.BlockSpec((1, LANES), lambda b, h, qi: (0, 0))
    return pl.pallas_call(
        functools.partial(_attn_lat_body, lam_init, n_past),
        grid=(n_seq, H_B // LAT_HEADS, nq),
        in_specs=[pl.BlockSpec((tq, w), lambda b, h, qi: (rbq + b * nq + qi, Z_QB // w + h)),
                  seq_blk(Z_KB), seq_blk(Z_VB), past_blk, past_blk,
                  pl.BlockSpec((tq, LANES), lambda b, h, qi: (qi, 0)),
                  pl.BlockSpec((tq, LANES), lambda b, h, qi: (qi, 0)),
                  pl.BlockSpec((t_len, LANES), lambda b, h, qi: (0, 0)),
                  pl.BlockSpec((t_len, LANES), lambda b, h, qi: (0, 0)),
                  row_vec, row_vec],
        out_specs=pl.BlockSpec((tq, w), lambda b, h, qi: (b * nq + qi, h)),
        out_shape=jax.ShapeDtypeStruct((n_seq * t_len, H_B * DV_B), BF16),
        scratch_shapes=[pltpu.VMEM((LAT_HEADS, n_past + t_len, LANES), BF16)] * 2,
        compiler_params=_params("arbitrary", "arbitrary", "arbitrary"),
        name="attn_lat",
    )(z, z, z, cache_k, cache_v, cos, sin, cos, sin, lam, subln)


def _rope_tables(t_len):
    t = np.arange(t_len)
    pos = np.stack([t // GRID_W, t % GRID_W], axis=1).astype(np.float32)
    nf = DQK_B // 4
    inv_freq = np.float32(ROPE_BASE) ** (-np.arange(nf, dtype=np.float32) / np.float32(nf))
    lane = np.arange(LANES)
    half = (lane % DQK_B) // (DQK_B // 2)
    ang = (pos[:, half] * inv_freq[lane % nf][None, :]).astype(np.float32)
    sign = np.where((lane % (DQK_B // 2)) < nf, -1.0, 1.0).astype(np.float32)
    return jnp.asarray(np.cos(ang), F32), jnp.asarray(np.sin(ang) * sign[None, :], F32)


def _merge_body(nct, oac_ref, oal_ref, ga_ref, on_ref, obc_ref, obl_ref, wa_f32, wb_f32, *rest):
    gm_refs, m_ref, wa_ref, wb_ref = rest[:-3], rest[-3], rest[-2], rest[-1]
    i = pl.program_id(0)
    n_blk = len(gm_refs) // 2

    @pl.when(i == 0)
    def _():
        wa_ref[...] = wa_f32[...].astype(BF16)
        wb_ref[...] = wb_f32[...].astype(BF16)

    def run(oa_ref, ob_ref):
        a = jnp.concatenate(
            [(_rms_scale(oa_ref[:, c]) * on_ref[...] * _silu(ga_ref[:, c].astype(F32))).astype(BF16)
             for c in (slice(h * DV_A, (h + 1) * DV_A) for h in range(H_A))], axis=1)
        ya = jnp.dot(a, wa_ref[...], preferred_element_type=F32)
        yb = jnp.dot(ob_ref[...], wb_ref[...], preferred_element_type=F32)
        tn = gm_refs[0].shape[1]
        for j in range(n_blk):
            c = slice(j * tn, (j + 1) * tn)
            m_ref[:, c] = (jax.nn.sigmoid(gm_refs[j][...].astype(F32)) * ya[:, c]
                           + jax.nn.sigmoid(gm_refs[n_blk + j][...].astype(F32)) * yb[:, c]).astype(BF16)

    @pl.when(i < nct)
    def _():
        run(oac_ref, obc_ref)

    @pl.when(i >= nct)
    def _():
        run(oal_ref, obl_ref)


def _merge(rows, z, oa_c, oa_l, ob_c, ob_l, onorm, w_up_a, w_up_b):
    n_tok = z.shape[0]
    tm = rows.tm
    ka, d = w_up_a.shape
    tn = 1024
    nj = d // tn
    ctx_blk = pl.BlockSpec((tm, ka), lambda i: (rows.ctx_idx(i), 0))
    lat_blk = pl.BlockSpec((tm, ka), lambda i: (rows.lat_idx(i), 0))
    weight = pl.BlockSpec((ka, d), lambda i: (0, 0), pipeline_mode=pl.Buffered(1))
    gate_cols = [pl.BlockSpec((tm, tn), functools.partial(lambda i, c: (i, c), c=Z_GM // tn + j)) for j in range(2 * nj)]
    return pl.pallas_call(
        functools.partial(_merge_body, rows.nct),
        grid=(n_tok // tm,),
        in_specs=[ctx_blk, lat_blk,
                  pl.BlockSpec((tm, ka), lambda i: (i, Z_GATE_A // ka)),
                  pl.BlockSpec((1, DV_A), lambda i: (0, 0)),
                  ctx_blk, lat_blk, weight, weight] + gate_cols,
        out_specs=pl.BlockSpec((tm, d), lambda i: (i, 0)),
        out_shape=jax.ShapeDtypeStruct((n_tok, d), BF16),
        scratch_shapes=[pltpu.VMEM((ka, d), BF16)] * 2,
        compiler_params=_params("arbitrary"),
        name="merge",
    )(oa_c, oa_l, z, onorm, ob_c, ob_l, w_up_a, w_up_b, *([z] * (2 * nj)))


ROUTE_E = 0
ROUTE_W = TOP_K


def _route_rows(lg):
    lane = lax.broadcasted_iota(jnp.int32, lg.shape, 1)
    neg = -jnp.inf

    def first_max(x):
        m = jnp.max(x, axis=1, keepdims=True)
        return m, jnp.min(jnp.where(x == m, lane, LANES), axis=1, keepdims=True)

    gl = jnp.where(lane < N_GROUPS, lg, neg)
    gmax, g_idx = first_max(gl)
    pg_top = 1.0 / jnp.sum(jnp.exp(gl - gmax), axis=1, keepdims=True)
    lo = N_GROUPS + E_PER_GROUP * g_idx
    el = jnp.where(jnp.logical_and(lane >= lo, lane < lo + E_PER_GROUP), lg, neg)
    emax, i1 = first_max(el)
    esum = jnp.sum(jnp.exp(el - emax), axis=1, keepdims=True)
    e2max, i2 = first_max(jnp.where(lane == i1, neg, el))
    p1 = 1.0 / esum
    p2 = jnp.exp(e2max - emax) / esum
    den = p1 + p2
    vals = [(i1 - N_GROUPS).astype(F32), (i2 - N_GROUPS).astype(F32), pg_top * p1 / den, pg_top * p2 / den]
    out = jnp.zeros(lg.shape, F32)
    for pos, val in enumerate(vals):
        out = jnp.where(lane == pos, val, out)
    return out


def _outproj_body(nct, m_ref, wo_ref, xp_ref, xs_ref, gt_ref, g2_ref, sh_ref, sc_ref, wr_ref, br_ref,
                  x1_ref, h2_ref, rt_ref):
    i = pl.program_id(0)
    half = m_ref.shape[0] // 2

    def finish(x_ref):
        for r in (slice(0, half), slice(half, 2 * half)):
            y = jnp.dot(m_ref[r, :], wo_ref[...], preferred_element_type=F32)
            x1 = x_ref[r, :] + gt_ref[0] * y
            x1_ref[r, :] = x1
            h2 = _rms_scale(x1) * g2_ref[...]
            h2 = h2 * (1.0 + sc_ref[0]) + sh_ref[0]
            h2_ref[r, :] = h2
            hi = h2.astype(BF16)
            lo = (h2 - hi.astype(F32)).astype(BF16)
            p_hi = jnp.dot(hi, wr_ref[...], preferred_element_type=F32)
            p_lo = jnp.dot(lo, wr_ref[...], preferred_element_type=F32)
            lg = p_hi[:, :LANES] + p_hi[:, LANES:] + p_lo[:, :LANES] + p_lo[:, LANES:] + br_ref[...]
            rt_ref[r, :] = _route_rows(lg)

    @pl.when(i < nct)
    def _():
        finish(xp_ref)

    @pl.when(i >= nct)
    def _():
        finish(xs_ref)


def _outproj(rows, mixed, w_o, xp, xs, gt1, gain2, sh2, sc2, w_r, b_r):
    d = xp.shape[1]
    tm = rows.tm
    n_tok = rows.n * tm
    mod = lambda: pl.BlockSpec((1, 1, d), lambda i: (rows.mod_idx(i), 0, 0))
    tok = pl.BlockSpec((tm, d), lambda i: (i, 0))
    return pl.pallas_call(
        functools.partial(_outproj_body, rows.nct),
        grid=(rows.n,),
        in_specs=[tok,
                  pl.BlockSpec((d, d), lambda i: (0, 0)),
                  pl.BlockSpec((tm, d), lambda i: (rows.ctx_idx(i), 0)),
                  pl.BlockSpec((tm, d), lambda i: (rows.lat_idx(i), 0)),
                  mod(),
                  pl.BlockSpec((1, d), lambda i: (0, 0)),
                  mod(), mod(),
                  pl.BlockSpec((d, 2 * LANES), lambda i: (0, 0)),
                  pl.BlockSpec((1, LANES), lambda i: (0, 0))],
        out_specs=[tok, tok, pl.BlockSpec((tm, LANES), lambda i: (i, 0))],
        out_shape=[jax.ShapeDtypeStruct((n_tok, d), F32), jax.ShapeDtypeStruct((n_tok, d), F32),
                   jax.ShapeDtypeStruct((n_tok, LANES), F32)],
        compiler_params=_params("arbitrary"),
        name="outproj",
    )(mixed, w_o, xp, xs, gt1, gain2, sh2, sc2, w_r, b_r)


DISPATCH_BLOCK = 256
TBL_EXPERT, TBL_VALID, TBL_COUNT, TBL_OFFSET = 0, 1, 2, 3


def _dispatch_body(n_tok, route_ref, pos_ref, tbl_ref):
    nb = n_tok // DISPATCH_BLOCK
    lane = lax.broadcasted_iota(jnp.int32, (1, LANES), 1)
    lane_f = lane.astype(F32)

    def one_hot(b, k):
        r = pl.ds(pl.multiple_of(b * DISPATCH_BLOCK, DISPATCH_BLOCK), DISPATCH_BLOCK)
        return jnp.where(route_ref[r, ROUTE_E + k:ROUTE_E + k + 1] == lane_f, 1.0, 0.0)

    def count(b, acc):
        return acc + jnp.sum(one_hot(b, 0) + one_hot(b, 1), axis=0, keepdims=True)

    counts = lax.fori_loop(0, nb, count, jnp.zeros((1, LANES), F32))
    padded = jnp.floor((counts + (MOE_TILE - 1)) * (1.0 / MOE_TILE)) * MOE_TILE
    pad_end = padded
    shift = 1
    while shift < LANES:
        pad_end = pad_end + jnp.where(lane >= shift, pltpu.roll(pad_end, shift, 1), 0.0)
        shift *= 2
    pad_off = pad_end - padded

    ri = lax.broadcasted_iota(jnp.int32, (DISPATCH_BLOCK, DISPATCH_BLOCK), 0)
    ci = lax.broadcasted_iota(jnp.int32, (DISPATCH_BLOCK, DISPATCH_BLOCK), 1)
    before = jnp.where(ci < ri, 1.0, 0.0).astype(BF16)

    def place(b, run):
        oh = [one_hot(b, k) for k in range(TOP_K)]
        base = pad_off + run
        out = jnp.zeros((DISPATCH_BLOCK, LANES), F32)
        lane_b = lax.broadcasted_iota(jnp.int32, (DISPATCH_BLOCK, LANES), 1)
        for k in range(TOP_K):
            prior = jnp.dot(before, oh[k].astype(BF16), preferred_element_type=F32)
            pos = jnp.sum(oh[k] * (base + prior), axis=1, keepdims=True)
            out = jnp.where(lane_b == k, pos, out)
            base = base + jnp.sum(oh[k], axis=0, keepdims=True)
        r = pl.ds(pl.multiple_of(b * DISPATCH_BLOCK, DISPATCH_BLOCK), DISPATCH_BLOCK)
        pos_ref[r, :] = out.astype(jnp.int32)
        return base - pad_off

    lax.fori_loop(0, nb, place, jnp.zeros((1, LANES), F32))

    end_col = jnp.transpose(jnp.broadcast_to(pad_end, (8, LANES)))[:, 0:1]
    e_col = lax.broadcasted_iota(jnp.int32, (LANES, 1), 0)
    tile_start = lane_f * MOE_TILE
    passed = jnp.where(jnp.logical_and(end_col <= tile_start, e_col < N_EXPERTS), 1.0, 0.0)
    tile_expert = jnp.minimum(jnp.sum(passed, axis=0, keepdims=True), N_EXPERTS - 1.0)
    total = jnp.sum(jnp.where(lane == N_EXPERTS - 1, pad_end, 0.0), axis=1, keepdims=True)
    tile_valid = jnp.where(tile_start < total, 1.0, 0.0)
    row = lax.broadcasted_iota(jnp.int32, (8, LANES), 0)
    tbl = jnp.zeros((8, LANES), F32)
    for idx, val in ((TBL_EXPERT, tile_expert), (TBL_VALID, tile_valid), (TBL_COUNT, counts), (TBL_OFFSET, pad_off)):
        tbl = jnp.where(row == idx, val, tbl)
    tbl_ref[...] = tbl.astype(jnp.int32)


def _dispatch(route):
    n_tok = route.shape[0]
    return pl.pallas_call(
        functools.partial(_dispatch_body, n_tok),
        out_shape=[jax.ShapeDtypeStruct((n_tok, LANES), jnp.int32), jax.ShapeDtypeStruct((8, LANES), jnp.int32)],
        compiler_params=pltpu.CompilerParams(vmem_limit_bytes=VMEM_LIMIT),
        name="dispatch",
    )(route)


def _scatter_body(tm, n_tiles, pos_ref, cnt_ref, off_ref, h_ref, o_hbm, zbuf, sem):
    i = pl.program_id(0)

    @pl.when(i == 0)
    def _():
        zbuf[...] = jnp.zeros_like(zbuf)
        last = N_EXPERTS - 1
        first_empty = (off_ref[last] + cnt_ref[last] + MOE_TILE - 1) // MOE_TILE

        def tile_copy(t):
            rows = pl.ds(pl.multiple_of(t * MOE_TILE, MOE_TILE), MOE_TILE)
            return pltpu.make_async_copy(zbuf, o_hbm.at[rows, :], sem.at[1])

        def partial_tile(e, start):
            cnt = cnt_ref[e]

            @pl.when(cnt % MOE_TILE != 0)
            def _():
                cp = tile_copy((off_ref[e] + cnt) // MOE_TILE)
                if start:
                    cp.start()
                else:
                    cp.wait()

        def zero_partial(e, carry):
            partial_tile(e, True)
            return carry

        def partial_done(e, carry):
            partial_tile(e, False)
            return carry

        def zero_tile(t, carry):
            tile_copy(t).start()
            return carry

        def zero_done(t, carry):
            tile_copy(t).wait()
            return carry

        lax.fori_loop(0, N_EXPERTS, zero_partial, 0)
        lax.fori_loop(first_empty, n_tiles, zero_tile, 0)
        lax.fori_loop(0, N_EXPERTS, partial_done, 0)
        lax.fori_loop(first_empty, n_tiles, zero_done, 0)

    def row_copies(r):
        return [pltpu.make_async_copy(h_ref.at[pl.ds(r, 1), :],
                                      o_hbm.at[pl.ds(pos_ref[(i * tm + r) * TOP_K + k], 1), :], sem.at[0])
                for k in range(TOP_K)]

    def start(r, carry):
        for k, cp in enumerate(row_copies(r)):
            cp.start(priority=k)
        return carry

    def wait(r, carry):
        for cp in row_copies(r):
            cp.wait()
        return carry

    lax.fori_loop(0, tm, start, 0, unroll=8)
    lax.fori_loop(0, tm, wait, 0, unroll=8)


def _scatter_rows(h2, pos_flat, counts, offsets, n_rows):
    n_tok, d = h2.shape
    tm = 512
    grid_spec = pltpu.PrefetchScalarGridSpec(
        num_scalar_prefetch=3,
        grid=(n_tok // tm,),
        in_specs=[pl.BlockSpec((tm, d), lambda i, pos, cnt, off: (i, 0))],
        out_specs=pl.BlockSpec(memory_space=pl.ANY),
        scratch_shapes=[pltpu.VMEM((MOE_TILE, d), F32), pltpu.SemaphoreType.DMA((2,))],
    )
    return pl.pallas_call(
        functools.partial(_scatter_body, tm, n_rows // MOE_TILE),
        grid_spec=grid_spec,
        out_shape=jax.ShapeDtypeStruct((n_rows, d), F32),
        compiler_params=_params("arbitrary"),
        name="scatter_rows",
    )(pos_flat, counts, offsets, h2)


def _moe_body(n_tiles, te_ref, tv_ref, cnt_ref, x_ref, wg_hbm, wu_hbm, wd_hbm, y_ref,
              wg_f, wu_f, wd_f, wg_b, wu_b, wd_b, slot_ref, sem):
    i = pl.program_id(0)
    prev = jnp.maximum(i - 1, 0)
    valid = tv_ref[i] == 1
    e = te_ref[i]

    def fetch(expert, slot):
        return [pltpu.make_async_copy(src.at[expert], dst.at[slot], sem.at[slot])
                for src, dst in ((wg_hbm, wg_f), (wu_hbm, wu_f), (wd_hbm, wd_f))]

    @pl.when(jnp.logical_and(valid, i == 0))
    def _():
        slot_ref[0] = 1
        for cp in fetch(e, 0):
            cp.start()

    @pl.when(jnp.logical_and(valid, jnp.logical_or(i == 0, e != te_ref[prev])))
    def _():
        slot = 1 - slot_ref[0]
        slot_ref[0] = slot
        for cp in fetch(e, slot):
            cp.wait()
        nxt = jnp.minimum(i + (cnt_ref[e] + MOE_TILE - 1) // MOE_TILE, n_tiles - 1)

        @pl.when(jnp.logical_and(nxt > i, tv_ref[nxt] == 1))
        def _():
            for cp in fetch(te_ref[nxt], 1 - slot):
                cp.start()

        wg_b[...] = wg_f[slot].astype(BF16)
        wu_b[...] = wu_f[slot].astype(BF16)
        wd_b[...] = wd_f[slot].astype(BF16)

    @pl.when(valid)
    def _():
        x = x_ref[...].astype(BF16)
        g = jnp.dot(x, wg_b[...], preferred_element_type=F32)
        u = jnp.dot(x, wu_b[...], preferred_element_type=F32)
        y_ref[...] = jnp.dot((_silu(g) * u).astype(BF16), wd_b[...], preferred_element_type=F32)

    @pl.when(tv_ref[i] == 0)
    def _():
        y_ref[...] = jnp.zeros_like(y_ref)


def _moe(x_sorted, tile_expert, tile_valid, counts, w_g, w_u, w_d):
    n_tiles = tile_expert.shape[0]
    d = x_sorted.shape[1]
    f = w_g.shape[-1]
    hbm = pl.BlockSpec(memory_space=pl.ANY)
    grid_spec = pltpu.PrefetchScalarGridSpec(
        num_scalar_prefetch=3,
        grid=(n_tiles,),
        in_specs=[pl.BlockSpec((MOE_TILE, d), lambda i, te, tv, cnt: (i * tv[i], 0)), hbm, hbm, hbm],
        out_specs=pl.BlockSpec((MOE_TILE, d), lambda i, te, tv, cnt: (i, 0)),
        scratch_shapes=[pltpu.VMEM((2, d, f), F32), pltpu.VMEM((2, d, f), F32), pltpu.VMEM((2, f, d), F32),
                        pltpu.VMEM((d, f), BF16), pltpu.VMEM((d, f), BF16), pltpu.VMEM((f, d), BF16),
                        pltpu.SMEM((1,), jnp.int32), pltpu.SemaphoreType.DMA((2,))],
    )
    return pl.pallas_call(
        functools.partial(_moe_body, n_tiles),
        grid_spec=grid_spec,
        out_shape=jax.ShapeDtypeStruct((n_tiles * MOE_TILE, d), F32),
        compiler_params=_params("arbitrary"),
        name="moe",
    )(tile_expert, tile_valid, counts, x_sorted, w_g, w_u, w_d)


def _combine_body(tm, n_tiles, row0, pos_ref, y_hbm, rt_ref, x1_ref, gt_ref, gf_ref, o_ref, ybuf, sem):
    i = pl.program_id(0)
    slot = i % 2

    def gather(tile, dst_slot, start):
        def row(r, carry):
            for k in range(TOP_K):
                p = pos_ref[(row0 + tile * tm + r) * TOP_K + k]
                cp = pltpu.make_async_copy(y_hbm.at[pl.ds(p, 1), :], ybuf.at[dst_slot, k, pl.ds(r, 1), :],
                                           sem.at[dst_slot])
                if start:
                    cp.start(priority=k)
                else:
                    cp.wait()
            return carry

        lax.fori_loop(0, tm, row, 0, unroll=8)

    @pl.when(i == 0)
    def _():
        gather(0, 0, True)

    @pl.when(i + 1 < n_tiles)
    def _():
        gather(i + 1, 1 - slot, True)

    gather(i, slot, False)
    rt = rt_ref[...]
    y = rt[:, ROUTE_W:ROUTE_W + 1] * ybuf[slot, 0] + rt[:, ROUTE_W + 1:ROUTE_W + 2] * ybuf[slot, 1]
    x2 = x1_ref[...] + gt_ref[0] * y
    o_ref[...] = _rms_scale(x2) * gf_ref[...]


def _combine(y_sorted, pos_flat, route, x1, gt2, gain_f, row0, n_rows, mod_of_tile, tm):
    d = x1.shape[1]
    rb = row0 // tm
    n_tiles = n_rows // tm
    grid_spec = pltpu.PrefetchScalarGridSpec(
        num_scalar_prefetch=1,
        grid=(n_tiles,),
        in_specs=[pl.BlockSpec(memory_space=pl.ANY),
                  pl.BlockSpec((tm, LANES), lambda i, pos: (rb + i, 0)),
                  pl.BlockSpec((tm, d), lambda i, pos: (rb + i, 0)),
                  pl.BlockSpec((1, 1, d), lambda i, pos: (mod_of_tile(i), 0, 0)),
                  pl.BlockSpec((1, d), lambda i, pos: (0, 0))],
        out_specs=pl.BlockSpec((tm, d), lambda i, pos: (i, 0)),
        scratch_shapes=[pltpu.VMEM((2, TOP_K, tm, d), F32), pltpu.SemaphoreType.DMA((2,))],
    )
    return pl.pallas_call(
        functools.partial(_combine_body, tm, n_tiles, row0),
        grid_spec=grid_spec,
        out_shape=jax.ShapeDtypeStruct((n_rows, d), F32),
        compiler_params=_params("arbitrary"),
        name="combine",
    )(pos_flat, y_sorted, route, x1, gt2, gain_f)


def kernel(x_prompt, x_sample, c, state_delta, cache_k, cache_v, c_ctx, w_ada, b_ada, norm_mix, norm_ffn, w_in,
           conv_a, a_log, dt_bias, onorm_a, lam, subln_b, w_up_a, w_up_b, w_o, w_rg, b_rg, w_re, b_re,
           w_e_gate, w_e_up, w_e_down, norm_final):
    assert w_in.shape[0] == 1, "single trunk layer"
    l = 0
    lam_init = 0.8 - 0.6 * math.exp(-0.3 * l)
    bc, tc, d = x_prompt.shape
    bl, tl, _ = x_sample.shape
    n_ctx, n_lat = bc * tc, bl * tl
    xp = x_prompt.reshape(n_ctx, d)
    xs = x_sample.reshape(n_lat, d)

    cvec = jnp.zeros((N_MOD_ROWS, d), F32).at[0].set(c_ctx).at[1:1 + bl].set(c)
    mod = _adaln(cvec, w_ada[l], b_ada[l][None, :])
    sh1, sc1, gt1, sh2, sc2, gt2 = [m.reshape(N_MOD_ROWS, 1, d) for m in jnp.split(mod, 6, axis=1)]

    rows = _Rows(n_ctx, n_lat, tl, 512)
    w_in_t = jnp.swapaxes(w_in[l], 0, 1)
    h1, gates = _prenorm(rows, xp, xs, norm_mix[l][None, :], sh1, sc1, w_in_t, a_log[l], dt_bias[l])
    z = _inproj(h1, w_in_t)

    oa_c, new_state = _deltanet(z, conv_a[l], gates, None, bc, tc, 0, 8)
    oa_l = _deltanet(z, conv_a[l], gates, state_delta, bl, tl, n_ctx, 4)

    lv = lam[l]
    lam_val = jnp.exp(jnp.sum(lv[0] * lv[1])) - jnp.exp(jnp.sum(lv[2] * lv[3])) + lam_init
    lam_row = jnp.full((1, LANES), lam_val, F32)
    sub_row = subln_b[l][None, :]
    ob_c, new_k, new_v = _attn_ctx(z, lam_row, sub_row, bc, tc, lam_init)
    cos, sin = _rope_tables(tl)
    ob_l = _attn_lat(z, cache_k, cache_v, cos, sin, lam_row, sub_row, bl, tl, n_ctx, lam_init)

    mixed = _merge(_Rows(n_ctx, n_lat, tl, 256), z, oa_c, oa_l, ob_c, ob_l, onorm_a[l][None, :],
                   w_up_a[l], w_up_b[l])
    w_r = jnp.pad(jnp.concatenate([w_rg[l], w_re[l]], axis=1), ((0, 0), (0, LANES - N_GROUPS - N_EXPERTS)))
    w_r_hi = w_r.astype(BF16)
    w_r = jnp.concatenate([w_r_hi, (w_r - w_r_hi.astype(F32)).astype(BF16)], axis=1)
    b_r = jnp.pad(jnp.concatenate([b_rg[l], b_re[l]]), (0, LANES - N_GROUPS - N_EXPERTS))[None, :]
    x1, h2, route = _outproj(rows, mixed, w_o[l].astype(BF16), xp, xs, gt1, norm_ffn[l][None, :], sh2, sc2, w_r, b_r)

    n_tok = n_ctx + n_lat
    n_tiles = (n_tok * TOP_K) // MOE_TILE + N_EXPERTS
    pos, tbl = _dispatch(route)
    pos_flat = pos[:, :TOP_K].reshape(-1)
    counts, offsets = tbl[TBL_COUNT, :N_EXPERTS], tbl[TBL_OFFSET, :N_EXPERTS]
    x_sorted = _scatter_rows(h2, pos_flat, counts, offsets, n_tiles * MOE_TILE)
    y_sorted = _moe(x_sorted, tbl[TBL_EXPERT, :n_tiles], tbl[TBL_VALID, :n_tiles], counts,
                    w_e_gate[l], w_e_up[l], w_e_down[l])
    gf = norm_final[None, :]
    tm_c = 256
    y_prompt = _combine(y_sorted, pos_flat, route, x1, gt2, gf, 0, n_ctx, lambda i: 0 * i, tm_c)
    y_sample = _combine(y_sorted, pos_flat, route, x1, gt2, gf, n_ctx, n_lat, lambda i: 1 + i // (tl // tm_c), tm_c)
    return (y_prompt.reshape(bc, tc, d), y_sample.reshape(bl, tl, d), new_state, new_k, new_v)
```

```python
import functools
import math

import jax
import jax.numpy as jnp
import numpy as np
from jax import lax
from jax.experimental import pallas as pl
from jax.experimental.pallas import tpu as pltpu

F32 = jnp.float32
BF16 = jnp.bfloat16

D_MODEL = 2048
GRID_W = 64
H_A = 8
DK_A = 128
DV_A = 128
CONV_K = 3
CHUNK = 64
H_B = 8
DQK_B = 64
DV_B = 2 * DQK_B
ROPE_BASE = 10000.0
N_GROUPS = 4
E_PER_GROUP = 8
N_EXPERTS = N_GROUPS * E_PER_GROUP
TOP_K = 2
D_FF_E = D_MODEL // 4
EPS = 1e-6

LANES = 128
QKV_A = 2 * H_A * DK_A + H_A * DV_A
N_SMALL = 4 * H_A
Z_GATE_A = QKV_A
Z_QB = Z_GATE_A + H_A * DV_A
Z_KB = Z_QB + H_B * 2 * DQK_B
Z_VB = Z_KB + H_B * 2 * DQK_B
Z_GM = Z_VB + H_B * DV_B
N_MAIN = Z_GM + 2 * D_MODEL

N_MOD_ROWS = 8
VMEM_LIMIT = 56 * 1024 * 1024
ROW_TILE = 512
INPROJ_ROWS = 2048
MERGE_TILE = 256
SCATTER_TILE = 512
COMBINE_TILE = 256
MOE_TILE = 256
ATTN_Q_TILE = 256
DELTA_HEADS_CTX = 8
DELTA_HEADS_LAT = 4

NT_DIMS = (((1,), (1,)), ((), ()))


def _params(*sem):
    return pltpu.CompilerParams(dimension_semantics=sem, vmem_limit_bytes=VMEM_LIMIT)


def _mm(a, b):
    return jnp.dot(a.astype(BF16), b.astype(BF16), preferred_element_type=F32)


def _mm_nt(a, b):
    return lax.dot_general(a.astype(BF16), b.astype(BF16), NT_DIMS, preferred_element_type=F32)


def _silu(x):
    return x * jax.nn.sigmoid(x)


def _rms_scale(x):
    return x * lax.rsqrt(jnp.mean(x * x, axis=-1, keepdims=True) + EPS)


def _adaln_body(c_ref, w_ref, b_ref, o_ref):
    s = _silu(c_ref[...])
    o_ref[...] = _mm(s, w_ref[...]) + b_ref[...]


def _adaln(cvec, w, b):
    d, n = w.shape
    tn = 1024
    return pl.pallas_call(
        _adaln_body,
        grid=(n // tn,),
        in_specs=[pl.BlockSpec((N_MOD_ROWS, d), lambda j: (0, 0)),
                  pl.BlockSpec((d, tn), lambda j: (0, j)),
                  pl.BlockSpec((1, tn), lambda j: (0, j))],
        out_specs=pl.BlockSpec((N_MOD_ROWS, tn), lambda j: (0, j)),
        out_shape=jax.ShapeDtypeStruct((N_MOD_ROWS, n), F32),
        compiler_params=_params("arbitrary"),
        name="adaln",
    )(cvec, w, b)


class _Rows:
    def __init__(self, n_ctx, n_lat, t_lat, tm):
        assert n_ctx % tm == 0 and n_lat % tm == 0 and t_lat % tm == 0
        self.tm = tm
        self.nct = n_ctx // tm
        self.nlt = n_lat // tm
        self.per_seq = t_lat // tm
        self.n = self.nct + self.nlt

    def ctx_idx(self, i):
        return jnp.minimum(i, self.nct - 1)

    def lat_idx(self, i):
        return jnp.maximum(i - self.nct, 0)

    def mod_idx(self, i):
        return jnp.where(i < self.nct, 0, 1 + (i - self.nct) // self.per_seq)


SMALL_COL0 = QKV_A + H_A * DV_A
GATE_KINDS = 4
GATE_BETA, GATE_GC, GATE_EGC, GATE_EKD = (kind * 2 * H_A for kind in range(GATE_KINDS))


def _gate_rows(zs, a_log_row, dt_row):
    lane = lax.broadcasted_iota(jnp.int32, (1, LANES), 1)
    x = zs + dt_row
    softplus = jnp.maximum(x, 0.0) + jnp.log(1.0 + jnp.exp(-jnp.abs(x)))
    in_g = jnp.logical_and(lane >= GATE_GC, lane < GATE_EGC)
    g = jnp.where(in_g, -jnp.exp(a_log_row) * softplus, 0.0)
    beta = jax.nn.sigmoid(zs)
    reversed_lane = lane >= GATE_GC + H_A
    ri = lax.broadcasted_iota(jnp.int32, (CHUNK, CHUNK), 0)
    ci = lax.broadcasted_iota(jnp.int32, (CHUNK, CHUNK), 1)
    prefix = jnp.where(ci <= ri, 1.0, 0.0)
    suffix = jnp.where(ci >= ri, 1.0, 0.0)
    exact = dict(preferred_element_type=F32, precision=lax.Precision.HIGHEST)
    rows = []
    for c in range(zs.shape[0] // CHUNK):
        r = slice(c * CHUNK, (c + 1) * CHUNK)
        gch = g[r]
        gc = jnp.where(reversed_lane, jnp.dot(suffix, gch, **exact), jnp.dot(prefix, gch, **exact))
        tot = jnp.sum(gch, axis=0, keepdims=True)
        rows.append(jnp.where(lane < GATE_GC, beta[r],
                              jnp.where(lane < GATE_EGC, gc,
                                        jnp.where(lane < GATE_EKD, pltpu.roll(jnp.exp(gc), GATE_EGC - GATE_GC, 1),
                                                  jnp.where(lane < GATE_EKD + 2 * H_A,
                                                            pltpu.roll(jnp.exp(tot - gc), GATE_EKD - GATE_GC, 1),
                                                            0.0)))))
    return jnp.concatenate(rows, axis=0)


def _prenorm_body(nct, xp_ref, xs_ref, g_ref, sh_ref, sc_ref, ws_ref, al_ref, dt_ref, h_ref, gates_ref):
    i = pl.program_id(0)

    def run(x_ref):
        h = _rms_scale(x_ref[...]) * g_ref[...]
        h = (h * (1.0 + sc_ref[0]) + sh_ref[0]).astype(BF16)
        h_ref[...] = h
        zs = lax.dot_general(h, ws_ref[...].astype(BF16), NT_DIMS, preferred_element_type=F32)
        gates_ref[...] = _gate_rows(zs, al_ref[...], dt_ref[...])

    @pl.when(i < nct)
    def _():
        run(xp_ref)

    @pl.when(i >= nct)
    def _():
        run(xs_ref)


def _prenorm(rows, xp, xs, gain, sh, sc, w_in_t, a_log, dt_bias):
    place = lambda v: jnp.pad(v.reshape(1, 2 * H_A), ((0, 0), (GATE_GC, LANES - GATE_EGC)))
    d = xp.shape[1]
    tm = rows.tm
    n_tok = rows.n * tm
    return pl.pallas_call(
        functools.partial(_prenorm_body, rows.nct),
        grid=(rows.n,),
        in_specs=[pl.BlockSpec((tm, d), lambda i: (rows.ctx_idx(i), 0)),
                  pl.BlockSpec((tm, d), lambda i: (rows.lat_idx(i), 0)),
                  pl.BlockSpec((1, d), lambda i: (0, 0)),
                  pl.BlockSpec((1, 1, d), lambda i: (rows.mod_idx(i), 0, 0)),
                  pl.BlockSpec((1, 1, d), lambda i: (rows.mod_idx(i), 0, 0)),
                  pl.BlockSpec((LANES, d), lambda i: (SMALL_COL0 // LANES, 0)),
                  pl.BlockSpec((1, LANES), lambda i: (0, 0)),
                  pl.BlockSpec((1, LANES), lambda i: (0, 0))],
        out_specs=[pl.BlockSpec((tm, d), lambda i: (i, 0)),
                   pl.BlockSpec((tm, LANES), lambda i: (i, 0))],
        out_shape=[jax.ShapeDtypeStruct((n_tok, d), BF16),
                   jax.ShapeDtypeStruct((n_tok, LANES), F32)],
        compiler_params=_params("arbitrary"),
        name="prenorm",
    )(xp, xs, gain, sh, sc, w_in_t, place(a_log), place(dt_bias))


INPROJ_TN = 1024
N_ALIGNED = SMALL_COL0 // INPROJ_TN


def _inproj_body(h_ref, wt_ref, z_ref, w_scr):
    @pl.when(pl.program_id(1) == 0)
    def _():
        w_scr[...] = wt_ref[...].astype(BF16)

    z_ref[...] = lax.dot_general(h_ref[...], w_scr[...], NT_DIMS, preferred_element_type=F32).astype(BF16)


def _inproj(h, w_in_t):
    n_tok, d = h.shape
    tm, tn = INPROJ_ROWS, INPROJ_TN
    first_row = lambda j: pl.multiple_of(j * tn + N_SMALL * jnp.minimum(j // N_ALIGNED, 1), N_SMALL)
    return pl.pallas_call(
        _inproj_body,
        grid=(N_MAIN // tn, n_tok // tm),
        in_specs=[pl.BlockSpec((tm, d), lambda j, i: (i, 0)),
                  pl.BlockSpec((pl.Element(tn), pl.Element(d)), lambda j, i: (first_row(j), 0))],
        out_specs=pl.BlockSpec((tm, tn), lambda j, i: (i, j)),
        out_shape=jax.ShapeDtypeStruct((n_tok, N_MAIN), BF16),
        scratch_shapes=[pltpu.VMEM((tn, d), BF16)],
        compiler_params=_params("arbitrary", "arbitrary"),
        name="inproj",
    )(h, w_in_t)


INV_BLOCK = 16


def _chunk_solve(chains, ii, jj):
    eye = ii == jj
    blk = (ii // INV_BLOCK) == (jj // INV_BLOCK)
    blk2 = (ii // (2 * INV_BLOCK)) == (jj // (2 * INV_BLOCK))
    cols, a_qk, a0, off, d0, rhs = [], [], [], [], [], []
    for kk, qk, v, k, gates, reverse, lane in chains:
        beta, gc, egc, ekd = (gates[:, b + lane:b + lane + 1] for b in (GATE_BETA, GATE_GC, GATE_EGC, GATE_EKD))
        incl = (jj >= ii) if reverse else (jj <= ii)
        strict = (jj > ii) if reverse else (jj < ii)
        gc_row = jnp.sum(jnp.where(eye, gc, 0.0), axis=0, keepdims=True)
        dec = jnp.where(incl, jnp.exp(jnp.where(incl, gc - gc_row, 0.0)), 0.0)
        a = jnp.where(strict, kk * dec, 0.0)
        cols.append((egc, ekd))
        a_qk.append(qk * dec)
        a0.append(jnp.where(blk, a, 0.0))
        off.append(jnp.where(blk, 0.0, a))
        d0.append(jnp.where(eye, 1.0, 0.0) - a0[-1])
        rhs.append(jnp.concatenate([v * beta, k * (beta * egc)], axis=1))
    p = [_mm(x, x) for x in a0]
    for _ in range(int(math.log2(INV_BLOCK)) - 2):
        r = [_mm(jnp.concatenate([pi, di], axis=0), pi) for pi, di in zip(p, d0)]
        p = [ri[:CHUNK] for ri in r]
        d0 = [di + ri[CHUNK:] for di, ri in zip(d0, r)]
    d0 = [di + _mm(di, pi) for di, pi in zip(d0, p)]
    wl = [_mm(di, oi) for di, oi in zip(d0, off)]
    yr = [_mm(di, ri) for di, ri in zip(d0, rhs)]
    x1 = [jnp.where(blk2, x, 0.0) for x in wl]
    yl = [jnp.where(blk2, 0.0, x) for x in wl]
    zl = [y - _mm(x, y) for x, y in zip(x1, yl)]
    zr = [y - _mm(x, y) for x, y in zip(x1, yr)]
    sol = [y - _mm(x, y) for x, y in zip(zl, zr)]
    return [(so[:, :DV_A], so[:, DV_A:], aq, egc, ekd) for so, aq, (egc, ekd) in zip(sol, a_qk, cols)]


def _delta_body(t_len, hp, has_s0, q_ref, k_ref, v_ref, cq_ref, ck_ref, cv_ref, g_ref, *rest):
    if has_s0:
        s0_ref, o_ref, qs, ks, vs, g_scr, u_scr, wq_scr, ak_scr, s_scr = rest
    else:
        o_ref, sfin_ref, qs, ks, vs, g_scr, u_scr, wq_scr, ak_scr, s_scr = rest
    n = t_len // CHUNK
    tpos = lax.broadcasted_iota(jnp.int32, (t_len, 1), 0)

    def conv_act(x, w):
        x_prev = jnp.where(tpos == 0, 0.0, pltpu.roll(x, 1, 0))
        x_next = jnp.where(tpos == t_len - 1, 0.0, pltpu.roll(x, t_len - 1, 0))
        return _silu(x_prev * w[0:1] + x * w[1:2] + x_next * w[2:3])

    def l2n(x):
        return x * lax.rsqrt(jnp.sum(x * x, axis=-1, keepdims=True) + EPS)

    for hh in range(hp):
        cols = slice(hh * LANES, (hh + 1) * LANES)
        qs[:, cols] = l2n(conv_act(q_ref[:, cols].astype(F32), cq_ref[:, cols])) * (DK_A ** -0.5)
        ks[:, cols] = l2n(conv_act(k_ref[:, cols].astype(F32), ck_ref[:, cols]))
        vs[:, cols] = conv_act(v_ref[:, cols].astype(F32), cv_ref[:, cols])
    if hp == H_A:
        g_scr[...] = g_ref[...]
    else:
        g_scr[...] = pltpu.roll(g_ref[...], (LANES - pl.program_id(1) * hp) % LANES, 1)
    if has_s0:
        s_scr[...] = s0_ref[...]
    else:
        s_scr[...] = jnp.zeros_like(s_scr)
    o_ref[...] = jnp.zeros_like(o_ref)
    ii = lax.broadcasted_iota(jnp.int32, (CHUNK, CHUNK), 0)
    jj = lax.broadcasted_iota(jnp.int32, (CHUNK, CHUNK), 1)

    def prep(c, carry):
        r = pl.ds(pl.multiple_of(c * CHUNK, CHUNK), CHUNK)
        g = g_scr[r, :]
        heads = []
        for hh in range(hp):
            cols = slice(hh * LANES, (hh + 1) * LANES)
            heads.append((qs[r, cols], ks[r, cols], vs[r, cols]))
        beta = lambda d, hh: g[:, GATE_BETA + d * H_A + hh:GATE_BETA + d * H_A + hh + 1]
        rr = [_mm_nt(jnp.concatenate([k * beta(0, hh), k * beta(1, hh), q], axis=0), k)
              for hh, (q, k, v) in enumerate(heads)]
        chains = [(rr[hh][d * CHUNK:(d + 1) * CHUNK], rr[hh][2 * CHUNK:], heads[hh][2], heads[hh][1], g,
                   d == 1, d * H_A + hh) for hh in range(hp) for d in (0, 1)]
        solved = _chunk_solve(chains, ii, jj)
        for hh in range(hp):
            cols = slice(hh * LANES, (hh + 1) * LANES)
            q, k = heads[hh][0], heads[hh][1]
            for d in (0, 1):
                u, w, a_qk, egc, ekd = solved[hh * 2 + d]
                slot = (d * hp + hh) * n + c
                u_scr[d, r, cols] = u
                wq_scr[slot] = jnp.concatenate([w, q * egc], axis=0).astype(BF16)
                ak_scr[slot] = jnp.concatenate([a_qk, (k * ekd).T], axis=0).astype(BF16)
        return carry

    lax.fori_loop(0, n, prep, 0, unroll=4)

    def scan(c, carry):
        chains = []
        for hh in range(hp):
            for d in (0, 1):
                cc = c if d == 0 else n - 1 - c
                chains.append((hh, d, pl.multiple_of(cc * CHUNK, CHUNK), (d * hp + hh) * n + cc))
        s = [s_scr[d, hh] for hh, d, r0, slot in chains]
        r1 = [jnp.dot(wq_scr[slot], si.astype(BF16), preferred_element_type=F32)
              for si, (hh, d, r0, slot) in zip(s, chains)]
        v_new = [u_scr[d, pl.ds(r0, CHUNK), hh * LANES:(hh + 1) * LANES] - ri[:CHUNK]
                 for ri, (hh, d, r0, slot) in zip(r1, chains)]
        r2 = [jnp.dot(ak_scr[slot], vi.astype(BF16), preferred_element_type=F32)
              for vi, (hh, d, r0, slot) in zip(v_new, chains)]
        for si, r1i, r2i, (hh, d, r0, slot) in zip(s, r1, r2, chains):
            o_ref[pl.ds(r0, CHUNK), hh * LANES:(hh + 1) * LANES] += r1i[CHUNK:] + r2i[:CHUNK]
            lane = GATE_EGC + d * H_A + hh
            edge = g_scr[pl.ds(pl.multiple_of(r0 + (0 if d else CHUNK - 8), 8), 8), lane:lane + 1]
            s_scr[d, hh] = si * (edge[0:1] if d else edge[7:8]) + r2i[CHUNK:]
        return carry

    lax.fori_loop(0, n, scan, 0, unroll=4)
    if not has_s0:
        sfin_ref[...] = s_scr[...]


def _deltanet(z, conv_w, gates, s0, n_seq, t_len, row0, hp):
    rb = row0 // t_len
    w = hp * LANES
    n = t_len // CHUNK
    seq_blk = lambda col0: pl.BlockSpec((t_len, w), lambda b, h: (rb + b, col0 // hp + h))
    cw_blk = lambda col0: pl.BlockSpec((CONV_K, w), lambda b, h: (0, col0 // hp + h))
    state_blk = pl.BlockSpec((None, None, 2, hp, DK_A, DV_A), lambda b, h: (b, 0, 0, h, 0, 0))
    in_specs = [seq_blk(0), seq_blk(H_A), seq_blk(2 * H_A), cw_blk(0), cw_blk(H_A), cw_blk(2 * H_A),
                pl.BlockSpec((t_len, LANES), lambda b, h: (rb + b, 0))]
    args = [z, z, z, conv_w, conv_w, conv_w, gates]
    o_spec = pl.BlockSpec((t_len, w), lambda b, h: (b, h))
    o_shape = jax.ShapeDtypeStruct((n_seq * t_len, H_A * DV_A), F32)
    has_s0 = s0 is not None
    if has_s0:
        in_specs += [state_blk]
        args += [s0]
        out_specs, out_shape = o_spec, o_shape
    else:
        out_specs = [o_spec, state_blk]
        out_shape = [o_shape, jax.ShapeDtypeStruct((n_seq, 1, 2, H_A, DK_A, DV_A), F32)]

    return pl.pallas_call(
        functools.partial(_delta_body, t_len, hp, has_s0),
        grid=(n_seq, H_A // hp),
        in_specs=in_specs,
        out_specs=out_specs,
        out_shape=out_shape,
        scratch_shapes=[pltpu.VMEM((t_len, w), F32)] * 3
        + [pltpu.VMEM((t_len, LANES), F32),
           pltpu.VMEM((2, t_len, w), F32),
           pltpu.VMEM((2 * hp * n, 2 * CHUNK, DV_A), BF16),
           pltpu.VMEM((2 * hp * n, CHUNK + DK_A, CHUNK), BF16),
           pltpu.VMEM((2, hp, DK_A, DV_A), F32)],
        compiler_params=_params("arbitrary", "arbitrary"),
        name="deltanet_lat" if has_s0 else "deltanet_ctx",
    )(*args)


def _subln(o, sub_ref, lam_init):
    return _rms_scale(o) * sub_ref[...] * (1.0 - lam_init)


def _attn_ctx_body(lam_init, q_ref, k_ref, v_ref, lam_ref, sub_ref, o_ref, ck_ref, cv_ref):
    heads = [slice(h * LANES, (h + 1) * LANES) for h in range(H_B)]
    lam = lam_ref[0:1, 0:1]
    ks = [k_ref[:, c] for c in heads]
    vs = [v_ref[:, c] for c in heads]
    for h in range(H_B):
        ck_ref[h] = ks[h].astype(F32)
        cv_ref[h] = vs[h].astype(F32)
    qb = [q_ref[:, c] * (DQK_B ** -0.5) for c in heads]
    kb = ks
    probs = []
    for lo in (0, DQK_B):
        s = [lax.dot_general(q[:, lo:lo + DQK_B], k[:, lo:lo + DQK_B], NT_DIMS, preferred_element_type=F32)
             for q, k in zip(qb, kb)]
        e = [jnp.exp(x - jnp.max(x, axis=-1, keepdims=True)) for x in s]
        probs.append([x * (1.0 / jnp.sum(x, axis=-1, keepdims=True)) for x in e])
    o = [jnp.dot((p1 - lam * p2).astype(BF16), v, preferred_element_type=F32)
         for p1, p2, v in zip(probs[0], probs[1], vs)]
    for c, oh in zip(heads, o):
        o_ref[:, c] = _subln(oh, sub_ref, lam_init).astype(BF16)


def _attn_ctx(z, lam, subln, n_seq, t_len, lam_init):
    n_tok = n_seq * t_len
    w = H_B * LANES
    blk = lambda col0: pl.BlockSpec((t_len, w), lambda b: (b, col0 // w))
    cache_blk = pl.BlockSpec((None, None, H_B, t_len, LANES), lambda b: (b, 0, 0, 0, 0))
    cache_shape = jax.ShapeDtypeStruct((n_seq, 1, H_B, t_len, LANES), F32)
    return pl.pallas_call(
        functools.partial(_attn_ctx_body, lam_init),
        grid=(n_seq,),
        in_specs=[blk(Z_QB), blk(Z_KB), blk(Z_VB),
                  pl.BlockSpec((1, LANES), lambda b: (0, 0)),
                  pl.BlockSpec((1, LANES), lambda b: (0, 0))],
        out_specs=[pl.BlockSpec((t_len, w), lambda b: (b, 0)), cache_blk, cache_blk],
        out_shape=[jax.ShapeDtypeStruct((n_tok, H_B * DV_B), BF16), cache_shape, cache_shape],
        compiler_params=_params("arbitrary"),
        name="attn_ctx",
    )(z, z, z, lam, subln)


def _rope(x, cos, sin_signed):
    lane = lax.broadcasted_iota(jnp.int32, (1, LANES), 1)
    first = (lane % 32) < 16
    partner = jnp.where(first, pltpu.roll(x, LANES - 16, 1), pltpu.roll(x, 16, 1))
    return x * cos + partner * sin_signed


LAT_HEADS = 2


def _attn_lat_body(lam_init, n_past, q_ref, k_ref, v_ref, pk_ref, pv_ref, cosq_ref, sinq_ref, cos_ref, sin_ref,
                   lam_ref, sub_ref, o_ref, keys, vals):
    heads = [slice(h * LANES, (h + 1) * LANES) for h in range(LAT_HEADS)]

    @pl.when(pl.program_id(2) == 0)
    def _():
        for h, c in enumerate(heads):
            keys[h, 0:n_past, :] = pk_ref[h].astype(BF16)
            vals[h, 0:n_past, :] = pv_ref[h].astype(BF16)
            keys[h, n_past:, :] = _rope(k_ref[:, c].astype(F32), cos_ref[...], sin_ref[...]).astype(BF16)
            vals[h, n_past:, :] = v_ref[:, c]

    lam = lam_ref[0:1, 0:1]
    qb = [(_rope(q_ref[:, c].astype(F32), cosq_ref[...], sinq_ref[...]) * (DQK_B ** -0.5)).astype(BF16)
          for c in heads]
    maps = []
    for lo in (0, DQK_B):
        s = [lax.dot_general(q[:, lo:lo + DQK_B], keys[h, :, lo:lo + DQK_B], NT_DIMS, preferred_element_type=F32)
             for h, q in enumerate(qb)]
        e = [jnp.exp(x - jnp.max(x, axis=-1, keepdims=True)) for x in s]
        inv = [1.0 / jnp.sum(x, axis=-1, keepdims=True) for x in e]
        maps.append([jnp.dot(x.astype(BF16), vals[h], preferred_element_type=F32) * r
                     for h, (x, r) in enumerate(zip(e, inv))])
    for h, c in enumerate(heads):
        o_ref[:, c] = _subln(maps[0][h] - lam * maps[1][h], sub_ref, lam_init).astype(BF16)


def _attn_lat(z, cache_k, cache_v, cos, sin, lam, subln, n_seq, t_len, row0, lam_init):
    tq = ATTN_Q_TILE
    nq = t_len // tq
    n_past = cache_k.shape[3]
    rbq = row0 // tq
    rbs = row0 // t_len
    w = LAT_HEADS * LANES
    seq_blk = lambda col0: pl.BlockSpec((t_len, w), lambda b, h, qi: (rbs + b, col0 // w + h))
    past_blk = pl.BlockSpec((None, None, LAT_HEADS, n_past, LANES), lambda b, h, qi: (b, 0, h, 0, 0))
    row_vec = pl.BlockSpec((1, LANES), lambda b, h, qi: (0, 0))
    return pl.pallas_call(
        functools.partial(_attn_lat_body, lam_init, n_past),
        grid=(n_seq, H_B // LAT_HEADS, nq),
        in_specs=[pl.BlockSpec((tq, w), lambda b, h, qi: (rbq + b * nq + qi, Z_QB // w + h)),
                  seq_blk(Z_KB), seq_blk(Z_VB), past_blk, past_blk,
                  pl.BlockSpec((tq, LANES), lambda b, h, qi: (qi, 0)),
                  pl.BlockSpec((tq, LANES), lambda b, h, qi: (qi, 0)),
                  pl.BlockSpec((t_len, LANES), lambda b, h, qi: (0, 0)),
                  pl.BlockSpec((t_len, LANES), lambda b, h, qi: (0, 0)),
                  row_vec, row_vec],
        out_specs=pl.BlockSpec((tq, w), lambda b, h, qi: (b * nq + qi, h)),
        out_shape=jax.ShapeDtypeStruct((n_seq * t_len, H_B * DV_B), BF16),
        scratch_shapes=[pltpu.VMEM((LAT_HEADS, n_past + t_len, LANES), BF16)] * 2,
        compiler_params=_params("arbitrary", "arbitrary", "arbitrary"),
        name="attn_lat",
    )(z, z, z, cache_k, cache_v, cos, sin, cos, sin, lam, subln)


def _rope_tables(t_len):
    t = np.arange(t_len)
    pos = np.stack([t // GRID_W, t % GRID_W], axis=1).astype(np.float32)
    nf = DQK_B // 4
    inv_freq = np.float32(ROPE_BASE) ** (-np.arange(nf, dtype=np.float32) / np.float32(nf))
    lane = np.arange(LANES)
    half = (lane % DQK_B) // (DQK_B // 2)
    ang = (pos[:, half] * inv_freq[lane % nf][None, :]).astype(np.float32)
    sign = np.where((lane % (DQK_B // 2)) < nf, -1.0, 1.0).astype(np.float32)
    return jnp.asarray(np.cos(ang), F32), jnp.asarray(np.sin(ang) * sign[None, :], F32)


def _merge_body(nct, oac_ref, oal_ref, ga_ref, on_ref, obc_ref, obl_ref, wa_f32, wb_f32, *rest):
    gm_refs, m_ref, wa_ref, wb_ref = rest[:-3], rest[-3], rest[-2], rest[-1]
    i = pl.program_id(0)
    n_blk = len(gm_refs) // 2

    @pl.when(i == 0)
    def _():
        wa_ref[...] = wa_f32[...].astype(BF16)
        wb_ref[...] = wb_f32[...].astype(BF16)

    def run(oa_ref, ob_ref):
        a = jnp.concatenate(
            [(_rms_scale(oa_ref[:, c]) * on_ref[...] * _silu(ga_ref[:, c].astype(F32))).astype(BF16)
             for c in (slice(h * DV_A, (h + 1) * DV_A) for h in range(H_A))], axis=1)
        ya = jnp.dot(a, wa_ref[...], preferred_element_type=F32)
        yb = jnp.dot(ob_ref[...], wb_ref[...], preferred_element_type=F32)
        tn = gm_refs[0].shape[1]
        for j in range(n_blk):
            c = slice(j * tn, (j + 1) * tn)
            m_ref[:, c] = (jax.nn.sigmoid(gm_refs[j][...].astype(F32)) * ya[:, c]
                           + jax.nn.sigmoid(gm_refs[n_blk + j][...].astype(F32)) * yb[:, c]).astype(BF16)

    @pl.when(i < nct)
    def _():
        run(oac_ref, obc_ref)

    @pl.when(i >= nct)
    def _():
        run(oal_ref, obl_ref)


def _merge(rows, z, oa_c, oa_l, ob_c, ob_l, onorm, w_up_a, w_up_b):
    n_tok = z.shape[0]
    tm = rows.tm
    ka, d = w_up_a.shape
    tn = 1024
    nj = d // tn
    ctx_blk = pl.BlockSpec((tm, ka), lambda i: (rows.ctx_idx(i), 0))
    lat_blk = pl.BlockSpec((tm, ka), lambda i: (rows.lat_idx(i), 0))
    weight = pl.BlockSpec((ka, d), lambda i: (0, 0), pipeline_mode=pl.Buffered(1))
    gate_cols = [pl.BlockSpec((tm, tn), functools.partial(lambda i, c: (i, c), c=Z_GM // tn + j)) for j in range(2 * nj)]
    return pl.pallas_call(
        functools.partial(_merge_body, rows.nct),
        grid=(n_tok // tm,),
        in_specs=[ctx_blk, lat_blk,
                  pl.BlockSpec((tm, ka), lambda i: (i, Z_GATE_A // ka)),
                  pl.BlockSpec((1, DV_A), lambda i: (0, 0)),
                  ctx_blk, lat_blk, weight, weight] + gate_cols,
        out_specs=pl.BlockSpec((tm, d), lambda i: (i, 0)),
        out_shape=jax.ShapeDtypeStruct((n_tok, d), BF16),
        scratch_shapes=[pltpu.VMEM((ka, d), BF16)] * 2,
        compiler_params=_params("arbitrary"),
        name="merge",
    )(oa_c, oa_l, z, onorm, ob_c, ob_l, w_up_a, w_up_b, *([z] * (2 * nj)))


ROUTE_E = 0
ROUTE_W = TOP_K


def _route_rows(lg):
    lane = lax.broadcasted_iota(jnp.int32, lg.shape, 1)
    neg = -jnp.inf

    def first_max(x):
        m = jnp.max(x, axis=1, keepdims=True)
        return m, jnp.min(jnp.where(x == m, lane, LANES), axis=1, keepdims=True)

    gl = jnp.where(lane < N_GROUPS, lg, neg)
    gmax, g_idx = first_max(gl)
    pg_top = 1.0 / jnp.sum(jnp.exp(gl - gmax), axis=1, keepdims=True)
    lo = N_GROUPS + E_PER_GROUP * g_idx
    el = jnp.where(jnp.logical_and(lane >= lo, lane < lo + E_PER_GROUP), lg, neg)
    emax, i1 = first_max(el)
    esum = jnp.sum(jnp.exp(el - emax), axis=1, keepdims=True)
    e2max, i2 = first_max(jnp.where(lane == i1, neg, el))
    p1 = 1.0 / esum
    p2 = jnp.exp(e2max - emax) / esum
    den = p1 + p2
    vals = [(i1 - N_GROUPS).astype(F32), (i2 - N_GROUPS).astype(F32), pg_top * p1 / den, pg_top * p2 / den]
    out = jnp.zeros(lg.shape, F32)
    for pos, val in enumerate(vals):
        out = jnp.where(lane == pos, val, out)
    return out


def _outproj_body(nct, m_ref, wo_ref, xp_ref, xs_ref, gt_ref, g2_ref, sh_ref, sc_ref, wr_ref, br_ref,
                  x1_ref, h2_ref, rt_ref):
    i = pl.program_id(0)
    half = m_ref.shape[0] // 2

    def finish(x_ref):
        for r in (slice(0, half), slice(half, 2 * half)):
            y = jnp.dot(m_ref[r, :], wo_ref[...], preferred_element_type=F32)
            x1 = x_ref[r, :] + gt_ref[0] * y
            x1_ref[r, :] = x1
            h2 = _rms_scale(x1) * g2_ref[...]
            h2 = h2 * (1.0 + sc_ref[0]) + sh_ref[0]
            h2_ref[r, :] = h2
            hi = h2.astype(BF16)
            lo = (h2 - hi.astype(F32)).astype(BF16)
            p_hi = jnp.dot(hi, wr_ref[...], preferred_element_type=F32)
            p_lo = jnp.dot(lo, wr_ref[...], preferred_element_type=F32)
            lg = p_hi[:, :LANES] + p_hi[:, LANES:] + p_lo[:, :LANES] + p_lo[:, LANES:] + br_ref[...]
            rt_ref[r, :] = _route_rows(lg)

    @pl.when(i < nct)
    def _():
        finish(xp_ref)

    @pl.when(i >= nct)
    def _():
        finish(xs_ref)


def _outproj(rows, mixed, w_o, xp, xs, gt1, gain2, sh2, sc2, w_r, b_r):
    d = xp.shape[1]
    tm = rows.tm
    n_tok = rows.n * tm
    mod = lambda: pl.BlockSpec((1, 1, d), lambda i: (rows.mod_idx(i), 0, 0))
    tok = pl.BlockSpec((tm, d), lambda i: (i, 0))
    return pl.pallas_call(
        functools.partial(_outproj_body, rows.nct),
        grid=(rows.n,),
        in_specs=[tok,
                  pl.BlockSpec((d, d), lambda i: (0, 0)),
                  pl.BlockSpec((tm, d), lambda i: (rows.ctx_idx(i), 0)),
                  pl.BlockSpec((tm, d), lambda i: (rows.lat_idx(i), 0)),
                  mod(),
                  pl.BlockSpec((1, d), lambda i: (0, 0)),
                  mod(), mod(),
                  pl.BlockSpec((d, 2 * LANES), lambda i: (0, 0)),
                  pl.BlockSpec((1, LANES), lambda i: (0, 0))],
        out_specs=[tok, tok, pl.BlockSpec((tm, LANES), lambda i: (i, 0))],
        out_shape=[jax.ShapeDtypeStruct((n_tok, d), F32), jax.ShapeDtypeStruct((n_tok, d), F32),
                   jax.ShapeDtypeStruct((n_tok, LANES), F32)],
        compiler_params=_params("arbitrary"),
        name="outproj",
    )(mixed, w_o, xp, xs, gt1, gain2, sh2, sc2, w_r, b_r)


DISPATCH_BLOCK = 256
TBL_EXPERT, TBL_VALID, TBL_COUNT, TBL_OFFSET = 0, 1, 2, 3


def _dispatch_body(n_tok, route_ref, pos_ref, tbl_ref):
    nb = n_tok // DISPATCH_BLOCK
    lane = lax.broadcasted_iota(jnp.int32, (1, LANES), 1)
    lane_f = lane.astype(F32)

    def one_hot(b, k):
        r = pl.ds(pl.multiple_of(b * DISPATCH_BLOCK, DISPATCH_BLOCK), DISPATCH_BLOCK)
        return jnp.where(route_ref[r, ROUTE_E + k:ROUTE_E + k + 1] == lane_f, 1.0, 0.0)

    def count(b, acc):
        return acc + jnp.sum(one_hot(b, 0) + one_hot(b, 1), axis=0, keepdims=True)

    counts = lax.fori_loop(0, nb, count, jnp.zeros((1, LANES), F32))
    padded = jnp.floor((counts + (MOE_TILE - 1)) * (1.0 / MOE_TILE)) * MOE_TILE
    pad_end = padded
    shift = 1
    while shift < LANES:
        pad_end = pad_end + jnp.where(lane >= shift, pltpu.roll(pad_end, shift, 1), 0.0)
        shift *= 2
    pad_off = pad_end - padded

    ri = lax.broadcasted_iota(jnp.int32, (DISPATCH_BLOCK, DISPATCH_BLOCK), 0)
    ci = lax.broadcasted_iota(jnp.int32, (DISPATCH_BLOCK, DISPATCH_BLOCK), 1)
    before = jnp.where(ci < ri, 1.0, 0.0).astype(BF16)

    def place(b, run):
        oh = [one_hot(b, k) for k in range(TOP_K)]
        base = pad_off + run
        out = jnp.zeros((DISPATCH_BLOCK, LANES), F32)
        lane_b = lax.broadcasted_iota(jnp.int32, (DISPATCH_BLOCK, LANES), 1)
        for k in range(TOP_K):
            prior = jnp.dot(before, oh[k].astype(BF16), preferred_element_type=F32)
            pos = jnp.sum(oh[k] * (base + prior), axis=1, keepdims=True)
            out = jnp.where(lane_b == k, pos, out)
            base = base + jnp.sum(oh[k], axis=0, keepdims=True)
        r = pl.ds(pl.multiple_of(b * DISPATCH_BLOCK, DISPATCH_BLOCK), DISPATCH_BLOCK)
        pos_ref[r, :] = out.astype(jnp.int32)
        return base - pad_off

    lax.fori_loop(0, nb, place, jnp.zeros((1, LANES), F32))

    end_col = jnp.transpose(jnp.broadcast_to(pad_end, (8, LANES)))[:, 0:1]
    e_col = lax.broadcasted_iota(jnp.int32, (LANES, 1), 0)
    tile_start = lane_f * MOE_TILE
    passed = jnp.where(jnp.logical_and(end_col <= tile_start, e_col < N_EXPERTS), 1.0, 0.0)
    tile_expert = jnp.minimum(jnp.sum(passed, axis=0, keepdims=True), N_EXPERTS - 1.0)
    total = jnp.sum(jnp.where(lane == N_EXPERTS - 1, pad_end, 0.0), axis=1, keepdims=True)
    tile_valid = jnp.where(tile_start < total, 1.0, 0.0)
    row = lax.broadcasted_iota(jnp.int32, (8, LANES), 0)
    tbl = jnp.zeros((8, LANES), F32)
    for idx, val in ((TBL_EXPERT, tile_expert), (TBL_VALID, tile_valid), (TBL_COUNT, counts), (TBL_OFFSET, pad_off)):
        tbl = jnp.where(row == idx, val, tbl)
    tbl_ref[...] = tbl.astype(jnp.int32)


def _dispatch(route):
    n_tok = route.shape[0]
    return pl.pallas_call(
        functools.partial(_dispatch_body, n_tok),
        out_shape=[jax.ShapeDtypeStruct((n_tok, LANES), jnp.int32), jax.ShapeDtypeStruct((8, LANES), jnp.int32)],
        compiler_params=pltpu.CompilerParams(vmem_limit_bytes=VMEM_LIMIT),
        name="dispatch",
    )(route)


def _scatter_body(tm, n_tiles, pos_ref, cnt_ref, off_ref, h_ref, o_hbm, zbuf, sem):
    i = pl.program_id(0)

    @pl.when(i == 0)
    def _():
        zbuf[...] = jnp.zeros_like(zbuf)
        last = N_EXPERTS - 1
        first_empty = (off_ref[last] + cnt_ref[last] + MOE_TILE - 1) // MOE_TILE

        def tile_copy(t):
            rows = pl.ds(pl.multiple_of(t * MOE_TILE, MOE_TILE), MOE_TILE)
            return pltpu.make_async_copy(zbuf, o_hbm.at[rows, :], sem.at[1])

        def partial_tile(e, start):
            cnt = cnt_ref[e]

            @pl.when(cnt % MOE_TILE != 0)
            def _():
                cp = tile_copy((off_ref[e] + cnt) // MOE_TILE)
                if start:
                    cp.start()
                else:
                    cp.wait()

        def zero_partial(e, carry):
            partial_tile(e, True)
            return carry

        def partial_done(e, carry):
            partial_tile(e, False)
            return carry

        def zero_tile(t, carry):
            tile_copy(t).start()
            return carry

        def zero_done(t, carry):
            tile_copy(t).wait()
            return carry

        lax.fori_loop(0, N_EXPERTS, zero_partial, 0)
        lax.fori_loop(first_empty, n_tiles, zero_tile, 0)
        lax.fori_loop(0, N_EXPERTS, partial_done, 0)
        lax.fori_loop(first_empty, n_tiles, zero_done, 0)

    def row_copies(r):
        return [pltpu.make_async_copy(h_ref.at[pl.ds(r, 1), :],
                                      o_hbm.at[pl.ds(pos_ref[(i * tm + r) * TOP_K + k], 1), :], sem.at[0])
                for k in range(TOP_K)]

    def start(r, carry):
        for k, cp in enumerate(row_copies(r)):
            cp.start(priority=k)
        return carry

    def wait(r, carry):
        for cp in row_copies(r):
            cp.wait()
        return carry

    lax.fori_loop(0, tm, start, 0, unroll=8)
    lax.fori_loop(0, tm, wait, 0, unroll=8)


def _scatter_rows(h2, pos_flat, counts, offsets, n_rows):
    n_tok, d = h2.shape
    tm = SCATTER_TILE
    grid_spec = pltpu.PrefetchScalarGridSpec(
        num_scalar_prefetch=3,
        grid=(n_tok // tm,),
        in_specs=[pl.BlockSpec((tm, d), lambda i, pos, cnt, off: (i, 0))],
        out_specs=pl.BlockSpec(memory_space=pl.ANY),
        scratch_shapes=[pltpu.VMEM((MOE_TILE, d), F32), pltpu.SemaphoreType.DMA((2,))],
    )
    return pl.pallas_call(
        functools.partial(_scatter_body, tm, n_rows // MOE_TILE),
        grid_spec=grid_spec,
        out_shape=jax.ShapeDtypeStruct((n_rows, d), F32),
        compiler_params=_params("arbitrary"),
        name="scatter_rows",
    )(pos_flat, counts, offsets, h2)


def _moe_body(n_tiles, te_ref, tv_ref, cnt_ref, x_ref, wg_hbm, wu_hbm, wd_hbm, y_ref,
              wg_f, wu_f, wd_f, wg_b, wu_b, wd_b, slot_ref, sem):
    i = pl.program_id(0)
    prev = jnp.maximum(i - 1, 0)
    valid = tv_ref[i] == 1
    e = te_ref[i]

    def fetch(expert, slot):
        return [pltpu.make_async_copy(src.at[expert], dst.at[slot], sem.at[slot])
                for src, dst in ((wg_hbm, wg_f), (wu_hbm, wu_f), (wd_hbm, wd_f))]

    @pl.when(jnp.logical_and(valid, i == 0))
    def _():
        slot_ref[0] = 1
        for cp in fetch(e, 0):
            cp.start()

    @pl.when(jnp.logical_and(valid, jnp.logical_or(i == 0, e != te_ref[prev])))
    def _():
        slot = 1 - slot_ref[0]
        slot_ref[0] = slot
        for cp in fetch(e, slot):
            cp.wait()
        nxt = jnp.minimum(i + (cnt_ref[e] + MOE_TILE - 1) // MOE_TILE, n_tiles - 1)

        @pl.when(jnp.logical_and(nxt > i, tv_ref[nxt] == 1))
        def _():
            for cp in fetch(te_ref[nxt], 1 - slot):
                cp.start()

        wg_b[...] = wg_f[slot].astype(BF16)
        wu_b[...] = wu_f[slot].astype(BF16)
        wd_b[...] = wd_f[slot].astype(BF16)

    @pl.when(valid)
    def _():
        x = x_ref[...].astype(BF16)
        g = jnp.dot(x, wg_b[...], preferred_element_type=F32)
        u = jnp.dot(x, wu_b[...], preferred_element_type=F32)
        y_ref[...] = jnp.dot((_silu(g) * u).astype(BF16), wd_b[...], preferred_element_type=F32)

    @pl.when(tv_ref[i] == 0)
    def _():
        y_ref[...] = jnp.zeros_like(y_ref)


def _moe(x_sorted, tile_expert, tile_valid, counts, w_g, w_u, w_d):
    n_tiles = tile_expert.shape[0]
    d = x_sorted.shape[1]
    f = w_g.shape[-1]
    hbm = pl.BlockSpec(memory_space=pl.ANY)
    grid_spec = pltpu.PrefetchScalarGridSpec(
        num_scalar_prefetch=3,
        grid=(n_tiles,),
        in_specs=[pl.BlockSpec((MOE_TILE, d), lambda i, te, tv, cnt: (i * tv[i], 0)), hbm, hbm, hbm],
        out_specs=pl.BlockSpec((MOE_TILE, d), lambda i, te, tv, cnt: (i, 0)),
        scratch_shapes=[pltpu.VMEM((2, d, f), F32), pltpu.VMEM((2, d, f), F32), pltpu.VMEM((2, f, d), F32),
                        pltpu.VMEM((d, f), BF16), pltpu.VMEM((d, f), BF16), pltpu.VMEM((f, d), BF16),
                        pltpu.SMEM((1,), jnp.int32), pltpu.SemaphoreType.DMA((2,))],
    )
    return pl.pallas_call(
        functools.partial(_moe_body, n_tiles),
        grid_spec=grid_spec,
        out_shape=jax.ShapeDtypeStruct((n_tiles * MOE_TILE, d), F32),
        compiler_params=_params("arbitrary"),
        name="moe",
    )(tile_expert, tile_valid, counts, x_sorted, w_g, w_u, w_d)


def _combine_body(tm, n_tiles, row0, pos_ref, y_hbm, rt_ref, x1_ref, gt_ref, gf_ref, o_ref, ybuf, sem):
    i = pl.program_id(0)
    slot = i % 2

    def gather(tile, dst_slot, start):
        def row(r, carry):
            for k in range(TOP_K):
                p = pos_ref[(row0 + tile * tm + r) * TOP_K + k]
                cp = pltpu.make_async_copy(y_hbm.at[pl.ds(p, 1), :], ybuf.at[dst_slot, k, pl.ds(r, 1), :],
                                           sem.at[dst_slot])
                if start:
                    cp.start(priority=k)
                else:
                    cp.wait()
            return carry

        lax.fori_loop(0, tm, row, 0, unroll=8)

    @pl.when(i == 0)
    def _():
        gather(0, 0, True)

    @pl.when(i + 1 < n_tiles)
    def _():
        gather(i + 1, 1 - slot, True)

    gather(i, slot, False)
    rt = rt_ref[...]
    y = rt[:, ROUTE_W:ROUTE_W + 1] * ybuf[slot, 0] + rt[:, ROUTE_W + 1:ROUTE_W + 2] * ybuf[slot, 1]
    x2 = x1_ref[...] + gt_ref[0] * y
    o_ref[...] = _rms_scale(x2) * gf_ref[...]


def _combine(y_sorted, pos_flat, route, x1, gt2, gain_f, row0, n_rows, mod_of_tile, tm):
    d = x1.shape[1]
    rb = row0 // tm
    n_tiles = n_rows // tm
    grid_spec = pltpu.PrefetchScalarGridSpec(
        num_scalar_prefetch=1,
        grid=(n_tiles,),
        in_specs=[pl.BlockSpec(memory_space=pl.ANY),
                  pl.BlockSpec((tm, LANES), lambda i, pos: (rb + i, 0)),
                  pl.BlockSpec((tm, d), lambda i, pos: (rb + i, 0)),
                  pl.BlockSpec((1, 1, d), lambda i, pos: (mod_of_tile(i), 0, 0)),
                  pl.BlockSpec((1, d), lambda i, pos: (0, 0))],
        out_specs=pl.BlockSpec((tm, d), lambda i, pos: (i, 0)),
        scratch_shapes=[pltpu.VMEM((2, TOP_K, tm, d), F32), pltpu.SemaphoreType.DMA((2,))],
    )
    return pl.pallas_call(
        functools.partial(_combine_body, tm, n_tiles, row0),
        grid_spec=grid_spec,
        out_shape=jax.ShapeDtypeStruct((n_rows, d), F32),
        compiler_params=_params("arbitrary"),
        name="combine",
    )(pos_flat, y_sorted, route, x1, gt2, gain_f)


def kernel(x_prompt, x_sample, c, state_delta, cache_k, cache_v, c_ctx, w_ada, b_ada, norm_mix, norm_ffn, w_in,
           conv_a, a_log, dt_bias, onorm_a, lam, subln_b, w_up_a, w_up_b, w_o, w_rg, b_rg, w_re, b_re,
           w_e_gate, w_e_up, w_e_down, norm_final):
    assert w_in.shape[0] == 1, "single trunk layer"
    l = 0
    lam_init = 0.8 - 0.6 * math.exp(-0.3 * l)
    bc, tc, d = x_prompt.shape
    bl, tl, _ = x_sample.shape
    n_ctx, n_lat = bc * tc, bl * tl
    xp = x_prompt.reshape(n_ctx, d)
    xs = x_sample.reshape(n_lat, d)

    cvec = jnp.zeros((N_MOD_ROWS, d), F32).at[0].set(c_ctx).at[1:1 + bl].set(c)
    mod = _adaln(cvec, w_ada[l], b_ada[l][None, :])
    sh1, sc1, gt1, sh2, sc2, gt2 = [m.reshape(N_MOD_ROWS, 1, d) for m in jnp.split(mod, 6, axis=1)]

    rows = _Rows(n_ctx, n_lat, tl, ROW_TILE)
    w_in_t = jnp.swapaxes(w_in[l], 0, 1)
    h1, gates = _prenorm(rows, xp, xs, norm_mix[l][None, :], sh1, sc1, w_in_t, a_log[l], dt_bias[l])
    z = _inproj(h1, w_in_t)

    oa_c, new_state = _deltanet(z, conv_a[l], gates, None, bc, tc, 0, DELTA_HEADS_CTX)
    oa_l = _deltanet(z, conv_a[l], gates, state_delta, bl, tl, n_ctx, DELTA_HEADS_LAT)

    lv = lam[l]
    lam_val = jnp.exp(jnp.sum(lv[0] * lv[1])) - jnp.exp(jnp.sum(lv[2] * lv[3])) + lam_init
    lam_row = jnp.full((1, LANES), lam_val, F32)
    sub_row = subln_b[l][None, :]
    ob_c, new_k, new_v = _attn_ctx(z, lam_row, sub_row, bc, tc, lam_init)
    cos, sin = _rope_tables(tl)
    ob_l = _attn_lat(z, cache_k, cache_v, cos, sin, lam_row, sub_row, bl, tl, n_ctx, lam_init)

    mixed = _merge(_Rows(n_ctx, n_lat, tl, MERGE_TILE), z, oa_c, oa_l, ob_c, ob_l, onorm_a[l][None, :],
                   w_up_a[l], w_up_b[l])
    w_r = jnp.pad(jnp.concatenate([w_rg[l], w_re[l]], axis=1), ((0, 0), (0, LANES - N_GROUPS - N_EXPERTS)))
    w_r_hi = w_r.astype(BF16)
    w_r = jnp.concatenate([w_r_hi, (w_r - w_r_hi.astype(F32)).astype(BF16)], axis=1)
    b_r = jnp.pad(jnp.concatenate([b_rg[l], b_re[l]]), (0, LANES - N_GROUPS - N_EXPERTS))[None, :]
    x1, h2, route = _outproj(rows, mixed, w_o[l].astype(BF16), xp, xs, gt1, norm_ffn[l][None, :], sh2, sc2, w_r, b_r)

    n_tok = n_ctx + n_lat
    n_tiles = (n_tok * TOP_K) // MOE_TILE + N_EXPERTS
    pos, tbl = _dispatch(route)
    pos_flat = pos[:, :TOP_K].reshape(-1)
    counts, offsets = tbl[TBL_COUNT, :N_EXPERTS], tbl[TBL_OFFSET, :N_EXPERTS]
    x_sorted = _scatter_rows(h2, pos_flat, counts, offsets, n_tiles * MOE_TILE)
    y_sorted = _moe(x_sorted, tbl[TBL_EXPERT, :n_tiles], tbl[TBL_VALID, :n_tiles], counts,
                    w_e_gate[l], w_e_up[l], w_e_down[l])
    gf = norm_final[None, :]
    tm_c = COMBINE_TILE
    y_prompt = _combine(y_sorted, pos_flat, route, x1, gt2, gf, 0, n_ctx, lambda i: 0 * i, tm_c)
    y_sample = _combine(y_sorted, pos_flat, route, x1, gt2, gf, n_ctx, n_lat, lambda i: 1 + i // (tl // tm_c), tm_c)
    return (y_prompt.reshape(bc, tc, d), y_sample.reshape(bl, tl, d), new_state, new_k, new_v)
```

```python
import functools
import math

import jax
import jax.numpy as jnp
import numpy as np
from jax import lax
from jax.experimental import pallas as pl
from jax.experimental.pallas import tpu as pltpu

F32 = jnp.float32
BF16 = jnp.bfloat16

D_MODEL = 2048
GRID_W = 64
H_A = 8
DK_A = 128
DV_A = 128
CONV_K = 3
CHUNK = 64
H_B = 8
DQK_B = 64
DV_B = 2 * DQK_B
ROPE_BASE = 10000.0
N_GROUPS = 4
E_PER_GROUP = 8
N_EXPERTS = N_GROUPS * E_PER_GROUP
TOP_K = 2
D_FF_E = D_MODEL // 4
EPS = 1e-6

LANES = 128
QKV_A = 2 * H_A * DK_A + H_A * DV_A
N_SMALL = 4 * H_A
Z_GATE_A = QKV_A
Z_QB = Z_GATE_A + H_A * DV_A
Z_KB = Z_QB + H_B * 2 * DQK_B
Z_VB = Z_KB + H_B * 2 * DQK_B
Z_GM = Z_VB + H_B * DV_B
N_MAIN = Z_GM + 2 * D_MODEL

N_MOD_ROWS = 8
VMEM_LIMIT = 60 * 1024 * 1024
ROW_TILE = 512
INPROJ_ROWS = 2048
MERGE_TILE = 256
SCATTER_TILE = 512
COMBINE_TILE = 256
MOE_TILE = 256
ATTN_Q_TILE = 256
DELTA_HEADS_CTX = 8
DELTA_HEADS_LAT = 8
DELTA_SINGLE_BUFFER_ELEMS = 1 << 20

NT_DIMS = (((1,), (1,)), ((), ()))


def _params(*sem):
    return pltpu.CompilerParams(dimension_semantics=sem, vmem_limit_bytes=VMEM_LIMIT)


def _mm(a, b):
    return jnp.dot(a.astype(BF16), b.astype(BF16), preferred_element_type=F32)


def _mm_nt(a, b):
    return lax.dot_general(a.astype(BF16), b.astype(BF16), NT_DIMS, preferred_element_type=F32)


def _silu(x):
    return x * jax.nn.sigmoid(x)


def _rms_scale(x):
    return x * lax.rsqrt(jnp.mean(x * x, axis=-1, keepdims=True) + EPS)


def _adaln_body(c_ref, w_ref, b_ref, o_ref):
    s = _silu(c_ref[...])
    o_ref[...] = _mm(s, w_ref[...]) + b_ref[...]


def _adaln(cvec, w, b):
    d, n = w.shape
    tn = 1024
    return pl.pallas_call(
        _adaln_body,
        grid=(n // tn,),
        in_specs=[pl.BlockSpec((N_MOD_ROWS, d), lambda j: (0, 0)),
                  pl.BlockSpec((d, tn), lambda j: (0, j)),
                  pl.BlockSpec((1, tn), lambda j: (0, j))],
        out_specs=pl.BlockSpec((N_MOD_ROWS, tn), lambda j: (0, j)),
        out_shape=jax.ShapeDtypeStruct((N_MOD_ROWS, n), F32),
        compiler_params=_params("arbitrary"),
        name="adaln",
    )(cvec, w, b)


class _Rows:
    def __init__(self, n_ctx, n_lat, t_lat, tm):
        assert n_ctx % tm == 0 and n_lat % tm == 0 and t_lat % tm == 0
        self.tm = tm
        self.nct = n_ctx // tm
        self.nlt = n_lat // tm
        self.per_seq = t_lat // tm
        self.n = self.nct + self.nlt

    def ctx_idx(self, i):
        return jnp.minimum(i, self.nct - 1)

    def lat_idx(self, i):
        return jnp.maximum(i - self.nct, 0)

    def mod_idx(self, i):
        return jnp.where(i < self.nct, 0, 1 + (i - self.nct) // self.per_seq)


SMALL_COL0 = QKV_A + H_A * DV_A
GATE_KINDS = 4
GATE_BETA, GATE_GC, GATE_EGC, GATE_EKD = (kind * 2 * H_A for kind in range(GATE_KINDS))


def _gate_rows(zs, a_log_row, dt_row):
    lane = lax.broadcasted_iota(jnp.int32, (1, LANES), 1)
    x = zs + dt_row
    softplus = jnp.maximum(x, 0.0) + jnp.log(1.0 + jnp.exp(-jnp.abs(x)))
    in_g = jnp.logical_and(lane >= GATE_GC, lane < GATE_EGC)
    g = jnp.where(in_g, -jnp.exp(a_log_row) * softplus, 0.0)
    beta = jax.nn.sigmoid(zs)
    reversed_lane = lane >= GATE_GC + H_A
    ri = lax.broadcasted_iota(jnp.int32, (CHUNK, CHUNK), 0)
    ci = lax.broadcasted_iota(jnp.int32, (CHUNK, CHUNK), 1)
    prefix = jnp.where(ci <= ri, 1.0, 0.0)
    suffix = jnp.where(ci >= ri, 1.0, 0.0)
    exact = dict(preferred_element_type=F32, precision=lax.Precision.HIGHEST)
    rows = []
    for c in range(zs.shape[0] // CHUNK):
        r = slice(c * CHUNK, (c + 1) * CHUNK)
        gch = g[r]
        gc = jnp.where(reversed_lane, jnp.dot(suffix, gch, **exact), jnp.dot(prefix, gch, **exact))
        tot = jnp.sum(gch, axis=0, keepdims=True)
        rows.append(jnp.where(lane < GATE_GC, beta[r],
                              jnp.where(lane < GATE_EGC, gc,
                                        jnp.where(lane < GATE_EKD, pltpu.roll(jnp.exp(gc), GATE_EGC - GATE_GC, 1),
                                                  jnp.where(lane < GATE_EKD + 2 * H_A,
                                                            pltpu.roll(jnp.exp(tot - gc), GATE_EKD - GATE_GC, 1),
                                                            0.0)))))
    return jnp.concatenate(rows, axis=0)


def _prenorm_body(nct, xp_ref, xs_ref, g_ref, sh_ref, sc_ref, ws_ref, al_ref, dt_ref, h_ref, gates_ref):
    i = pl.program_id(0)

    def run(x_ref):
        h = _rms_scale(x_ref[...]) * g_ref[...]
        h = (h * (1.0 + sc_ref[0]) + sh_ref[0]).astype(BF16)
        h_ref[...] = h
        zs = lax.dot_general(h, ws_ref[...].astype(BF16), NT_DIMS, preferred_element_type=F32)
        gates_ref[...] = _gate_rows(zs, al_ref[...], dt_ref[...])

    @pl.when(i < nct)
    def _():
        run(xp_ref)

    @pl.when(i >= nct)
    def _():
        run(xs_ref)


def _prenorm(rows, xp, xs, gain, sh, sc, w_in_t, a_log, dt_bias):
    place = lambda v: jnp.pad(v.reshape(1, 2 * H_A), ((0, 0), (GATE_GC, LANES - GATE_EGC)))
    d = xp.shape[1]
    tm = rows.tm
    n_tok = rows.n * tm
    return pl.pallas_call(
        functools.partial(_prenorm_body, rows.nct),
        grid=(rows.n,),
        in_specs=[pl.BlockSpec((tm, d), lambda i: (rows.ctx_idx(i), 0)),
                  pl.BlockSpec((tm, d), lambda i: (rows.lat_idx(i), 0)),
                  pl.BlockSpec((1, d), lambda i: (0, 0)),
                  pl.BlockSpec((1, 1, d), lambda i: (rows.mod_idx(i), 0, 0)),
                  pl.BlockSpec((1, 1, d), lambda i: (rows.mod_idx(i), 0, 0)),
                  pl.BlockSpec((LANES, d), lambda i: (SMALL_COL0 // LANES, 0)),
                  pl.BlockSpec((1, LANES), lambda i: (0, 0)),
                  pl.BlockSpec((1, LANES), lambda i: (0, 0))],
        out_specs=[pl.BlockSpec((tm, d), lambda i: (i, 0)),
                   pl.BlockSpec((tm, LANES), lambda i: (i, 0))],
        out_shape=[jax.ShapeDtypeStruct((n_tok, d), BF16),
                   jax.ShapeDtypeStruct((n_tok, LANES), F32)],
        compiler_params=_params("arbitrary"),
        name="prenorm",
    )(xp, xs, gain, sh, sc, w_in_t, place(a_log), place(dt_bias))


INPROJ_TN = 1024
N_ALIGNED = SMALL_COL0 // INPROJ_TN


def _inproj_body(h_ref, wt_ref, z_ref, w_scr):
    @pl.when(pl.program_id(1) == 0)
    def _():
        w_scr[...] = wt_ref[...].astype(BF16)

    z_ref[...] = lax.dot_general(h_ref[...], w_scr[...], NT_DIMS, preferred_element_type=F32).astype(BF16)


def _inproj(h, w_in_t):
    n_tok, d = h.shape
    tm, tn = INPROJ_ROWS, INPROJ_TN
    first_row = lambda j: pl.multiple_of(j * tn + N_SMALL * jnp.minimum(j // N_ALIGNED, 1), N_SMALL)
    return pl.pallas_call(
        _inproj_body,
        grid=(N_MAIN // tn, n_tok // tm),
        in_specs=[pl.BlockSpec((tm, d), lambda j, i: (i, 0)),
                  pl.BlockSpec((pl.Element(tn), pl.Element(d)), lambda j, i: (first_row(j), 0))],
        out_specs=pl.BlockSpec((tm, tn), lambda j, i: (i, j)),
        out_shape=jax.ShapeDtypeStruct((n_tok, N_MAIN), BF16),
        scratch_shapes=[pltpu.VMEM((tn, d), BF16)],
        compiler_params=_params("arbitrary", "arbitrary"),
        name="inproj",
    )(h, w_in_t)


INV_BLOCK = 16


def _chunk_solve(chains, ii, jj):
    eye = ii == jj
    blk = (ii // INV_BLOCK) == (jj // INV_BLOCK)
    blk2 = (ii // (2 * INV_BLOCK)) == (jj // (2 * INV_BLOCK))
    cols, a_qk, a0, off, d0, rhs = [], [], [], [], [], []
    for kk, qk, v, k, gates, reverse, lane in chains:
        beta, gc, egc, ekd = (gates[:, b + lane:b + lane + 1] for b in (GATE_BETA, GATE_GC, GATE_EGC, GATE_EKD))
        incl = (jj >= ii) if reverse else (jj <= ii)
        strict = (jj > ii) if reverse else (jj < ii)
        gc_row = jnp.sum(jnp.where(eye, gc, 0.0), axis=0, keepdims=True)
        dec = jnp.where(incl, jnp.exp(jnp.where(incl, gc - gc_row, 0.0)), 0.0)
        a = jnp.where(strict, kk * dec, 0.0)
        cols.append((egc, ekd))
        a_qk.append(qk * dec)
        a0.append(jnp.where(blk, a, 0.0))
        off.append(jnp.where(blk, 0.0, a))
        d0.append(jnp.where(eye, 1.0, 0.0) - a0[-1])
        rhs.append(jnp.concatenate([v * beta, k * (beta * egc)], axis=1))
    p = [_mm(x, x) for x in a0]
    for _ in range(int(math.log2(INV_BLOCK)) - 2):
        r = [_mm(jnp.concatenate([pi, di], axis=0), pi) for pi, di in zip(p, d0)]
        p = [ri[:CHUNK] for ri in r]
        d0 = [di + ri[CHUNK:] for di, ri in zip(d0, r)]
    d0 = [di + _mm(di, pi) for di, pi in zip(d0, p)]
    wl = [_mm(di, oi) for di, oi in zip(d0, off)]
    yr = [_mm(di, ri) for di, ri in zip(d0, rhs)]
    x1 = [jnp.where(blk2, x, 0.0) for x in wl]
    yl = [jnp.where(blk2, 0.0, x) for x in wl]
    zl = [y - _mm(x, y) for x, y in zip(x1, yl)]
    zr = [y - _mm(x, y) for x, y in zip(x1, yr)]
    sol = [y - _mm(x, y) for x, y in zip(zl, zr)]
    return [(so[:, :DV_A], so[:, DV_A:], aq, egc, ekd) for so, aq, (egc, ekd) in zip(sol, a_qk, cols)]


def _delta_body(t_len, hp, has_s0, q_ref, k_ref, v_ref, cq_ref, ck_ref, cv_ref, g_ref, *rest):
    if has_s0:
        s0_ref, o_ref, qs, ks, vs, g_scr, u_scr, wq_scr, ak_scr, s_scr = rest
    else:
        o_ref, sfin_ref, qs, ks, vs, g_scr, u_scr, wq_scr, ak_scr, s_scr = rest
    n = t_len // CHUNK
    tpos = lax.broadcasted_iota(jnp.int32, (t_len, 1), 0)

    def conv_act(x, w):
        x_prev = jnp.where(tpos == 0, 0.0, pltpu.roll(x, 1, 0))
        x_next = jnp.where(tpos == t_len - 1, 0.0, pltpu.roll(x, t_len - 1, 0))
        return _silu(x_prev * w[0:1] + x * w[1:2] + x_next * w[2:3])

    def l2n(x):
        return x * lax.rsqrt(jnp.sum(x * x, axis=-1, keepdims=True) + EPS)

    for hh in range(hp):
        cols = slice(hh * LANES, (hh + 1) * LANES)
        qs[:, cols] = l2n(conv_act(q_ref[:, cols].astype(F32), cq_ref[:, cols])) * (DK_A ** -0.5)
        ks[:, cols] = l2n(conv_act(k_ref[:, cols].astype(F32), ck_ref[:, cols]))
        vs[:, cols] = conv_act(v_ref[:, cols].astype(F32), cv_ref[:, cols])
    if hp == H_A:
        g_scr[...] = g_ref[...]
    else:
        g_scr[...] = pltpu.roll(g_ref[...], (LANES - pl.program_id(1) * hp) % LANES, 1)
    if has_s0:
        s_scr[...] = s0_ref[...]
    else:
        s_scr[...] = jnp.zeros_like(s_scr)
    o_ref[...] = jnp.zeros_like(o_ref)
    ii = lax.broadcasted_iota(jnp.int32, (CHUNK, CHUNK), 0)
    jj = lax.broadcasted_iota(jnp.int32, (CHUNK, CHUNK), 1)

    def prep(c, carry):
        r = pl.ds(pl.multiple_of(c * CHUNK, CHUNK), CHUNK)
        g = g_scr[r, :]
        heads = []
        for hh in range(hp):
            cols = slice(hh * LANES, (hh + 1) * LANES)
            heads.append((qs[r, cols], ks[r, cols], vs[r, cols]))
        beta = lambda d, hh: g[:, GATE_BETA + d * H_A + hh:GATE_BETA + d * H_A + hh + 1]
        rr = [_mm_nt(jnp.concatenate([k * beta(0, hh), k * beta(1, hh), q], axis=0), k)
              for hh, (q, k, v) in enumerate(heads)]
        chains = [(rr[hh][d * CHUNK:(d + 1) * CHUNK], rr[hh][2 * CHUNK:], heads[hh][2], heads[hh][1], g,
                   d == 1, d * H_A + hh) for hh in range(hp) for d in (0, 1)]
        solved = _chunk_solve(chains, ii, jj)
        for hh in range(hp):
            cols = slice(hh * LANES, (hh + 1) * LANES)
            q, k = heads[hh][0], heads[hh][1]
            for d in (0, 1):
                u, w, a_qk, egc, ekd = solved[hh * 2 + d]
                slot = (d * hp + hh) * n + c
                u_scr[d, r, cols] = u
                wq_scr[slot] = jnp.concatenate([w, q * egc], axis=0).astype(BF16)
                ak_scr[slot] = jnp.concatenate([a_qk, (k * ekd).T], axis=0).astype(BF16)
        return carry

    lax.fori_loop(0, n, prep, 0, unroll=4)

    def scan(c, carry):
        chains = []
        for hh in range(hp):
            for d in (0, 1):
                cc = c if d == 0 else n - 1 - c
                chains.append((hh, d, pl.multiple_of(cc * CHUNK, CHUNK), (d * hp + hh) * n + cc))
        s = [s_scr[d, hh] for hh, d, r0, slot in chains]
        r1 = [jnp.dot(wq_scr[slot], si.astype(BF16), preferred_element_type=F32)
              for si, (hh, d, r0, slot) in zip(s, chains)]
        v_new = [u_scr[d, pl.ds(r0, CHUNK), hh * LANES:(hh + 1) * LANES] - ri[:CHUNK]
                 for ri, (hh, d, r0, slot) in zip(r1, chains)]
        r2 = [jnp.dot(ak_scr[slot], vi.astype(BF16), preferred_element_type=F32)
              for vi, (hh, d, r0, slot) in zip(v_new, chains)]
        for si, r1i, r2i, (hh, d, r0, slot) in zip(s, r1, r2, chains):
            o_ref[pl.ds(r0, CHUNK), hh * LANES:(hh + 1) * LANES] += r1i[CHUNK:] + r2i[:CHUNK]
            lane = GATE_EGC + d * H_A + hh
            edge = g_scr[pl.ds(pl.multiple_of(r0 + (0 if d else CHUNK - 8), 8), 8), lane:lane + 1]
            s_scr[d, hh] = si * (edge[0:1] if d else edge[7:8]) + r2i[CHUNK:]
        return carry

    lax.fori_loop(0, n, scan, 0, unroll=4)
    if not has_s0:
        sfin_ref[...] = s_scr[...]


def _deltanet(z, conv_w, gates, s0, n_seq, t_len, row0, hp):
    rb = row0 // t_len
    w = hp * LANES
    n = t_len // CHUNK
    mode = dict(pipeline_mode=pl.Buffered(1)) if t_len * w >= DELTA_SINGLE_BUFFER_ELEMS else {}
    seq_blk = lambda col0: pl.BlockSpec((t_len, w), lambda b, h: (rb + b, col0 // hp + h), **mode)
    cw_blk = lambda col0: pl.BlockSpec((CONV_K, w), lambda b, h: (0, col0 // hp + h))
    state_blk = pl.BlockSpec((None, None, 2, hp, DK_A, DV_A), lambda b, h: (b, 0, 0, h, 0, 0), **mode)
    in_specs = [seq_blk(0), seq_blk(H_A), seq_blk(2 * H_A), cw_blk(0), cw_blk(H_A), cw_blk(2 * H_A),
                pl.BlockSpec((t_len, LANES), lambda b, h: (rb + b, 0), **mode)]
    args = [z, z, z, conv_w, conv_w, conv_w, gates]
    o_spec = pl.BlockSpec((t_len, w), lambda b, h: (b, h), **mode)
    o_shape = jax.ShapeDtypeStruct((n_seq * t_len, H_A * DV_A), F32)
    has_s0 = s0 is not None
    if has_s0:
        in_specs += [state_blk]
        args += [s0]
        out_specs, out_shape = o_spec, o_shape
    else:
        out_specs = [o_spec, state_blk]
        out_shape = [o_shape, jax.ShapeDtypeStruct((n_seq, 1, 2, H_A, DK_A, DV_A), F32)]

    return pl.pallas_call(
        functools.partial(_delta_body, t_len, hp, has_s0),
        grid=(n_seq, H_A // hp),
        in_specs=in_specs,
        out_specs=out_specs,
        out_shape=out_shape,
        scratch_shapes=[pltpu.VMEM((t_len, w), F32)] * 3
        + [pltpu.VMEM((t_len, LANES), F32),
           pltpu.VMEM((2, t_len, w), F32),
           pltpu.VMEM((2 * hp * n, 2 * CHUNK, DV_A), BF16),
           pltpu.VMEM((2 * hp * n, CHUNK + DK_A, CHUNK), BF16),
           pltpu.VMEM((2, hp, DK_A, DV_A), F32)],
        compiler_params=_params("arbitrary", "arbitrary"),
        name="deltanet_lat" if has_s0 else "deltanet_ctx",
    )(*args)


def _subln(o, sub_ref, lam_init):
    return _rms_scale(o) * sub_ref[...] * (1.0 - lam_init)


def _attn_ctx_body(lam_init, q_ref, k_ref, v_ref, lam_ref, sub_ref, o_ref, ck_ref, cv_ref):
    heads = [slice(h * LANES, (h + 1) * LANES) for h in range(H_B)]
    lam = lam_ref[0:1, 0:1]
    ks = [k_ref[:, c] for c in heads]
    vs = [v_ref[:, c] for c in heads]
    for h in range(H_B):
        ck_ref[h] = ks[h].astype(F32)
        cv_ref[h] = vs[h].astype(F32)
    qb = [q_ref[:, c] * (DQK_B ** -0.5) for c in heads]
    kb = ks
    probs = []
    for lo in (0, DQK_B):
        s = [lax.dot_general(q[:, lo:lo + DQK_B], k[:, lo:lo + DQK_B], NT_DIMS, preferred_element_type=F32)
             for q, k in zip(qb, kb)]
        e = [jnp.exp(x - jnp.max(x, axis=-1, keepdims=True)) for x in s]
        probs.append([x * (1.0 / jnp.sum(x, axis=-1, keepdims=True)) for x in e])
    o = [jnp.dot((p1 - lam * p2).astype(BF16), v, preferred_element_type=F32)
         for p1, p2, v in zip(probs[0], probs[1], vs)]
    for c, oh in zip(heads, o):
        o_ref[:, c] = _subln(oh, sub_ref, lam_init).astype(BF16)


def _attn_ctx(z, lam, subln, n_seq, t_len, lam_init):
    n_tok = n_seq * t_len
    w = H_B * LANES
    blk = lambda col0: pl.BlockSpec((t_len, w), lambda b: (b, col0 // w))
    cache_blk = pl.BlockSpec((None, None, H_B, t_len, LANES), lambda b: (b, 0, 0, 0, 0))
    cache_shape = jax.ShapeDtypeStruct((n_seq, 1, H_B, t_len, LANES), F32)
    return pl.pallas_call(
        functools.partial(_attn_ctx_body, lam_init),
        grid=(n_seq,),
        in_specs=[blk(Z_QB), blk(Z_KB), blk(Z_VB),
                  pl.BlockSpec((1, LANES), lambda b: (0, 0)),
                  pl.BlockSpec((1, LANES), lambda b: (0, 0))],
        out_specs=[pl.BlockSpec((t_len, w), lambda b: (b, 0)), cache_blk, cache_blk],
        out_shape=[jax.ShapeDtypeStruct((n_tok, H_B * DV_B), BF16), cache_shape, cache_shape],
        compiler_params=_params("arbitrary"),
        name="attn_ctx",
    )(z, z, z, lam, subln)


def _rope(x, cos, sin_signed):
    lane = lax.broadcasted_iota(jnp.int32, (1, LANES), 1)
    first = (lane % 32) < 16
    partner = jnp.where(first, pltpu.roll(x, LANES - 16, 1), pltpu.roll(x, 16, 1))
    return x * cos + partner * sin_signed


LAT_HEADS = 2


def _attn_lat_body(lam_init, n_past, q_ref, k_ref, v_ref, pk_ref, pv_ref, cosq_ref, sinq_ref, cos_ref, sin_ref,
                   lam_ref, sub_ref, o_ref, keys, vals):
    heads = [slice(h * LANES, (h + 1) * LANES) for h in range(LAT_HEADS)]

    @pl.when(pl.program_id(2) == 0)
    def _():
        for h, c in enumerate(heads):
            keys[h, 0:n_past, :] = pk_ref[h].astype(BF16)
            vals[h, 0:n_past, :] = pv_ref[h].astype(BF16)
            keys[h, n_past:, :] = _rope(k_ref[:, c].astype(F32), cos_ref[...], sin_ref[...]).astype(BF16)
            vals[h, n_past:, :] = v_ref[:, c]

    lam = lam_ref[0:1, 0:1]
    qb = [(_rope(q_ref[:, c].astype(F32), cosq_ref[...], sinq_ref[...]) * (DQK_B ** -0.5)).astype(BF16)
          for c in heads]
    maps = []
    for lo in (0, DQK_B):
        s = [lax.dot_general(q[:, lo:lo + DQK_B], keys[h, :, lo:lo + DQK_B], NT_DIMS, preferred_element_type=F32)
             for h, q in enumerate(qb)]
        e = [jnp.exp(x - jnp.max(x, axis=-1, keepdims=True)) for x in s]
        inv = [1.0 / jnp.sum(x, axis=-1, keepdims=True) for x in e]
        maps.append([jnp.dot(x.astype(BF16), vals[h], preferred_element_type=F32) * r
                     for h, (x, r) in enumerate(zip(e, inv))])
    for h, c in enumerate(heads):
        o_ref[:, c] = _subln(maps[0][h] - lam * maps[1][h], sub_ref, lam_init).astype(BF16)


def _attn_lat(z, cache_k, cache_v, cos, sin, lam, subln, n_seq, t_len, row0, lam_init):
    tq = ATTN_Q_TILE
    nq = t_len // tq
    n_past = cache_k.shape[3]
    rbq = row0 // tq
    rbs = row0 // t_len
    w = LAT_HEADS * LANES
    seq_blk = lambda col0: pl.BlockSpec((t_len, w), lambda b, h, qi: (rbs + b, col0 // w + h))
    past_blk = pl.BlockSpec((None, None, LAT_HEADS, n_past, LANES), lambda b, h, qi: (b, 0, h, 0, 0))
    row_vec = pl.BlockSpec((1, LANES), lambda b, h, qi: (0, 0))
    return pl.pallas_call(
        functools.partial(_attn_lat_body, lam_init, n_past),
        grid=(n_seq, H_B // LAT_HEADS, nq),
        in_specs=[pl.BlockSpec((tq, w), lambda b, h, qi: (rbq + b * nq + qi, Z_QB // w + h)),
                  seq_blk(Z_KB), seq_blk(Z_VB), past_blk, past_blk,
                  pl.BlockSpec((tq, LANES), lambda b, h, qi: (qi, 0)),
                  pl.BlockSpec((tq, LANES), lambda b, h, qi: (qi, 0)),
                  pl.BlockSpec((t_len, LANES), lambda b, h, qi: (0, 0)),
                  pl.BlockSpec((t_len, LANES), lambda b, h, qi: (0, 0)),
                  row_vec, row_vec],
        out_specs=pl.BlockSpec((tq, w), lambda b, h, qi: (b * nq + qi, h)),
        out_shape=jax.ShapeDtypeStruct((n_seq * t_len, H_B * DV_B), BF16),
        scratch_shapes=[pltpu.VMEM((LAT_HEADS, n_past + t_len, LANES), BF16)] * 2,
        compiler_params=_params("arbitrary", "arbitrary", "arbitrary"),
        name="attn_lat",
    )(z, z, z, cache_k, cache_v, cos, sin, cos, sin, lam, subln)


def _rope_tables(t_len):
    t = np.arange(t_len)
    pos = np.stack([t // GRID_W, t % GRID_W], axis=1).astype(np.float32)
    nf = DQK_B // 4
    inv_freq = np.float32(ROPE_BASE) ** (-np.arange(nf, dtype=np.float32) / np.float32(nf))
    lane = np.arange(LANES)
    half = (lane % DQK_B) // (DQK_B // 2)
    ang = (pos[:, half] * inv_freq[lane % nf][None, :]).astype(np.float32)
    sign = np.where((lane % (DQK_B // 2)) < nf, -1.0, 1.0).astype(np.float32)
    return jnp.asarray(np.cos(ang), F32), jnp.asarray(np.sin(ang) * sign[None, :], F32)


def _merge_body(nct, oac_ref, oal_ref, ga_ref, on_ref, obc_ref, obl_ref, wa_f32, wb_f32, *rest):
    gm_refs, m_ref, wa_ref, wb_ref = rest[:-3], rest[-3], rest[-2], rest[-1]
    i = pl.program_id(0)
    n_blk = len(gm_refs) // 2

    @pl.when(i == 0)
    def _():
        wa_ref[...] = wa_f32[...].astype(BF16)
        wb_ref[...] = wb_f32[...].astype(BF16)

    def run(oa_ref, ob_ref):
        a = jnp.concatenate(
            [(_rms_scale(oa_ref[:, c]) * on_ref[...] * _silu(ga_ref[:, c].astype(F32))).astype(BF16)
             for c in (slice(h * DV_A, (h + 1) * DV_A) for h in range(H_A))], axis=1)
        ya = jnp.dot(a, wa_ref[...], preferred_element_type=F32)
        yb = jnp.dot(ob_ref[...], wb_ref[...], preferred_element_type=F32)
        tn = gm_refs[0].shape[1]
        for j in range(n_blk):
            c = slice(j * tn, (j + 1) * tn)
            m_ref[:, c] = (jax.nn.sigmoid(gm_refs[j][...].astype(F32)) * ya[:, c]
                           + jax.nn.sigmoid(gm_refs[n_blk + j][...].astype(F32)) * yb[:, c]).astype(BF16)

    @pl.when(i < nct)
    def _():
        run(oac_ref, obc_ref)

    @pl.when(i >= nct)
    def _():
        run(oal_ref, obl_ref)


def _merge(rows, z, oa_c, oa_l, ob_c, ob_l, onorm, w_up_a, w_up_b):
    n_tok = z.shape[0]
    tm = rows.tm
    ka, d = w_up_a.shape
    tn = 1024
    nj = d // tn
    ctx_blk = pl.BlockSpec((tm, ka), lambda i: (rows.ctx_idx(i), 0))
    lat_blk = pl.BlockSpec((tm, ka), lambda i: (rows.lat_idx(i), 0))
    weight = pl.BlockSpec((ka, d), lambda i: (0, 0), pipeline_mode=pl.Buffered(1))
    gate_cols = [pl.BlockSpec((tm, tn), functools.partial(lambda i, c: (i, c), c=Z_GM // tn + j)) for j in range(2 * nj)]
    return pl.pallas_call(
        functools.partial(_merge_body, rows.nct),
        grid=(n_tok // tm,),
        in_specs=[ctx_blk, lat_blk,
                  pl.BlockSpec((tm, ka), lambda i: (i, Z_GATE_A // ka)),
                  pl.BlockSpec((1, DV_A), lambda i: (0, 0)),
                  ctx_blk, lat_blk, weight, weight] + gate_cols,
        out_specs=pl.BlockSpec((tm, d), lambda i: (i, 0)),
        out_shape=jax.ShapeDtypeStruct((n_tok, d), BF16),
        scratch_shapes=[pltpu.VMEM((ka, d), BF16)] * 2,
        compiler_params=_params("arbitrary"),
        name="merge",
    )(oa_c, oa_l, z, onorm, ob_c, ob_l, w_up_a, w_up_b, *([z] * (2 * nj)))


ROUTE_E = 0
ROUTE_W = TOP_K


def _route_rows(lg):
    lane = lax.broadcasted_iota(jnp.int32, lg.shape, 1)
    neg = -jnp.inf

    def first_max(x):
        m = jnp.max(x, axis=1, keepdims=True)
        return m, jnp.min(jnp.where(x == m, lane, LANES), axis=1, keepdims=True)

    gl = jnp.where(lane < N_GROUPS, lg, neg)
    gmax, g_idx = first_max(gl)
    pg_top = 1.0 / jnp.sum(jnp.exp(gl - gmax), axis=1, keepdims=True)
    lo = N_GROUPS + E_PER_GROUP * g_idx
    el = jnp.where(jnp.logical_and(lane >= lo, lane < lo + E_PER_GROUP), lg, neg)
    emax, i1 = first_max(el)
    esum = jnp.sum(jnp.exp(el - emax), axis=1, keepdims=True)
    e2max, i2 = first_max(jnp.where(lane == i1, neg, el))
    p1 = 1.0 / esum
    p2 = jnp.exp(e2max - emax) / esum
    den = p1 + p2
    vals = [(i1 - N_GROUPS).astype(F32), (i2 - N_GROUPS).astype(F32), pg_top * p1 / den, pg_top * p2 / den]
    out = jnp.zeros(lg.shape, F32)
    for pos, val in enumerate(vals):
        out = jnp.where(lane == pos, val, out)
    return out


def _outproj_body(nct, m_ref, wo_ref, xp_ref, xs_ref, gt_ref, g2_ref, sh_ref, sc_ref, wr_ref, br_ref,
                  x1_ref, h2_ref, rt_ref):
    i = pl.program_id(0)
    half = m_ref.shape[0] // 2

    def finish(x_ref):
        for r in (slice(0, half), slice(half, 2 * half)):
            y = jnp.dot(m_ref[r, :], wo_ref[...], preferred_element_type=F32)
            x1 = x_ref[r, :] + gt_ref[0] * y
            x1_ref[r, :] = x1
            h2 = _rms_scale(x1) * g2_ref[...]
            h2 = h2 * (1.0 + sc_ref[0]) + sh_ref[0]
            h2_ref[r, :] = h2
            hi = h2.astype(BF16)
            lo = (h2 - hi.astype(F32)).astype(BF16)
            p_hi = jnp.dot(hi, wr_ref[...], preferred_element_type=F32)
            p_lo = jnp.dot(lo, wr_ref[...], preferred_element_type=F32)
            lg = p_hi[:, :LANES] + p_hi[:, LANES:] + p_lo[:, :LANES] + p_lo[:, LANES:] + br_ref[...]
            rt_ref[r, :] = _route_rows(lg)

    @pl.when(i < nct)
    def _():
        finish(xp_ref)

    @pl.when(i >= nct)
    def _():
        finish(xs_ref)


def _outproj(rows, mixed, w_o, xp, xs, gt1, gain2, sh2, sc2, w_r, b_r):
    d = xp.shape[1]
    tm = rows.tm
    n_tok = rows.n * tm
    mod = lambda: pl.BlockSpec((1, 1, d), lambda i: (rows.mod_idx(i), 0, 0))
    tok = pl.BlockSpec((tm, d), lambda i: (i, 0))
    return pl.pallas_call(
        functools.partial(_outproj_body, rows.nct),
        grid=(rows.n,),
        in_specs=[tok,
                  pl.BlockSpec((d, d), lambda i: (0, 0)),
                  pl.BlockSpec((tm, d), lambda i: (rows.ctx_idx(i), 0)),
                  pl.BlockSpec((tm, d), lambda i: (rows.lat_idx(i), 0)),
                  mod(),
                  pl.BlockSpec((1, d), lambda i: (0, 0)),
                  mod(), mod(),
                  pl.BlockSpec((d, 2 * LANES), lambda i: (0, 0)),
                  pl.BlockSpec((1, LANES), lambda i: (0, 0))],
        out_specs=[tok, tok, pl.BlockSpec((tm, LANES), lambda i: (i, 0))],
        out_shape=[jax.ShapeDtypeStruct((n_tok, d), F32), jax.ShapeDtypeStruct((n_tok, d), F32),
                   jax.ShapeDtypeStruct((n_tok, LANES), F32)],
        compiler_params=_params("arbitrary"),
        name="outproj",
    )(mixed, w_o, xp, xs, gt1, gain2, sh2, sc2, w_r, b_r)


DISPATCH_BLOCK = 256
TBL_EXPERT, TBL_VALID, TBL_COUNT, TBL_OFFSET = 0, 1, 2, 3


def _dispatch_body(n_tok, route_ref, pos_ref, tbl_ref):
    nb = n_tok // DISPATCH_BLOCK
    lane = lax.broadcasted_iota(jnp.int32, (1, LANES), 1)
    lane_f = lane.astype(F32)

    def one_hot(b, k):
        r = pl.ds(pl.multiple_of(b * DISPATCH_BLOCK, DISPATCH_BLOCK), DISPATCH_BLOCK)
        return jnp.where(route_ref[r, ROUTE_E + k:ROUTE_E + k + 1] == lane_f, 1.0, 0.0)

    def count(b, acc):
        return acc + jnp.sum(one_hot(b, 0) + one_hot(b, 1), axis=0, keepdims=True)

    counts = lax.fori_loop(0, nb, count, jnp.zeros((1, LANES), F32))
    padded = jnp.floor((counts + (MOE_TILE - 1)) * (1.0 / MOE_TILE)) * MOE_TILE
    pad_end = padded
    shift = 1
    while shift < LANES:
        pad_end = pad_end + jnp.where(lane >= shift, pltpu.roll(pad_end, shift, 1), 0.0)
        shift *= 2
    pad_off = pad_end - padded

    ri = lax.broadcasted_iota(jnp.int32, (DISPATCH_BLOCK, DISPATCH_BLOCK), 0)
    ci = lax.broadcasted_iota(jnp.int32, (DISPATCH_BLOCK, DISPATCH_BLOCK), 1)
    before = jnp.where(ci < ri, 1.0, 0.0).astype(BF16)

    def place(b, run):
        oh = [one_hot(b, k) for k in range(TOP_K)]
        base = pad_off + run
        out = jnp.zeros((DISPATCH_BLOCK, LANES), F32)
        lane_b = lax.broadcasted_iota(jnp.int32, (DISPATCH_BLOCK, LANES), 1)
        for k in range(TOP_K):
            prior = jnp.dot(before, oh[k].astype(BF16), preferred_element_type=F32)
            pos = jnp.sum(oh[k] * (base + prior), axis=1, keepdims=True)
            out = jnp.where(lane_b == k, pos, out)
            base = base + jnp.sum(oh[k], axis=0, keepdims=True)
        r = pl.ds(pl.multiple_of(b * DISPATCH_BLOCK, DISPATCH_BLOCK), DISPATCH_BLOCK)
        pos_ref[r, :] = out.astype(jnp.int32)
        return base - pad_off

    lax.fori_loop(0, nb, place, jnp.zeros((1, LANES), F32))

    end_col = jnp.transpose(jnp.broadcast_to(pad_end, (8, LANES)))[:, 0:1]
    e_col = lax.broadcasted_iota(jnp.int32, (LANES, 1), 0)
    tile_start = lane_f * MOE_TILE
    passed = jnp.where(jnp.logical_and(end_col <= tile_start, e_col < N_EXPERTS), 1.0, 0.0)
    tile_expert = jnp.minimum(jnp.sum(passed, axis=0, keepdims=True), N_EXPERTS - 1.0)
    total = jnp.sum(jnp.where(lane == N_EXPERTS - 1, pad_end, 0.0), axis=1, keepdims=True)
    tile_valid = jnp.where(tile_start < total, 1.0, 0.0)
    row = lax.broadcasted_iota(jnp.int32, (8, LANES), 0)
    tbl = jnp.zeros((8, LANES), F32)
    for idx, val in ((TBL_EXPERT, tile_expert), (TBL_VALID, tile_valid), (TBL_COUNT, counts), (TBL_OFFSET, pad_off)):
        tbl = jnp.where(row == idx, val, tbl)
    tbl_ref[...] = tbl.astype(jnp.int32)


def _dispatch(route):
    n_tok = route.shape[0]
    return pl.pallas_call(
        functools.partial(_dispatch_body, n_tok),
        out_shape=[jax.ShapeDtypeStruct((n_tok, LANES), jnp.int32), jax.ShapeDtypeStruct((8, LANES), jnp.int32)],
        compiler_params=pltpu.CompilerParams(vmem_limit_bytes=VMEM_LIMIT),
        name="dispatch",
    )(route)


def _scatter_body(tm, n_tiles, pos_ref, cnt_ref, off_ref, h_ref, o_hbm, zbuf, sem):
    i = pl.program_id(0)

    @pl.when(i == 0)
    def _():
        zbuf[...] = jnp.zeros_like(zbuf)
        last = N_EXPERTS - 1
        first_empty = (off_ref[last] + cnt_ref[last] + MOE_TILE - 1) // MOE_TILE

        def tile_copy(t):
            rows = pl.ds(pl.multiple_of(t * MOE_TILE, MOE_TILE), MOE_TILE)
            return pltpu.make_async_copy(zbuf, o_hbm.at[rows, :], sem.at[1])

        def partial_tile(e, start):
            cnt = cnt_ref[e]

            @pl.when(cnt % MOE_TILE != 0)
            def _():
                cp = tile_copy((off_ref[e] + cnt) // MOE_TILE)
                if start:
                    cp.start()
                else:
                    cp.wait()

        def zero_partial(e, carry):
            partial_tile(e, True)
            return carry

        def partial_done(e, carry):
            partial_tile(e, False)
            return carry

        def zero_tile(t, carry):
            tile_copy(t).start()
            return carry

        def zero_done(t, carry):
            tile_copy(t).wait()
            return carry

        lax.fori_loop(0, N_EXPERTS, zero_partial, 0)
        lax.fori_loop(first_empty, n_tiles, zero_tile, 0)
        lax.fori_loop(0, N_EXPERTS, partial_done, 0)
        lax.fori_loop(first_empty, n_tiles, zero_done, 0)

    def row_copies(r):
        return [pltpu.make_async_copy(h_ref.at[pl.ds(r, 1), :],
                                      o_hbm.at[pl.ds(pos_ref[(i * tm + r) * TOP_K + k], 1), :], sem.at[0])
                for k in range(TOP_K)]

    def start(r, carry):
        for k, cp in enumerate(row_copies(r)):
            cp.start(priority=k)
        return carry

    def wait(r, carry):
        for cp in row_copies(r):
            cp.wait()
        return carry

    lax.fori_loop(0, tm, start, 0, unroll=8)
    lax.fori_loop(0, tm, wait, 0, unroll=8)


def _scatter_rows(h2, pos_flat, counts, offsets, n_rows):
    n_tok, d = h2.shape
    tm = SCATTER_TILE
    grid_spec = pltpu.PrefetchScalarGridSpec(
        num_scalar_prefetch=3,
        grid=(n_tok // tm,),
        in_specs=[pl.BlockSpec((tm, d), lambda i, pos, cnt, off: (i, 0))],
        out_specs=pl.BlockSpec(memory_space=pl.ANY),
        scratch_shapes=[pltpu.VMEM((MOE_TILE, d), F32), pltpu.SemaphoreType.DMA((2,))],
    )
    return pl.pallas_call(
        functools.partial(_scatter_body, tm, n_rows // MOE_TILE),
        grid_spec=grid_spec,
        out_shape=jax.ShapeDtypeStruct((n_rows, d), F32),
        compiler_params=_params("arbitrary"),
        name="scatter_rows",
    )(pos_flat, counts, offsets, h2)


def _moe_body(n_tiles, te_ref, tv_ref, cnt_ref, x_ref, wg_hbm, wu_hbm, wd_hbm, y_ref,
              wg_f, wu_f, wd_f, wg_b, wu_b, wd_b, slot_ref, sem):
    i = pl.program_id(0)
    prev = jnp.maximum(i - 1, 0)
    valid = tv_ref[i] == 1
    e = te_ref[i]

    def fetch(expert, slot):
        return [pltpu.make_async_copy(src.at[expert], dst.at[slot], sem.at[slot])
                for src, dst in ((wg_hbm, wg_f), (wu_hbm, wu_f), (wd_hbm, wd_f))]

    @pl.when(jnp.logical_and(valid, i == 0))
    def _():
        slot_ref[0] = 1
        for cp in fetch(e, 0):
            cp.start()

    @pl.when(jnp.logical_and(valid, jnp.logical_or(i == 0, e != te_ref[prev])))
    def _():
        slot = 1 - slot_ref[0]
        slot_ref[0] = slot
        for cp in fetch(e, slot):
            cp.wait()
        nxt = jnp.minimum(i + (cnt_ref[e] + MOE_TILE - 1) // MOE_TILE, n_tiles - 1)

        @pl.when(jnp.logical_and(nxt > i, tv_ref[nxt] == 1))
        def _():
            for cp in fetch(te_ref[nxt], 1 - slot):
                cp.start()

        wg_b[...] = wg_f[slot].astype(BF16)
        wu_b[...] = wu_f[slot].astype(BF16)
        wd_b[...] = wd_f[slot].astype(BF16)

    @pl.when(valid)
    def _():
        x = x_ref[...].astype(BF16)
        g = jnp.dot(x, wg_b[...], preferred_element_type=F32)
        u = jnp.dot(x, wu_b[...], preferred_element_type=F32)
        y_ref[...] = jnp.dot((_silu(g) * u).astype(BF16), wd_b[...], preferred_element_type=F32)

    @pl.when(tv_ref[i] == 0)
    def _():
        y_ref[...] = jnp.zeros_like(y_ref)


def _moe(x_sorted, tile_expert, tile_valid, counts, w_g, w_u, w_d):
    n_tiles = tile_expert.shape[0]
    d = x_sorted.shape[1]
    f = w_g.shape[-1]
    hbm = pl.BlockSpec(memory_space=pl.ANY)
    grid_spec = pltpu.PrefetchScalarGridSpec(
        num_scalar_prefetch=3,
        grid=(n_tiles,),
        in_specs=[pl.BlockSpec((MOE_TILE, d), lambda i, te, tv, cnt: (i * tv[i], 0)), hbm, hbm, hbm],
        out_specs=pl.BlockSpec((MOE_TILE, d), lambda i, te, tv, cnt: (i, 0)),
        scratch_shapes=[pltpu.VMEM((2, d, f), F32), pltpu.VMEM((2, d, f), F32), pltpu.VMEM((2, f, d), F32),
                        pltpu.VMEM((d, f), BF16), pltpu.VMEM((d, f), BF16), pltpu.VMEM((f, d), BF16),
                        pltpu.SMEM((1,), jnp.int32), pltpu.SemaphoreType.DMA((2,))],
    )
    return pl.pallas_call(
        functools.partial(_moe_body, n_tiles),
        grid_spec=grid_spec,
        out_shape=jax.ShapeDtypeStruct((n_tiles * MOE_TILE, d), F32),
        compiler_params=_params("arbitrary"),
        name="moe",
    )(tile_expert, tile_valid, counts, x_sorted, w_g, w_u, w_d)


def _combine_body(tm, n_tiles, row0, pos_ref, y_hbm, rt_ref, x1_ref, gt_ref, gf_ref, o_ref, ybuf, sem):
    i = pl.program_id(0)
    slot = i % 2

    def gather(tile, dst_slot, start):
        def row(r, carry):
            for k in range(TOP_K):
                p = pos_ref[(row0 + tile * tm + r) * TOP_K + k]
                cp = pltpu.make_async_copy(y_hbm.at[pl.ds(p, 1), :], ybuf.at[dst_slot, k, pl.ds(r, 1), :],
                                           sem.at[dst_slot])
                if start:
                    cp.start(priority=k)
                else:
                    cp.wait()
            return carry

        lax.fori_loop(0, tm, row, 0, unroll=8)

    @pl.when(i == 0)
    def _():
        gather(0, 0, True)

    @pl.when(i + 1 < n_tiles)
    def _():
        gather(i + 1, 1 - slot, True)

    gather(i, slot, False)
    rt = rt_ref[...]
    y = rt[:, ROUTE_W:ROUTE_W + 1] * ybuf[slot, 0] + rt[:, ROUTE_W + 1:ROUTE_W + 2] * ybuf[slot, 1]
    x2 = x1_ref[...] + gt_ref[0] * y
    o_ref[...] = _rms_scale(x2) * gf_ref[...]


def _combine(y_sorted, pos_flat, route, x1, gt2, gain_f, row0, n_rows, mod_of_tile, tm):
    d = x1.shape[1]
    rb = row0 // tm
    n_tiles = n_rows // tm
    grid_spec = pltpu.PrefetchScalarGridSpec(
        num_scalar_prefetch=1,
        grid=(n_tiles,),
        in_specs=[pl.BlockSpec(memory_space=pl.ANY),
                  pl.BlockSpec((tm, LANES), lambda i, pos: (rb + i, 0)),
                  pl.BlockSpec((tm, d), lambda i, pos: (rb + i, 0)),
                  pl.BlockSpec((1, 1, d), lambda i, pos: (mod_of_tile(i), 0, 0)),
                  pl.BlockSpec((1, d), lambda i, pos: (0, 0))],
        out_specs=pl.BlockSpec((tm, d), lambda i, pos: (i, 0)),
        scratch_shapes=[pltpu.VMEM((2, TOP_K, tm, d), F32), pltpu.SemaphoreType.DMA((2,))],
    )
    return pl.pallas_call(
        functools.partial(_combine_body, tm, n_tiles, row0),
        grid_spec=grid_spec,
        out_shape=jax.ShapeDtypeStruct((n_rows, d), F32),
        compiler_params=_params("arbitrary"),
        name="combine",
    )(pos_flat, y_sorted, route, x1, gt2, gain_f)


def kernel(x_prompt, x_sample, c, state_delta, cache_k, cache_v, c_ctx, w_ada, b_ada, norm_mix, norm_ffn, w_in,
           conv_a, a_log, dt_bias, onorm_a, lam, subln_b, w_up_a, w_up_b, w_o, w_rg, b_rg, w_re, b_re,
           w_e_gate, w_e_up, w_e_down, norm_final):
    assert w_in.shape[0] == 1, "single trunk layer"
    l = 0
    lam_init = 0.8 - 0.6 * math.exp(-0.3 * l)
    bc, tc, d = x_prompt.shape
    bl, tl, _ = x_sample.shape
    n_ctx, n_lat = bc * tc, bl * tl
    xp = x_prompt.reshape(n_ctx, d)
    xs = x_sample.reshape(n_lat, d)

    cvec = jnp.zeros((N_MOD_ROWS, d), F32).at[0].set(c_ctx).at[1:1 + bl].set(c)
    mod = _adaln(cvec, w_ada[l], b_ada[l][None, :])
    sh1, sc1, gt1, sh2, sc2, gt2 = [m.reshape(N_MOD_ROWS, 1, d) for m in jnp.split(mod, 6, axis=1)]

    rows = _Rows(n_ctx, n_lat, tl, ROW_TILE)
    w_in_t = jnp.swapaxes(w_in[l], 0, 1)
    h1, gates = _prenorm(rows, xp, xs, norm_mix[l][None, :], sh1, sc1, w_in_t, a_log[l], dt_bias[l])
    z = _inproj(h1, w_in_t)

    oa_c, new_state = _deltanet(z, conv_a[l], gates, None, bc, tc, 0, DELTA_HEADS_CTX)
    oa_l = _deltanet(z, conv_a[l], gates, state_delta, bl, tl, n_ctx, DELTA_HEADS_LAT)

    lv = lam[l]
    lam_val = jnp.exp(jnp.sum(lv[0] * lv[1])) - jnp.exp(jnp.sum(lv[2] * lv[3])) + lam_init
    lam_row = jnp.full((1, LANES), lam_val, F32)
    sub_row = subln_b[l][None, :]
    ob_c, new_k, new_v = _attn_ctx(z, lam_row, sub_row, bc, tc, lam_init)
    cos, sin = _rope_tables(tl)
    ob_l = _attn_lat(z, cache_k, cache_v, cos, sin, lam_row, sub_row, bl, tl, n_ctx, lam_init)

    mixed = _merge(_Rows(n_ctx, n_lat, tl, MERGE_TILE), z, oa_c, oa_l, ob_c, ob_l, onorm_a[l][None, :],
                   w_up_a[l], w_up_b[l])
    w_r = jnp.pad(jnp.concatenate([w_rg[l], w_re[l]], axis=1), ((0, 0), (0, LANES - N_GROUPS - N_EXPERTS)))
    w_r_hi = w_r.astype(BF16)
    w_r = jnp.concatenate([w_r_hi, (w_r - w_r_hi.astype(F32)).astype(BF16)], axis=1)
    b_r = jnp.pad(jnp.concatenate([b_rg[l], b_re[l]]), (0, LANES - N_GROUPS - N_EXPERTS))[None, :]
    x1, h2, route = _outproj(rows, mixed, w_o[l].astype(BF16), xp, xs, gt1, norm_ffn[l][None, :], sh2, sc2, w_r, b_r)

    n_tok = n_ctx + n_lat
    n_tiles = (n_tok * TOP_K) // MOE_TILE + N_EXPERTS
    pos, tbl = _dispatch(route)
    pos_flat = pos[:, :TOP_K].reshape(-1)
    counts, offsets = tbl[TBL_COUNT, :N_EXPERTS], tbl[TBL_OFFSET, :N_EXPERTS]
    x_sorted = _scatter_rows(h2, pos_flat, counts, offsets, n_tiles * MOE_TILE)
    y_sorted = _moe(x_sorted, tbl[TBL_EXPERT, :n_tiles], tbl[TBL_VALID, :n_tiles], counts,
                    w_e_gate[l], w_e_up[l], w_e_down[l])
    gf = norm_final[None, :]
    tm_c = COMBINE_TILE
    y_prompt = _combine(y_sorted, pos_flat, route, x1, gt2, gf, 0, n_ctx, lambda i: 0 * i, tm_c)
    y_sample = _combine(y_sorted, pos_flat, route, x1, gt2, gf, n_ctx, n_lat, lambda i: 1 + i // (tl // tm_c), tm_c)
    return (y_prompt.reshape(bc, tc, d), y_sample.reshape(bl, tl, d), new_state, new_k, new_v)
```

```python
import functools
import math

import jax
import jax.numpy as jnp
import numpy as np
from jax import lax
from jax.experimental import pallas as pl
from jax.experimental.pallas import tpu as pltpu

F32 = jnp.float32
BF16 = jnp.bfloat16

D_MODEL = 2048
GRID_W = 64
H_A = 8
DK_A = 128
DV_A = 128
CONV_K = 3
CHUNK = 64
H_B = 8
DQK_B = 64
DV_B = 2 * DQK_B
ROPE_BASE = 10000.0
N_GROUPS = 4
E_PER_GROUP = 8
N_EXPERTS = N_GROUPS * E_PER_GROUP
TOP_K = 2
D_FF_E = D_MODEL // 4
EPS = 1e-6

LANES = 128
QKV_A = 2 * H_A * DK_A + H_A * DV_A
N_SMALL = 4 * H_A
Z_GATE_A = QKV_A
Z_QB = Z_GATE_A + H_A * DV_A
Z_KB = Z_QB + H_B * 2 * DQK_B
Z_VB = Z_KB + H_B * 2 * DQK_B
Z_GM = Z_VB + H_B * DV_B
N_MAIN = Z_GM + 2 * D_MODEL

N_MOD_ROWS = 8
VMEM_LIMIT = 60 * 1024 * 1024
ROW_TILE = 512
INPROJ_ROWS = 2048
MERGE_TILE = 256
SCATTER_TILE = 512
COMBINE_TILE = 256
MOE_TILE = 256
ATTN_Q_TILE = 256
DELTA_HEADS_CTX = 8
DELTA_HEADS_LAT = 8
DELTA_SINGLE_BUFFER_ELEMS = 1 << 20

NT_DIMS = (((1,), (1,)), ((), ()))


def _params(*sem):
    return pltpu.CompilerParams(dimension_semantics=sem, vmem_limit_bytes=VMEM_LIMIT)


def _mm(a, b):
    return jnp.dot(a.astype(BF16), b.astype(BF16), preferred_element_type=F32)


def _mm_nt(a, b):
    return lax.dot_general(a.astype(BF16), b.astype(BF16), NT_DIMS, preferred_element_type=F32)


def _silu(x):
    return x * jax.nn.sigmoid(x)


def _rms_scale(x):
    return x * lax.rsqrt(jnp.mean(x * x, axis=-1, keepdims=True) + EPS)


def _adaln_body(c_ref, w_ref, b_ref, o_ref):
    s = _silu(c_ref[...])
    o_ref[...] = _mm(s, w_ref[...]) + b_ref[...]


def _adaln(cvec, w, b):
    d, n = w.shape
    tn = 1024
    return pl.pallas_call(
        _adaln_body,
        grid=(n // tn,),
        in_specs=[pl.BlockSpec((N_MOD_ROWS, d), lambda j: (0, 0)),
                  pl.BlockSpec((d, tn), lambda j: (0, j)),
                  pl.BlockSpec((1, tn), lambda j: (0, j))],
        out_specs=pl.BlockSpec((N_MOD_ROWS, tn), lambda j: (0, j)),
        out_shape=jax.ShapeDtypeStruct((N_MOD_ROWS, n), F32),
        compiler_params=_params("arbitrary"),
        name="adaln",
    )(cvec, w, b)


class _Rows:
    def __init__(self, n_ctx, n_lat, t_lat, tm):
        assert n_ctx % tm == 0 and n_lat % tm == 0 and t_lat % tm == 0
        self.tm = tm
        self.nct = n_ctx // tm
        self.nlt = n_lat // tm
        self.per_seq = t_lat // tm
        self.n = self.nct + self.nlt

    def ctx_idx(self, i):
        return jnp.minimum(i, self.nct - 1)

    def lat_idx(self, i):
        return jnp.maximum(i - self.nct, 0)

    def mod_idx(self, i):
        return jnp.where(i < self.nct, 0, 1 + (i - self.nct) // self.per_seq)


SMALL_COL0 = QKV_A + H_A * DV_A
GATE_KINDS = 4
GATE_BETA, GATE_GC, GATE_EGC, GATE_EKD = (kind * 2 * H_A for kind in range(GATE_KINDS))


def _gate_rows(zs, a_log_row, dt_row):
    lane = lax.broadcasted_iota(jnp.int32, (1, LANES), 1)
    x = zs + dt_row
    softplus = jnp.maximum(x, 0.0) + jnp.log(1.0 + jnp.exp(-jnp.abs(x)))
    in_g = jnp.logical_and(lane >= GATE_GC, lane < GATE_EGC)
    g = jnp.where(in_g, -jnp.exp(a_log_row) * softplus, 0.0)
    beta = jax.nn.sigmoid(zs)
    reversed_lane = lane >= GATE_GC + H_A
    ri = lax.broadcasted_iota(jnp.int32, (CHUNK, CHUNK), 0)
    ci = lax.broadcasted_iota(jnp.int32, (CHUNK, CHUNK), 1)
    prefix = jnp.where(ci <= ri, 1.0, 0.0)
    suffix = jnp.where(ci >= ri, 1.0, 0.0)
    exact = dict(preferred_element_type=F32, precision=lax.Precision.HIGHEST)
    rows = []
    for c in range(zs.shape[0] // CHUNK):
        r = slice(c * CHUNK, (c + 1) * CHUNK)
        gch = g[r]
        gc = jnp.where(reversed_lane, jnp.dot(suffix, gch, **exact), jnp.dot(prefix, gch, **exact))
        tot = jnp.sum(gch, axis=0, keepdims=True)
        rows.append(jnp.where(lane < GATE_GC, beta[r],
                              jnp.where(lane < GATE_EGC, gc,
                                        jnp.where(lane < GATE_EKD, pltpu.roll(jnp.exp(gc), GATE_EGC - GATE_GC, 1),
                                                  jnp.where(lane < GATE_EKD + 2 * H_A,
                                                            pltpu.roll(jnp.exp(tot - gc), GATE_EKD - GATE_GC, 1),
                                                            0.0)))))
    return jnp.concatenate(rows, axis=0)


def _prenorm_body(nct, xp_ref, xs_ref, g_ref, sh_ref, sc_ref, ws_ref, al_ref, dt_ref, h_ref, gates_ref):
    i = pl.program_id(0)

    def run(x_ref):
        h = _rms_scale(x_ref[...]) * g_ref[...]
        h = (h * (1.0 + sc_ref[0]) + sh_ref[0]).astype(BF16)
        h_ref[...] = h
        zs = lax.dot_general(h, ws_ref[...].astype(BF16), NT_DIMS, preferred_element_type=F32)
        gates_ref[...] = _gate_rows(zs, al_ref[...], dt_ref[...])

    @pl.when(i < nct)
    def _():
        run(xp_ref)

    @pl.when(i >= nct)
    def _():
        run(xs_ref)


def _prenorm(rows, xp, xs, gain, sh, sc, w_in_t, a_log, dt_bias):
    place = lambda v: jnp.pad(v.reshape(1, 2 * H_A), ((0, 0), (GATE_GC, LANES - GATE_EGC)))
    d = xp.shape[1]
    tm = rows.tm
    n_tok = rows.n * tm
    return pl.pallas_call(
        functools.partial(_prenorm_body, rows.nct),
        grid=(rows.n,),
        in_specs=[pl.BlockSpec((tm, d), lambda i: (rows.ctx_idx(i), 0)),
                  pl.BlockSpec((tm, d), lambda i: (rows.lat_idx(i), 0)),
                  pl.BlockSpec((1, d), lambda i: (0, 0)),
                  pl.BlockSpec((1, 1, d), lambda i: (rows.mod_idx(i), 0, 0)),
                  pl.BlockSpec((1, 1, d), lambda i: (rows.mod_idx(i), 0, 0)),
                  pl.BlockSpec((LANES, d), lambda i: (SMALL_COL0 // LANES, 0)),
                  pl.BlockSpec((1, LANES), lambda i: (0, 0)),
                  pl.BlockSpec((1, LANES), lambda i: (0, 0))],
        out_specs=[pl.BlockSpec((tm, d), lambda i: (i, 0)),
                   pl.BlockSpec((tm, LANES), lambda i: (i, 0))],
        out_shape=[jax.ShapeDtypeStruct((n_tok, d), BF16),
                   jax.ShapeDtypeStruct((n_tok, LANES), F32)],
        compiler_params=_params("arbitrary"),
        name="prenorm",
    )(xp, xs, gain, sh, sc, w_in_t, place(a_log), place(dt_bias))


INPROJ_TN = 1024
N_ALIGNED = SMALL_COL0 // INPROJ_TN


def _inproj_body(h_ref, wt_ref, z_ref, w_scr):
    @pl.when(pl.program_id(1) == 0)
    def _():
        w_scr[...] = wt_ref[...].astype(BF16)

    z_ref[...] = lax.dot_general(h_ref[...], w_scr[...], NT_DIMS, preferred_element_type=F32).astype(BF16)


def _inproj(h, w_in_t):
    n_tok, d = h.shape
    tm, tn = INPROJ_ROWS, INPROJ_TN
    first_row = lambda j: pl.multiple_of(j * tn + N_SMALL * jnp.minimum(j // N_ALIGNED, 1), N_SMALL)
    return pl.pallas_call(
        _inproj_body,
        grid=(N_MAIN // tn, n_tok // tm),
        in_specs=[pl.BlockSpec((tm, d), lambda j, i: (i, 0)),
                  pl.BlockSpec((pl.Element(tn), pl.Element(d)), lambda j, i: (first_row(j), 0))],
        out_specs=pl.BlockSpec((tm, tn), lambda j, i: (i, j)),
        out_shape=jax.ShapeDtypeStruct((n_tok, N_MAIN), BF16),
        scratch_shapes=[pltpu.VMEM((tn, d), BF16)],
        compiler_params=_params("arbitrary", "arbitrary"),
        name="inproj",
    )(h, w_in_t)


INV_BLOCK = 16


def _chunk_solve(chains, ii, jj):
    eye = ii == jj
    blk = (ii // INV_BLOCK) == (jj // INV_BLOCK)
    blk2 = (ii // (2 * INV_BLOCK)) == (jj // (2 * INV_BLOCK))
    cols, a_qk, a0, off, d0, rhs = [], [], [], [], [], []
    for kk, qk, v, k, gates, reverse, lane in chains:
        beta, gc, egc, ekd = (gates[:, b + lane:b + lane + 1] for b in (GATE_BETA, GATE_GC, GATE_EGC, GATE_EKD))
        incl = (jj >= ii) if reverse else (jj <= ii)
        strict = (jj > ii) if reverse else (jj < ii)
        gc_row = jnp.sum(jnp.where(eye, gc, 0.0), axis=0, keepdims=True)
        dec = jnp.where(incl, jnp.exp(jnp.where(incl, gc - gc_row, 0.0)), 0.0)
        a = jnp.where(strict, kk * dec, 0.0)
        cols.append((egc, ekd))
        a_qk.append(qk * dec)
        a0.append(jnp.where(blk, a, 0.0))
        off.append(jnp.where(blk, 0.0, a))
        d0.append(jnp.where(eye, 1.0, 0.0) - a0[-1])
        rhs.append(jnp.concatenate([v * beta, k * (beta * egc)], axis=1))
    p = [_mm(x, x) for x in a0]
    for _ in range(int(math.log2(INV_BLOCK)) - 2):
        r = [_mm(jnp.concatenate([pi, di], axis=0), pi) for pi, di in zip(p, d0)]
        p = [ri[:CHUNK] for ri in r]
        d0 = [di + ri[CHUNK:] for di, ri in zip(d0, r)]
    d0 = [di + _mm(di, pi) for di, pi in zip(d0, p)]
    wl = [_mm(di, oi) for di, oi in zip(d0, off)]
    yr = [_mm(di, ri) for di, ri in zip(d0, rhs)]
    x1 = [jnp.where(blk2, x, 0.0) for x in wl]
    yl = [jnp.where(blk2, 0.0, x) for x in wl]
    zl = [y - _mm(x, y) for x, y in zip(x1, yl)]
    zr = [y - _mm(x, y) for x, y in zip(x1, yr)]
    sol = [y - _mm(x, y) for x, y in zip(zl, zr)]
    return [(so[:, :DV_A], so[:, DV_A:], aq, egc, ekd) for so, aq, (egc, ekd) in zip(sol, a_qk, cols)]


def _delta_body(t_len, hp, has_s0, q_ref, k_ref, v_ref, cq_ref, ck_ref, cv_ref, g_ref, *rest):
    if has_s0:
        s0_ref, o_ref, qs, ks, vs, g_scr, u_scr, wq_scr, ak_scr, s_scr = rest
    else:
        o_ref, sfin_ref, qs, ks, vs, g_scr, u_scr, wq_scr, ak_scr, s_scr = rest
    n = t_len // CHUNK
    tpos = lax.broadcasted_iota(jnp.int32, (t_len, 1), 0)

    def conv_act(x, w):
        x_prev = jnp.where(tpos == 0, 0.0, pltpu.roll(x, 1, 0))
        x_next = jnp.where(tpos == t_len - 1, 0.0, pltpu.roll(x, t_len - 1, 0))
        return _silu(x_prev * w[0:1] + x * w[1:2] + x_next * w[2:3])

    def l2n(x):
        return x * lax.rsqrt(jnp.sum(x * x, axis=-1, keepdims=True) + EPS)

    for hh in range(hp):
        cols = slice(hh * LANES, (hh + 1) * LANES)
        qs[:, cols] = l2n(conv_act(q_ref[:, cols].astype(F32), cq_ref[:, cols])) * (DK_A ** -0.5)
        ks[:, cols] = l2n(conv_act(k_ref[:, cols].astype(F32), ck_ref[:, cols]))
        vs[:, cols] = conv_act(v_ref[:, cols].astype(F32), cv_ref[:, cols])
    if hp == H_A:
        g_scr[...] = g_ref[...]
    else:
        g_scr[...] = pltpu.roll(g_ref[...], (LANES - pl.program_id(1) * hp) % LANES, 1)
    if has_s0:
        s_scr[...] = s0_ref[...]
    else:
        s_scr[...] = jnp.zeros_like(s_scr)
    o_ref[...] = jnp.zeros_like(o_ref)
    ii = lax.broadcasted_iota(jnp.int32, (CHUNK, CHUNK), 0)
    jj = lax.broadcasted_iota(jnp.int32, (CHUNK, CHUNK), 1)

    def prep(c, carry):
        r = pl.ds(pl.multiple_of(c * CHUNK, CHUNK), CHUNK)
        g = g_scr[r, :]
        heads = []
        for hh in range(hp):
            cols = slice(hh * LANES, (hh + 1) * LANES)
            heads.append((qs[r, cols], ks[r, cols], vs[r, cols]))
        beta = lambda d, hh: g[:, GATE_BETA + d * H_A + hh:GATE_BETA + d * H_A + hh + 1]
        rr = [_mm_nt(jnp.concatenate([k * beta(0, hh), k * beta(1, hh), q], axis=0), k)
              for hh, (q, k, v) in enumerate(heads)]
        chains = [(rr[hh][d * CHUNK:(d + 1) * CHUNK], rr[hh][2 * CHUNK:], heads[hh][2], heads[hh][1], g,
                   d == 1, d * H_A + hh) for hh in range(hp) for d in (0, 1)]
        solved = _chunk_solve(chains, ii, jj)
        for hh in range(hp):
            cols = slice(hh * LANES, (hh + 1) * LANES)
            q, k = heads[hh][0], heads[hh][1]
            for d in (0, 1):
                u, w, a_qk, egc, ekd = solved[hh * 2 + d]
                slot = (d * hp + hh) * n + c
                u_scr[d, r, cols] = u
                wq_scr[slot] = jnp.concatenate([w, q * egc], axis=0).astype(BF16)
                ak_scr[slot] = jnp.concatenate([a_qk, (k * ekd).T], axis=0).astype(BF16)
        return carry

    lax.fori_loop(0, n, prep, 0, unroll=4)

    def scan(c, carry):
        chains = []
        for hh in range(hp):
            for d in (0, 1):
                cc = c if d == 0 else n - 1 - c
                chains.append((hh, d, pl.multiple_of(cc * CHUNK, CHUNK), (d * hp + hh) * n + cc))
        s = [s_scr[d, hh] for hh, d, r0, slot in chains]
        r1 = [jnp.dot(wq_scr[slot], si.astype(BF16), preferred_element_type=F32)
              for si, (hh, d, r0, slot) in zip(s, chains)]
        v_new = [u_scr[d, pl.ds(r0, CHUNK), hh * LANES:(hh + 1) * LANES] - ri[:CHUNK]
                 for ri, (hh, d, r0, slot) in zip(r1, chains)]
        r2 = [jnp.dot(ak_scr[slot], vi.astype(BF16), preferred_element_type=F32)
              for vi, (hh, d, r0, slot) in zip(v_new, chains)]
        for si, r1i, r2i, (hh, d, r0, slot) in zip(s, r1, r2, chains):
            o_ref[pl.ds(r0, CHUNK), hh * LANES:(hh + 1) * LANES] += r1i[CHUNK:] + r2i[:CHUNK]
            lane = GATE_EGC + d * H_A + hh
            edge = g_scr[pl.ds(pl.multiple_of(r0 + (0 if d else CHUNK - 8), 8), 8), lane:lane + 1]
            s_scr[d, hh] = si * (edge[0:1] if d else edge[7:8]) + r2i[CHUNK:]
        return carry

    lax.fori_loop(0, n, scan, 0, unroll=4)
    if not has_s0:
        sfin_ref[...] = s_scr[...]


def _deltanet(z, conv_w, gates, s0, n_seq, t_len, row0, hp):
    rb = row0 // t_len
    w = hp * LANES
    n = t_len // CHUNK
    mode = dict(pipeline_mode=pl.Buffered(1)) if t_len * w >= DELTA_SINGLE_BUFFER_ELEMS else {}
    seq_blk = lambda col0: pl.BlockSpec((t_len, w), lambda b, h: (rb + b, col0 // hp + h), **mode)
    cw_blk = lambda col0: pl.BlockSpec((CONV_K, w), lambda b, h: (0, col0 // hp + h))
    state_blk = pl.BlockSpec((None, None, 2, hp, DK_A, DV_A), lambda b, h: (b, 0, 0, h, 0, 0), **mode)
    in_specs = [seq_blk(0), seq_blk(H_A), seq_blk(2 * H_A), cw_blk(0), cw_blk(H_A), cw_blk(2 * H_A),
                pl.BlockSpec((t_len, LANES), lambda b, h: (rb + b, 0), **mode)]
    args = [z, z, z, conv_w, conv_w, conv_w, gates]
    o_spec = pl.BlockSpec((t_len, w), lambda b, h: (b, h), **mode)
    o_shape = jax.ShapeDtypeStruct((n_seq * t_len, H_A * DV_A), F32)
    has_s0 = s0 is not None
    if has_s0:
        in_specs += [state_blk]
        args += [s0]
        out_specs, out_shape = o_spec, o_shape
    else:
        out_specs = [o_spec, state_blk]
        out_shape = [o_shape, jax.ShapeDtypeStruct((n_seq, 1, 2, H_A, DK_A, DV_A), F32)]

    return pl.pallas_call(
        functools.partial(_delta_body, t_len, hp, has_s0),
        grid=(n_seq, H_A // hp),
        in_specs=in_specs,
        out_specs=out_specs,
        out_shape=out_shape,
        scratch_shapes=[pltpu.VMEM((t_len, w), F32)] * 3
        + [pltpu.VMEM((t_len, LANES), F32),
           pltpu.VMEM((2, t_len, w), F32),
           pltpu.VMEM((2 * hp * n, 2 * CHUNK, DV_A), BF16),
           pltpu.VMEM((2 * hp * n, CHUNK + DK_A, CHUNK), BF16),
           pltpu.VMEM((2, hp, DK_A, DV_A), F32)],
        compiler_params=_params("arbitrary", "arbitrary"),
        name="deltanet_lat" if has_s0 else "deltanet_ctx",
    )(*args)


def _subln(o, sub_ref, lam_init):
    return _rms_scale(o) * sub_ref[...] * (1.0 - lam_init)


def _attn_ctx_body(lam_init, q_ref, k_ref, v_ref, lam_ref, sub_ref, o_ref, ck_ref, cv_ref):
    heads = [slice(h * LANES, (h + 1) * LANES) for h in range(H_B)]
    lam = lam_ref[0:1, 0:1]
    ks = [k_ref[:, c] for c in heads]
    vs = [v_ref[:, c] for c in heads]
    for h in range(H_B):
        ck_ref[h] = ks[h].astype(F32)
        cv_ref[h] = vs[h].astype(F32)
    qb = [q_ref[:, c] * (DQK_B ** -0.5) for c in heads]
    kb = ks
    probs = []
    for lo in (0, DQK_B):
        s = [lax.dot_general(q[:, lo:lo + DQK_B], k[:, lo:lo + DQK_B], NT_DIMS, preferred_element_type=F32)
             for q, k in zip(qb, kb)]
        e = [jnp.exp(x - jnp.max(x, axis=-1, keepdims=True)) for x in s]
        probs.append([x * (1.0 / jnp.sum(x, axis=-1, keepdims=True)) for x in e])
    o = [jnp.dot((p1 - lam * p2).astype(BF16), v, preferred_element_type=F32)
         for p1, p2, v in zip(probs[0], probs[1], vs)]
    for c, oh in zip(heads, o):
        o_ref[:, c] = _subln(oh, sub_ref, lam_init).astype(BF16)


def _attn_ctx(z, lam, subln, n_seq, t_len, lam_init):
    n_tok = n_seq * t_len
    w = H_B * LANES
    blk = lambda col0: pl.BlockSpec((t_len, w), lambda b: (b, col0 // w))
    cache_blk = pl.BlockSpec((None, None, H_B, t_len, LANES), lambda b: (b, 0, 0, 0, 0))
    cache_shape = jax.ShapeDtypeStruct((n_seq, 1, H_B, t_len, LANES), F32)
    return pl.pallas_call(
        functools.partial(_attn_ctx_body, lam_init),
        grid=(n_seq,),
        in_specs=[blk(Z_QB), blk(Z_KB), blk(Z_VB),
                  pl.BlockSpec((1, LANES), lambda b: (0, 0)),
                  pl.BlockSpec((1, LANES), lambda b: (0, 0))],
        out_specs=[pl.BlockSpec((t_len, w), lambda b: (b, 0)), cache_blk, cache_blk],
        out_shape=[jax.ShapeDtypeStruct((n_tok, H_B * DV_B), BF16), cache_shape, cache_shape],
        compiler_params=_params("arbitrary"),
        name="attn_ctx",
    )(z, z, z, lam, subln)


def _rope(x, cos, sin_signed):
    lane = lax.broadcasted_iota(jnp.int32, (1, LANES), 1)
    first = (lane % 32) < 16
    partner = jnp.where(first, pltpu.roll(x, LANES - 16, 1), pltpu.roll(x, 16, 1))
    return x * cos + partner * sin_signed


LAT_HEADS = 2


def _attn_lat_body(lam_init, n_past, q_ref, k_ref, v_ref, pk_ref, pv_ref, cosq_ref, sinq_ref, cos_ref, sin_ref,
                   lam_ref, sub_ref, o_ref, keys, vals):
    heads = [slice(h * LANES, (h + 1) * LANES) for h in range(LAT_HEADS)]

    @pl.when(pl.program_id(2) == 0)
    def _():
        for h, c in enumerate(heads):
            keys[h, 0:n_past, :] = pk_ref[h].astype(BF16)
            vals[h, 0:n_past, :] = pv_ref[h].astype(BF16)
            keys[h, n_past:, :] = _rope(k_ref[:, c].astype(F32), cos_ref[...], sin_ref[...]).astype(BF16)
            vals[h, n_past:, :] = v_ref[:, c]

    lam = lam_ref[0:1, 0:1]
    qb = [(_rope(q_ref[:, c].astype(F32), cosq_ref[...], sinq_ref[...]) * (DQK_B ** -0.5)).astype(BF16)
          for c in heads]
    maps = []
    for lo in (0, DQK_B):
        s = [lax.dot_general(q[:, lo:lo + DQK_B], keys[h, :, lo:lo + DQK_B], NT_DIMS, preferred_element_type=F32)
             for h, q in enumerate(qb)]
        e = [jnp.exp(x - jnp.max(x, axis=-1, keepdims=True)) for x in s]
        inv = [1.0 / jnp.sum(x, axis=-1, keepdims=True) for x in e]
        maps.append([jnp.dot(x.astype(BF16), vals[h], preferred_element_type=F32) * r
                     for h, (x, r) in enumerate(zip(e, inv))])
    for h, c in enumerate(heads):
        o_ref[:, c] = _subln(maps[0][h] - lam * maps[1][h], sub_ref, lam_init).astype(BF16)


def _attn_lat(z, cache_k, cache_v, cos, sin, lam, subln, n_seq, t_len, row0, lam_init):
    tq = ATTN_Q_TILE
    nq = t_len // tq
    n_past = cache_k.shape[3]
    rbq = row0 // tq
    rbs = row0 // t_len
    w = LAT_HEADS * LANES
    seq_blk = lambda col0: pl.BlockSpec((t_len, w), lambda b, h, qi: (rbs + b, col0 // w + h))
    past_blk = pl.BlockSpec((None, None, LAT_HEADS, n_past, LANES), lambda b, h, qi: (b, 0, h, 0, 0))
    row_vec = pl.BlockSpec((1, LANES), lambda b, h, qi: (0, 0))
    return pl.pallas_call(
        functools.partial(_attn_lat_body, lam_init, n_past),
        grid=(n_seq, H_B // LAT_HEADS, nq),
        in_specs=[pl.BlockSpec((tq, w), lambda b, h, qi: (rbq + b * nq + qi, Z_QB // w + h)),
                  seq_blk(Z_KB), seq_blk(Z_VB), past_blk, past_blk,
                  pl.BlockSpec((tq, LANES), lambda b, h, qi: (qi, 0)),
                  pl.BlockSpec((tq, LANES), lambda b, h, qi: (qi, 0)),
                  pl.BlockSpec((t_len, LANES), lambda b, h, qi: (0, 0)),
                  pl.BlockSpec((t_len, LANES), lambda b, h, qi: (0, 0)),
                  row_vec, row_vec],
        out_specs=pl.BlockSpec((tq, w), lambda b, h, qi: (b * nq + qi, h)),
        out_shape=jax.ShapeDtypeStruct((n_seq * t_len, H_B * DV_B), BF16),
        scratch_shapes=[pltpu.VMEM((LAT_HEADS, n_past + t_len, LANES), BF16)] * 2,
        compiler_params=_params("arbitrary", "arbitrary", "arbitrary"),
        name="attn_lat",
    )(z, z, z, cache_k, cache_v, cos, sin, cos, sin, lam, subln)


def _rope_tables(t_len):
    t = np.arange(t_len)
    pos = np.stack([t // GRID_W, t % GRID_W], axis=1).astype(np.float32)
    nf = DQK_B // 4
    inv_freq = np.float32(ROPE_BASE) ** (-np.arange(nf, dtype=np.float32) / np.float32(nf))
    lane = np.arange(LANES)
    half = (lane % DQK_B) // (DQK_B // 2)
    ang = (pos[:, half] * inv_freq[lane % nf][None, :]).astype(np.float32)
    sign = np.where((lane % (DQK_B // 2)) < nf, -1.0, 1.0).astype(np.float32)
    return jnp.asarray(np.cos(ang), F32), jnp.asarray(np.sin(ang) * sign[None, :], F32)


def _merge_body(nct, oac_ref, oal_ref, ga_ref, on_ref, obc_ref, obl_ref, wa_f32, wb_f32, *rest):
    gm_refs, m_ref, wa_ref, wb_ref = rest[:-3], rest[-3], rest[-2], rest[-1]
    i = pl.program_id(0)
    n_blk = len(gm_refs) // 2

    @pl.when(i == 0)
    def _():
        wa_ref[...] = wa_f32[...].astype(BF16)
        wb_ref[...] = wb_f32[...].astype(BF16)

    def run(oa_ref, ob_ref):
        a = jnp.concatenate(
            [(_rms_scale(oa_ref[:, c]) * on_ref[...] * _silu(ga_ref[:, c].astype(F32))).astype(BF16)
             for c in (slice(h * DV_A, (h + 1) * DV_A) for h in range(H_A))], axis=1)
        ya = jnp.dot(a, wa_ref[...], preferred_element_type=F32)
        yb = jnp.dot(ob_ref[...], wb_ref[...], preferred_element_type=F32)
        tn = gm_refs[0].shape[1]
        for j in range(n_blk):
            c = slice(j * tn, (j + 1) * tn)
            m_ref[:, c] = (jax.nn.sigmoid(gm_refs[j][...].astype(F32)) * ya[:, c]
                           + jax.nn.sigmoid(gm_refs[n_blk + j][...].astype(F32)) * yb[:, c]).astype(BF16)

    @pl.when(i < nct)
    def _():
        run(oac_ref, obc_ref)

    @pl.when(i >= nct)
    def _():
        run(oal_ref, obl_ref)


def _merge(rows, z, oa_c, oa_l, ob_c, ob_l, onorm, w_up_a, w_up_b):
    n_tok = z.shape[0]
    tm = rows.tm
    ka, d = w_up_a.shape
    tn = 1024
    nj = d // tn
    ctx_blk = pl.BlockSpec((tm, ka), lambda i: (rows.ctx_idx(i), 0))
    lat_blk = pl.BlockSpec((tm, ka), lambda i: (rows.lat_idx(i), 0))
    weight = pl.BlockSpec((ka, d), lambda i: (0, 0), pipeline_mode=pl.Buffered(1))
    gate_cols = [pl.BlockSpec((tm, tn), functools.partial(lambda i, c: (i, c), c=Z_GM // tn + j)) for j in range(2 * nj)]
    return pl.pallas_call(
        functools.partial(_merge_body, rows.nct),
        grid=(n_tok // tm,),
        in_specs=[ctx_blk, lat_blk,
                  pl.BlockSpec((tm, ka), lambda i: (i, Z_GATE_A // ka)),
                  pl.BlockSpec((1, DV_A), lambda i: (0, 0)),
                  ctx_blk, lat_blk, weight, weight] + gate_cols,
        out_specs=pl.BlockSpec((tm, d), lambda i: (i, 0)),
        out_shape=jax.ShapeDtypeStruct((n_tok, d), BF16),
        scratch_shapes=[pltpu.VMEM((ka, d), BF16)] * 2,
        compiler_params=_params("arbitrary"),
        name="merge",
    )(oa_c, oa_l, z, onorm, ob_c, ob_l, w_up_a, w_up_b, *([z] * (2 * nj)))


ROUTE_E = 0
ROUTE_W = TOP_K


def _route_rows(lg):
    lane = lax.broadcasted_iota(jnp.int32, lg.shape, 1)
    neg = -jnp.inf

    def first_max(x):
        m = jnp.max(x, axis=1, keepdims=True)
        return m, jnp.min(jnp.where(x == m, lane, LANES), axis=1, keepdims=True)

    gl = jnp.where(lane < N_GROUPS, lg, neg)
    gmax, g_idx = first_max(gl)
    pg_top = 1.0 / jnp.sum(jnp.exp(gl - gmax), axis=1, keepdims=True)
    lo = N_GROUPS + E_PER_GROUP * g_idx
    el = jnp.where(jnp.logical_and(lane >= lo, lane < lo + E_PER_GROUP), lg, neg)
    emax, i1 = first_max(el)
    esum = jnp.sum(jnp.exp(el - emax), axis=1, keepdims=True)
    e2max, i2 = first_max(jnp.where(lane == i1, neg, el))
    p1 = 1.0 / esum
    p2 = jnp.exp(e2max - emax) / esum
    den = p1 + p2
    vals = [(i1 - N_GROUPS).astype(F32), (i2 - N_GROUPS).astype(F32), pg_top * p1 / den, pg_top * p2 / den]
    out = jnp.zeros(lg.shape, F32)
    for pos, val in enumerate(vals):
        out = jnp.where(lane == pos, val, out)
    return out


def _outproj_body(nct, m_ref, wo_ref, xp_ref, xs_ref, gt_ref, g2_ref, sh_ref, sc_ref, wr_ref, br_ref,
                  x1_ref, h2_ref, rt_ref):
    i = pl.program_id(0)
    half = m_ref.shape[0] // 2

    def finish(x_ref):
        for r in (slice(0, half), slice(half, 2 * half)):
            y = jnp.dot(m_ref[r, :], wo_ref[...], preferred_element_type=F32)
            x1 = x_ref[r, :] + gt_ref[0] * y
            x1_ref[r, :] = x1
            h2 = _rms_scale(x1) * g2_ref[...]
            h2 = h2 * (1.0 + sc_ref[0]) + sh_ref[0]
            h2_ref[r, :] = h2
            hi = h2.astype(BF16)
            lo = (h2 - hi.astype(F32)).astype(BF16)
            p_hi = jnp.dot(hi, wr_ref[...], preferred_element_type=F32)
            p_lo = jnp.dot(lo, wr_ref[...], preferred_element_type=F32)
            lg = p_hi[:, :LANES] + p_hi[:, LANES:] + p_lo[:, :LANES] + p_lo[:, LANES:] + br_ref[...]
            rt_ref[r, :] = _route_rows(lg)

    @pl.when(i < nct)
    def _():
        finish(xp_ref)

    @pl.when(i >= nct)
    def _():
        finish(xs_ref)


def _outproj(rows, mixed, w_o, xp, xs, gt1, gain2, sh2, sc2, w_r, b_r):
    d = xp.shape[1]
    tm = rows.tm
    n_tok = rows.n * tm
    mod = lambda: pl.BlockSpec((1, 1, d), lambda i: (rows.mod_idx(i), 0, 0))
    tok = pl.BlockSpec((tm, d), lambda i: (i, 0))
    return pl.pallas_call(
        functools.partial(_outproj_body, rows.nct),
        grid=(rows.n,),
        in_specs=[tok,
                  pl.BlockSpec((d, d), lambda i: (0, 0)),
                  pl.BlockSpec((tm, d), lambda i: (rows.ctx_idx(i), 0)),
                  pl.BlockSpec((tm, d), lambda i: (rows.lat_idx(i), 0)),
                  mod(),
                  pl.BlockSpec((1, d), lambda i: (0, 0)),
                  mod(), mod(),
                  pl.BlockSpec((d, 2 * LANES), lambda i: (0, 0)),
                  pl.BlockSpec((1, LANES), lambda i: (0, 0))],
        out_specs=[tok, tok, pl.BlockSpec((tm, LANES), lambda i: (i, 0))],
        out_shape=[jax.ShapeDtypeStruct((n_tok, d), F32), jax.ShapeDtypeStruct((n_tok, d), F32),
                   jax.ShapeDtypeStruct((n_tok, LANES), F32)],
        compiler_params=_params("arbitrary"),
        name="outproj",
    )(mixed, w_o, xp, xs, gt1, gain2, sh2, sc2, w_r, b_r)


DISPATCH_BLOCK = 256
TBL_EXPERT, TBL_VALID, TBL_COUNT, TBL_OFFSET = 0, 1, 2, 3


def _dispatch_body(n_tok, route_ref, pos_ref, tbl_ref):
    nb = n_tok // DISPATCH_BLOCK
    lane = lax.broadcasted_iota(jnp.int32, (1, LANES), 1)
    lane_f = lane.astype(F32)

    def one_hot(b, k):
        r = pl.ds(pl.multiple_of(b * DISPATCH_BLOCK, DISPATCH_BLOCK), DISPATCH_BLOCK)
        return jnp.where(route_ref[r, ROUTE_E + k:ROUTE_E + k + 1] == lane_f, 1.0, 0.0)

    def count(b, acc):
        return acc + jnp.sum(one_hot(b, 0) + one_hot(b, 1), axis=0, keepdims=True)

    counts = lax.fori_loop(0, nb, count, jnp.zeros((1, LANES), F32))
    padded = jnp.floor((counts + (MOE_TILE - 1)) * (1.0 / MOE_TILE)) * MOE_TILE
    pad_end = padded
    shift = 1
    while shift < LANES:
        pad_end = pad_end + jnp.where(lane >= shift, pltpu.roll(pad_end, shift, 1), 0.0)
        shift *= 2
    pad_off = pad_end - padded

    ri = lax.broadcasted_iota(jnp.int32, (DISPATCH_BLOCK, DISPATCH_BLOCK), 0)
    ci = lax.broadcasted_iota(jnp.int32, (DISPATCH_BLOCK, DISPATCH_BLOCK), 1)
    before = jnp.where(ci < ri, 1.0, 0.0).astype(BF16)

    def place(b, run):
        oh = [one_hot(b, k) for k in range(TOP_K)]
        base = pad_off + run
        out = jnp.zeros((DISPATCH_BLOCK, LANES), F32)
        lane_b = lax.broadcasted_iota(jnp.int32, (DISPATCH_BLOCK, LANES), 1)
        for k in range(TOP_K):
            prior = jnp.dot(before, oh[k].astype(BF16), preferred_element_type=F32)
            pos = jnp.sum(oh[k] * (base + prior), axis=1, keepdims=True)
            out = jnp.where(lane_b == k, pos, out)
            base = base + jnp.sum(oh[k], axis=0, keepdims=True)
        r = pl.ds(pl.multiple_of(b * DISPATCH_BLOCK, DISPATCH_BLOCK), DISPATCH_BLOCK)
        pos_ref[r, :] = out.astype(jnp.int32)
        return base - pad_off

    lax.fori_loop(0, nb, place, jnp.zeros((1, LANES), F32))

    end_col = jnp.transpose(jnp.broadcast_to(pad_end, (8, LANES)))[:, 0:1]
    e_col = lax.broadcasted_iota(jnp.int32, (LANES, 1), 0)
    tile_start = lane_f * MOE_TILE
    passed = jnp.where(jnp.logical_and(end_col <= tile_start, e_col < N_EXPERTS), 1.0, 0.0)
    tile_expert = jnp.minimum(jnp.sum(passed, axis=0, keepdims=True), N_EXPERTS - 1.0)
    total = jnp.sum(jnp.where(lane == N_EXPERTS - 1, pad_end, 0.0), axis=1, keepdims=True)
    tile_valid = jnp.where(tile_start < total, 1.0, 0.0)
    row = lax.broadcasted_iota(jnp.int32, (8, LANES), 0)
    tbl = jnp.zeros((8, LANES), F32)
    for idx, val in ((TBL_EXPERT, tile_expert), (TBL_VALID, tile_valid), (TBL_COUNT, counts), (TBL_OFFSET, pad_off)):
        tbl = jnp.where(row == idx, val, tbl)
    tbl_ref[...] = tbl.astype(jnp.int32)


def _dispatch(route):
    n_tok = route.shape[0]
    return pl.pallas_call(
        functools.partial(_dispatch_body, n_tok),
        out_shape=[jax.ShapeDtypeStruct((n_tok, LANES), jnp.int32), jax.ShapeDtypeStruct((8, LANES), jnp.int32)],
        compiler_params=pltpu.CompilerParams(vmem_limit_bytes=VMEM_LIMIT),
        name="dispatch",
    )(route)


def _scatter_body(tm, n_tiles, pos_ref, cnt_ref, off_ref, h_ref, o_hbm, zbuf, sem):
    i = pl.program_id(0)

    @pl.when(i == 0)
    def _():
        zbuf[...] = jnp.zeros_like(zbuf)
        last = N_EXPERTS - 1
        first_empty = (off_ref[last] + cnt_ref[last] + MOE_TILE - 1) // MOE_TILE

        def tile_copy(t):
            rows = pl.ds(pl.multiple_of(t * MOE_TILE, MOE_TILE), MOE_TILE)
            return pltpu.make_async_copy(zbuf, o_hbm.at[rows, :], sem.at[1])

        def partial_tile(e, start):
            cnt = cnt_ref[e]

            @pl.when(cnt % MOE_TILE != 0)
            def _():
                cp = tile_copy((off_ref[e] + cnt) // MOE_TILE)
                if start:
                    cp.start()
                else:
                    cp.wait()

        def zero_partial(e, carry):
            partial_tile(e, True)
            return carry

        def partial_done(e, carry):
            partial_tile(e, False)
            return carry

        def zero_tile(t, carry):
            tile_copy(t).start()
            return carry

        def zero_done(t, carry):
            tile_copy(t).wait()
            return carry

        lax.fori_loop(0, N_EXPERTS, zero_partial, 0)
        lax.fori_loop(first_empty, n_tiles, zero_tile, 0)
        lax.fori_loop(0, N_EXPERTS, partial_done, 0)
        lax.fori_loop(first_empty, n_tiles, zero_done, 0)

    def row_copies(r):
        return [pltpu.make_async_copy(h_ref.at[pl.ds(r, 1), :],
                                      o_hbm.at[pl.ds(pos_ref[(i * tm + r) * TOP_K + k], 1), :], sem.at[0])
                for k in range(TOP_K)]

    def start(r, carry):
        for k, cp in enumerate(row_copies(r)):
            cp.start(priority=k)
        return carry

    def wait(r, carry):
        for cp in row_copies(r):
            cp.wait()
        return carry

    lax.fori_loop(0, tm, start, 0, unroll=8)
    lax.fori_loop(0, tm, wait, 0, unroll=8)


def _scatter_rows(h2, pos_flat, counts, offsets, n_rows):
    n_tok, d = h2.shape
    tm = SCATTER_TILE
    grid_spec = pltpu.PrefetchScalarGridSpec(
        num_scalar_prefetch=3,
        grid=(n_tok // tm,),
        in_specs=[pl.BlockSpec((tm, d), lambda i, pos, cnt, off: (i, 0))],
        out_specs=pl.BlockSpec(memory_space=pl.ANY),
        scratch_shapes=[pltpu.VMEM((MOE_TILE, d), F32), pltpu.SemaphoreType.DMA((2,))],
    )
    return pl.pallas_call(
        functools.partial(_scatter_body, tm, n_rows // MOE_TILE),
        grid_spec=grid_spec,
        out_shape=jax.ShapeDtypeStruct((n_rows, d), F32),
        compiler_params=_params("arbitrary"),
        name="scatter_rows",
    )(pos_flat, counts, offsets, h2)


def _moe_body(n_tiles, te_ref, tv_ref, cnt_ref, x_ref, wg_hbm, wu_hbm, wd_hbm, y_ref,
              wg_f, wu_f, wd_f, wg_b, wu_b, wd_b, slot_ref, sem):
    i = pl.program_id(0)
    prev = jnp.maximum(i - 1, 0)
    valid = tv_ref[i] == 1
    e = te_ref[i]

    def fetch(expert, slot):
        return [pltpu.make_async_copy(src.at[expert], dst.at[slot], sem.at[slot])
                for src, dst in ((wg_hbm, wg_f), (wu_hbm, wu_f), (wd_hbm, wd_f))]

    @pl.when(jnp.logical_and(valid, i == 0))
    def _():
        slot_ref[0] = 1
        for cp in fetch(e, 0):
            cp.start()

    new_expert = jnp.logical_or(i == 0, e != te_ref[prev])

    def experts(cast_slot):
        x = x_ref[...].astype(BF16)
        if cast_slot is not None:
            wg_b[...] = wg_f[cast_slot].astype(BF16)
        g = jnp.dot(x, wg_b[...], preferred_element_type=F32)
        if cast_slot is not None:
            wu_b[...] = wu_f[cast_slot].astype(BF16)
        u = jnp.dot(x, wu_b[...], preferred_element_type=F32)
        if cast_slot is not None:
            wd_b[...] = wd_f[cast_slot].astype(BF16)
        y_ref[...] = jnp.dot((_silu(g) * u).astype(BF16), wd_b[...], preferred_element_type=F32)

    @pl.when(jnp.logical_and(valid, new_expert))
    def _():
        slot = 1 - slot_ref[0]
        slot_ref[0] = slot
        for cp in fetch(e, slot):
            cp.wait()
        nxt = jnp.minimum(i + (cnt_ref[e] + MOE_TILE - 1) // MOE_TILE, n_tiles - 1)

        @pl.when(jnp.logical_and(nxt > i, tv_ref[nxt] == 1))
        def _():
            for cp in fetch(te_ref[nxt], 1 - slot):
                cp.start()

        experts(slot)

    @pl.when(jnp.logical_and(valid, jnp.logical_not(new_expert)))
    def _():
        experts(None)

    @pl.when(tv_ref[i] == 0)
    def _():
        y_ref[...] = jnp.zeros_like(y_ref)


def _moe(x_sorted, tile_expert, tile_valid, counts, w_g, w_u, w_d):
    n_tiles = tile_expert.shape[0]
    d = x_sorted.shape[1]
    f = w_g.shape[-1]
    hbm = pl.BlockSpec(memory_space=pl.ANY)
    grid_spec = pltpu.PrefetchScalarGridSpec(
        num_scalar_prefetch=3,
        grid=(n_tiles,),
        in_specs=[pl.BlockSpec((MOE_TILE, d), lambda i, te, tv, cnt: (i * tv[i], 0)), hbm, hbm, hbm],
        out_specs=pl.BlockSpec((MOE_TILE, d), lambda i, te, tv, cnt: (i, 0)),
        scratch_shapes=[pltpu.VMEM((2, d, f), F32), pltpu.VMEM((2, d, f), F32), pltpu.VMEM((2, f, d), F32),
                        pltpu.VMEM((d, f), BF16), pltpu.VMEM((d, f), BF16), pltpu.VMEM((f, d), BF16),
                        pltpu.SMEM((1,), jnp.int32), pltpu.SemaphoreType.DMA((2,))],
    )
    return pl.pallas_call(
        functools.partial(_moe_body, n_tiles),
        grid_spec=grid_spec,
        out_shape=jax.ShapeDtypeStruct((n_tiles * MOE_TILE, d), F32),
        compiler_params=_params("arbitrary"),
        name="moe",
    )(tile_expert, tile_valid, counts, x_sorted, w_g, w_u, w_d)


def _combine_body(tm, n_tiles, row0, pos_ref, y_hbm, rt_ref, x1_ref, gt_ref, gf_ref, o_ref, ybuf, sem):
    i = pl.program_id(0)
    slot = i % 2

    def gather(tile, dst_slot, start):
        def row(r, carry):
            for k in range(TOP_K):
                p = pos_ref[(row0 + tile * tm + r) * TOP_K + k]
                cp = pltpu.make_async_copy(y_hbm.at[pl.ds(p, 1), :], ybuf.at[dst_slot, k, pl.ds(r, 1), :],
                                           sem.at[dst_slot])
                if start:
                    cp.start(priority=k)
                else:
                    cp.wait()
            return carry

        lax.fori_loop(0, tm, row, 0, unroll=8)

    @pl.when(i == 0)
    def _():
        gather(0, 0, True)

    @pl.when(i + 1 < n_tiles)
    def _():
        gather(i + 1, 1 - slot, True)

    gather(i, slot, False)
    rt = rt_ref[...]
    y = rt[:, ROUTE_W:ROUTE_W + 1] * ybuf[slot, 0] + rt[:, ROUTE_W + 1:ROUTE_W + 2] * ybuf[slot, 1]
    x2 = x1_ref[...] + gt_ref[0] * y
    o_ref[...] = _rms_scale(x2) * gf_ref[...]


def _combine(y_sorted, pos_flat, route, x1, gt2, gain_f, row0, n_rows, mod_of_tile, tm):
    d = x1.shape[1]
    rb = row0 // tm
    n_tiles = n_rows // tm
    grid_spec = pltpu.PrefetchScalarGridSpec(
        num_scalar_prefetch=1,
        grid=(n_tiles,),
        in_specs=[pl.BlockSpec(memory_space=pl.ANY),
                  pl.BlockSpec((tm, LANES), lambda i, pos: (rb + i, 0)),
                  pl.BlockSpec((tm, d), lambda i, pos: (rb + i, 0)),
                  pl.BlockSpec((1, 1, d), lambda i, pos: (mod_of_tile(i), 0, 0)),
                  pl.BlockSpec((1, d), lambda i, pos: (0, 0))],
        out_specs=pl.BlockSpec((tm, d), lambda i, pos: (i, 0)),
        scratch_shapes=[pltpu.VMEM((2, TOP_K, tm, d), F32), pltpu.SemaphoreType.DMA((2,))],
    )
    return pl.pallas_call(
        functools.partial(_combine_body, tm, n_tiles, row0),
        grid_spec=grid_spec,
        out_shape=jax.ShapeDtypeStruct((n_rows, d), F32),
        compiler_params=_params("arbitrary"),
        name="combine",
    )(pos_flat, y_sorted, route, x1, gt2, gain_f)


def kernel(x_prompt, x_sample, c, state_delta, cache_k, cache_v, c_ctx, w_ada, b_ada, norm_mix, norm_ffn, w_in,
           conv_a, a_log, dt_bias, onorm_a, lam, subln_b, w_up_a, w_up_b, w_o, w_rg, b_rg, w_re, b_re,
           w_e_gate, w_e_up, w_e_down, norm_final):
    assert w_in.shape[0] == 1, "single trunk layer"
    l = 0
    lam_init = 0.8 - 0.6 * math.exp(-0.3 * l)
    bc, tc, d = x_prompt.shape
    bl, tl, _ = x_sample.shape
    n_ctx, n_lat = bc * tc, bl * tl
    xp = x_prompt.reshape(n_ctx, d)
    xs = x_sample.reshape(n_lat, d)

    cvec = jnp.zeros((N_MOD_ROWS, d), F32).at[0].set(c_ctx).at[1:1 + bl].set(c)
    mod = _adaln(cvec, w_ada[l], b_ada[l][None, :])
    sh1, sc1, gt1, sh2, sc2, gt2 = [m.reshape(N_MOD_ROWS, 1, d) for m in jnp.split(mod, 6, axis=1)]

    rows = _Rows(n_ctx, n_lat, tl, ROW_TILE)
    w_in_t = jnp.swapaxes(w_in[l], 0, 1)
    h1, gates = _prenorm(rows, xp, xs, norm_mix[l][None, :], sh1, sc1, w_in_t, a_log[l], dt_bias[l])
    z = _inproj(h1, w_in_t)

    oa_c, new_state = _deltanet(z, conv_a[l], gates, None, bc, tc, 0, DELTA_HEADS_CTX)
    oa_l = _deltanet(z, conv_a[l], gates, state_delta, bl, tl, n_ctx, DELTA_HEADS_LAT)

    lv = lam[l]
    lam_val = jnp.exp(jnp.sum(lv[0] * lv[1])) - jnp.exp(jnp.sum(lv[2] * lv[3])) + lam_init
    lam_row = jnp.full((1, LANES), lam_val, F32)
    sub_row = subln_b[l][None, :]
    ob_c, new_k, new_v = _attn_ctx(z, lam_row, sub_row, bc, tc, lam_init)
    cos, sin = _rope_tables(tl)
    ob_l = _attn_lat(z, cache_k, cache_v, cos, sin, lam_row, sub_row, bl, tl, n_ctx, lam_init)

    mixed = _merge(_Rows(n_ctx, n_lat, tl, MERGE_TILE), z, oa_c, oa_l, ob_c, ob_l, onorm_a[l][None, :],
                   w_up_a[l], w_up_b[l])
    w_r = jnp.pad(jnp.concatenate([w_rg[l], w_re[l]], axis=1), ((0, 0), (0, LANES - N_GROUPS - N_EXPERTS)))
    w_r_hi = w_r.astype(BF16)
    w_r = jnp.concatenate([w_r_hi, (w_r - w_r_hi.astype(F32)).astype(BF16)], axis=1)
    b_r = jnp.pad(jnp.concatenate([b_rg[l], b_re[l]]), (0, LANES - N_GROUPS - N_EXPERTS))[None, :]
    x1, h2, route = _outproj(rows, mixed, w_o[l].astype(BF16), xp, xs, gt1, norm_ffn[l][None, :], sh2, sc2, w_r, b_r)

    n_tok = n_ctx + n_lat
    n_tiles = (n_tok * TOP_K) // MOE_TILE + N_EXPERTS
    pos, tbl = _dispatch(route)
    pos_flat = pos[:, :TOP_K].reshape(-1)
    counts, offsets = tbl[TBL_COUNT, :N_EXPERTS], tbl[TBL_OFFSET, :N_EXPERTS]
    x_sorted = _scatter_rows(h2, pos_flat, counts, offsets, n_tiles * MOE_TILE)
    y_sorted = _moe(x_sorted, tbl[TBL_EXPERT, :n_tiles], tbl[TBL_VALID, :n_tiles], counts,
                    w_e_gate[l], w_e_up[l], w_e_down[l])
    gf = norm_final[None, :]
    tm_c = COMBINE_TILE
    y_prompt = _combine(y_sorted, pos_flat, route, x1, gt2, gf, 0, n_ctx, lambda i: 0 * i, tm_c)
    y_sample = _combine(y_sorted, pos_flat, route, x1, gt2, gf, n_ctx, n_lat, lambda i: 1 + i // (tl // tm_c), tm_c)
    return (y_prompt.reshape(bc, tc, d), y_sample.reshape(bl, tl, d), new_state, new_k, new_v)
```

```python
import functools
import math

import jax
import jax.numpy as jnp
import numpy as np
from jax import lax
from jax.experimental import pallas as pl
from jax.experimental.pallas import tpu as pltpu

F32 = jnp.float32
BF16 = jnp.bfloat16

D_MODEL = 2048
GRID_W = 64
H_A = 8
DK_A = 128
DV_A = 128
CONV_K = 3
CHUNK = 64
H_B = 8
DQK_B = 64
DV_B = 2 * DQK_B
ROPE_BASE = 10000.0
N_GROUPS = 4
E_PER_GROUP = 8
N_EXPERTS = N_GROUPS * E_PER_GROUP
TOP_K = 2
D_FF_E = D_MODEL // 4
EPS = 1e-6

LANES = 128
QKV_A = 2 * H_A * DK_A + H_A * DV_A
N_SMALL = 4 * H_A
Z_GATE_A = QKV_A
Z_QB = Z_GATE_A + H_A * DV_A
Z_KB = Z_QB + H_B * 2 * DQK_B
Z_VB = Z_KB + H_B * 2 * DQK_B
Z_GM = Z_VB + H_B * DV_B
N_MAIN = Z_GM + 2 * D_MODEL

N_MOD_ROWS = 8
VMEM_LIMIT = 60 * 1024 * 1024
ROW_TILE = 512
INPROJ_ROWS = 2048
MERGE_TILE = 256
SCATTER_TILE = 1024
COMBINE_TILE = 512
MOE_TILE = 256
ATTN_Q_TILE = 256
DELTA_HEADS_CTX = 8
DELTA_HEADS_LAT = 8
DELTA_SINGLE_BUFFER_ELEMS = 1 << 20

NT_DIMS = (((1,), (1,)), ((), ()))


def _params(*sem):
    return pltpu.CompilerParams(dimension_semantics=sem, vmem_limit_bytes=VMEM_LIMIT)


def _mm(a, b):
    return jnp.dot(a.astype(BF16), b.astype(BF16), preferred_element_type=F32)


def _mm_nt(a, b):
    return lax.dot_general(a.astype(BF16), b.astype(BF16), NT_DIMS, preferred_element_type=F32)


def _silu(x):
    return x * jax.nn.sigmoid(x)


def _rms_scale(x):
    return x * lax.rsqrt(jnp.mean(x * x, axis=-1, keepdims=True) + EPS)


def _adaln_body(c_ref, w_ref, b_ref, o_ref):
    s = _silu(c_ref[...])
    o_ref[...] = _mm(s, w_ref[...]) + b_ref[...]


def _adaln(cvec, w, b):
    d, n = w.shape
    tn = 1024
    return pl.pallas_call(
        _adaln_body,
        grid=(n // tn,),
        in_specs=[pl.BlockSpec((N_MOD_ROWS, d), lambda j: (0, 0)),
                  pl.BlockSpec((d, tn), lambda j: (0, j)),
                  pl.BlockSpec((1, tn), lambda j: (0, j))],
        out_specs=pl.BlockSpec((N_MOD_ROWS, tn), lambda j: (0, j)),
        out_shape=jax.ShapeDtypeStruct((N_MOD_ROWS, n), F32),
        compiler_params=_params("arbitrary"),
        name="adaln",
    )(cvec, w, b)


class _Rows:
    def __init__(self, n_ctx, n_lat, t_lat, tm):
        assert n_ctx % tm == 0 and n_lat % tm == 0 and t_lat % tm == 0
        self.tm = tm
        self.nct = n_ctx // tm
        self.nlt = n_lat // tm
        self.per_seq = t_lat // tm
        self.n = self.nct + self.nlt

    def ctx_idx(self, i):
        return jnp.minimum(i, self.nct - 1)

    def lat_idx(self, i):
        return jnp.maximum(i - self.nct, 0)

    def mod_idx(self, i):
        return jnp.where(i < self.nct, 0, 1 + (i - self.nct) // self.per_seq)


SMALL_COL0 = QKV_A + H_A * DV_A
GATE_KINDS = 4
GATE_BETA, GATE_GC, GATE_EGC, GATE_EKD = (kind * 2 * H_A for kind in range(GATE_KINDS))


def _gate_rows(zs, a_log_row, dt_row):
    lane = lax.broadcasted_iota(jnp.int32, (1, LANES), 1)
    x = zs + dt_row
    softplus = jnp.maximum(x, 0.0) + jnp.log(1.0 + jnp.exp(-jnp.abs(x)))
    in_g = jnp.logical_and(lane >= GATE_GC, lane < GATE_EGC)
    g = jnp.where(in_g, -jnp.exp(a_log_row) * softplus, 0.0)
    beta = jax.nn.sigmoid(zs)
    reversed_lane = lane >= GATE_GC + H_A
    ri = lax.broadcasted_iota(jnp.int32, (CHUNK, CHUNK), 0)
    ci = lax.broadcasted_iota(jnp.int32, (CHUNK, CHUNK), 1)
    prefix = jnp.where(ci <= ri, 1.0, 0.0)
    suffix = jnp.where(ci >= ri, 1.0, 0.0)
    exact = dict(preferred_element_type=F32, precision=lax.Precision.HIGHEST)
    rows = []
    for c in range(zs.shape[0] // CHUNK):
        r = slice(c * CHUNK, (c + 1) * CHUNK)
        gch = g[r]
        gc = jnp.where(reversed_lane, jnp.dot(suffix, gch, **exact), jnp.dot(prefix, gch, **exact))
        tot = jnp.sum(gch, axis=0, keepdims=True)
        rows.append(jnp.where(lane < GATE_GC, beta[r],
                              jnp.where(lane < GATE_EGC, gc,
                                        jnp.where(lane < GATE_EKD, pltpu.roll(jnp.exp(gc), GATE_EGC - GATE_GC, 1),
                                                  jnp.where(lane < GATE_EKD + 2 * H_A,
                                                            pltpu.roll(jnp.exp(tot - gc), GATE_EKD - GATE_GC, 1),
                                                            0.0)))))
    return jnp.concatenate(rows, axis=0)


def _prenorm_body(nct, xp_ref, xs_ref, g_ref, sh_ref, sc_ref, ws_ref, al_ref, dt_ref, h_ref, gates_ref):
    i = pl.program_id(0)

    def run(x_ref):
        h = _rms_scale(x_ref[...]) * g_ref[...]
        h = (h * (1.0 + sc_ref[0]) + sh_ref[0]).astype(BF16)
        h_ref[...] = h
        zs = lax.dot_general(h, ws_ref[...].astype(BF16), NT_DIMS, preferred_element_type=F32)
        gates_ref[...] = _gate_rows(zs, al_ref[...], dt_ref[...])

    @pl.when(i < nct)
    def _():
        run(xp_ref)

    @pl.when(i >= nct)
    def _():
        run(xs_ref)


def _prenorm(rows, xp, xs, gain, sh, sc, w_in_t, a_log, dt_bias):
    place = lambda v: jnp.pad(v.reshape(1, 2 * H_A), ((0, 0), (GATE_GC, LANES - GATE_EGC)))
    d = xp.shape[1]
    tm = rows.tm
    n_tok = rows.n * tm
    return pl.pallas_call(
        functools.partial(_prenorm_body, rows.nct),
        grid=(rows.n,),
        in_specs=[pl.BlockSpec((tm, d), lambda i: (rows.ctx_idx(i), 0)),
                  pl.BlockSpec((tm, d), lambda i: (rows.lat_idx(i), 0)),
                  pl.BlockSpec((1, d), lambda i: (0, 0)),
                  pl.BlockSpec((1, 1, d), lambda i: (rows.mod_idx(i), 0, 0)),
                  pl.BlockSpec((1, 1, d), lambda i: (rows.mod_idx(i), 0, 0)),
                  pl.BlockSpec((LANES, d), lambda i: (SMALL_COL0 // LANES, 0)),
                  pl.BlockSpec((1, LANES), lambda i: (0, 0)),
                  pl.BlockSpec((1, LANES), lambda i: (0, 0))],
        out_specs=[pl.BlockSpec((tm, d), lambda i: (i, 0)),
                   pl.BlockSpec((tm, LANES), lambda i: (i, 0))],
        out_shape=[jax.ShapeDtypeStruct((n_tok, d), BF16),
                   jax.ShapeDtypeStruct((n_tok, LANES), F32)],
        compiler_params=_params("arbitrary"),
        name="prenorm",
    )(xp, xs, gain, sh, sc, w_in_t, place(a_log), place(dt_bias))


INPROJ_TN = 1024
N_ALIGNED = SMALL_COL0 // INPROJ_TN


def _inproj_body(h_ref, wt_ref, z_ref, w_scr):
    @pl.when(pl.program_id(1) == 0)
    def _():
        w_scr[...] = wt_ref[...].astype(BF16)

    z_ref[...] = lax.dot_general(h_ref[...], w_scr[...], NT_DIMS, preferred_element_type=F32).astype(BF16)


def _inproj(h, w_in_t):
    n_tok, d = h.shape
    tm, tn = INPROJ_ROWS, INPROJ_TN
    first_row = lambda j: pl.multiple_of(j * tn + N_SMALL * jnp.minimum(j // N_ALIGNED, 1), N_SMALL)
    return pl.pallas_call(
        _inproj_body,
        grid=(N_MAIN // tn, n_tok // tm),
        in_specs=[pl.BlockSpec((tm, d), lambda j, i: (i, 0)),
                  pl.BlockSpec((pl.Element(tn), pl.Element(d)), lambda j, i: (first_row(j), 0))],
        out_specs=pl.BlockSpec((tm, tn), lambda j, i: (i, j)),
        out_shape=jax.ShapeDtypeStruct((n_tok, N_MAIN), BF16),
        scratch_shapes=[pltpu.VMEM((tn, d), BF16)],
        compiler_params=_params("arbitrary", "arbitrary"),
        name="inproj",
    )(h, w_in_t)


INV_BLOCK = 16


def _chunk_solve(chains, ii, jj):
    eye = ii == jj
    blk = (ii // INV_BLOCK) == (jj // INV_BLOCK)
    blk2 = (ii // (2 * INV_BLOCK)) == (jj // (2 * INV_BLOCK))
    cols, a_qk, a0, off, d0, rhs = [], [], [], [], [], []
    for kk, qk, v, k, gates, reverse, lane in chains:
        beta, gc, egc, ekd = (gates[:, b + lane:b + lane + 1] for b in (GATE_BETA, GATE_GC, GATE_EGC, GATE_EKD))
        incl = (jj >= ii) if reverse else (jj <= ii)
        strict = (jj > ii) if reverse else (jj < ii)
        gc_row = jnp.sum(jnp.where(eye, gc, 0.0), axis=0, keepdims=True)
        dec = jnp.where(incl, jnp.exp(jnp.where(incl, gc - gc_row, 0.0)), 0.0)
        a = jnp.where(strict, kk * dec, 0.0)
        cols.append((egc, ekd))
        a_qk.append(qk * dec)
        a0.append(jnp.where(blk, a, 0.0))
        off.append(jnp.where(blk, 0.0, a))
        d0.append(jnp.where(eye, 1.0, 0.0) - a0[-1])
        rhs.append(jnp.concatenate([v * beta, k * (beta * egc)], axis=1))
    p = [_mm(x, x) for x in a0]
    for _ in range(int(math.log2(INV_BLOCK)) - 2):
        r = [_mm(jnp.concatenate([pi, di], axis=0), pi) for pi, di in zip(p, d0)]
        p = [ri[:CHUNK] for ri in r]
        d0 = [di + ri[CHUNK:] for di, ri in zip(d0, r)]
    d0 = [di + _mm(di, pi) for di, pi in zip(d0, p)]
    wl = [_mm(di, oi) for di, oi in zip(d0, off)]
    yr = [_mm(di, ri) for di, ri in zip(d0, rhs)]
    x1 = [jnp.where(blk2, x, 0.0) for x in wl]
    yl = [jnp.where(blk2, 0.0, x) for x in wl]
    zl = [y - _mm(x, y) for x, y in zip(x1, yl)]
    zr = [y - _mm(x, y) for x, y in zip(x1, yr)]
    sol = [y - _mm(x, y) for x, y in zip(zl, zr)]
    return [(so[:, :DV_A], so[:, DV_A:], aq, egc, ekd) for so, aq, (egc, ekd) in zip(sol, a_qk, cols)]


def _delta_body(t_len, hp, has_s0, q_ref, k_ref, v_ref, cq_ref, ck_ref, cv_ref, g_ref, *rest):
    if has_s0:
        s0_ref, o_ref, qs, ks, vs, g_scr, u_scr, wq_scr, ak_scr, s_scr = rest
    else:
        o_ref, sfin_ref, qs, ks, vs, g_scr, u_scr, wq_scr, ak_scr, s_scr = rest
    n = t_len // CHUNK
    tpos = lax.broadcasted_iota(jnp.int32, (t_len, 1), 0)

    def conv_act(x, w):
        x_prev = jnp.where(tpos == 0, 0.0, pltpu.roll(x, 1, 0))
        x_next = jnp.where(tpos == t_len - 1, 0.0, pltpu.roll(x, t_len - 1, 0))
        return _silu(x_prev * w[0:1] + x * w[1:2] + x_next * w[2:3])

    def l2n(x):
        return x * lax.rsqrt(jnp.sum(x * x, axis=-1, keepdims=True) + EPS)

    for hh in range(hp):
        cols = slice(hh * LANES, (hh + 1) * LANES)
        qs[:, cols] = l2n(conv_act(q_ref[:, cols].astype(F32), cq_ref[:, cols])) * (DK_A ** -0.5)
        ks[:, cols] = l2n(conv_act(k_ref[:, cols].astype(F32), ck_ref[:, cols]))
        vs[:, cols] = conv_act(v_ref[:, cols].astype(F32), cv_ref[:, cols])
    if hp == H_A:
        g_scr[...] = g_ref[...]
    else:
        g_scr[...] = pltpu.roll(g_ref[...], (LANES - pl.program_id(1) * hp) % LANES, 1)
    if has_s0:
        s_scr[...] = s0_ref[...]
    else:
        s_scr[...] = jnp.zeros_like(s_scr)
    o_ref[...] = jnp.zeros_like(o_ref)
    ii = lax.broadcasted_iota(jnp.int32, (CHUNK, CHUNK), 0)
    jj = lax.broadcasted_iota(jnp.int32, (CHUNK, CHUNK), 1)

    def prep(c, carry):
        r = pl.ds(pl.multiple_of(c * CHUNK, CHUNK), CHUNK)
        g = g_scr[r, :]
        heads = []
        for hh in range(hp):
            cols = slice(hh * LANES, (hh + 1) * LANES)
            heads.append((qs[r, cols], ks[r, cols], vs[r, cols]))
        beta = lambda d, hh: g[:, GATE_BETA + d * H_A + hh:GATE_BETA + d * H_A + hh + 1]
        rr = [_mm_nt(jnp.concatenate([k * beta(0, hh), k * beta(1, hh), q], axis=0), k)
              for hh, (q, k, v) in enumerate(heads)]
        chains = [(rr[hh][d * CHUNK:(d + 1) * CHUNK], rr[hh][2 * CHUNK:], heads[hh][2], heads[hh][1], g,
                   d == 1, d * H_A + hh) for hh in range(hp) for d in (0, 1)]
        solved = _chunk_solve(chains, ii, jj)
        for hh in range(hp):
            cols = slice(hh * LANES, (hh + 1) * LANES)
            q, k = heads[hh][0], heads[hh][1]
            for d in (0, 1):
                u, w, a_qk, egc, ekd = solved[hh * 2 + d]
                slot = (d * hp + hh) * n + c
                u_scr[d, r, cols] = u
                wq_scr[slot] = jnp.concatenate([w, q * egc], axis=0).astype(BF16)
                ak_scr[slot] = jnp.concatenate([a_qk, (k * ekd).T], axis=0).astype(BF16)
        return carry

    lax.fori_loop(0, n, prep, 0, unroll=4)

    def scan(c, carry):
        chains = []
        for hh in range(hp):
            for d in (0, 1):
                cc = c if d == 0 else n - 1 - c
                chains.append((hh, d, pl.multiple_of(cc * CHUNK, CHUNK), (d * hp + hh) * n + cc))
        s = [s_scr[d, hh] for hh, d, r0, slot in chains]
        r1 = [jnp.dot(wq_scr[slot], si.astype(BF16), preferred_element_type=F32)
              for si, (hh, d, r0, slot) in zip(s, chains)]
        v_new = [u_scr[d, pl.ds(r0, CHUNK), hh * LANES:(hh + 1) * LANES] - ri[:CHUNK]
                 for ri, (hh, d, r0, slot) in zip(r1, chains)]
        r2 = [jnp.dot(ak_scr[slot], vi.astype(BF16), preferred_element_type=F32)
              for vi, (hh, d, r0, slot) in zip(v_new, chains)]
        for si, r1i, r2i, (hh, d, r0, slot) in zip(s, r1, r2, chains):
            o_ref[pl.ds(r0, CHUNK), hh * LANES:(hh + 1) * LANES] += r1i[CHUNK:] + r2i[:CHUNK]
            lane = GATE_EGC + d * H_A + hh
            edge = g_scr[pl.ds(pl.multiple_of(r0 + (0 if d else CHUNK - 8), 8), 8), lane:lane + 1]
            s_scr[d, hh] = si * (edge[0:1] if d else edge[7:8]) + r2i[CHUNK:]
        return carry

    lax.fori_loop(0, n, scan, 0, unroll=4)
    if not has_s0:
        sfin_ref[...] = s_scr[...]


def _deltanet(z, conv_w, gates, s0, n_seq, t_len, row0, hp):
    rb = row0 // t_len
    w = hp * LANES
    n = t_len // CHUNK
    mode = dict(pipeline_mode=pl.Buffered(1)) if t_len * w >= DELTA_SINGLE_BUFFER_ELEMS else {}
    seq_blk = lambda col0: pl.BlockSpec((t_len, w), lambda b, h: (rb + b, col0 // hp + h), **mode)
    cw_blk = lambda col0: pl.BlockSpec((CONV_K, w), lambda b, h: (0, col0 // hp + h))
    state_blk = pl.BlockSpec((None, None, 2, hp, DK_A, DV_A), lambda b, h: (b, 0, 0, h, 0, 0), **mode)
    in_specs = [seq_blk(0), seq_blk(H_A), seq_blk(2 * H_A), cw_blk(0), cw_blk(H_A), cw_blk(2 * H_A),
                pl.BlockSpec((t_len, LANES), lambda b, h: (rb + b, 0), **mode)]
    args = [z, z, z, conv_w, conv_w, conv_w, gates]
    o_spec = pl.BlockSpec((t_len, w), lambda b, h: (b, h), **mode)
    o_shape = jax.ShapeDtypeStruct((n_seq * t_len, H_A * DV_A), F32)
    has_s0 = s0 is not None
    if has_s0:
        in_specs += [state_blk]
        args += [s0]
        out_specs, out_shape = o_spec, o_shape
    else:
        out_specs = [o_spec, state_blk]
        out_shape = [o_shape, jax.ShapeDtypeStruct((n_seq, 1, 2, H_A, DK_A, DV_A), F32)]

    return pl.pallas_call(
        functools.partial(_delta_body, t_len, hp, has_s0),
        grid=(n_seq, H_A // hp),
        in_specs=in_specs,
        out_specs=out_specs,
        out_shape=out_shape,
        scratch_shapes=[pltpu.VMEM((t_len, w), F32)] * 3
        + [pltpu.VMEM((t_len, LANES), F32),
           pltpu.VMEM((2, t_len, w), F32),
           pltpu.VMEM((2 * hp * n, 2 * CHUNK, DV_A), BF16),
           pltpu.VMEM((2 * hp * n, CHUNK + DK_A, CHUNK), BF16),
           pltpu.VMEM((2, hp, DK_A, DV_A), F32)],
        compiler_params=_params("arbitrary", "arbitrary"),
        name="deltanet_lat" if has_s0 else "deltanet_ctx",
    )(*args)


def _subln(o, sub_ref, lam_init):
    return _rms_scale(o) * sub_ref[...] * (1.0 - lam_init)


def _attn_ctx_body(lam_init, q_ref, k_ref, v_ref, lam_ref, sub_ref, o_ref, ck_ref, cv_ref):
    heads = [slice(h * LANES, (h + 1) * LANES) for h in range(H_B)]
    lam = lam_ref[0:1, 0:1]
    ks = [k_ref[:, c] for c in heads]
    vs = [v_ref[:, c] for c in heads]
    for h in range(H_B):
        ck_ref[h] = ks[h].astype(F32)
        cv_ref[h] = vs[h].astype(F32)
    qb = [q_ref[:, c] * (DQK_B ** -0.5) for c in heads]
    kb = ks
    probs = []
    for lo in (0, DQK_B):
        s = [lax.dot_general(q[:, lo:lo + DQK_B], k[:, lo:lo + DQK_B], NT_DIMS, preferred_element_type=F32)
             for q, k in zip(qb, kb)]
        e = [jnp.exp(x - jnp.max(x, axis=-1, keepdims=True)) for x in s]
        probs.append([x * (1.0 / jnp.sum(x, axis=-1, keepdims=True)) for x in e])
    o = [jnp.dot((p1 - lam * p2).astype(BF16), v, preferred_element_type=F32)
         for p1, p2, v in zip(probs[0], probs[1], vs)]
    for c, oh in zip(heads, o):
        o_ref[:, c] = _subln(oh, sub_ref, lam_init).astype(BF16)


def _attn_ctx(z, lam, subln, n_seq, t_len, lam_init):
    n_tok = n_seq * t_len
    w = H_B * LANES
    blk = lambda col0: pl.BlockSpec((t_len, w), lambda b: (b, col0 // w))
    cache_blk = pl.BlockSpec((None, None, H_B, t_len, LANES), lambda b: (b, 0, 0, 0, 0))
    cache_shape = jax.ShapeDtypeStruct((n_seq, 1, H_B, t_len, LANES), F32)
    return pl.pallas_call(
        functools.partial(_attn_ctx_body, lam_init),
        grid=(n_seq,),
        in_specs=[blk(Z_QB), blk(Z_KB), blk(Z_VB),
                  pl.BlockSpec((1, LANES), lambda b: (0, 0)),
                  pl.BlockSpec((1, LANES), lambda b: (0, 0))],
        out_specs=[pl.BlockSpec((t_len, w), lambda b: (b, 0)), cache_blk, cache_blk],
        out_shape=[jax.ShapeDtypeStruct((n_tok, H_B * DV_B), BF16), cache_shape, cache_shape],
        compiler_params=_params("arbitrary"),
        name="attn_ctx",
    )(z, z, z, lam, subln)


def _rope(x, cos, sin_signed):
    lane = lax.broadcasted_iota(jnp.int32, (1, LANES), 1)
    first = (lane % 32) < 16
    partner = jnp.where(first, pltpu.roll(x, LANES - 16, 1), pltpu.roll(x, 16, 1))
    return x * cos + partner * sin_signed


LAT_HEADS = 2


def _attn_lat_body(lam_init, n_past, q_ref, k_ref, v_ref, pk_ref, pv_ref, cosq_ref, sinq_ref, cos_ref, sin_ref,
                   lam_ref, sub_ref, o_ref, keys, vals):
    heads = [slice(h * LANES, (h + 1) * LANES) for h in range(LAT_HEADS)]

    @pl.when(pl.program_id(2) == 0)
    def _():
        for h, c in enumerate(heads):
            keys[h, 0:n_past, :] = pk_ref[h].astype(BF16)
            vals[h, 0:n_past, :] = pv_ref[h].astype(BF16)
            keys[h, n_past:, :] = _rope(k_ref[:, c].astype(F32), cos_ref[...], sin_ref[...]).astype(BF16)
            vals[h, n_past:, :] = v_ref[:, c]

    lam = lam_ref[0:1, 0:1]
    qb = [(_rope(q_ref[:, c].astype(F32), cosq_ref[...], sinq_ref[...]) * (DQK_B ** -0.5)).astype(BF16)
          for c in heads]
    maps = []
    for lo in (0, DQK_B):
        s = [lax.dot_general(q[:, lo:lo + DQK_B], keys[h, :, lo:lo + DQK_B], NT_DIMS, preferred_element_type=F32)
             for h, q in enumerate(qb)]
        e = [jnp.exp(x - jnp.max(x, axis=-1, keepdims=True)) for x in s]
        inv = [1.0 / jnp.sum(x, axis=-1, keepdims=True) for x in e]
        maps.append([jnp.dot(x.astype(BF16), vals[h], preferred_element_type=F32) * r
                     for h, (x, r) in enumerate(zip(e, inv))])
    for h, c in enumerate(heads):
        o_ref[:, c] = _subln(maps[0][h] - lam * maps[1][h], sub_ref, lam_init).astype(BF16)


def _attn_lat(z, cache_k, cache_v, cos, sin, lam, subln, n_seq, t_len, row0, lam_init):
    tq = ATTN_Q_TILE
    nq = t_len // tq
    n_past = cache_k.shape[3]
    rbq = row0 // tq
    rbs = row0 // t_len
    w = LAT_HEADS * LANES
    seq_blk = lambda col0: pl.BlockSpec((t_len, w), lambda b, h, qi: (rbs + b, col0 // w + h))
    past_blk = pl.BlockSpec((None, None, LAT_HEADS, n_past, LANES), lambda b, h, qi: (b, 0, h, 0, 0))
    row_vec = pl.BlockSpec((1, LANES), lambda b, h, qi: (0, 0))
    return pl.pallas_call(
        functools.partial(_attn_lat_body, lam_init, n_past),
        grid=(n_seq, H_B // LAT_HEADS, nq),
        in_specs=[pl.BlockSpec((tq, w), lambda b, h, qi: (rbq + b * nq + qi, Z_QB // w + h)),
                  seq_blk(Z_KB), seq_blk(Z_VB), past_blk, past_blk,
                  pl.BlockSpec((tq, LANES), lambda b, h, qi: (qi, 0)),
                  pl.BlockSpec((tq, LANES), lambda b, h, qi: (qi, 0)),
                  pl.BlockSpec((t_len, LANES), lambda b, h, qi: (0, 0)),
                  pl.BlockSpec((t_len, LANES), lambda b, h, qi: (0, 0)),
                  row_vec, row_vec],
        out_specs=pl.BlockSpec((tq, w), lambda b, h, qi: (b * nq + qi, h)),
        out_shape=jax.ShapeDtypeStruct((n_seq * t_len, H_B * DV_B), BF16),
        scratch_shapes=[pltpu.VMEM((LAT_HEADS, n_past + t_len, LANES), BF16)] * 2,
        compiler_params=_params("arbitrary", "arbitrary", "arbitrary"),
        name="attn_lat",
    )(z, z, z, cache_k, cache_v, cos, sin, cos, sin, lam, subln)


def _rope_tables(t_len):
    t = np.arange(t_len)
    pos = np.stack([t // GRID_W, t % GRID_W], axis=1).astype(np.float32)
    nf = DQK_B // 4
    inv_freq = np.float32(ROPE_BASE) ** (-np.arange(nf, dtype=np.float32) / np.float32(nf))
    lane = np.arange(LANES)
    half = (lane % DQK_B) // (DQK_B // 2)
    ang = (pos[:, half] * inv_freq[lane % nf][None, :]).astype(np.float32)
    sign = np.where((lane % (DQK_B // 2)) < nf, -1.0, 1.0).astype(np.float32)
    return jnp.asarray(np.cos(ang), F32), jnp.asarray(np.sin(ang) * sign[None, :], F32)


def _merge_body(nct, oac_ref, oal_ref, ga_ref, on_ref, obc_ref, obl_ref, wa_f32, wb_f32, *rest):
    gm_refs, m_ref, wa_ref, wb_ref = rest[:-3], rest[-3], rest[-2], rest[-1]
    i = pl.program_id(0)
    n_blk = len(gm_refs) // 2

    @pl.when(i == 0)
    def _():
        wa_ref[...] = wa_f32[...].astype(BF16)
        wb_ref[...] = wb_f32[...].astype(BF16)

    def run(oa_ref, ob_ref):
        a = jnp.concatenate(
            [(_rms_scale(oa_ref[:, c]) * on_ref[...] * _silu(ga_ref[:, c].astype(F32))).astype(BF16)
             for c in (slice(h * DV_A, (h + 1) * DV_A) for h in range(H_A))], axis=1)
        ya = jnp.dot(a, wa_ref[...], preferred_element_type=F32)
        yb = jnp.dot(ob_ref[...], wb_ref[...], preferred_element_type=F32)
        tn = gm_refs[0].shape[1]
        for j in range(n_blk):
            c = slice(j * tn, (j + 1) * tn)
            m_ref[:, c] = (jax.nn.sigmoid(gm_refs[j][...].astype(F32)) * ya[:, c]
                           + jax.nn.sigmoid(gm_refs[n_blk + j][...].astype(F32)) * yb[:, c]).astype(BF16)

    @pl.when(i < nct)
    def _():
        run(oac_ref, obc_ref)

    @pl.when(i >= nct)
    def _():
        run(oal_ref, obl_ref)


def _merge(rows, z, oa_c, oa_l, ob_c, ob_l, onorm, w_up_a, w_up_b):
    n_tok = z.shape[0]
    tm = rows.tm
    ka, d = w_up_a.shape
    tn = 1024
    nj = d // tn
    ctx_blk = pl.BlockSpec((tm, ka), lambda i: (rows.ctx_idx(i), 0))
    lat_blk = pl.BlockSpec((tm, ka), lambda i: (rows.lat_idx(i), 0))
    weight = pl.BlockSpec((ka, d), lambda i: (0, 0), pipeline_mode=pl.Buffered(1))
    gate_cols = [pl.BlockSpec((tm, tn), functools.partial(lambda i, c: (i, c), c=Z_GM // tn + j)) for j in range(2 * nj)]
    return pl.pallas_call(
        functools.partial(_merge_body, rows.nct),
        grid=(n_tok // tm,),
        in_specs=[ctx_blk, lat_blk,
                  pl.BlockSpec((tm, ka), lambda i: (i, Z_GATE_A // ka)),
                  pl.BlockSpec((1, DV_A), lambda i: (0, 0)),
                  ctx_blk, lat_blk, weight, weight] + gate_cols,
        out_specs=pl.BlockSpec((tm, d), lambda i: (i, 0)),
        out_shape=jax.ShapeDtypeStruct((n_tok, d), BF16),
        scratch_shapes=[pltpu.VMEM((ka, d), BF16)] * 2,
        compiler_params=_params("arbitrary"),
        name="merge",
    )(oa_c, oa_l, z, onorm, ob_c, ob_l, w_up_a, w_up_b, *([z] * (2 * nj)))


ROUTE_E = 0
ROUTE_W = TOP_K


def _route_rows(lg):
    lane = lax.broadcasted_iota(jnp.int32, lg.shape, 1)
    neg = -jnp.inf

    def first_max(x):
        m = jnp.max(x, axis=1, keepdims=True)
        return m, jnp.min(jnp.where(x == m, lane, LANES), axis=1, keepdims=True)

    gl = jnp.where(lane < N_GROUPS, lg, neg)
    gmax, g_idx = first_max(gl)
    pg_top = 1.0 / jnp.sum(jnp.exp(gl - gmax), axis=1, keepdims=True)
    lo = N_GROUPS + E_PER_GROUP * g_idx
    el = jnp.where(jnp.logical_and(lane >= lo, lane < lo + E_PER_GROUP), lg, neg)
    emax, i1 = first_max(el)
    esum = jnp.sum(jnp.exp(el - emax), axis=1, keepdims=True)
    e2max, i2 = first_max(jnp.where(lane == i1, neg, el))
    p1 = 1.0 / esum
    p2 = jnp.exp(e2max - emax) / esum
    den = p1 + p2
    vals = [(i1 - N_GROUPS).astype(F32), (i2 - N_GROUPS).astype(F32), pg_top * p1 / den, pg_top * p2 / den]
    out = jnp.zeros(lg.shape, F32)
    for pos, val in enumerate(vals):
        out = jnp.where(lane == pos, val, out)
    return out


def _outproj_body(nct, m_ref, wo_ref, xp_ref, xs_ref, gt_ref, g2_ref, sh_ref, sc_ref, wr_ref, br_ref,
                  x1_ref, h2_ref, rt_ref):
    i = pl.program_id(0)
    half = m_ref.shape[0] // 2

    def finish(x_ref):
        for r in (slice(0, half), slice(half, 2 * half)):
            y = jnp.dot(m_ref[r, :], wo_ref[...], preferred_element_type=F32)
            x1 = x_ref[r, :] + gt_ref[0] * y
            x1_ref[r, :] = x1
            h2 = _rms_scale(x1) * g2_ref[...]
            h2 = h2 * (1.0 + sc_ref[0]) + sh_ref[0]
            h2_ref[r, :] = h2
            hi = h2.astype(BF16)
            lo = (h2 - hi.astype(F32)).astype(BF16)
            p_hi = jnp.dot(hi, wr_ref[...], preferred_element_type=F32)
            p_lo = jnp.dot(lo, wr_ref[...], preferred_element_type=F32)
            lg = p_hi[:, :LANES] + p_hi[:, LANES:] + p_lo[:, :LANES] + p_lo[:, LANES:] + br_ref[...]
            rt_ref[r, :] = _route_rows(lg)

    @pl.when(i < nct)
    def _():
        finish(xp_ref)

    @pl.when(i >= nct)
    def _():
        finish(xs_ref)


def _outproj(rows, mixed, w_o, xp, xs, gt1, gain2, sh2, sc2, w_r, b_r):
    d = xp.shape[1]
    tm = rows.tm
    n_tok = rows.n * tm
    mod = lambda: pl.BlockSpec((1, 1, d), lambda i: (rows.mod_idx(i), 0, 0))
    tok = pl.BlockSpec((tm, d), lambda i: (i, 0))
    return pl.pallas_call(
        functools.partial(_outproj_body, rows.nct),
        grid=(rows.n,),
        in_specs=[tok,
                  pl.BlockSpec((d, d), lambda i: (0, 0)),
                  pl.BlockSpec((tm, d), lambda i: (rows.ctx_idx(i), 0)),
                  pl.BlockSpec((tm, d), lambda i: (rows.lat_idx(i), 0)),
                  mod(),
                  pl.BlockSpec((1, d), lambda i: (0, 0)),
                  mod(), mod(),
                  pl.BlockSpec((d, 2 * LANES), lambda i: (0, 0)),
                  pl.BlockSpec((1, LANES), lambda i: (0, 0))],
        out_specs=[tok, tok, pl.BlockSpec((tm, LANES), lambda i: (i, 0))],
        out_shape=[jax.ShapeDtypeStruct((n_tok, d), F32), jax.ShapeDtypeStruct((n_tok, d), F32),
                   jax.ShapeDtypeStruct((n_tok, LANES), F32)],
        compiler_params=_params("arbitrary"),
        name="outproj",
    )(mixed, w_o, xp, xs, gt1, gain2, sh2, sc2, w_r, b_r)


DISPATCH_BLOCK = 256
TBL_EXPERT, TBL_VALID, TBL_COUNT, TBL_OFFSET = 0, 1, 2, 3


def _dispatch_body(n_tok, route_ref, pos_ref, tbl_ref):
    nb = n_tok // DISPATCH_BLOCK
    lane = lax.broadcasted_iota(jnp.int32, (1, LANES), 1)
    lane_f = lane.astype(F32)

    def one_hot(b, k):
        r = pl.ds(pl.multiple_of(b * DISPATCH_BLOCK, DISPATCH_BLOCK), DISPATCH_BLOCK)
        return jnp.where(route_ref[r, ROUTE_E + k:ROUTE_E + k + 1] == lane_f, 1.0, 0.0)

    def count(b, acc):
        return acc + jnp.sum(one_hot(b, 0) + one_hot(b, 1), axis=0, keepdims=True)

    counts = lax.fori_loop(0, nb, count, jnp.zeros((1, LANES), F32))
    padded = jnp.floor((counts + (MOE_TILE - 1)) * (1.0 / MOE_TILE)) * MOE_TILE
    pad_end = padded
    shift = 1
    while shift < LANES:
        pad_end = pad_end + jnp.where(lane >= shift, pltpu.roll(pad_end, shift, 1), 0.0)
        shift *= 2
    pad_off = pad_end - padded

    ri = lax.broadcasted_iota(jnp.int32, (DISPATCH_BLOCK, DISPATCH_BLOCK), 0)
    ci = lax.broadcasted_iota(jnp.int32, (DISPATCH_BLOCK, DISPATCH_BLOCK), 1)
    before = jnp.where(ci < ri, 1.0, 0.0).astype(BF16)

    def place(b, run):
        oh = [one_hot(b, k) for k in range(TOP_K)]
        base = pad_off + run
        out = jnp.zeros((DISPATCH_BLOCK, LANES), F32)
        lane_b = lax.broadcasted_iota(jnp.int32, (DISPATCH_BLOCK, LANES), 1)
        for k in range(TOP_K):
            prior = jnp.dot(before, oh[k].astype(BF16), preferred_element_type=F32)
            pos = jnp.sum(oh[k] * (base + prior), axis=1, keepdims=True)
            out = jnp.where(lane_b == k, pos, out)
            base = base + jnp.sum(oh[k], axis=0, keepdims=True)
        r = pl.ds(pl.multiple_of(b * DISPATCH_BLOCK, DISPATCH_BLOCK), DISPATCH_BLOCK)
        pos_ref[r, :] = out.astype(jnp.int32)
        return base - pad_off

    lax.fori_loop(0, nb, place, jnp.zeros((1, LANES), F32))

    end_col = jnp.transpose(jnp.broadcast_to(pad_end, (8, LANES)))[:, 0:1]
    e_col = lax.broadcasted_iota(jnp.int32, (LANES, 1), 0)
    tile_start = lane_f * MOE_TILE
    passed = jnp.where(jnp.logical_and(end_col <= tile_start, e_col < N_EXPERTS), 1.0, 0.0)
    tile_expert = jnp.minimum(jnp.sum(passed, axis=0, keepdims=True), N_EXPERTS - 1.0)
    total = jnp.sum(jnp.where(lane == N_EXPERTS - 1, pad_end, 0.0), axis=1, keepdims=True)
    tile_valid = jnp.where(tile_start < total, 1.0, 0.0)
    row = lax.broadcasted_iota(jnp.int32, (8, LANES), 0)
    tbl = jnp.zeros((8, LANES), F32)
    for idx, val in ((TBL_EXPERT, tile_expert), (TBL_VALID, tile_valid), (TBL_COUNT, counts), (TBL_OFFSET, pad_off)):
        tbl = jnp.where(row == idx, val, tbl)
    tbl_ref[...] = tbl.astype(jnp.int32)


def _dispatch(route):
    n_tok = route.shape[0]
    return pl.pallas_call(
        functools.partial(_dispatch_body, n_tok),
        out_shape=[jax.ShapeDtypeStruct((n_tok, LANES), jnp.int32), jax.ShapeDtypeStruct((8, LANES), jnp.int32)],
        compiler_params=pltpu.CompilerParams(vmem_limit_bytes=VMEM_LIMIT),
        name="dispatch",
    )(route)


def _scatter_body(tm, n_tiles, pos_ref, cnt_ref, off_ref, h_ref, o_hbm, zbuf, sem):
    i = pl.program_id(0)

    @pl.when(i == 0)
    def _():
        zbuf[...] = jnp.zeros_like(zbuf)
        last = N_EXPERTS - 1
        first_empty = (off_ref[last] + cnt_ref[last] + MOE_TILE - 1) // MOE_TILE

        def tile_copy(t):
            rows = pl.ds(pl.multiple_of(t * MOE_TILE, MOE_TILE), MOE_TILE)
            return pltpu.make_async_copy(zbuf, o_hbm.at[rows, :], sem.at[1])

        def partial_tile(e, start):
            cnt = cnt_ref[e]

            @pl.when(cnt % MOE_TILE != 0)
            def _():
                cp = tile_copy((off_ref[e] + cnt) // MOE_TILE)
                if start:
                    cp.start()
                else:
                    cp.wait()

        def zero_partial(e, carry):
            partial_tile(e, True)
            return carry

        def partial_done(e, carry):
            partial_tile(e, False)
            return carry

        def zero_tile(t, carry):
            tile_copy(t).start()
            return carry

        def zero_done(t, carry):
            tile_copy(t).wait()
            return carry

        lax.fori_loop(0, N_EXPERTS, zero_partial, 0)
        lax.fori_loop(first_empty, n_tiles, zero_tile, 0)
        lax.fori_loop(0, N_EXPERTS, partial_done, 0)
        lax.fori_loop(first_empty, n_tiles, zero_done, 0)

    def row_copies(r):
        return [pltpu.make_async_copy(h_ref.at[pl.ds(r, 1), :],
                                      o_hbm.at[pl.ds(pos_ref[(i * tm + r) * TOP_K + k], 1), :], sem.at[0])
                for k in range(TOP_K)]

    def start(r, carry):
        for k, cp in enumerate(row_copies(r)):
            cp.start(priority=k)
        return carry

    def wait(r, carry):
        for cp in row_copies(r):
            cp.wait()
        return carry

    lax.fori_loop(0, tm, start, 0, unroll=8)
    lax.fori_loop(0, tm, wait, 0, unroll=8)


def _scatter_rows(h2, pos_flat, counts, offsets, n_rows):
    n_tok, d = h2.shape
    tm = SCATTER_TILE
    grid_spec = pltpu.PrefetchScalarGridSpec(
        num_scalar_prefetch=3,
        grid=(n_tok // tm,),
        in_specs=[pl.BlockSpec((tm, d), lambda i, pos, cnt, off: (i, 0))],
        out_specs=pl.BlockSpec(memory_space=pl.ANY),
        scratch_shapes=[pltpu.VMEM((MOE_TILE, d), F32), pltpu.SemaphoreType.DMA((2,))],
    )
    return pl.pallas_call(
        functools.partial(_scatter_body, tm, n_rows // MOE_TILE),
        grid_spec=grid_spec,
        out_shape=jax.ShapeDtypeStruct((n_rows, d), F32),
        compiler_params=_params("arbitrary"),
        name="scatter_rows",
    )(pos_flat, counts, offsets, h2)


def _moe_body(n_tiles, te_ref, tv_ref, cnt_ref, x_ref, wg_hbm, wu_hbm, wd_hbm, y_ref,
              wg_f, wu_f, wd_f, wg_b, wu_b, wd_b, slot_ref, sem):
    i = pl.program_id(0)
    prev = jnp.maximum(i - 1, 0)
    valid = tv_ref[i] == 1
    e = te_ref[i]

    def fetch(expert, slot):
        return [pltpu.make_async_copy(src.at[expert], dst.at[slot], sem.at[slot])
                for src, dst in ((wg_hbm, wg_f), (wu_hbm, wu_f), (wd_hbm, wd_f))]

    @pl.when(jnp.logical_and(valid, i == 0))
    def _():
        slot_ref[0] = 1
        for cp in fetch(e, 0):
            cp.start()

    @pl.when(jnp.logical_and(valid, jnp.logical_or(i == 0, e != te_ref[prev])))
    def _():
        slot = 1 - slot_ref[0]
        slot_ref[0] = slot
        for cp in fetch(e, slot):
            cp.wait()
        nxt = jnp.minimum(i + (cnt_ref[e] + MOE_TILE - 1) // MOE_TILE, n_tiles - 1)

        @pl.when(jnp.logical_and(nxt > i, tv_ref[nxt] == 1))
        def _():
            for cp in fetch(te_ref[nxt], 1 - slot):
                cp.start()

        wg_b[...] = wg_f[slot].astype(BF16)
        wu_b[...] = wu_f[slot].astype(BF16)
        wd_b[...] = wd_f[slot].astype(BF16)

    @pl.when(valid)
    def _():
        x = x_ref[...].astype(BF16)
        g = jnp.dot(x, wg_b[...], preferred_element_type=F32)
        u = jnp.dot(x, wu_b[...], preferred_element_type=F32)
        y_ref[...] = jnp.dot((_silu(g) * u).astype(BF16), wd_b[...], preferred_element_type=F32)

    @pl.when(tv_ref[i] == 0)
    def _():
        y_ref[...] = jnp.zeros_like(y_ref)


def _moe(x_sorted, tile_expert, tile_valid, counts, w_g, w_u, w_d):
    n_tiles = tile_expert.shape[0]
    d = x_sorted.shape[1]
    f = w_g.shape[-1]
    hbm = pl.BlockSpec(memory_space=pl.ANY)
    grid_spec = pltpu.PrefetchScalarGridSpec(
        num_scalar_prefetch=3,
        grid=(n_tiles,),
        in_specs=[pl.BlockSpec((MOE_TILE, d), lambda i, te, tv, cnt: (i * tv[i], 0)), hbm, hbm, hbm],
        out_specs=pl.BlockSpec((MOE_TILE, d), lambda i, te, tv, cnt: (i, 0)),
        scratch_shapes=[pltpu.VMEM((2, d, f), F32), pltpu.VMEM((2, d, f), F32), pltpu.VMEM((2, f, d), F32),
                        pltpu.VMEM((d, f), BF16), pltpu.VMEM((d, f), BF16), pltpu.VMEM((f, d), BF16),
                        pltpu.SMEM((1,), jnp.int32), pltpu.SemaphoreType.DMA((2,))],
    )
    return pl.pallas_call(
        functools.partial(_moe_body, n_tiles),
        grid_spec=grid_spec,
        out_shape=jax.ShapeDtypeStruct((n_tiles * MOE_TILE, d), F32),
        compiler_params=_params("arbitrary"),
        name="moe",
    )(tile_expert, tile_valid, counts, x_sorted, w_g, w_u, w_d)


def _combine_body(tm, n_tiles, row0, pos_ref, y_hbm, rt_ref, x1_ref, gt_ref, gf_ref, o_ref, ybuf, sem):
    i = pl.program_id(0)
    slot = i % 2

    def gather(tile, dst_slot, start):
        def row(r, carry):
            for k in range(TOP_K):
                p = pos_ref[(row0 + tile * tm + r) * TOP_K + k]
                cp = pltpu.make_async_copy(y_hbm.at[pl.ds(p, 1), :], ybuf.at[dst_slot, k, pl.ds(r, 1), :],
                                           sem.at[dst_slot])
                if start:
                    cp.start(priority=k)
                else:
                    cp.wait()
            return carry

        lax.fori_loop(0, tm, row, 0, unroll=8)

    @pl.when(i == 0)
    def _():
        gather(0, 0, True)

    @pl.when(i + 1 < n_tiles)
    def _():
        gather(i + 1, 1 - slot, True)

    gather(i, slot, False)
    rt = rt_ref[...]
    y = rt[:, ROUTE_W:ROUTE_W + 1] * ybuf[slot, 0] + rt[:, ROUTE_W + 1:ROUTE_W + 2] * ybuf[slot, 1]
    x2 = x1_ref[...] + gt_ref[0] * y
    o_ref[...] = _rms_scale(x2) * gf_ref[...]


def _combine(y_sorted, pos_flat, route, x1, gt2, gain_f, row0, n_rows, mod_of_tile, tm):
    d = x1.shape[1]
    rb = row0 // tm
    n_tiles = n_rows // tm
    grid_spec = pltpu.PrefetchScalarGridSpec(
        num_scalar_prefetch=1,
        grid=(n_tiles,),
        in_specs=[pl.BlockSpec(memory_space=pl.ANY),
                  pl.BlockSpec((tm, LANES), lambda i, pos: (rb + i, 0)),
                  pl.BlockSpec((tm, d), lambda i, pos: (rb + i, 0)),
                  pl.BlockSpec((1, 1, d), lambda i, pos: (mod_of_tile(i), 0, 0)),
                  pl.BlockSpec((1, d), lambda i, pos: (0, 0))],
        out_specs=pl.BlockSpec((tm, d), lambda i, pos: (i, 0)),
        scratch_shapes=[pltpu.VMEM((2, TOP_K, tm, d), F32), pltpu.SemaphoreType.DMA((2,))],
    )
    return pl.pallas_call(
        functools.partial(_combine_body, tm, n_tiles, row0),
        grid_spec=grid_spec,
        out_shape=jax.ShapeDtypeStruct((n_rows, d), F32),
        compiler_params=_params("arbitrary"),
        name="combine",
    )(pos_flat, y_sorted, route, x1, gt2, gain_f)


def kernel(x_prompt, x_sample, c, state_delta, cache_k, cache_v, c_ctx, w_ada, b_ada, norm_mix, norm_ffn, w_in,
           conv_a, a_log, dt_bias, onorm_a, lam, subln_b, w_up_a, w_up_b, w_o, w_rg, b_rg, w_re, b_re,
           w_e_gate, w_e_up, w_e_down, norm_final):
    assert w_in.shape[0] == 1, "single trunk layer"
    l = 0
    lam_init = 0.8 - 0.6 * math.exp(-0.3 * l)
    bc, tc, d = x_prompt.shape
    bl, tl, _ = x_sample.shape
    n_ctx, n_lat = bc * tc, bl * tl
    xp = x_prompt.reshape(n_ctx, d)
    xs = x_sample.reshape(n_lat, d)

    cvec = jnp.zeros((N_MOD_ROWS, d), F32).at[0].set(c_ctx).at[1:1 + bl].set(c)
    mod = _adaln(cvec, w_ada[l], b_ada[l][None, :])
    sh1, sc1, gt1, sh2, sc2, gt2 = [m.reshape(N_MOD_ROWS, 1, d) for m in jnp.split(mod, 6, axis=1)]

    rows = _Rows(n_ctx, n_lat, tl, ROW_TILE)
    w_in_t = jnp.swapaxes(w_in[l], 0, 1)
    h1, gates = _prenorm(rows, xp, xs, norm_mix[l][None, :], sh1, sc1, w_in_t, a_log[l], dt_bias[l])
    z = _inproj(h1, w_in_t)

    oa_c, new_state = _deltanet(z, conv_a[l], gates, None, bc, tc, 0, DELTA_HEADS_CTX)
    oa_l = _deltanet(z, conv_a[l], gates, state_delta, bl, tl, n_ctx, DELTA_HEADS_LAT)

    lv = lam[l]
    lam_val = jnp.exp(jnp.sum(lv[0] * lv[1])) - jnp.exp(jnp.sum(lv[2] * lv[3])) + lam_init
    lam_row = jnp.full((1, LANES), lam_val, F32)
    sub_row = subln_b[l][None, :]
    ob_c, new_k, new_v = _attn_ctx(z, lam_row, sub_row, bc, tc, lam_init)
    cos, sin = _rope_tables(tl)
    ob_l = _attn_lat(z, cache_k, cache_v, cos, sin, lam_row, sub_row, bl, tl, n_ctx, lam_init)

    mixed = _merge(_Rows(n_ctx, n_lat, tl, MERGE_TILE), z, oa_c, oa_l, ob_c, ob_l, onorm_a[l][None, :],
                   w_up_a[l], w_up_b[l])
    w_r = jnp.pad(jnp.concatenate([w_rg[l], w_re[l]], axis=1), ((0, 0), (0, LANES - N_GROUPS - N_EXPERTS)))
    w_r_hi = w_r.astype(BF16)
    w_r = jnp.concatenate([w_r_hi, (w_r - w_r_hi.astype(F32)).astype(BF16)], axis=1)
    b_r = jnp.pad(jnp.concatenate([b_rg[l], b_re[l]]), (0, LANES - N_GROUPS - N_EXPERTS))[None, :]
    x1, h2, route = _outproj(rows, mixed, w_o[l].astype(BF16), xp, xs, gt1, norm_ffn[l][None, :], sh2, sc2, w_r, b_r)

    n_tok = n_ctx + n_lat
    n_tiles = (n_tok * TOP_K) // MOE_TILE + N_EXPERTS
    pos, tbl = _dispatch(route)
    pos_flat = pos[:, :TOP_K].reshape(-1)
    counts, offsets = tbl[TBL_COUNT, :N_EXPERTS], tbl[TBL_OFFSET, :N_EXPERTS]
    x_sorted = _scatter_rows(h2, pos_flat, counts, offsets, n_tiles * MOE_TILE)
    y_sorted = _moe(x_sorted, tbl[TBL_EXPERT, :n_tiles], tbl[TBL_VALID, :n_tiles], counts,
                    w_e_gate[l], w_e_up[l], w_e_down[l])
    gf = norm_final[None, :]
    tm_c = COMBINE_TILE
    y_prompt = _combine(y_sorted, pos_flat, route, x1, gt2, gf, 0, n_ctx, lambda i: 0 * i, tm_c)
    y_sample = _combine(y_sorted, pos_flat, route, x1, gt2, gf, n_ctx, n_lat, lambda i: 1 + i // (tl // tm_c), tm_c)
    return (y_prompt.reshape(bc, tc, d), y_sample.reshape(bl, tl, d), new_state, new_k, new_v)
```

```python
import functools
import math

import jax
import jax.numpy as jnp
import numpy as np
from jax import lax
from jax.experimental import pallas as pl
from jax.experimental.pallas import tpu as pltpu

F32 = jnp.float32
BF16 = jnp.bfloat16

D_MODEL = 2048
GRID_W = 64
H_A = 8
DK_A = 128
DV_A = 128
CONV_K = 3
CHUNK = 64
H_B = 8
DQK_B = 64
DV_B = 2 * DQK_B
ROPE_BASE = 10000.0
N_GROUPS = 4
E_PER_GROUP = 8
N_EXPERTS = N_GROUPS * E_PER_GROUP
TOP_K = 2
D_FF_E = D_MODEL // 4
EPS = 1e-6

LANES = 128
QKV_A = 2 * H_A * DK_A + H_A * DV_A
N_SMALL = 4 * H_A
Z_GATE_A = QKV_A
Z_QB = Z_GATE_A + H_A * DV_A
Z_KB = Z_QB + H_B * 2 * DQK_B
Z_VB = Z_KB + H_B * 2 * DQK_B
Z_GM = Z_VB + H_B * DV_B
N_MAIN = Z_GM + 2 * D_MODEL

N_MOD_ROWS = 8
VMEM_LIMIT = 60 * 1024 * 1024
ROW_TILE = 512
INPROJ_ROWS = 2048
MERGE_TILE = 256
SCATTER_TILE = 512
COMBINE_TILE = 256
MOE_TILE = 256
ATTN_Q_TILE = 256
DELTA_HEADS_CTX = 8
DELTA_HEADS_LAT = 8
DELTA_SINGLE_BUFFER_ELEMS = 1 << 20

NT_DIMS = (((1,), (1,)), ((), ()))


def _params(*sem):
    return pltpu.CompilerParams(dimension_semantics=sem, vmem_limit_bytes=VMEM_LIMIT)


def _mm(a, b):
    return jnp.dot(a.astype(BF16), b.astype(BF16), preferred_element_type=F32)


def _mm_nt(a, b):
    return lax.dot_general(a.astype(BF16), b.astype(BF16), NT_DIMS, preferred_element_type=F32)


def _silu(x):
    return x * jax.nn.sigmoid(x)


def _rms_scale(x):
    return x * lax.rsqrt(jnp.mean(x * x, axis=-1, keepdims=True) + EPS)


def _adaln_body(c_ref, w_ref, b_ref, o_ref):
    s = _silu(c_ref[...])
    o_ref[...] = _mm(s, w_ref[...]) + b_ref[...]


def _adaln(cvec, w, b):
    d, n = w.shape
    tn = 1024
    return pl.pallas_call(
        _adaln_body,
        grid=(n // tn,),
        in_specs=[pl.BlockSpec((N_MOD_ROWS, d), lambda j: (0, 0)),
                  pl.BlockSpec((d, tn), lambda j: (0, j)),
                  pl.BlockSpec((1, tn), lambda j: (0, j))],
        out_specs=pl.BlockSpec((N_MOD_ROWS, tn), lambda j: (0, j)),
        out_shape=jax.ShapeDtypeStruct((N_MOD_ROWS, n), F32),
        compiler_params=_params("arbitrary"),
        name="adaln",
    )(cvec, w, b)


class _Rows:
    def __init__(self, n_ctx, n_lat, t_lat, tm):
        assert n_ctx % tm == 0 and n_lat % tm == 0 and t_lat % tm == 0
        self.tm = tm
        self.nct = n_ctx // tm
        self.nlt = n_lat // tm
        self.per_seq = t_lat // tm
        self.n = self.nct + self.nlt

    def ctx_idx(self, i):
        return jnp.minimum(i, self.nct - 1)

    def lat_idx(self, i):
        return jnp.maximum(i - self.nct, 0)

    def mod_idx(self, i):
        return jnp.where(i < self.nct, 0, 1 + (i - self.nct) // self.per_seq)


SMALL_COL0 = QKV_A + H_A * DV_A
GATE_KINDS = 4
GATE_BETA, GATE_GC, GATE_EGC, GATE_EKD = (kind * 2 * H_A for kind in range(GATE_KINDS))


def _gate_rows(zs, a_log_row, dt_row):
    lane = lax.broadcasted_iota(jnp.int32, (1, LANES), 1)
    x = zs + dt_row
    softplus = jnp.maximum(x, 0.0) + jnp.log(1.0 + jnp.exp(-jnp.abs(x)))
    in_g = jnp.logical_and(lane >= GATE_GC, lane < GATE_EGC)
    g = jnp.where(in_g, -jnp.exp(a_log_row) * softplus, 0.0)
    beta = jax.nn.sigmoid(zs)
    reversed_lane = lane >= GATE_GC + H_A
    ri = lax.broadcasted_iota(jnp.int32, (CHUNK, CHUNK), 0)
    ci = lax.broadcasted_iota(jnp.int32, (CHUNK, CHUNK), 1)
    prefix = jnp.where(ci <= ri, 1.0, 0.0)
    suffix = jnp.where(ci >= ri, 1.0, 0.0)
    exact = dict(preferred_element_type=F32, precision=lax.Precision.HIGHEST)
    rows = []
    for c in range(zs.shape[0] // CHUNK):
        r = slice(c * CHUNK, (c + 1) * CHUNK)
        gch = g[r]
        gc = jnp.where(reversed_lane, jnp.dot(suffix, gch, **exact), jnp.dot(prefix, gch, **exact))
        tot = jnp.sum(gch, axis=0, keepdims=True)
        rows.append(jnp.where(lane < GATE_GC, beta[r],
                              jnp.where(lane < GATE_EGC, gc,
                                        jnp.where(lane < GATE_EKD, pltpu.roll(jnp.exp(gc), GATE_EGC - GATE_GC, 1),
                                                  jnp.where(lane < GATE_EKD + 2 * H_A,
                                                            pltpu.roll(jnp.exp(tot - gc), GATE_EKD - GATE_GC, 1),
                                                            0.0)))))
    return jnp.concatenate(rows, axis=0)


def _prenorm_body(nct, xp_ref, xs_ref, g_ref, sh_ref, sc_ref, ws_ref, al_ref, dt_ref, h_ref, gates_ref):
    i = pl.program_id(0)

    def run(x_ref):
        h = _rms_scale(x_ref[...]) * g_ref[...]
        h = (h * (1.0 + sc_ref[0]) + sh_ref[0]).astype(BF16)
        h_ref[...] = h
        zs = lax.dot_general(h, ws_ref[...].astype(BF16), NT_DIMS, preferred_element_type=F32)
        gates_ref[...] = _gate_rows(zs, al_ref[...], dt_ref[...])

    @pl.when(i < nct)
    def _():
        run(xp_ref)

    @pl.when(i >= nct)
    def _():
        run(xs_ref)


def _prenorm(rows, xp, xs, gain, sh, sc, w_in_t, a_log, dt_bias):
    place = lambda v: jnp.pad(v.reshape(1, 2 * H_A), ((0, 0), (GATE_GC, LANES - GATE_EGC)))
    d = xp.shape[1]
    tm = rows.tm
    n_tok = rows.n * tm
    return pl.pallas_call(
        functools.partial(_prenorm_body, rows.nct),
        grid=(rows.n,),
        in_specs=[pl.BlockSpec((tm, d), lambda i: (rows.ctx_idx(i), 0)),
                  pl.BlockSpec((tm, d), lambda i: (rows.lat_idx(i), 0)),
                  pl.BlockSpec((1, d), lambda i: (0, 0)),
                  pl.BlockSpec((1, 1, d), lambda i: (rows.mod_idx(i), 0, 0)),
                  pl.BlockSpec((1, 1, d), lambda i: (rows.mod_idx(i), 0, 0)),
                  pl.BlockSpec((LANES, d), lambda i: (SMALL_COL0 // LANES, 0)),
                  pl.BlockSpec((1, LANES), lambda i: (0, 0)),
                  pl.BlockSpec((1, LANES), lambda i: (0, 0))],
        out_specs=[pl.BlockSpec((tm, d), lambda i: (i, 0)),
                   pl.BlockSpec((tm, LANES), lambda i: (i, 0))],
        out_shape=[jax.ShapeDtypeStruct((n_tok, d), BF16),
                   jax.ShapeDtypeStruct((n_tok, LANES), F32)],
        compiler_params=_params("arbitrary"),
        name="prenorm",
    )(xp, xs, gain, sh, sc, w_in_t, place(a_log), place(dt_bias))


INPROJ_TN = 1024
N_ALIGNED = SMALL_COL0 // INPROJ_TN


def _inproj_body(h_ref, wt_ref, z_ref, w_scr):
    @pl.when(pl.program_id(1) == 0)
    def _():
        w_scr[...] = wt_ref[...].astype(BF16)

    z_ref[...] = lax.dot_general(h_ref[...], w_scr[...], NT_DIMS, preferred_element_type=F32).astype(BF16)


def _inproj(h, w_in_t):
    n_tok, d = h.shape
    tm, tn = INPROJ_ROWS, INPROJ_TN
    first_row = lambda j: pl.multiple_of(j * tn + N_SMALL * jnp.minimum(j // N_ALIGNED, 1), N_SMALL)
    return pl.pallas_call(
        _inproj_body,
        grid=(N_MAIN // tn, n_tok // tm),
        in_specs=[pl.BlockSpec((tm, d), lambda j, i: (i, 0)),
                  pl.BlockSpec((pl.Element(tn), pl.Element(d)), lambda j, i: (first_row(j), 0))],
        out_specs=pl.BlockSpec((tm, tn), lambda j, i: (i, j)),
        out_shape=jax.ShapeDtypeStruct((n_tok, N_MAIN), BF16),
        scratch_shapes=[pltpu.VMEM((tn, d), BF16)],
        compiler_params=_params("arbitrary", "arbitrary"),
        name="inproj",
    )(h, w_in_t)


INV_BLOCK = 16


def _chunk_solve(chains, ii, jj):
    eye = ii == jj
    blk = (ii // INV_BLOCK) == (jj // INV_BLOCK)
    blk2 = (ii // (2 * INV_BLOCK)) == (jj // (2 * INV_BLOCK))
    cols, a_qk, a0, off, d0, rhs = [], [], [], [], [], []
    for kk, qk, v, k, gates, reverse, lane in chains:
        beta, gc, egc, ekd = (gates[:, b + lane:b + lane + 1] for b in (GATE_BETA, GATE_GC, GATE_EGC, GATE_EKD))
        incl = (jj >= ii) if reverse else (jj <= ii)
        strict = (jj > ii) if reverse else (jj < ii)
        gc_row = jnp.sum(jnp.where(eye, gc, 0.0), axis=0, keepdims=True)
        dec = jnp.where(incl, jnp.exp(jnp.where(incl, gc - gc_row, 0.0)), 0.0)
        a = jnp.where(strict, kk * dec, 0.0)
        cols.append((egc, ekd))
        a_qk.append(qk * dec)
        a0.append(jnp.where(blk, a, 0.0))
        off.append(jnp.where(blk, 0.0, a))
        d0.append(jnp.where(eye, 1.0, 0.0) - a0[-1])
        rhs.append(jnp.concatenate([v * beta, k * (beta * egc)], axis=1))
    p = [_mm(x, x) for x in a0]
    for _ in range(int(math.log2(INV_BLOCK)) - 2):
        r = [_mm(jnp.concatenate([pi, di], axis=0), pi) for pi, di in zip(p, d0)]
        p = [ri[:CHUNK] for ri in r]
        d0 = [di + ri[CHUNK:] for di, ri in zip(d0, r)]
    d0 = [di + _mm(di, pi) for di, pi in zip(d0, p)]
    wl = [_mm(di, oi) for di, oi in zip(d0, off)]
    yr = [_mm(di, ri) for di, ri in zip(d0, rhs)]
    x1 = [jnp.where(blk2, x, 0.0) for x in wl]
    yl = [jnp.where(blk2, 0.0, x) for x in wl]
    zl = [y - _mm(x, y) for x, y in zip(x1, yl)]
    zr = [y - _mm(x, y) for x, y in zip(x1, yr)]
    sol = [y - _mm(x, y) for x, y in zip(zl, zr)]
    return [(so[:, :DV_A], so[:, DV_A:], aq, egc, ekd) for so, aq, (egc, ekd) in zip(sol, a_qk, cols)]


def _delta_body(t_len, hp, has_s0, q_ref, k_ref, v_ref, cq_ref, ck_ref, cv_ref, g_ref, *rest):
    if has_s0:
        s0_ref, o_ref, qs, ks, vs, g_scr, u_scr, wq_scr, ak_scr, s_scr = rest
    else:
        o_ref, sfin_ref, qs, ks, vs, g_scr, u_scr, wq_scr, ak_scr, s_scr = rest
    n = t_len // CHUNK
    tpos = lax.broadcasted_iota(jnp.int32, (t_len, 1), 0)

    def conv_act(x, w):
        x_prev = jnp.where(tpos == 0, 0.0, pltpu.roll(x, 1, 0))
        x_next = jnp.where(tpos == t_len - 1, 0.0, pltpu.roll(x, t_len - 1, 0))
        return _silu(x_prev * w[0:1] + x * w[1:2] + x_next * w[2:3])

    def l2n(x):
        return x * lax.rsqrt(jnp.sum(x * x, axis=-1, keepdims=True) + EPS)

    for hh in range(hp):
        cols = slice(hh * LANES, (hh + 1) * LANES)
        qs[:, cols] = l2n(conv_act(q_ref[:, cols].astype(F32), cq_ref[:, cols])) * (DK_A ** -0.5)
        ks[:, cols] = l2n(conv_act(k_ref[:, cols].astype(F32), ck_ref[:, cols]))
        vs[:, cols] = conv_act(v_ref[:, cols].astype(F32), cv_ref[:, cols])
    if hp == H_A:
        g_scr[...] = g_ref[...]
    else:
        g_scr[...] = pltpu.roll(g_ref[...], (LANES - pl.program_id(1) * hp) % LANES, 1)
    if has_s0:
        s_scr[...] = s0_ref[...]
    else:
        s_scr[...] = jnp.zeros_like(s_scr)
    o_ref[...] = jnp.zeros_like(o_ref)
    ii = lax.broadcasted_iota(jnp.int32, (CHUNK, CHUNK), 0)
    jj = lax.broadcasted_iota(jnp.int32, (CHUNK, CHUNK), 1)

    def prep(c, carry):
        r = pl.ds(pl.multiple_of(c * CHUNK, CHUNK), CHUNK)
        g = g_scr[r, :]
        heads = []
        for hh in range(hp):
            cols = slice(hh * LANES, (hh + 1) * LANES)
            heads.append((qs[r, cols], ks[r, cols], vs[r, cols]))
        beta = lambda d, hh: g[:, GATE_BETA + d * H_A + hh:GATE_BETA + d * H_A + hh + 1]
        rr = [_mm_nt(jnp.concatenate([k * beta(0, hh), k * beta(1, hh), q], axis=0), k)
              for hh, (q, k, v) in enumerate(heads)]
        chains = [(rr[hh][d * CHUNK:(d + 1) * CHUNK], rr[hh][2 * CHUNK:], heads[hh][2], heads[hh][1], g,
                   d == 1, d * H_A + hh) for hh in range(hp) for d in (0, 1)]
        solved = _chunk_solve(chains, ii, jj)
        for hh in range(hp):
            cols = slice(hh * LANES, (hh + 1) * LANES)
            q, k = heads[hh][0], heads[hh][1]
            for d in (0, 1):
                u, w, a_qk, egc, ekd = solved[hh * 2 + d]
                slot = (d * hp + hh) * n + c
                u_scr[d, r, cols] = u
                wq_scr[slot] = jnp.concatenate([w, q * egc], axis=0).astype(BF16)
                ak_scr[slot] = jnp.concatenate([a_qk, (k * ekd).T], axis=0).astype(BF16)
        return carry

    lax.fori_loop(0, n, prep, 0, unroll=4)

    def scan(c, carry):
        chains = []
        for hh in range(hp):
            for d in (0, 1):
                cc = c if d == 0 else n - 1 - c
                chains.append((hh, d, pl.multiple_of(cc * CHUNK, CHUNK), (d * hp + hh) * n + cc))
        s = [s_scr[d, hh] for hh, d, r0, slot in chains]
        r1 = [jnp.dot(wq_scr[slot], si.astype(BF16), preferred_element_type=F32)
              for si, (hh, d, r0, slot) in zip(s, chains)]
        v_new = [u_scr[d, pl.ds(r0, CHUNK), hh * LANES:(hh + 1) * LANES] - ri[:CHUNK]
                 for ri, (hh, d, r0, slot) in zip(r1, chains)]
        r2 = [jnp.dot(ak_scr[slot], vi.astype(BF16), preferred_element_type=F32)
              for vi, (hh, d, r0, slot) in zip(v_new, chains)]
        for si, r1i, r2i, (hh, d, r0, slot) in zip(s, r1, r2, chains):
            o_ref[pl.ds(r0, CHUNK), hh * LANES:(hh + 1) * LANES] += r1i[CHUNK:] + r2i[:CHUNK]
            lane = GATE_EGC + d * H_A + hh
            edge = g_scr[pl.ds(pl.multiple_of(r0 + (0 if d else CHUNK - 8), 8), 8), lane:lane + 1]
            s_scr[d, hh] = si * (edge[0:1] if d else edge[7:8]) + r2i[CHUNK:]
        return carry

    lax.fori_loop(0, n, scan, 0, unroll=4)
    if not has_s0:
        sfin_ref[...] = s_scr[...]


def _deltanet(z, conv_w, gates, s0, n_seq, t_len, row0, hp):
    rb = row0 // t_len
    w = hp * LANES
    n = t_len // CHUNK
    mode = dict(pipeline_mode=pl.Buffered(1)) if t_len * w >= DELTA_SINGLE_BUFFER_ELEMS else {}
    seq_blk = lambda col0: pl.BlockSpec((t_len, w), lambda b, h: (rb + b, col0 // hp + h), **mode)
    cw_blk = lambda col0: pl.BlockSpec((CONV_K, w), lambda b, h: (0, col0 // hp + h))
    state_blk = pl.BlockSpec((None, None, 2, hp, DK_A, DV_A), lambda b, h: (b, 0, 0, h, 0, 0), **mode)
    in_specs = [seq_blk(0), seq_blk(H_A), seq_blk(2 * H_A), cw_blk(0), cw_blk(H_A), cw_blk(2 * H_A),
                pl.BlockSpec((t_len, LANES), lambda b, h: (rb + b, 0), **mode)]
    args = [z, z, z, conv_w, conv_w, conv_w, gates]
    o_spec = pl.BlockSpec((t_len, w), lambda b, h: (b, h), **mode)
    o_shape = jax.ShapeDtypeStruct((n_seq * t_len, H_A * DV_A), F32)
    has_s0 = s0 is not None
    if has_s0:
        in_specs += [state_blk]
        args += [s0]
        out_specs, out_shape = o_spec, o_shape
    else:
        out_specs = [o_spec, state_blk]
        out_shape = [o_shape, jax.ShapeDtypeStruct((n_seq, 1, 2, H_A, DK_A, DV_A), F32)]

    return pl.pallas_call(
        functools.partial(_delta_body, t_len, hp, has_s0),
        grid=(n_seq, H_A // hp),
        in_specs=in_specs,
        out_specs=out_specs,
        out_shape=out_shape,
        scratch_shapes=[pltpu.VMEM((t_len, w), F32)] * 3
        + [pltpu.VMEM((t_len, LANES), F32),
           pltpu.VMEM((2, t_len, w), F32),
           pltpu.VMEM((2 * hp * n, 2 * CHUNK, DV_A), BF16),
           pltpu.VMEM((2 * hp * n, CHUNK + DK_A, CHUNK), BF16),
           pltpu.VMEM((2, hp, DK_A, DV_A), F32)],
        compiler_params=_params("arbitrary", "arbitrary"),
        name="deltanet_lat" if has_s0 else "deltanet_ctx",
    )(*args)


def _subln(o, sub_ref, lam_init):
    return _rms_scale(o) * sub_ref[...] * (1.0 - lam_init)


def _attn_ctx_body(lam_init, q_ref, k_ref, v_ref, lam_ref, sub_ref, o_ref, ck_ref, cv_ref):
    heads = [slice(h * LANES, (h + 1) * LANES) for h in range(H_B)]
    lam = lam_ref[0:1, 0:1]
    ks = [k_ref[:, c] for c in heads]
    vs = [v_ref[:, c] for c in heads]
    for h in range(H_B):
        ck_ref[h] = ks[h].astype(F32)
        cv_ref[h] = vs[h].astype(F32)
    qb = [q_ref[:, c] * (DQK_B ** -0.5) for c in heads]
    kb = ks
    probs = []
    for lo in (0, DQK_B):
        s = [lax.dot_general(q[:, lo:lo + DQK_B], k[:, lo:lo + DQK_B], NT_DIMS, preferred_element_type=F32)
             for q, k in zip(qb, kb)]
        e = [jnp.exp(x - jnp.max(x, axis=-1, keepdims=True)) for x in s]
        probs.append([x * (1.0 / jnp.sum(x, axis=-1, keepdims=True)) for x in e])
    o = [jnp.dot((p1 - lam * p2).astype(BF16), v, preferred_element_type=F32)
         for p1, p2, v in zip(probs[0], probs[1], vs)]
    for c, oh in zip(heads, o):
        o_ref[:, c] = _subln(oh, sub_ref, lam_init).astype(BF16)


def _attn_ctx(z, lam, subln, n_seq, t_len, lam_init):
    n_tok = n_seq * t_len
    w = H_B * LANES
    blk = lambda col0: pl.BlockSpec((t_len, w), lambda b: (b, col0 // w))
    cache_blk = pl.BlockSpec((None, None, H_B, t_len, LANES), lambda b: (b, 0, 0, 0, 0))
    cache_shape = jax.ShapeDtypeStruct((n_seq, 1, H_B, t_len, LANES), F32)
    return pl.pallas_call(
        functools.partial(_attn_ctx_body, lam_init),
        grid=(n_seq,),
        in_specs=[blk(Z_QB), blk(Z_KB), blk(Z_VB),
                  pl.BlockSpec((1, LANES), lambda b: (0, 0)),
                  pl.BlockSpec((1, LANES), lambda b: (0, 0))],
        out_specs=[pl.BlockSpec((t_len, w), lambda b: (b, 0)), cache_blk, cache_blk],
        out_shape=[jax.ShapeDtypeStruct((n_tok, H_B * DV_B), BF16), cache_shape, cache_shape],
        compiler_params=_params("arbitrary"),
        name="attn_ctx",
    )(z, z, z, lam, subln)


def _rope(x, cos, sin_signed):
    lane = lax.broadcasted_iota(jnp.int32, (1, LANES), 1)
    first = (lane % 32) < 16
    partner = jnp.where(first, pltpu.roll(x, LANES - 16, 1), pltpu.roll(x, 16, 1))
    return x * cos + partner * sin_signed


LAT_HEADS = 2


def _attn_lat_body(lam_init, n_past, q_ref, k_ref, v_ref, pk_ref, pv_ref, cosq_ref, sinq_ref, cos_ref, sin_ref,
                   lam_ref, sub_ref, o_ref, keys, vals):
    heads = [slice(h * LANES, (h + 1) * LANES) for h in range(LAT_HEADS)]

    @pl.when(pl.program_id(2) == 0)
    def _():
        for h, c in enumerate(heads):
            keys[h, 0:n_past, :] = pk_ref[h].astype(BF16)
            vals[h, 0:n_past, :] = pv_ref[h].astype(BF16)
            keys[h, n_past:, :] = _rope(k_ref[:, c].astype(F32), cos_ref[...], sin_ref[...]).astype(BF16)
            vals[h, n_past:, :] = v_ref[:, c]

    lam = lam_ref[0:1, 0:1]
    qb = [(_rope(q_ref[:, c].astype(F32), cosq_ref[...], sinq_ref[...]) * (DQK_B ** -0.5)).astype(BF16)
          for c in heads]
    maps = []
    for lo in (0, DQK_B):
        s = [lax.dot_general(q[:, lo:lo + DQK_B], keys[h, :, lo:lo + DQK_B], NT_DIMS, preferred_element_type=F32)
             for h, q in enumerate(qb)]
        e = [jnp.exp(x - jnp.max(x, axis=-1, keepdims=True)) for x in s]
        inv = [1.0 / jnp.sum(x, axis=-1, keepdims=True) for x in e]
        maps.append([jnp.dot(x.astype(BF16), vals[h], preferred_element_type=F32) * r
                     for h, (x, r) in enumerate(zip(e, inv))])
    for h, c in enumerate(heads):
        o_ref[:, c] = _subln(maps[0][h] - lam * maps[1][h], sub_ref, lam_init).astype(BF16)


def _attn_lat(z, cache_k, cache_v, cos, sin, lam, subln, n_seq, t_len, row0, lam_init):
    tq = ATTN_Q_TILE
    nq = t_len // tq
    n_past = cache_k.shape[3]
    rbq = row0 // tq
    rbs = row0 // t_len
    w = LAT_HEADS * LANES
    seq_blk = lambda col0: pl.BlockSpec((t_len, w), lambda b, h, qi: (rbs + b, col0 // w + h))
    past_blk = pl.BlockSpec((None, None, LAT_HEADS, n_past, LANES), lambda b, h, qi: (b, 0, h, 0, 0))
    row_vec = pl.BlockSpec((1, LANES), lambda b, h, qi: (0, 0))
    return pl.pallas_call(
        functools.partial(_attn_lat_body, lam_init, n_past),
        grid=(n_seq, H_B // LAT_HEADS, nq),
        in_specs=[pl.BlockSpec((tq, w), lambda b, h, qi: (rbq + b * nq + qi, Z_QB // w + h)),
                  seq_blk(Z_KB), seq_blk(Z_VB), past_blk, past_blk,
                  pl.BlockSpec((tq, LANES), lambda b, h, qi: (qi, 0)),
                  pl.BlockSpec((tq, LANES), lambda b, h, qi: (qi, 0)),
                  pl.BlockSpec((t_len, LANES), lambda b, h, qi: (0, 0)),
                  pl.BlockSpec((t_len, LANES), lambda b, h, qi: (0, 0)),
                  row_vec, row_vec],
        out_specs=pl.BlockSpec((tq, w), lambda b, h, qi: (b * nq + qi, h)),
        out_shape=jax.ShapeDtypeStruct((n_seq * t_len, H_B * DV_B), BF16),
        scratch_shapes=[pltpu.VMEM((LAT_HEADS, n_past + t_len, LANES), BF16)] * 2,
        compiler_params=_params("arbitrary", "arbitrary", "arbitrary"),
        name="attn_lat",
    )(z, z, z, cache_k, cache_v, cos, sin, cos, sin, lam, subln)


def _rope_tables(t_len):
    t = np.arange(t_len)
    pos = np.stack([t // GRID_W, t % GRID_W], axis=1).astype(np.float32)
    nf = DQK_B // 4
    inv_freq = np.float32(ROPE_BASE) ** (-np.arange(nf, dtype=np.float32) / np.float32(nf))
    lane = np.arange(LANES)
    half = (lane % DQK_B) // (DQK_B // 2)
    ang = (pos[:, half] * inv_freq[lane % nf][None, :]).astype(np.float32)
    sign = np.where((lane % (DQK_B // 2)) < nf, -1.0, 1.0).astype(np.float32)
    return jnp.asarray(np.cos(ang), F32), jnp.asarray(np.sin(ang) * sign[None, :], F32)


def _merge_body(nct, oac_ref, oal_ref, ga_ref, on_ref, obc_ref, obl_ref, wa_f32, wb_f32, *rest):
    gm_refs, m_ref, wa_ref, wb_ref = rest[:-3], rest[-3], rest[-2], rest[-1]
    i = pl.program_id(0)
    n_blk = len(gm_refs) // 2

    @pl.when(i == 0)
    def _():
        wa_ref[...] = wa_f32[...].astype(BF16)
        wb_ref[...] = wb_f32[...].astype(BF16)

    def run(oa_ref, ob_ref):
        a = jnp.concatenate(
            [(_rms_scale(oa_ref[:, c]) * on_ref[...] * _silu(ga_ref[:, c].astype(F32))).astype(BF16)
             for c in (slice(h * DV_A, (h + 1) * DV_A) for h in range(H_A))], axis=1)
        ya = jnp.dot(a, wa_ref[...], preferred_element_type=F32)
        yb = jnp.dot(ob_ref[...], wb_ref[...], preferred_element_type=F32)
        tn = gm_refs[0].shape[1]
        for j in range(n_blk):
            c = slice(j * tn, (j + 1) * tn)
            m_ref[:, c] = (jax.nn.sigmoid(gm_refs[j][...].astype(F32)) * ya[:, c]
                           + jax.nn.sigmoid(gm_refs[n_blk + j][...].astype(F32)) * yb[:, c]).astype(BF16)

    @pl.when(i < nct)
    def _():
        run(oac_ref, obc_ref)

    @pl.when(i >= nct)
    def _():
        run(oal_ref, obl_ref)


def _merge(rows, z, oa_c, oa_l, ob_c, ob_l, onorm, w_up_a, w_up_b):
    n_tok = z.shape[0]
    tm = rows.tm
    ka, d = w_up_a.shape
    tn = 1024
    nj = d // tn
    ctx_blk = pl.BlockSpec((tm, ka), lambda i: (rows.ctx_idx(i), 0))
    lat_blk = pl.BlockSpec((tm, ka), lambda i: (rows.lat_idx(i), 0))
    weight = pl.BlockSpec((ka, d), lambda i: (0, 0), pipeline_mode=pl.Buffered(1))
    gate_cols = [pl.BlockSpec((tm, tn), functools.partial(lambda i, c: (i, c), c=Z_GM // tn + j)) for j in range(2 * nj)]
    return pl.pallas_call(
        functools.partial(_merge_body, rows.nct),
        grid=(n_tok // tm,),
        in_specs=[ctx_blk, lat_blk,
                  pl.BlockSpec((tm, ka), lambda i: (i, Z_GATE_A // ka)),
                  pl.BlockSpec((1, DV_A), lambda i: (0, 0)),
                  ctx_blk, lat_blk, weight, weight] + gate_cols,
        out_specs=pl.BlockSpec((tm, d), lambda i: (i, 0)),
        out_shape=jax.ShapeDtypeStruct((n_tok, d), BF16),
        scratch_shapes=[pltpu.VMEM((ka, d), BF16)] * 2,
        compiler_params=_params("arbitrary"),
        name="merge",
    )(oa_c, oa_l, z, onorm, ob_c, ob_l, w_up_a, w_up_b, *([z] * (2 * nj)))


ROUTE_E = 0
ROUTE_W = TOP_K


def _route_rows(lg):
    lane = lax.broadcasted_iota(jnp.int32, lg.shape, 1)
    neg = -jnp.inf

    def first_max(x):
        m = jnp.max(x, axis=1, keepdims=True)
        return m, jnp.min(jnp.where(x == m, lane, LANES), axis=1, keepdims=True)

    gl = jnp.where(lane < N_GROUPS, lg, neg)
    gmax, g_idx = first_max(gl)
    pg_top = 1.0 / jnp.sum(jnp.exp(gl - gmax), axis=1, keepdims=True)
    lo = N_GROUPS + E_PER_GROUP * g_idx
    el = jnp.where(jnp.logical_and(lane >= lo, lane < lo + E_PER_GROUP), lg, neg)
    emax, i1 = first_max(el)
    esum = jnp.sum(jnp.exp(el - emax), axis=1, keepdims=True)
    e2max, i2 = first_max(jnp.where(lane == i1, neg, el))
    p1 = 1.0 / esum
    p2 = jnp.exp(e2max - emax) / esum
    den = p1 + p2
    vals = [(i1 - N_GROUPS).astype(F32), (i2 - N_GROUPS).astype(F32), pg_top * p1 / den, pg_top * p2 / den]
    out = jnp.zeros(lg.shape, F32)
    for pos, val in enumerate(vals):
        out = jnp.where(lane == pos, val, out)
    return out


def _outproj_body(nct, m_ref, wo_ref, xp_ref, xs_ref, gt_ref, g2_ref, sh_ref, sc_ref, wr_ref, br_ref,
                  x1_ref, h2_ref, rt_ref):
    i = pl.program_id(0)
    half = m_ref.shape[0] // 2

    def finish(x_ref):
        for r in (slice(0, half), slice(half, 2 * half)):
            y = jnp.dot(m_ref[r, :], wo_ref[...], preferred_element_type=F32)
            x1 = x_ref[r, :] + gt_ref[0] * y
            x1_ref[r, :] = x1
            h2 = _rms_scale(x1) * g2_ref[...]
            h2 = h2 * (1.0 + sc_ref[0]) + sh_ref[0]
            h2_ref[r, :] = h2
            hi = h2.astype(BF16)
            lo = (h2 - hi.astype(F32)).astype(BF16)
            p_hi = jnp.dot(hi, wr_ref[...], preferred_element_type=F32)
            p_lo = jnp.dot(lo, wr_ref[...], preferred_element_type=F32)
            lg = p_hi[:, :LANES] + p_hi[:, LANES:] + p_lo[:, :LANES] + p_lo[:, LANES:] + br_ref[...]
            rt_ref[r, :] = _route_rows(lg)

    @pl.when(i < nct)
    def _():
        finish(xp_ref)

    @pl.when(i >= nct)
    def _():
        finish(xs_ref)


def _outproj(rows, mixed, w_o, xp, xs, gt1, gain2, sh2, sc2, w_r, b_r):
    d = xp.shape[1]
    tm = rows.tm
    n_tok = rows.n * tm
    mod = lambda: pl.BlockSpec((1, 1, d), lambda i: (rows.mod_idx(i), 0, 0))
    tok = pl.BlockSpec((tm, d), lambda i: (i, 0))
    return pl.pallas_call(
        functools.partial(_outproj_body, rows.nct),
        grid=(rows.n,),
        in_specs=[tok,
                  pl.BlockSpec((d, d), lambda i: (0, 0)),
                  pl.BlockSpec((tm, d), lambda i: (rows.ctx_idx(i), 0)),
                  pl.BlockSpec((tm, d), lambda i: (rows.lat_idx(i), 0)),
                  mod(),
                  pl.BlockSpec((1, d), lambda i: (0, 0)),
                  mod(), mod(),
                  pl.BlockSpec((d, 2 * LANES), lambda i: (0, 0)),
                  pl.BlockSpec((1, LANES), lambda i: (0, 0))],
        out_specs=[tok, tok, pl.BlockSpec((tm, LANES), lambda i: (i, 0))],
        out_shape=[jax.ShapeDtypeStruct((n_tok, d), F32), jax.ShapeDtypeStruct((n_tok, d), F32),
                   jax.ShapeDtypeStruct((n_tok, LANES), F32)],
        compiler_params=_params("arbitrary"),
        name="outproj",
    )(mixed, w_o, xp, xs, gt1, gain2, sh2, sc2, w_r, b_r)


DISPATCH_BLOCK = 256
TBL_EXPERT, TBL_VALID, TBL_COUNT, TBL_OFFSET = 0, 1, 2, 3


def _dispatch_body(n_tok, route_ref, pos_ref, tbl_ref):
    nb = n_tok // DISPATCH_BLOCK
    lane = lax.broadcasted_iota(jnp.int32, (1, LANES), 1)
    lane_f = lane.astype(F32)

    def one_hot(b, k):
        r = pl.ds(pl.multiple_of(b * DISPATCH_BLOCK, DISPATCH_BLOCK), DISPATCH_BLOCK)
        return jnp.where(route_ref[r, ROUTE_E + k:ROUTE_E + k + 1] == lane_f, 1.0, 0.0)

    def count(b, acc):
        return acc + jnp.sum(one_hot(b, 0) + one_hot(b, 1), axis=0, keepdims=True)

    counts = lax.fori_loop(0, nb, count, jnp.zeros((1, LANES), F32))
    padded = jnp.floor((counts + (MOE_TILE - 1)) * (1.0 / MOE_TILE)) * MOE_TILE
    pad_end = padded
    shift = 1
    while shift < LANES:
        pad_end = pad_end + jnp.where(lane >= shift, pltpu.roll(pad_end, shift, 1), 0.0)
        shift *= 2
    pad_off = pad_end - padded

    ri = lax.broadcasted_iota(jnp.int32, (DISPATCH_BLOCK, DISPATCH_BLOCK), 0)
    ci = lax.broadcasted_iota(jnp.int32, (DISPATCH_BLOCK, DISPATCH_BLOCK), 1)
    before = jnp.where(ci < ri, 1.0, 0.0).astype(BF16)

    def place(b, run):
        oh = [one_hot(b, k) for k in range(TOP_K)]
        base = pad_off + run
        out = jnp.zeros((DISPATCH_BLOCK, LANES), F32)
        lane_b = lax.broadcasted_iota(jnp.int32, (DISPATCH_BLOCK, LANES), 1)
        for k in range(TOP_K):
            prior = jnp.dot(before, oh[k].astype(BF16), preferred_element_type=F32)
            pos = jnp.sum(oh[k] * (base + prior), axis=1, keepdims=True)
            out = jnp.where(lane_b == k, pos, out)
            base = base + jnp.sum(oh[k], axis=0, keepdims=True)
        r = pl.ds(pl.multiple_of(b * DISPATCH_BLOCK, DISPATCH_BLOCK), DISPATCH_BLOCK)
        pos_ref[r, :] = out.astype(jnp.int32)
        return base - pad_off

    lax.fori_loop(0, nb, place, jnp.zeros((1, LANES), F32))

    end_col = jnp.transpose(jnp.broadcast_to(pad_end, (8, LANES)))[:, 0:1]
    e_col = lax.broadcasted_iota(jnp.int32, (LANES, 1), 0)
    tile_start = lane_f * MOE_TILE
    passed = jnp.where(jnp.logical_and(end_col <= tile_start, e_col < N_EXPERTS), 1.0, 0.0)
    tile_expert = jnp.minimum(jnp.sum(passed, axis=0, keepdims=True), N_EXPERTS - 1.0)
    total = jnp.sum(jnp.where(lane == N_EXPERTS - 1, pad_end, 0.0), axis=1, keepdims=True)
    tile_valid = jnp.where(tile_start < total, 1.0, 0.0)
    row = lax.broadcasted_iota(jnp.int32, (8, LANES), 0)
    tbl = jnp.zeros((8, LANES), F32)
    for idx, val in ((TBL_EXPERT, tile_expert), (TBL_VALID, tile_valid), (TBL_COUNT, counts), (TBL_OFFSET, pad_off)):
        tbl = jnp.where(row == idx, val, tbl)
    tbl_ref[...] = tbl.astype(jnp.int32)


def _dispatch(route):
    n_tok = route.shape[0]
    return pl.pallas_call(
        functools.partial(_dispatch_body, n_tok),
        out_shape=[jax.ShapeDtypeStruct((n_tok, LANES), jnp.int32), jax.ShapeDtypeStruct((8, LANES), jnp.int32)],
        compiler_params=pltpu.CompilerParams(vmem_limit_bytes=VMEM_LIMIT),
        name="dispatch",
    )(route)


def _scatter_body(tm, n_tiles, pos_ref, cnt_ref, off_ref, h_ref, o_hbm, zbuf, sem):
    i = pl.program_id(0)
    last = N_EXPERTS - 1
    first_empty = (off_ref[last] + cnt_ref[last] + MOE_TILE - 1) // MOE_TILE

    def tile_copy(t, which):
        rows = pl.ds(pl.multiple_of(t * MOE_TILE, MOE_TILE), MOE_TILE)
        return pltpu.make_async_copy(zbuf, o_hbm.at[rows, :], sem.at[which])

    def zero_done(t, carry):
        tile_copy(t, 2).wait()
        return carry

    @pl.when(i == 0)
    def _():
        zbuf[...] = jnp.zeros_like(zbuf)

        def partial_tile(e, start):
            cnt = cnt_ref[e]

            @pl.when(cnt % MOE_TILE != 0)
            def _():
                cp = tile_copy((off_ref[e] + cnt) // MOE_TILE, 1)
                if start:
                    cp.start()
                else:
                    cp.wait()

        def zero_partial(e, carry):
            partial_tile(e, True)
            return carry

        def partial_done(e, carry):
            partial_tile(e, False)
            return carry

        def zero_tile(t, carry):
            tile_copy(t, 2).start()
            return carry

        lax.fori_loop(0, N_EXPERTS, zero_partial, 0)
        lax.fori_loop(first_empty, n_tiles, zero_tile, 0)
        lax.fori_loop(0, N_EXPERTS, partial_done, 0)

    def row_copies(r):
        return [pltpu.make_async_copy(h_ref.at[pl.ds(r, 1), :],
                                      o_hbm.at[pl.ds(pos_ref[(i * tm + r) * TOP_K + k], 1), :], sem.at[0])
                for k in range(TOP_K)]

    def start(r, carry):
        for k, cp in enumerate(row_copies(r)):
            cp.start(priority=k)
        return carry

    def wait(r, carry):
        for cp in row_copies(r):
            cp.wait()
        return carry

    lax.fori_loop(0, tm, start, 0, unroll=8)

    @pl.when(i == 0)
    def _():
        lax.fori_loop(first_empty, n_tiles, zero_done, 0)

    lax.fori_loop(0, tm, wait, 0, unroll=8)


def _scatter_rows(h2, pos_flat, counts, offsets, n_rows):
    n_tok, d = h2.shape
    tm = SCATTER_TILE
    grid_spec = pltpu.PrefetchScalarGridSpec(
        num_scalar_prefetch=3,
        grid=(n_tok // tm,),
        in_specs=[pl.BlockSpec((tm, d), lambda i, pos, cnt, off: (i, 0))],
        out_specs=pl.BlockSpec(memory_space=pl.ANY),
        scratch_shapes=[pltpu.VMEM((MOE_TILE, d), F32), pltpu.SemaphoreType.DMA((3,))],
    )
    return pl.pallas_call(
        functools.partial(_scatter_body, tm, n_rows // MOE_TILE),
        grid_spec=grid_spec,
        out_shape=jax.ShapeDtypeStruct((n_rows, d), F32),
        compiler_params=_params("arbitrary"),
        name="scatter_rows",
    )(pos_flat, counts, offsets, h2)


def _moe_body(n_tiles, te_ref, tv_ref, cnt_ref, x_ref, wg_hbm, wu_hbm, wd_hbm, y_ref,
              wg_f, wu_f, wd_f, wg_b, wu_b, wd_b, slot_ref, sem):
    i = pl.program_id(0)
    prev = jnp.maximum(i - 1, 0)
    valid = tv_ref[i] == 1
    e = te_ref[i]

    def fetch(expert, slot):
        return [pltpu.make_async_copy(src.at[expert], dst.at[slot], sem.at[slot])
                for src, dst in ((wg_hbm, wg_f), (wu_hbm, wu_f), (wd_hbm, wd_f))]

    @pl.when(jnp.logical_and(valid, i == 0))
    def _():
        slot_ref[0] = 1
        for cp in fetch(e, 0):
            cp.start()

    @pl.when(jnp.logical_and(valid, jnp.logical_or(i == 0, e != te_ref[prev])))
    def _():
        slot = 1 - slot_ref[0]
        slot_ref[0] = slot
        for cp in fetch(e, slot):
            cp.wait()
        nxt = jnp.minimum(i + (cnt_ref[e] + MOE_TILE - 1) // MOE_TILE, n_tiles - 1)

        @pl.when(jnp.logical_and(nxt > i, tv_ref[nxt] == 1))
        def _():
            for cp in fetch(te_ref[nxt], 1 - slot):
                cp.start()

        wg_b[...] = wg_f[slot].astype(BF16)
        wu_b[...] = wu_f[slot].astype(BF16)
        wd_b[...] = wd_f[slot].astype(BF16)

    @pl.when(valid)
    def _():
        x = x_ref[...].astype(BF16)
        g = jnp.dot(x, wg_b[...], preferred_element_type=F32)
        u = jnp.dot(x, wu_b[...], preferred_element_type=F32)
        y_ref[...] = jnp.dot((_silu(g) * u).astype(BF16), wd_b[...], preferred_element_type=F32)

    @pl.when(tv_ref[i] == 0)
    def _():
        y_ref[...] = jnp.zeros_like(y_ref)


def _moe(x_sorted, tile_expert, tile_valid, counts, w_g, w_u, w_d):
    n_tiles = tile_expert.shape[0]
    d = x_sorted.shape[1]
    f = w_g.shape[-1]
    hbm = pl.BlockSpec(memory_space=pl.ANY)
    grid_spec = pltpu.PrefetchScalarGridSpec(
        num_scalar_prefetch=3,
        grid=(n_tiles,),
        in_specs=[pl.BlockSpec((MOE_TILE, d), lambda i, te, tv, cnt: (i * tv[i], 0)), hbm, hbm, hbm],
        out_specs=pl.BlockSpec((MOE_TILE, d), lambda i, te, tv, cnt: (i, 0)),
        scratch_shapes=[pltpu.VMEM((2, d, f), F32), pltpu.VMEM((2, d, f), F32), pltpu.VMEM((2, f, d), F32),
                        pltpu.VMEM((d, f), BF16), pltpu.VMEM((d, f), BF16), pltpu.VMEM((f, d), BF16),
                        pltpu.SMEM((1,), jnp.int32), pltpu.SemaphoreType.DMA((2,))],
    )
    return pl.pallas_call(
        functools.partial(_moe_body, n_tiles),
        grid_spec=grid_spec,
        out_shape=jax.ShapeDtypeStruct((n_tiles * MOE_TILE, d), F32),
        compiler_params=_params("arbitrary"),
        name="moe",
    )(tile_expert, tile_valid, counts, x_sorted, w_g, w_u, w_d)


def _combine_body(tm, n_tiles, row0, pos_ref, y_hbm, rt_ref, x1_ref, gt_ref, gf_ref, o_ref, ybuf, sem):
    i = pl.program_id(0)
    slot = i % 2

    def gather(tile, dst_slot, start):
        def row(r, carry):
            for k in range(TOP_K):
                p = pos_ref[(row0 + tile * tm + r) * TOP_K + k]
                cp = pltpu.make_async_copy(y_hbm.at[pl.ds(p, 1), :], ybuf.at[dst_slot, k, pl.ds(r, 1), :],
                                           sem.at[dst_slot])
                if start:
                    cp.start(priority=k)
                else:
                    cp.wait()
            return carry

        lax.fori_loop(0, tm, row, 0, unroll=8)

    @pl.when(i == 0)
    def _():
        gather(0, 0, True)

    @pl.when(i + 1 < n_tiles)
    def _():
        gather(i + 1, 1 - slot, True)

    gather(i, slot, False)
    rt = rt_ref[...]
    y = rt[:, ROUTE_W:ROUTE_W + 1] * ybuf[slot, 0] + rt[:, ROUTE_W + 1:ROUTE_W + 2] * ybuf[slot, 1]
    x2 = x1_ref[...] + gt_ref[0] * y
    o_ref[...] = _rms_scale(x2) * gf_ref[...]


def _combine(y_sorted, pos_flat, route, x1, gt2, gain_f, row0, n_rows, mod_of_tile, tm):
    d = x1.shape[1]
    rb = row0 // tm
    n_tiles = n_rows // tm
    grid_spec = pltpu.PrefetchScalarGridSpec(
        num_scalar_prefetch=1,
        grid=(n_tiles,),
        in_specs=[pl.BlockSpec(memory_space=pl.ANY),
                  pl.BlockSpec((tm, LANES), lambda i, pos: (rb + i, 0)),
                  pl.BlockSpec((tm, d), lambda i, pos: (rb + i, 0)),
                  pl.BlockSpec((1, 1, d), lambda i, pos: (mod_of_tile(i), 0, 0)),
                  pl.BlockSpec((1, d), lambda i, pos: (0, 0))],
        out_specs=pl.BlockSpec((tm, d), lambda i, pos: (i, 0)),
        scratch_shapes=[pltpu.VMEM((2, TOP_K, tm, d), F32), pltpu.SemaphoreType.DMA((2,))],
    )
    return pl.pallas_call(
        functools.partial(_combine_body, tm, n_tiles, row0),
        grid_spec=grid_spec,
        out_shape=jax.ShapeDtypeStruct((n_rows, d), F32),
        compiler_params=_params("arbitrary"),
        name="combine",
    )(pos_flat, y_sorted, route, x1, gt2, gain_f)


def kernel(x_prompt, x_sample, c, state_delta, cache_k, cache_v, c_ctx, w_ada, b_ada, norm_mix, norm_ffn, w_in,
           conv_a, a_log, dt_bias, onorm_a, lam, subln_b, w_up_a, w_up_b, w_o, w_rg, b_rg, w_re, b_re,
           w_e_gate, w_e_up, w_e_down, norm_final):
    assert w_in.shape[0] == 1, "single trunk layer"
    l = 0
    lam_init = 0.8 - 0.6 * math.exp(-0.3 * l)
    bc, tc, d = x_prompt.shape
    bl, tl, _ = x_sample.shape
    n_ctx, n_lat = bc * tc, bl * tl
    xp = x_prompt.reshape(n_ctx, d)
    xs = x_sample.reshape(n_lat, d)

    cvec = jnp.zeros((N_MOD_ROWS, d), F32).at[0].set(c_ctx).at[1:1 + bl].set(c)
    mod = _adaln(cvec, w_ada[l], b_ada[l][None, :])
    sh1, sc1, gt1, sh2, sc2, gt2 = [m.reshape(N_MOD_ROWS, 1, d) for m in jnp.split(mod, 6, axis=1)]

    rows = _Rows(n_ctx, n_lat, tl, ROW_TILE)
    w_in_t = jnp.swapaxes(w_in[l], 0, 1)
    h1, gates = _prenorm(rows, xp, xs, norm_mix[l][None, :], sh1, sc1, w_in_t, a_log[l], dt_bias[l])
    z = _inproj(h1, w_in_t)

    oa_c, new_state = _deltanet(z, conv_a[l], gates, None, bc, tc, 0, DELTA_HEADS_CTX)
    oa_l = _deltanet(z, conv_a[l], gates, state_delta, bl, tl, n_ctx, DELTA_HEADS_LAT)

    lv = lam[l]
    lam_val = jnp.exp(jnp.sum(lv[0] * lv[1])) - jnp.exp(jnp.sum(lv[2] * lv[3])) + lam_init
    lam_row = jnp.full((1, LANES), lam_val, F32)
    sub_row = subln_b[l][None, :]
    ob_c, new_k, new_v = _attn_ctx(z, lam_row, sub_row, bc, tc, lam_init)
    cos, sin = _rope_tables(tl)
    ob_l = _attn_lat(z, cache_k, cache_v, cos, sin, lam_row, sub_row, bl, tl, n_ctx, lam_init)

    mixed = _merge(_Rows(n_ctx, n_lat, tl, MERGE_TILE), z, oa_c, oa_l, ob_c, ob_l, onorm_a[l][None, :],
                   w_up_a[l], w_up_b[l])
    w_r = jnp.pad(jnp.concatenate([w_rg[l], w_re[l]], axis=1), ((0, 0), (0, LANES - N_GROUPS - N_EXPERTS)))
    w_r_hi = w_r.astype(BF16)
    w_r = jnp.concatenate([w_r_hi, (w_r - w_r_hi.astype(F32)).astype(BF16)], axis=1)
    b_r = jnp.pad(jnp.concatenate([b_rg[l], b_re[l]]), (0, LANES - N_GROUPS - N_EXPERTS))[None, :]
    x1, h2, route = _outproj(rows, mixed, w_o[l].astype(BF16), xp, xs, gt1, norm_ffn[l][None, :], sh2, sc2, w_r, b_r)

    n_tok = n_ctx + n_lat
    n_tiles = (n_tok * TOP_K) // MOE_TILE + N_EXPERTS
    pos, tbl = _dispatch(route)
    pos_flat = pos[:, :TOP_K].reshape(-1)
    counts, offsets = tbl[TBL_COUNT, :N_EXPERTS], tbl[TBL_OFFSET, :N_EXPERTS]
    x_sorted = _scatter_rows(h2, pos_flat, counts, offsets, n_tiles * MOE_TILE)
    y_sorted = _moe(x_sorted, tbl[TBL_EXPERT, :n_tiles], tbl[TBL_VALID, :n_tiles], counts,
                    w_e_gate[l], w_e_up[l], w_e_down[l])
    gf = norm_final[None, :]
    tm_c = COMBINE_TILE
    y_prompt = _combine(y_sorted, pos_flat, route, x1, gt2, gf, 0, n_ctx, lambda i: 0 * i, tm_c)
    y_sample = _combine(y_sorted, pos_flat, route, x1, gt2, gf, n_ctx, n_lat, lambda i: 1 + i // (tl // tm_c), tm_c)
    return (y_prompt.reshape(bc, tc, d), y_sample.reshape(bl, tl, d), new_state, new_k, new_v)
```
